```python
import math
import jax, jax.numpy as jnp
from jax import lax
import numpy as np

D_MODEL = 1024
BATCH = 8
SEQ = 2048
DEPTH = 1
DEC_BATCH = 32
DEC_SEQ = 8
PAST_LEN = 16384
PAGE_SIZE = 128

D_INNER = 2 * D_MODEL
SSD_HEAD_DIM = 64
SSD_HEADS = D_INNER // SSD_HEAD_DIM
SSD_GROUPS = 4
HEADS_PER_GROUP = SSD_HEADS // SSD_GROUPS
D_STATE = 128
D_CONV = 4
CONV_DIM = D_INNER + 2 * SSD_GROUPS * D_STATE
SSD_CHUNK = 128
MLA_HEADS = 8
Q_LORA = D_MODEL // 4
KV_LORA = D_MODEL // 4
QK_NOPE = 128
ROPE_DIM = 64
V_HEAD = 128
ROPE_THETA = 10000.0
ATTN_SCALE = (QK_NOPE + ROPE_DIM) ** -0.5
Q_BLOCK = 128
N_EXPERTS = 32
TOP_K = 4
D_FF = D_MODEL
SWIGLU_ALPHA = 1.702
SWIGLU_LIMIT = 7.0
MOE_BLOCK = 128
DN_ALPHA = (2 * DEPTH) ** 0.25
DN_BETA = (8 * DEPTH) ** -0.25
LN_EPS = 1e-5
RMS_EPS = 1e-6
IN_SIZES = (Q_LORA, KV_LORA, ROPE_DIM, D_INNER, CONV_DIM, SSD_HEADS, D_MODEL, D_MODEL)
IN_SPLITS = tuple(int(s) for s in np.cumsum(IN_SIZES)[:-1])
D_IN_TOTAL = int(sum(IN_SIZES))

kernel_name = 'hybrid_ssd_mla_moe_deepnorm_step'


def layer_norm(x, g, b):
    xf = x.astype(jnp.float32)
    mu = jnp.mean(xf, axis=-1, keepdims=True)
    var = jnp.mean(jnp.square(xf - mu), axis=-1, keepdims=True)
    out = (xf - mu) * lax.rsqrt(var + LN_EPS) * g.astype(jnp.float32) + b.astype(jnp.float32)
    return out.astype(x.dtype)


def rms_norm(x, g):
    xf = x.astype(jnp.float32)
    out = xf * lax.rsqrt(jnp.mean(xf * xf, axis=-1, keepdims=True) + RMS_EPS) * g.astype(jnp.float32)
    return out.astype(x.dtype)


def rope(x, pos):
    half = ROPE_DIM // 2
    inv = ROPE_THETA ** (-jnp.arange(half, dtype=jnp.float32) / half)
    ang = pos.astype(jnp.float32)[:, None] * inv[None, :]
    ang = ang.reshape((ang.shape[0],) + (1,) * (x.ndim - 3) + (half,))
    cos, sin = jnp.cos(ang), jnp.sin(ang)
    x1 = x[..., :half].astype(jnp.float32)
    x2 = x[..., half:].astype(jnp.float32)
    return jnp.concatenate([x1 * cos - x2 * sin, x2 * cos + x1 * sin], axis=-1).astype(x.dtype)


def causal_conv(xbc, conv_state, w, b):
    t = xbc.shape[1]
    full = jnp.concatenate([conv_state.astype(xbc.dtype), xbc], axis=1)
    out = b
    for k in range(D_CONV):
        out = out + full[:, k:k + t] * w[k]
    return jax.nn.silu(out), full[:, t:]


def ssd_scan(x, dt, a, bm, cm, h0):
    b, t = x.shape[:2]
    ln = SSD_CHUNK if t % SSD_CHUNK == 0 else t
    nc = t // ln
    f32 = jnp.float32
    x = x.astype(f32).reshape((b, nc, ln) + x.shape[2:])
    dt = dt.astype(f32).reshape((b, nc, ln) + dt.shape[2:])
    bm = bm.astype(f32).reshape((b, nc, ln) + bm.shape[2:])
    cm = cm.astype(f32).reshape((b, nc, ln) + cm.shape[2:])
    a_cs = jnp.cumsum(dt * a, axis=2)
    causal = jnp.tril(jnp.ones((ln, ln), dtype=bool))
    seg = a_cs[:, :, :, None] - a_cs[:, :, None, :]
    decay_ls = jnp.exp(jnp.where(causal[:, :, None, None], seg, -jnp.inf))
    cb = jnp.einsum('bclgn,bcsgn->bclsg', cm, bm)
    w_ls = cb[..., None] * decay_ls * dt[:, :, None]
    y_diag = jnp.einsum('bclsgh,bcsghp->bclghp', w_ls, x)
    decay_end = jnp.exp(a_cs[:, :, -1:] - a_cs)
    states = jnp.einsum('bclgn,bclgh,bclghp->bcghpn', bm, decay_end * dt, x)
    chunk_decay = jnp.exp(a_cs[:, :, -1])

    def step(h, inp):
        s_c, d_c = inp
        return d_c[..., None, None] * h + s_c, h

    h_final, h_prev = lax.scan(step, h0.astype(f32),
                               (jnp.moveaxis(states, 1, 0), jnp.moveaxis(chunk_decay, 1, 0)))
    h_prev = jnp.moveaxis(h_prev, 0, 1)
    y_off = jnp.einsum('bclgn,bcghpn->bclghp', cm, h_prev) * jnp.exp(a_cs)[..., None]
    y = (y_diag + y_off).reshape((b, t) + y_diag.shape[3:])
    return y, h_final


def latent_attention(q_lat, q_rope, c, kr, q_pos, k_pos):
    s = (jnp.einsum('bqhr,bkr->bhqk', q_lat, c)
         + jnp.einsum('bqhp,bkp->bhqk', q_rope, kr)).astype(jnp.float32) * ATTN_SCALE
    s = jnp.where(k_pos[None, :] <= q_pos[:, None], s, -jnp.inf)
    pr = jax.nn.softmax(s, axis=-1).astype(c.dtype)
    return jnp.einsum('bhqk,bkr->bqhr', pr, c)


def blocked_attention(q_lat, q_rope, c, kr, q_pos, k_pos):
    b, t, h, r = q_lat.shape
    qb = Q_BLOCK if t % Q_BLOCK == 0 else t
    nb = t // qb
    ql = jnp.moveaxis(q_lat.reshape(b, nb, qb, h, r), 1, 0)
    qr = jnp.moveaxis(q_rope.reshape(b, nb, qb, h, ROPE_DIM), 1, 0)
    qp = q_pos.reshape(nb, qb)
    o = lax.map(lambda arg: latent_attention(arg[0], arg[1], c, kr, arg[2], k_pos), (ql, qr, qp))
    return jnp.moveaxis(o, 0, 1).reshape(b, t, h, r)


def token_mixers(x, conv_state, ssm_state, c_past, kr_past, pos0, p):
    b, t, _ = x.shape
    proj = x @ p['w_in']
    q_a, kv_a, kr_raw, z, xbc, dt_raw, g_a, g_b = jnp.split(proj, IN_SPLITS, axis=-1)
    pos = pos0 + jnp.arange(t, dtype=jnp.int32)
    xbc, new_conv = causal_conv(xbc, conv_state, p['ssd_conv_w'], p['ssd_conv_b'])
    xs, bm, cm = jnp.split(xbc, [D_INNER, D_INNER + SSD_GROUPS * D_STATE], axis=-1)
    xs = xs.reshape(b, t, SSD_GROUPS, HEADS_PER_GROUP, SSD_HEAD_DIM)
    bm = bm.reshape(b, t, SSD_GROUPS, D_STATE)
    cm = cm.reshape(b, t, SSD_GROUPS, D_STATE)
    dt = jax.nn.softplus(dt_raw.astype(jnp.float32) + p['ssd_dt_bias'].astype(jnp.float32))
    dt = dt.reshape(b, t, SSD_GROUPS, HEADS_PER_GROUP)
    a = -jnp.exp(p['ssd_a_log'].astype(jnp.float32)).reshape(SSD_GROUPS, HEADS_PER_GROUP)
    h0 = ssm_state.reshape(b, SSD_GROUPS, HEADS_PER_GROUP, SSD_HEAD_DIM, D_STATE)
    y, h_final = ssd_scan(xs, dt, a, bm, cm, h0)
    y = y + p['ssd_d'].astype(jnp.float32).reshape(SSD_GROUPS, HEADS_PER_GROUP)[:, :, None] * xs.astype(jnp.float32)
    y = y.reshape(b, t, D_INNER) * jax.nn.silu(z.astype(jnp.float32))
    y = rms_norm(y.reshape(b, t, SSD_GROUPS, D_INNER // SSD_GROUPS),
                 p['ssd_norm_g'].reshape(SSD_GROUPS, D_INNER // SSD_GROUPS))
    y_a = y.reshape(b, t, D_INNER).astype(x.dtype) @ p['w_branch_a']
    q = (rms_norm(q_a, p['mla_q_norm_g']) @ p['w_q_b']).reshape(b, t, MLA_HEADS, QK_NOPE + ROPE_DIM)
    q_nope = q[..., :QK_NOPE]
    q_rope = rope(q[..., QK_NOPE:], pos)
    c_new = rms_norm(kv_a, p['mla_kv_norm_g'])
    kr_new = rope(kr_raw, pos)
    w_kv = p['w_kv_b'].reshape(KV_LORA, MLA_HEADS, QK_NOPE + V_HEAD)
    q_lat = jnp.einsum('bthn,rhn->bthr', q_nope, w_kv[..., :QK_NOPE])
    if c_past is None:
        c_all, kr_all, k_pos = c_new, kr_new, pos
    else:
        c_all = jnp.concatenate([c_past.astype(c_new.dtype), c_new], axis=1)
        kr_all = jnp.concatenate([kr_past.astype(kr_new.dtype), kr_new], axis=1)
        k_pos = jnp.arange(c_all.shape[1], dtype=jnp.int32)
    o_lat = blocked_attention(q_lat, q_rope, c_all, kr_all, pos, k_pos)
    o = jnp.einsum('bthr,rhv->bthv', o_lat, w_kv[..., QK_NOPE:]).reshape(b, t, MLA_HEADS * V_HEAD)
    y_b = o @ p['w_branch_b']
    merged = jax.nn.sigmoid(g_a) * y_a + jax.nn.sigmoid(g_b) * y_b
    out = merged @ p['w_out']
    new_ssm = h_final.reshape(b, SSD_HEADS, SSD_HEAD_DIM, D_STATE).astype(x.dtype)
    return out, c_new, kr_new, new_ssm, new_conv


def moe(x, p):
    n, d = x.shape
    logits = (x @ p['router_w'] + p['router_b']).astype(jnp.float32)
    top_v, top_i = lax.top_k(logits, TOP_K)
    gate = jax.nn.softmax(top_v, axis=-1)
    nk = n * TOP_K
    flat_e = top_i.reshape(-1)
    flat_t = jnp.repeat(jnp.arange(n, dtype=jnp.int32), TOP_K)
    flat_w = gate.reshape(-1)
    order = jnp.argsort(flat_e)
    se = flat_e[order]
    counts = jnp.bincount(flat_e, length=N_EXPERTS)
    padded = (counts + MOE_BLOCK - 1) // MOE_BLOCK * MOE_BLOCK
    start = jnp.cumsum(counts) - counts
    pend = jnp.cumsum(padded)
    pstart = pend - padded
    dest = pstart[se] + jnp.arange(nk, dtype=jnp.int32) - start[se]
    n_blocks = -(-(nk + N_EXPERTS * (MOE_BLOCK - 1)) // MOE_BLOCK)
    cap = n_blocks * MOE_BLOCK
    row_tok = jnp.full((cap,), n, dtype=jnp.int32).at[dest].set(flat_t[order])
    row_w = jnp.zeros((cap,), jnp.float32).at[dest].set(flat_w[order])
    blk_e = jnp.minimum(jnp.searchsorted(pend, jnp.arange(n_blocks, dtype=jnp.int32) * MOE_BLOCK, side='right'),
                        N_EXPERTS - 1)
    xpad = jnp.concatenate([x, jnp.zeros((1, d), x.dtype)], axis=0)
    xr = xpad[row_tok].reshape(n_blocks, MOE_BLOCK, d)
    w_gu, b_gu, w_dn, b_dn = p['exp_w_gu'], p['exp_b_gu'], p['exp_w_down'], p['exp_b_down']

    def expert_block(arg):
        xb, e = arg
        gu = xb @ w_gu[e] + b_gu[e]
        g = jnp.minimum(gu[:, ::2], SWIGLU_LIMIT)
        u = jnp.clip(gu[:, 1::2], -SWIGLU_LIMIT, SWIGLU_LIMIT)
        h = (u + 1.0) * (g * jax.nn.sigmoid(SWIGLU_ALPHA * g))
        return h @ w_dn[e] + b_dn[e]

    out = lax.map(expert_block, (xr, blk_e)).reshape(cap, d)
    y = jnp.zeros_like(xpad).at[row_tok].add(out * row_w[:, None].astype(out.dtype))
    return y[:n]


def setup_inputs(seed: int = 0) -> dict:
    key = jax.random.key(seed)
    ks = iter(jax.random.split(key, 40))

    def nrm(shape, scale):
        return jax.random.normal(next(ks), shape, jnp.float32) * scale

    n_pages = PAST_LEN // PAGE_SIZE
    n_used = DEC_BATCH * n_pages
    n_phys = n_used + n_used // 4
    L = DEPTH
    x_prompt = nrm((BATCH, SEQ, D_MODEL), 1.0)
    x_sample = nrm((DEC_BATCH, DEC_SEQ, D_MODEL), 1.0)
    cache_kv_latent = nrm((L, n_phys, PAGE_SIZE, KV_LORA), 1.0)
    cache_k_rope = nrm((L, n_phys, PAGE_SIZE, ROPE_DIM), 1.0)
    state_ssm = nrm((L, DEC_BATCH, SSD_HEADS, SSD_HEAD_DIM, D_STATE), 0.5)
    state_conv = nrm((L, DEC_BATCH, D_CONV - 1, CONV_DIM), 1.0)
    page_table = jax.random.permutation(next(ks), n_phys)[:n_used].reshape(DEC_BATCH, n_pages).astype(jnp.int32)
    w_in = nrm((L, D_MODEL, D_IN_TOTAL), D_MODEL ** -0.5)
    ssd_conv_w = nrm((L, D_CONV, CONV_DIM), D_CONV ** -0.5)
    ssd_conv_b = nrm((L, CONV_DIM), 0.01)
    dt0 = jnp.exp(jax.random.uniform(next(ks), (L, SSD_HEADS), jnp.float32,
                                     minval=math.log(1e-3), maxval=math.log(1e-1)))
    ssd_dt_bias = dt0 + jnp.log(-jnp.expm1(-dt0))
    ssd_a_log = jnp.log(jax.random.uniform(next(ks), (L, SSD_HEADS), jnp.float32, minval=1.0, maxval=16.0))
    ssd_d = 1.0 + nrm((L, SSD_HEADS), 0.1)
    ssd_norm_g = 1.0 + nrm((L, D_INNER), 0.02)
    mla_q_norm_g = 1.0 + nrm((L, Q_LORA), 0.02)
    w_q_b = nrm((L, Q_LORA, MLA_HEADS * (QK_NOPE + ROPE_DIM)), Q_LORA ** -0.5)
    mla_kv_norm_g = 1.0 + nrm((L, KV_LORA), 0.02)
    w_kv_b = nrm((L, KV_LORA, MLA_HEADS * (QK_NOPE + V_HEAD)), KV_LORA ** -0.5)
    w_branch_a = nrm((L, D_INNER, D_MODEL), D_INNER ** -0.5 * DN_BETA)
    w_branch_b = nrm((L, MLA_HEADS * V_HEAD, D_MODEL), (MLA_HEADS * V_HEAD) ** -0.5 * DN_BETA)
    w_out = nrm((L, D_MODEL, D_MODEL), D_MODEL ** -0.5 * DN_BETA)
    ln1_g = 1.0 + nrm((L, D_MODEL), 0.02)
    ln1_b = nrm((L, D_MODEL), 0.02)
    router_w = nrm((L, D_MODEL, N_EXPERTS), D_MODEL ** -0.5)
    router_b = nrm((L, N_EXPERTS), 0.01)
    exp_w_gu = nrm((L, N_EXPERTS, D_MODEL, 2 * D_FF), D_MODEL ** -0.5)
    exp_b_gu = nrm((L, N_EXPERTS, 2 * D_FF), 0.01)
    exp_w_down = nrm((L, N_EXPERTS, D_FF, D_MODEL), D_FF ** -0.5 * DN_BETA)
    exp_b_down = nrm((L, N_EXPERTS, D_MODEL), 0.01)
    ln2_g = 1.0 + nrm((L, D_MODEL), 0.02)
    ln2_b = nrm((L, D_MODEL), 0.02)
    return {'x_prompt': x_prompt, 'x_sample': x_sample,
            'cache_kv_latent': cache_kv_latent, 'cache_k_rope': cache_k_rope,
            'state_ssm': state_ssm, 'state_conv': state_conv, 'page_table': page_table,
            'w_in': w_in, 'ssd_conv_w': ssd_conv_w, 'ssd_conv_b': ssd_conv_b,
            'ssd_dt_bias': ssd_dt_bias, 'ssd_a_log': ssd_a_log, 'ssd_d': ssd_d, 'ssd_norm_g': ssd_norm_g,
            'mla_q_norm_g': mla_q_norm_g, 'w_q_b': w_q_b, 'mla_kv_norm_g': mla_kv_norm_g, 'w_kv_b': w_kv_b,
            'w_branch_a': w_branch_a, 'w_branch_b': w_branch_b, 'w_out': w_out,
            'ln1_g': ln1_g, 'ln1_b': ln1_b, 'router_w': router_w, 'router_b': router_b,
            'exp_w_gu': exp_w_gu, 'exp_b_gu': exp_b_gu, 'exp_w_down': exp_w_down, 'exp_b_down': exp_b_down,
            'ln2_g': ln2_g, 'ln2_b': ln2_b}


def reference(x_prompt, x_sample, cache_kv_latent, cache_k_rope, state_ssm, state_conv, page_table,
              w_in, ssd_conv_w, ssd_conv_b, ssd_dt_bias, ssd_a_log, ssd_d, ssd_norm_g,
              mla_q_norm_g, w_q_b, mla_kv_norm_g, w_kv_b, w_branch_a, w_branch_b, w_out,
              ln1_g, ln1_b, router_w, router_b, exp_w_gu, exp_b_gu, exp_w_down, exp_b_down,
              ln2_g, ln2_b):
    b_p, t_p, _ = x_prompt.shape
    b_s, t_s, _ = x_sample.shape
    n_pages = page_table.shape[1]
    past_len = n_pages * cache_kv_latent.shape[2]
    hp, hs = x_prompt, x_sample
    c_p_l, kr_p_l, ssm_p_l, conv_p_l = [], [], [], []
    c_s_l, kr_s_l, ssm_s_l, conv_s_l = [], [], [], []
    for l in range(DEPTH):
        p = {'w_in': w_in[l], 'ssd_conv_w': ssd_conv_w[l], 'ssd_conv_b': ssd_conv_b[l],
             'ssd_dt_bias': ssd_dt_bias[l], 'ssd_a_log': ssd_a_log[l], 'ssd_d': ssd_d[l],
             'ssd_norm_g': ssd_norm_g[l], 'mla_q_norm_g': mla_q_norm_g[l], 'w_q_b': w_q_b[l],
             'mla_kv_norm_g': mla_kv_norm_g[l], 'w_kv_b': w_kv_b[l], 'w_branch_a': w_branch_a[l],
             'w_branch_b': w_branch_b[l], 'w_out': w_out[l], 'router_w': router_w[l],
             'router_b': router_b[l], 'exp_w_gu': exp_w_gu[l], 'exp_b_gu': exp_b_gu[l],
             'exp_w_down': exp_w_down[l], 'exp_b_down': exp_b_down[l]}
        conv0 = jnp.zeros((b_p, D_CONV - 1, CONV_DIM), hp.dtype)
        ssm0 = jnp.zeros((b_p, SSD_HEADS, SSD_HEAD_DIM, D_STATE), hp.dtype)
        mp, c_p, kr_p, ssm_p, conv_p = token_mixers(hp, conv0, ssm0, None, None, 0, p)
        c_past = cache_kv_latent[l][page_table].reshape(b_s, past_len, KV_LORA)
        kr_past = cache_k_rope[l][page_table].reshape(b_s, past_len, ROPE_DIM)
        ms, c_s, kr_s, ssm_s, conv_s = token_mixers(hs, state_conv[l], state_ssm[l], c_past, kr_past, past_len, p)
        hp = layer_norm(DN_ALPHA * hp + mp, ln1_g[l], ln1_b[l])
        hs = layer_norm(DN_ALPHA * hs + ms, ln1_g[l], ln1_b[l])
        flat = jnp.concatenate([hp.reshape(-1, D_MODEL), hs.reshape(-1, D_MODEL)], axis=0)
        flat = layer_norm(DN_ALPHA * flat + moe(flat, p), ln2_g[l], ln2_b[l])
        hp = flat[:b_p * t_p].reshape(b_p, t_p, D_MODEL)
        hs = flat[b_p * t_p:].reshape(b_s, t_s, D_MODEL)
        c_p_l.append(c_p); kr_p_l.append(kr_p); ssm_p_l.append(ssm_p); conv_p_l.append(conv_p)
        c_s_l.append(c_s); kr_s_l.append(kr_s); ssm_s_l.append(ssm_s); conv_s_l.append(conv_s)
    kv_latent_prompt = jnp.stack(c_p_l)
    k_rope_prompt = jnp.stack(kr_p_l)
    ssm_prompt = jnp.stack(ssm_p_l)
    conv_prompt = jnp.stack(conv_p_l)
    kv_latent_sample = jnp.stack(c_s_l)
    k_rope_sample = jnp.stack(kr_s_l)
    ssm_sample = jnp.stack(ssm_s_l)
    conv_sample = jnp.stack(conv_s_l)
    return (hp, hs, kv_latent_prompt, k_rope_prompt, ssm_prompt, conv_prompt,
            kv_latent_sample, k_rope_sample, ssm_sample, conv_sample)
```

```python
import functools

import jax
import jax.numpy as jnp
import numpy as np
from jax import lax
from jax.experimental import pallas as pl
from jax.experimental.pallas import tpu as pltpu

F32 = jnp.float32
BF16 = jnp.bfloat16

D_MODEL = 1024
D_INNER = 2048
SSD_HEAD_DIM = 64
SSD_HEADS = 32
SSD_GROUPS = 4
HEADS_PER_GROUP = 8
D_STATE = 128
D_CONV = 4
CONV_DIM = D_INNER + 2 * SSD_GROUPS * D_STATE
SSD_CHUNK = 128
MLA_HEADS = 8
Q_LORA = 256
KV_LORA = 256
QK_NOPE = 128
ROPE_DIM = 64
V_HEAD = 128
ROPE_THETA = 10000.0
ATTN_SCALE = (QK_NOPE + ROPE_DIM) ** -0.5
N_EXPERTS = 32
TOP_K = 4
D_FF = 1024
SWIGLU_ALPHA = 1.702
SWIGLU_LIMIT = 7.0
DEPTH = 1
DN_ALPHA = (2 * DEPTH) ** 0.25
LN_EPS = 1e-5
RMS_EPS = 1e-6
IN_SIZES = (Q_LORA, KV_LORA, ROPE_DIM, D_INNER, CONV_DIM, SSD_HEADS, D_MODEL, D_MODEL)
IN_OFFS = tuple(int(v) for v in np.cumsum((0,) + IN_SIZES))

LANES = 128
SUBLANES = 8
VMEM_LIMIT = 56 * 1024 * 1024

NEG = -1e30

ROW_TILE = 256
PROJ_TILE = 640
ATTN_TQ = 256
PAGES_PER_STEP = 8
MOE_TB = 256
GATHER_ROWS = 512
MLA_A_COLS = 896


def _cparams(n_axes):
    return pltpu.CompilerParams(dimension_semantics=("arbitrary",) * n_axes,
                                vmem_limit_bytes=VMEM_LIMIT)


def _dot(a, b):
    return jnp.dot(a, b, preferred_element_type=F32)


def _dot_nt(a, b):
    return lax.dot_general(a, b, (((1,), (1,)), ((), ())), preferred_element_type=F32)


def _split3(x):
    hi = x.astype(BF16)
    r1 = x - hi.astype(F32)
    mid = r1.astype(BF16)
    lo = (r1 - mid.astype(F32)).astype(BF16)
    return hi, mid, lo


def _sigmoid(x):
    return 1.0 / (1.0 + jnp.exp(-x))


def _layer_norm(x, g, b):
    mu = jnp.mean(x, axis=-1, keepdims=True)
    xc = x - mu
    var = jnp.mean(xc * xc, axis=-1, keepdims=True)
    return xc * lax.rsqrt(var + LN_EPS) * g + b


def _rms_norm(x, g):
    return x * lax.rsqrt(jnp.mean(x * x, axis=-1, keepdims=True) + RMS_EPS) * g


def _proj_kernel(x_ref, w_ref, o_ref):
    o_ref[...] = _dot(x_ref[...].astype(BF16), w_ref[...])


def _proj(x, w, tm):
    m, k = x.shape
    n = w.shape[1]
    return pl.pallas_call(
        _proj_kernel,
        out_shape=jax.ShapeDtypeStruct((m, n), F32),
        grid=(m // tm,),
        in_specs=[pl.BlockSpec((tm, k), lambda i: (i, 0)),
                  pl.BlockSpec((k, n), lambda i: (0, 0))],
        out_specs=pl.BlockSpec((tm, n), lambda i: (i, 0)),
        compiler_params=_cparams(1),
        name="proj",
    )(x, w)


def _mla_in_kernel(x_ref, wa_ref, qg_ref, kvg_ref, wq_ref, wuk_ref, cos_ref, sin_ref,
                   qlat_ref, qrope_ref, c_ref, cbf_ref, kr_ref, krbf_ref, dt_ref):
    xb = x_ref[...].astype(BF16)
    pa = _dot(xb, wa_ref[...])
    q_a = pa[:, 0:256]
    kv_a = pa[:, 256:512]
    kr_raw = pa[:, 512:640]
    kr_rot = pa[:, 640:768]
    dt_ref[...] = pa[:, 768:896]
    cos = cos_ref[...]
    sin = sin_ref[...]
    c = _rms_norm(kv_a, kvg_ref[...])
    c_ref[...] = c
    cbf_ref[...] = c.astype(BF16)
    kr = kr_raw * cos + kr_rot * sin
    kr_ref[...] = kr[:, 0:ROPE_DIM]
    krbf_ref[...] = kr[:, 0:ROPE_DIM].astype(BF16)
    qn = _rms_norm(q_a, qg_ref[...]).astype(BF16)
    qq = _dot(qn, wq_ref[...])
    cos4 = jnp.concatenate([cos] * 4, axis=1)
    sin4 = jnp.concatenate([sin] * 4, axis=1)
    qr = (qq[:, 1024:1536] * cos4 + qq[:, 1536:2048] * sin4) * ATTN_SCALE
    for h in range(MLA_HEADS):
        qrope_ref[h] = qr[:, h * ROPE_DIM:(h + 1) * ROPE_DIM].astype(BF16)
        nope = qq[:, h * QK_NOPE:(h + 1) * QK_NOPE].astype(BF16)
        qlat_ref[h] = (_dot(nope, wuk_ref[h]) * ATTN_SCALE).astype(BF16)


def _mla_in(x, wa, qg, kvg, wq, wuk, cos_t, sin_t, tm):
    m = x.shape[0]
    row = lambda w: pl.BlockSpec((tm, w), lambda i: (i, 0))
    full2 = lambda a: pl.BlockSpec(a.shape, lambda i: (0, 0))
    full3 = lambda a: pl.BlockSpec(a.shape, lambda i: (0, 0, 0))
    return pl.pallas_call(
        _mla_in_kernel,
        out_shape=(jax.ShapeDtypeStruct((MLA_HEADS, m, KV_LORA), BF16),
                   jax.ShapeDtypeStruct((MLA_HEADS, m, ROPE_DIM), BF16),
                   jax.ShapeDtypeStruct((m, KV_LORA), F32),
                   jax.ShapeDtypeStruct((m, KV_LORA), BF16),
                   jax.ShapeDtypeStruct((m, ROPE_DIM), F32),
                   jax.ShapeDtypeStruct((m, ROPE_DIM), BF16),
                   jax.ShapeDtypeStruct((m, LANES), F32)),
        grid=(m // tm,),
        in_specs=[row(D_MODEL), full2(wa), full2(qg), full2(kvg), full2(wq), full3(wuk),
                  row(LANES), row(LANES)],
        out_specs=(pl.BlockSpec((MLA_HEADS, tm, KV_LORA), lambda i: (0, i, 0)),
                   pl.BlockSpec((MLA_HEADS, tm, ROPE_DIM), lambda i: (0, i, 0)),
                   row(KV_LORA), row(KV_LORA), row(ROPE_DIM), row(ROPE_DIM), row(LANES)),
        compiler_params=_cparams(1),
        name="mla_in",
    )(x, wa, qg, kvg, wq, wuk, cos_t, sin_t)


def _ssd_kernel(*refs, t_in, has_h0):
    if has_h0:
        (xbc_ref, z_ref, dt_ref, conv0_ref, h0_ref, cw_ref, cb_ref, dtb_ref, a_ref, d_ref, ng_ref,
         y_ref, hfin_ref, ext, xcv, ybuf, hT, acs_s, acsT_s, dtT_s, dtv_s, ea_s) = refs
    else:
        (xbc_ref, z_ref, dt_ref, conv0_ref, cw_ref, cb_ref, dtb_ref, a_ref, d_ref, ng_ref,
         y_ref, hfin_ref, ext, xcv, ybuf, hT, acs_s, acsT_s, dtT_s, dtv_s, ea_s) = refs
        h0_ref = None
    L = SSD_CHUNK
    c = pl.program_id(1)
    nc = pl.num_programs(1)

    @pl.when(c == 0)
    def _():
        ext[0:SUBLANES, :] = conv0_ref[0]
        if t_in < L:
            ext[SUBLANES:SUBLANES + L, :] = jnp.zeros((L, CONV_DIM), F32)
        if has_h0:
            for j in range(D_INNER // LANES):
                hT[:, j * LANES:(j + 1) * LANES] = h0_ref[0, j * LANES:(j + 1) * LANES, :].T
        else:
            hT[...] = jnp.zeros_like(hT)

    ext[SUBLANES:SUBLANES + t_in, :] = xbc_ref[...]

    for j in range(CONV_DIM // LANES):
        sl = slice(j * LANES, (j + 1) * LANES)
        acc = cb_ref[:, sl] + ext[pl.ds(SUBLANES - 3, L), sl] * cw_ref[0:1, sl]
        acc = acc + ext[pl.ds(SUBLANES - 2, L), sl] * cw_ref[1:2, sl]
        acc = acc + ext[pl.ds(SUBLANES - 1, L), sl] * cw_ref[2:3, sl]
        acc = acc + ext[pl.ds(SUBLANES, L), sl] * cw_ref[3:4, sl]
        xcv[:, sl] = acc * _sigmoid(acc)
    ext[0:SUBLANES, :] = ext[L:L + SUBLANES, :]

    row = lax.broadcasted_iota(jnp.int32, (L, L), 0)
    col = lax.broadcasted_iota(jnp.int32, (L, L), 1)
    tri = row >= col
    tri_b = jnp.where(tri, 1.0, 0.0).astype(BF16)
    if t_in < L:
        dt_raw = jnp.concatenate([dt_ref[...], jnp.zeros((L - t_in, LANES), F32)], axis=0)
    else:
        dt_raw = dt_ref[...]
    xs = dt_raw + dtb_ref[...]
    dtv = jnp.maximum(xs, 0.0) + jnp.log1p(jnp.exp(-jnp.abs(xs)))
    if t_in < L:
        dtv = jnp.where(lax.broadcasted_iota(jnp.int32, (L, LANES), 0) < t_in, dtv, 0.0)
    da = dtv * a_ref[...]
    hi, mid, lo = _split3(da)
    acs = _dot(tri_b, hi) + _dot(tri_b, mid) + _dot(tri_b, lo)
    acs_s[...] = acs
    acsT_s[...] = acs.T
    dtv_s[...] = dtv
    dtT_s[...] = dtv.T
    ea_s[...] = jnp.exp(acs)

    for g in range(SSD_GROUPS):
        bo = D_INNER + g * D_STATE
        co = D_INNER + SSD_GROUPS * D_STATE + g * D_STATE
        bm = xcv[:, bo:bo + D_STATE]
        cmb = xcv[:, co:co + D_STATE].astype(BF16)
        cb = _dot_nt(cmb, bm.astype(BF16))
        bmT = bm.T.astype(BF16)
        for hh in range(HEADS_PER_GROUP):
            idx = g * HEADS_PER_GROUP + hh
            sl = slice(idx * SSD_HEAD_DIM, (idx + 1) * SSD_HEAD_DIM)
            xh = xcv[:, sl]
            a_col = acs_s[:, idx:idx + 1]
            a_row = acsT_s[idx:idx + 1, :]
            dt_row = dtT_s[idx:idx + 1, :]
            dt_col = dtv_s[:, idx:idx + 1]
            w = cb * jnp.exp(jnp.where(tri, a_col - a_row, NEG)) * dt_row
            yd = _dot(w.astype(BF16), xh.astype(BF16))
            hp = hT[:, sl]
            yo = _dot(cmb, hp.astype(BF16))
            ybuf[:, sl] = yd + yo * ea_s[:, idx:idx + 1] + d_ref[:, idx:idx + 1] * xh
            a_last = acs_s[L - 1:L, idx:idx + 1]
            wcol = jnp.exp(a_last - a_col) * dt_col
            s = _dot(bmT, (xh * wcol).astype(BF16))
            hT[:, sl] = jnp.exp(a_last) * hp + s

    gw = D_INNER // SSD_GROUPS
    for g in range(SSD_GROUPS):
        sl = slice(g * gw, (g + 1) * gw)
        zg = z_ref[:, sl]
        v = ybuf[0:t_in, sl] * (zg * _sigmoid(zg))
        ms = jnp.mean(v * v, axis=-1, keepdims=True)
        y_ref[:, sl] = (v * lax.rsqrt(ms + RMS_EPS) * ng_ref[:, sl]).astype(y_ref.dtype)

    @pl.when(c == nc - 1)
    def _():
        for j in range(D_INNER // LANES):
            hfin_ref[0, j * LANES:(j + 1) * LANES, :] = hT[:, j * LANES:(j + 1) * LANES].T


def _ssd(xbc, z, dt, conv0, h0, cw, cb, dtb, a_neg, d_row, ng, *, nb, nc, t_in, row0, y_dtype):
    L = SSD_CHUNK
    blk0 = row0 // t_in
    has_h0 = h0 is not None
    rowspec = lambda w: pl.BlockSpec((t_in, w), lambda b, c: (blk0 + b * nc + c, 0))
    par = lambda a: pl.BlockSpec(a.shape, lambda b, c: (0, 0))
    in_specs = [rowspec(CONV_DIM), rowspec(D_INNER), rowspec(LANES),
                pl.BlockSpec((1, SUBLANES, CONV_DIM), lambda b, c: (b, 0, 0))]
    args = [xbc, z, dt, conv0]
    if has_h0:
        in_specs.append(pl.BlockSpec((1, D_INNER, D_STATE), lambda b, c: (b, 0, 0)))
        args.append(h0)
    params = [cw, cb, dtb, a_neg, d_row, ng]
    in_specs += [par(p) for p in params]
    args += params
    return pl.pallas_call(
        functools.partial(_ssd_kernel, t_in=t_in, has_h0=has_h0),
        out_shape=(jax.ShapeDtypeStruct((nb * nc * t_in, D_INNER), y_dtype),
                   jax.ShapeDtypeStruct((nb, D_INNER, D_STATE), F32)),
        grid=(nb, nc),
        in_specs=in_specs,
        out_specs=(pl.BlockSpec((t_in, D_INNER), lambda b, c: (b * nc + c, 0)),
                   pl.BlockSpec((1, D_INNER, D_STATE), lambda b, c: (b, 0, 0))),
        scratch_shapes=[pltpu.VMEM((L + 2 * SUBLANES, CONV_DIM), F32),
                        pltpu.VMEM((L, CONV_DIM), F32),
                        pltpu.VMEM((L, D_INNER), F32),
                        pltpu.VMEM((D_STATE, D_INNER), F32),
                        pltpu.VMEM((L, LANES), F32), pltpu.VMEM((LANES, L), F32),
                        pltpu.VMEM((LANES, L), F32), pltpu.VMEM((L, LANES), F32),
                        pltpu.VMEM((L, LANES), F32)],
        compiler_params=_cparams(2),
        name="ssd",
    )(*args)


def _softmax_step(s, kc, m_scr, l_scr, acc_scr):
    reps = s.shape[1] // LANES
    m_prev = m_scr[...]
    m_new = jnp.maximum(m_prev, jnp.max(s, axis=1, keepdims=True))
    alpha = jnp.exp(m_prev - m_new)
    p = jnp.exp(s - (jnp.concatenate([m_new] * reps, axis=1) if reps > 1 else m_new))
    l_scr[...] = alpha * l_scr[...] + jnp.sum(p, axis=1, keepdims=True)
    acc_scr[...] = acc_scr[...] * jnp.concatenate([alpha] * (KV_LORA // LANES), axis=1) + _dot(p.astype(BF16), kc)
    m_scr[...] = m_new


def _attn_prompt_kernel(qlat_ref, qrope_ref, c_ref, kr_ref, wuv_ref, o_ref, m_scr, l_scr, acc_scr, *, tq):
    qi = pl.program_id(1)
    R = MLA_HEADS * tq
    q = qlat_ref[...].reshape(R, KV_LORA)
    qr = qrope_ref[...].reshape(R, ROPE_DIM)
    m_scr[...] = jnp.full_like(m_scr, NEG)
    l_scr[...] = jnp.zeros_like(l_scr)
    acc_scr[...] = jnp.zeros_like(acc_scr)

    def scores(j):
        start = pl.multiple_of(j * tq, tq)
        kc = c_ref[pl.ds(start, tq), :]
        kr = kr_ref[pl.ds(start, tq), :]
        return _dot_nt(q, kc) + _dot_nt(qr, kr), kc

    def body(j, carry):
        s, kc = scores(j)
        _softmax_step(s, kc, m_scr, l_scr, acc_scr)
        return carry

    lax.fori_loop(0, qi, body, 0)
    s, kc = scores(qi)
    t = lax.broadcasted_iota(jnp.int32, (R, tq), 0) & (tq - 1)
    kcol = lax.broadcasted_iota(jnp.int32, (R, tq), 1)
    s = jnp.where(kcol <= t, s, NEG)
    _softmax_step(s, kc, m_scr, l_scr, acc_scr)

    inv = 1.0 / l_scr[...]
    o_lat = acc_scr[...] * jnp.concatenate([inv] * (KV_LORA // LANES), axis=1)
    for h in range(MLA_HEADS):
        oh = _dot(o_lat[h * tq:(h + 1) * tq, :].astype(BF16), wuv_ref[h])
        o_ref[:, h * V_HEAD:(h + 1) * V_HEAD] = oh.astype(o_ref.dtype)


def _attn_prompt(qlat, qrope, cbf, krbf, wuv, *, nb, t, tq):
    nq = t // tq
    R = MLA_HEADS * tq
    return pl.pallas_call(
        functools.partial(_attn_prompt_kernel, tq=tq),
        out_shape=jax.ShapeDtypeStruct((nb * t, MLA_HEADS * V_HEAD), BF16),
        grid=(nb, nq),
        in_specs=[pl.BlockSpec((MLA_HEADS, tq, KV_LORA), lambda b, i: (0, b * nq + i, 0)),
                  pl.BlockSpec((MLA_HEADS, tq, ROPE_DIM), lambda b, i: (0, b * nq + i, 0)),
                  pl.BlockSpec((t, KV_LORA), lambda b, i: (b, 0)),
                  pl.BlockSpec((t, ROPE_DIM), lambda b, i: (b, 0)),
                  pl.BlockSpec(wuv.shape, lambda b, i: (0, 0, 0))],
        out_specs=pl.BlockSpec((tq, MLA_HEADS * V_HEAD), lambda b, i: (b * nq + i, 0)),
        scratch_shapes=[pltpu.VMEM((R, LANES), F32), pltpu.VMEM((R, LANES), F32),
                        pltpu.VMEM((R, KV_LORA), F32)],
        compiler_params=_cparams(2),
        name="attn_prompt",
    )(qlat, qrope, cbf, krbf, wuv)


def _attn_sample_kernel(pt_ref, qlat_ref, qrope_ref, cnew_ref, krnew_ref, *rest, n_pg, t_new):
    kv_refs = rest[:n_pg]
    kr_refs = rest[n_pg:2 * n_pg]
    o_ref, m_scr, l_scr, acc_scr = rest[2 * n_pg:]
    j = pl.program_id(1)
    R = qlat_ref.shape[1]
    q = qlat_ref[0]
    qr = qrope_ref[0]

    @pl.when(j == 0)
    def _():
        m_scr[...] = jnp.full_like(m_scr, NEG)
        l_scr[...] = jnp.zeros_like(l_scr)
        acc_scr[...] = jnp.zeros_like(acc_scr)
        kc = cnew_ref[0].astype(BF16)
        kr = krnew_ref[0].astype(BF16)
        s = _dot_nt(q, kc) + _dot_nt(qr, kr)
        t = lax.broadcasted_iota(jnp.int32, (R, LANES), 0) & (t_new - 1)
        kcol = lax.broadcasted_iota(jnp.int32, (R, LANES), 1)
        s = jnp.where(kcol <= t, s, NEG)
        _softmax_step(s, kc, m_scr, l_scr, acc_scr)

    kcs = [kv_refs[g][...].astype(BF16) for g in range(n_pg)]
    ss = [_dot_nt(q, kcs[g]) + _dot_nt(qr, kr_refs[g][...].astype(BF16)) for g in range(n_pg)]
    s = jnp.concatenate(ss, axis=1)
    kc = jnp.concatenate(kcs, axis=0)
    _softmax_step(s, kc, m_scr, l_scr, acc_scr)

    @pl.when(j == pl.num_programs(1) - 1)
    def _():
        inv = 1.0 / l_scr[...]
        o_ref[0] = acc_scr[...] * jnp.concatenate([inv] * (KV_LORA // LANES), axis=1)


def _attn_sample(page_table, qlat, qrope, cnew, krnew, cache_kv, cache_kr, *, t_new):
    nb, n_pages = page_table.shape
    n_pg = PAGES_PER_STEP
    R = qlat.shape[1]
    page = cache_kv.shape[1]

    def pg_spec(g, w):
        return pl.BlockSpec((None, page, w), lambda b, j, pt: (pt[b, j * n_pg + g], 0, 0))

    per_b = lambda r, w: pl.BlockSpec((1, r, w), lambda b, j, pt: (b, 0, 0))
    grid_spec = pltpu.PrefetchScalarGridSpec(
        num_scalar_prefetch=1,
        grid=(nb, n_pages // n_pg),
        in_specs=[per_b(R, KV_LORA), per_b(R, ROPE_DIM), per_b(LANES, KV_LORA), per_b(LANES, ROPE_DIM)]
                 + [pg_spec(g, KV_LORA) for g in range(n_pg)]
                 + [pg_spec(g, ROPE_DIM) for g in range(n_pg)],
        out_specs=per_b(R, KV_LORA),
        scratch_shapes=[pltpu.VMEM((R, LANES), F32), pltpu.VMEM((R, LANES), F32),
                        pltpu.VMEM((R, KV_LORA), F32)],
    )
    return pl.pallas_call(
        functools.partial(_attn_sample_kernel, n_pg=n_pg, t_new=t_new),
        out_shape=jax.ShapeDtypeStruct((nb, R, KV_LORA), F32),
        grid_spec=grid_spec,
        compiler_params=_cparams(2),
        name="attn_sample",
    )(page_table, qlat, qrope, cnew, krnew, *([cache_kv] * n_pg), *([cache_kr] * n_pg))


def _head_proj_kernel(x_ref, w_ref, o_ref):
    o_ref[...] = _dot(x_ref[0].astype(BF16), w_ref[0]).astype(o_ref.dtype)


def _head_proj(x, w):
    h, m, k = x.shape
    n = w.shape[2]
    return pl.pallas_call(
        _head_proj_kernel,
        out_shape=jax.ShapeDtypeStruct((m, h * n), BF16),
        grid=(h,),
        in_specs=[pl.BlockSpec((1, m, k), lambda i: (i, 0, 0)),
                  pl.BlockSpec((1, k, n), lambda i: (i, 0, 0))],
        out_specs=pl.BlockSpec((m, n), lambda i: (0, i)),
        compiler_params=_cparams(1),
        name="head_proj",
    )(x, w)


def _merge_kernel(ya_p_ref, ya_s_ref, o_p_ref, o_s_ref, g_ref, u_ref, wa_ref, wb_ref, wo_ref,
                  lg_ref, lb_ref, rwh_ref, rwl_ref, rb_ref,
                  h_ref, ti_ref, tg_ref, *, n_prompt_tiles):
    i = pl.program_id(0)
    is_p = i < n_prompt_tiles
    yn = jnp.where(is_p, ya_p_ref[...], ya_s_ref[...])
    ob = jnp.where(is_p, o_p_ref[...], o_s_ref[...])
    ya = _dot(yn, wa_ref[...])
    yb = _dot(ob, wb_ref[...])
    g = g_ref[...]
    merged = _sigmoid(g[:, 0:D_MODEL]) * ya + _sigmoid(g[:, D_MODEL:2 * D_MODEL]) * yb
    mix = _dot(merged.astype(BF16), wo_ref[...])
    h = _layer_norm(DN_ALPHA * u_ref[...] + mix, lg_ref[...], lb_ref[...])
    h_ref[...] = h
    hh = h.astype(BF16)
    hl = (h - hh.astype(F32)).astype(BF16)
    logits = _dot(hh, rwh_ref[...]) + _dot(hl, rwh_ref[...]) + _dot(hh, rwl_ref[...]) + rb_ref[...]
    lane = lax.broadcasted_iota(jnp.int32, logits.shape, 1).astype(F32)
    vals = jnp.zeros_like(logits)
    idxs = jnp.zeros_like(logits)
    cur = logits
    for k in range(TOP_K):
        mx = jnp.max(cur, axis=1, keepdims=True)
        ix = jnp.min(jnp.where(cur == mx, lane, float(LANES)), axis=1, keepdims=True)
        vals = jnp.where(lane == k, mx, vals)
        idxs = jnp.where(lane == k, ix, idxs)
        cur = jnp.where(lane == ix, NEG * 2.0, cur)
    v0 = vals[:, 0:1]
    e = jnp.where(lane < TOP_K, jnp.exp(vals - v0), 0.0)
    tg_ref[...] = e / jnp.sum(e, axis=1, keepdims=True)
    ti_ref[...] = idxs.astype(jnp.int32)


def _merge(ya_p, ya_s, o_p, o_s, gates, u, wa, wb, wo, lg, lb, rwh, rwl, rb, tm):
    m = u.shape[0]
    npt = ya_p.shape[0] // tm
    row = lambda w: pl.BlockSpec((tm, w), lambda i: (i, 0))
    prow = lambda w: pl.BlockSpec((tm, w), lambda i: (jnp.minimum(i, npt - 1), 0))
    srow = lambda w: pl.BlockSpec((tm, w), lambda i: (jnp.maximum(i - npt, 0), 0))
    par = lambda a: pl.BlockSpec(a.shape, lambda i: (0, 0))
    return pl.pallas_call(
        functools.partial(_merge_kernel, n_prompt_tiles=npt),
        out_shape=(jax.ShapeDtypeStruct((m, D_MODEL), F32),
                   jax.ShapeDtypeStruct((m, LANES), jnp.int32),
                   jax.ShapeDtypeStruct((m, LANES), F32)),
        grid=(m // tm,),
        in_specs=[prow(D_INNER), srow(D_INNER), prow(D_MODEL), srow(D_MODEL), row(2 * D_MODEL), row(D_MODEL),
                  par(wa), par(wb), par(wo), par(lg), par(lb), par(rwh), par(rwl), par(rb)],
        out_specs=(row(D_MODEL), row(LANES), row(LANES)),
        compiler_params=_cparams(1),
        name="merge",
    )(ya_p, ya_s, o_p, o_s, gates, u, wa, wb, wo, lg, lb, rwh, rwl, rb)


def _gather_kernel(idx_hbm, src_hbm, dst_hbm, idx_smem, idx_sem, row_sem, *, rows):
    i = pl.program_id(0)
    n = pl.num_programs(0)
    slot = lax.rem(i, 2)

    def idx_copy(chunk, s):
        return pltpu.make_async_copy(idx_hbm.at[chunk], idx_smem.at[s], idx_sem.at[s])

    def rows_done(chunk, s):
        base = pl.multiple_of(chunk * rows, rows)
        return pltpu.make_async_copy(src_hbm.at[pl.ds(0, rows)], dst_hbm.at[pl.ds(base, rows)], row_sem.at[s])

    @pl.when(i == 0)
    def _():
        idx_copy(0, 0).start()

    idx_copy(i, slot).wait()

    @pl.when(i + 1 < n)
    def _():
        idx_copy(i + 1, 1 - slot).start()

    base = i * rows

    def issue(r, carry):
        tok = idx_smem[slot, r]
        pltpu.make_async_copy(src_hbm.at[pl.ds(tok, 1)], dst_hbm.at[pl.ds(base + r, 1)], row_sem.at[slot]).start()
        return carry

    lax.fori_loop(0, rows, issue, 0)

    @pl.when(i > 0)
    def _():
        rows_done(i - 1, 1 - slot).wait()

    @pl.when(i == n - 1)
    def _():
        rows_done(i, slot).wait()


def _gather_rows(src, idx, rows):
    n = idx.shape[0]
    d = src.shape[1]
    idx2 = idx.reshape(n // rows, rows)
    return pl.pallas_call(
        functools.partial(_gather_kernel, rows=rows),
        out_shape=jax.ShapeDtypeStruct((n, d), src.dtype),
        grid=(n // rows,),
        in_specs=[pl.BlockSpec(memory_space=pl.ANY), pl.BlockSpec(memory_space=pl.ANY)],
        out_specs=pl.BlockSpec(memory_space=pl.ANY),
        scratch_shapes=[pltpu.SMEM((2, rows), jnp.int32),
                        pltpu.SemaphoreType.DMA((2,)), pltpu.SemaphoreType.DMA((2,))],
        compiler_params=pltpu.CompilerParams(dimension_semantics=("arbitrary",), has_side_effects=True),
        name="gather_rows",
    )(idx2, src)


def _ffn_kernel(be_ref, nu_ref, x_ref, wgu_ref, bgu_ref, wdn_ref, bdn_ref, o_ref):
    i = pl.program_id(0)

    @pl.when(i < nu_ref[0])
    def _():
        x = x_ref[...].astype(BF16)
        gu = _dot(x, wgu_ref[...]) + bgu_ref[...]
        g = jnp.minimum(gu[:, 0:D_FF], SWIGLU_LIMIT)
        u = jnp.clip(gu[:, D_FF:2 * D_FF], -SWIGLU_LIMIT, SWIGLU_LIMIT)
        hh = (u + 1.0) * (g * _sigmoid(SWIGLU_ALPHA * g))
        o_ref[...] = _dot(hh.astype(BF16), wdn_ref[...]) + bdn_ref[...]

    @pl.when(i >= nu_ref[0])
    def _():
        o_ref[...] = jnp.zeros_like(o_ref)


def _ffn(blk_e, n_used, xr, wgu, bgu, wdn, bdn, tb):
    cap = xr.shape[0]
    nblk = cap // tb
    grid_spec = pltpu.PrefetchScalarGridSpec(
        num_scalar_prefetch=2,
        grid=(nblk,),
        in_specs=[pl.BlockSpec((tb, D_MODEL), lambda i, be, nu: (jnp.minimum(i, nu[0] - 1), 0)),
                  pl.BlockSpec((None, D_MODEL, 2 * D_FF), lambda i, be, nu: (be[i], 0, 0)),
                  pl.BlockSpec((None, 1, 2 * D_FF), lambda i, be, nu: (be[i], 0, 0)),
                  pl.BlockSpec((None, D_FF, D_MODEL), lambda i, be, nu: (be[i], 0, 0)),
                  pl.BlockSpec((None, 1, D_MODEL), lambda i, be, nu: (be[i], 0, 0))],
        out_specs=pl.BlockSpec((tb, D_MODEL), lambda i, be, nu: (i, 0)),
    )
    return pl.pallas_call(
        _ffn_kernel,
        out_shape=jax.ShapeDtypeStruct((cap, D_MODEL), F32),
        grid_spec=grid_spec,
        compiler_params=_cparams(1),
        name="moe_ffn",
    )(blk_e, n_used, xr, wgu, bgu, wdn, bdn)


def _combine_kernel(yk_ref, tg_ref, h_ref, lg_ref, lb_ref, op_ref, os_ref, *, n_prompt_tiles):
    i = pl.program_id(0)
    tg = tg_ref[...]
    y = tg[:, 0:1] * yk_ref[:, 0:D_MODEL]
    for k in range(1, TOP_K):
        y = y + tg[:, k:k + 1] * yk_ref[:, k * D_MODEL:(k + 1) * D_MODEL]
    out = _layer_norm(DN_ALPHA * h_ref[...] + y, lg_ref[...], lb_ref[...])

    @pl.when(i < n_prompt_tiles)
    def _():
        op_ref[...] = out

    @pl.when(i >= n_prompt_tiles)
    def _():
        os_ref[...] = out


def _combine(yk, tg, h, lg, lb, n_prompt, tm):
    m = h.shape[0]
    npt = n_prompt // tm
    row = lambda w: pl.BlockSpec((tm, w), lambda i: (i, 0))
    par = lambda a: pl.BlockSpec(a.shape, lambda i: (0, 0))
    return pl.pallas_call(
        functools.partial(_combine_kernel, n_prompt_tiles=npt),
        out_shape=(jax.ShapeDtypeStruct((n_prompt, D_MODEL), F32),
                   jax.ShapeDtypeStruct((m - n_prompt, D_MODEL), F32)),
        grid=(m // tm,),
        in_specs=[row(TOP_K * D_MODEL), row(LANES), row(D_MODEL), par(lg), par(lb)],
        out_specs=(pl.BlockSpec((tm, D_MODEL), lambda i: (jnp.minimum(i, npt - 1), 0)),
                   pl.BlockSpec((tm, D_MODEL), lambda i: (jnp.maximum(i - npt, 0), 0))),
        compiler_params=_cparams(1),
        name="combine",
    )(yk, tg, h, lg, lb)


def _rope_tables(pos):
    half = ROPE_DIM // 2
    inv = ROPE_THETA ** (-jnp.arange(half, dtype=F32) / half)
    ang = pos.astype(F32)[:, None] * inv[None, :]
    cos, sin = jnp.cos(ang), jnp.sin(ang)
    return jnp.concatenate([cos] * 4, axis=1), jnp.concatenate([-sin, sin] * 2, axis=1)


def _pad_cols(w, width):
    return jnp.pad(w, ((0, 0), (0, width - w.shape[1])))


def _routing(top_i, n_blocks, tb):
    n = top_i.shape[0]
    nk = n * TOP_K
    flat_e = top_i.reshape(-1)
    onehot = (flat_e[:, None] == jnp.arange(N_EXPERTS, dtype=jnp.int32)[None, :]).astype(jnp.int32)
    csum = jnp.cumsum(onehot, axis=0)
    rank = jnp.sum(onehot * csum, axis=1) - 1
    counts = csum[-1]
    nblk_e = (counts + tb - 1) // tb
    pend = jnp.cumsum(nblk_e)
    pstart = (pend - nblk_e) * tb
    dest = (pstart[flat_e] + rank).astype(jnp.int32)
    flat_t = jnp.arange(nk, dtype=jnp.int32) // TOP_K
    row_tok = jnp.zeros((n_blocks * tb,), jnp.int32).at[dest].set(flat_t, unique_indices=True)
    n_used = pend[-1].astype(jnp.int32)
    bidx = jnp.minimum(jnp.arange(n_blocks, dtype=jnp.int32), n_used - 1)
    blk_e = jnp.minimum(jnp.searchsorted(pend, bidx, side='right'), N_EXPERTS - 1).astype(jnp.int32)
    return row_tok, dest, blk_e, n_used.reshape(1)


def kernel(x_prompt, x_sample, cache_kv_latent, cache_k_rope, state_ssm, state_conv, page_table, w_in, ssd_conv_w, ssd_conv_b, ssd_dt_bias, ssd_a_log, ssd_d, ssd_norm_g, mla_q_norm_g, w_q_b, mla_kv_norm_g, w_kv_b, w_branch_a, w_branch_b, w_out, ln1_g, ln1_b, router_w, router_b, exp_w_gu, exp_b_gu, exp_w_down, exp_b_down, ln2_g, ln2_b):
    b_p, t_p, _ = x_prompt.shape
    b_s, t_s, _ = x_sample.shape
    n_pages = page_table.shape[1]
    page = cache_kv_latent.shape[2]
    past_len = n_pages * page
    n_p = b_p * t_p
    n_s = b_s * t_s
    n = n_p + n_s
    assert DEPTH == 1 and w_in.shape[0] == 1
    assert n_p % PROJ_TILE == 0 or n % PROJ_TILE == 0
    assert n_p % ROW_TILE == 0 and n_s % ROW_TILE == 0 and t_p % ATTN_TQ == 0 and t_p % SSD_CHUNK == 0
    assert t_s == SUBLANES and n_pages % PAGES_PER_STEP == 0 and page == LANES

    wi = w_in[0]
    seg = lambda k: wi[:, IN_OFFS[k]:IN_OFFS[k + 1]]
    half = ROPE_DIM // 2
    w_kr = seg(2)
    w_kr_rot = jnp.concatenate([w_kr[:, half:], w_kr[:, :half]], axis=1)
    w_mla = jnp.concatenate([seg(0), seg(1), _pad_cols(w_kr, LANES), _pad_cols(w_kr_rot, LANES),
                             _pad_cols(seg(5), LANES)], axis=1).astype(BF16)
    w_z = seg(3).astype(BF16)
    w_xbc = seg(4).astype(BF16)
    w_gates = jnp.concatenate([seg(6), seg(7)], axis=1).astype(BF16)
    wq3 = w_q_b[0].reshape(Q_LORA, MLA_HEADS, QK_NOPE + ROPE_DIM)
    wq_nope = wq3[:, :, :QK_NOPE].reshape(Q_LORA, MLA_HEADS * QK_NOPE)
    wq_rope = wq3[:, :, QK_NOPE:]
    wq_rope_rot = jnp.concatenate([wq_rope[..., half:], wq_rope[..., :half]], axis=-1)
    wq = jnp.concatenate([wq_nope, wq_rope.reshape(Q_LORA, -1), wq_rope_rot.reshape(Q_LORA, -1)],
                         axis=1).astype(BF16)
    wkv3 = w_kv_b[0].reshape(KV_LORA, MLA_HEADS, QK_NOPE + V_HEAD)
    wuk = jnp.transpose(wkv3[:, :, :QK_NOPE], (1, 2, 0)).astype(BF16)
    wuv = jnp.transpose(wkv3[:, :, QK_NOPE:], (1, 0, 2)).astype(BF16)
    qg = mla_q_norm_g[0].reshape(1, Q_LORA)
    kvg = mla_kv_norm_g[0].reshape(1, KV_LORA)
    cw = ssd_conv_w[0]
    cb = ssd_conv_b[0].reshape(1, CONV_DIM)
    dtb = _pad_cols(ssd_dt_bias[0].reshape(1, SSD_HEADS), LANES)
    a_neg = _pad_cols(-jnp.exp(ssd_a_log[0].astype(F32)).reshape(1, SSD_HEADS), LANES)
    d_row = _pad_cols(ssd_d[0].reshape(1, SSD_HEADS), LANES)
    ng = ssd_norm_g[0].reshape(1, D_INNER)
    wa = w_branch_a[0].astype(BF16)
    wb = w_branch_b[0].astype(BF16)
    wo = w_out[0].astype(BF16)
    rw = _pad_cols(router_w[0], LANES)
    rwh = rw.astype(BF16)
    rwl = (rw - rwh.astype(F32)).astype(BF16)
    rb = jnp.concatenate([router_b[0].reshape(1, N_EXPERTS),
                          jnp.full((1, LANES - N_EXPERTS), NEG, F32)], axis=1)
    wgu = jnp.concatenate([exp_w_gu[0][:, :, 0::2], exp_w_gu[0][:, :, 1::2]], axis=2).astype(BF16)
    bgu = jnp.concatenate([exp_b_gu[0][:, 0::2], exp_b_gu[0][:, 1::2]], axis=1).reshape(N_EXPERTS, 1, 2 * D_FF)
    wdn = exp_w_down[0].astype(BF16)
    bdn = exp_b_down[0].reshape(N_EXPERTS, 1, D_MODEL)
    row2 = lambda v: v[0].reshape(1, D_MODEL)

    pos = jnp.concatenate([jnp.tile(jnp.arange(t_p, dtype=jnp.int32), b_p),
                           jnp.tile(past_len + jnp.arange(t_s, dtype=jnp.int32), b_s)])
    cos_t, sin_t = _rope_tables(pos)

    x = jnp.concatenate([x_prompt.reshape(n_p, D_MODEL), x_sample.reshape(n_s, D_MODEL)], axis=0)
    qlat, qrope, c_new, c_bf, kr_new, kr_bf, dt_raw = _mla_in(x, w_mla, qg, kvg, wq, wuk, cos_t, sin_t, ROW_TILE)
    z = _proj(x, w_z, PROJ_TILE)
    xbc = _proj(x, w_xbc, PROJ_TILE)
    gates = _proj(x, w_gates, PROJ_TILE)

    conv0_p = jnp.zeros((b_p, SUBLANES, CONV_DIM), F32)
    conv0_s = jnp.pad(state_conv[0], ((0, 0), (SUBLANES - (D_CONV - 1), 0), (0, 0)))
    ya_p, ssm_p = _ssd(xbc, z, dt_raw, conv0_p, None, cw, cb, dtb, a_neg, d_row, ng,
                       nb=b_p, nc=t_p // SSD_CHUNK, t_in=SSD_CHUNK, row0=0, y_dtype=BF16)
    ya_s, ssm_s = _ssd(xbc, z, dt_raw, conv0_s, state_ssm[0].reshape(b_s, D_INNER, D_STATE),
                       cw, cb, dtb, a_neg, d_row, ng,
                       nb=b_s, nc=1, t_in=t_s, row0=n_p, y_dtype=F32)
    conv_p = xbc[:n_p].reshape(b_p, t_p, CONV_DIM)[:, t_p - (D_CONV - 1):, :]
    conv_s = jnp.concatenate([state_conv[0].astype(F32), xbc[n_p:].reshape(b_s, t_s, CONV_DIM)],
                             axis=1)[:, t_s:, :]

    o_p = _attn_prompt(qlat, qrope, c_bf, kr_bf, wuv, nb=b_p, t=t_p, tq=ATTN_TQ)
    ql_s = jnp.transpose(qlat[:, n_p:, :].reshape(MLA_HEADS, b_s, t_s, KV_LORA), (1, 0, 2, 3))
    qr_s = jnp.transpose(qrope[:, n_p:, :].reshape(MLA_HEADS, b_s, t_s, ROPE_DIM), (1, 0, 2, 3))
    cnew_s = jnp.pad(c_new[n_p:].reshape(b_s, t_s, KV_LORA), ((0, 0), (0, LANES - t_s), (0, 0)))
    krnew_s = jnp.pad(kr_new[n_p:].reshape(b_s, t_s, ROPE_DIM), ((0, 0), (0, LANES - t_s), (0, 0)))
    olat_s = _attn_sample(page_table, ql_s.reshape(b_s, MLA_HEADS * t_s, KV_LORA),
                          qr_s.reshape(b_s, MLA_HEADS * t_s, ROPE_DIM), cnew_s, krnew_s,
                          cache_kv_latent[0], cache_k_rope[0], t_new=t_s)
    olat_s = jnp.transpose(olat_s.reshape(b_s, MLA_HEADS, t_s, KV_LORA), (1, 0, 2, 3)).reshape(MLA_HEADS, n_s, KV_LORA)
    o_s = _head_proj(olat_s, wuv)

    h, top_i, top_g = _merge(ya_p, ya_s.astype(BF16), o_p, o_s, gates, x, wa, wb, wo,
                             row2(ln1_g), row2(ln1_b), rwh, rwl, rb, ROW_TILE)

    tb = MOE_TB
    unit = max(tb, GATHER_ROWS)
    n_blocks = -(-(n * TOP_K + N_EXPERTS * (tb - 1)) // unit) * unit // tb
    row_tok, dest, blk_e, n_used = _routing(top_i[:, :TOP_K], n_blocks, tb)
    xr = _gather_rows(h, row_tok, GATHER_ROWS)
    yr = _ffn(blk_e, n_used, xr, wgu, bgu, wdn, bdn, tb)
    yk = _gather_rows(yr, dest, GATHER_ROWS).reshape(n, TOP_K * D_MODEL)
    y_p, y_s = _combine(yk, top_g, h, row2(ln2_g), row2(ln2_b), n_p, ROW_TILE)

    return (y_p.reshape(b_p, t_p, D_MODEL), y_s.reshape(b_s, t_s, D_MODEL),
            c_new[:n_p].reshape(1, b_p, t_p, KV_LORA), kr_new[:n_p].reshape(1, b_p, t_p, ROPE_DIM),
            ssm_p.reshape(1, b_p, SSD_HEADS, SSD_HEAD_DIM, D_STATE), conv_p[None],
            c_new[n_p:].reshape(1, b_s, t_s, KV_LORA), kr_new[n_p:].reshape(1, b_s, t_s, ROPE_DIM),
            ssm_s.reshape(1, b_s, SSD_HEADS, SSD_HEAD_DIM, D_STATE), conv_s[None])
```

```python
import functools

import jax
import jax.numpy as jnp
import numpy as np
from jax import lax
from jax.experimental import pallas as pl
from jax.experimental.pallas import tpu as pltpu

F32 = jnp.float32
BF16 = jnp.bfloat16

D_MODEL = 1024
D_INNER = 2048
SSD_HEAD_DIM = 64
SSD_HEADS = 32
SSD_GROUPS = 4
HEADS_PER_GROUP = 8
D_STATE = 128
D_CONV = 4
CONV_DIM = D_INNER + 2 * SSD_GROUPS * D_STATE
SSD_CHUNK = 128
MLA_HEADS = 8
Q_LORA = 256
KV_LORA = 256
QK_NOPE = 128
ROPE_DIM = 64
V_HEAD = 128
ROPE_THETA = 10000.0
ATTN_SCALE = (QK_NOPE + ROPE_DIM) ** -0.5
N_EXPERTS = 32
TOP_K = 4
D_FF = 1024
SWIGLU_ALPHA = 1.702
SWIGLU_LIMIT = 7.0
DEPTH = 1
DN_ALPHA = (2 * DEPTH) ** 0.25
LN_EPS = 1e-5
RMS_EPS = 1e-6
IN_SIZES = (Q_LORA, KV_LORA, ROPE_DIM, D_INNER, CONV_DIM, SSD_HEADS, D_MODEL, D_MODEL)
IN_OFFS = tuple(int(v) for v in np.cumsum((0,) + IN_SIZES))

LANES = 128
SUBLANES = 8
VMEM_LIMIT = 56 * 1024 * 1024

NEG = -1e30

ROW_TILE = 256
PROJ_TILE = 640
ATTN_TQ = 256
PAGES_PER_STEP = 8
MOE_TB = 256
COMBINE_TILE = 128
MLA_A_COLS = 896


def _cparams(n_axes):
    return pltpu.CompilerParams(dimension_semantics=("arbitrary",) * n_axes,
                                vmem_limit_bytes=VMEM_LIMIT)


def _dot(a, b):
    return jnp.dot(a, b, preferred_element_type=F32)


def _dot_nt(a, b):
    return lax.dot_general(a, b, (((1,), (1,)), ((), ())), preferred_element_type=F32)


def _split3(x):
    hi = x.astype(BF16)
    r1 = x - hi.astype(F32)
    mid = r1.astype(BF16)
    lo = (r1 - mid.astype(F32)).astype(BF16)
    return hi, mid, lo


def _sigmoid(x):
    return 1.0 / (1.0 + jnp.exp(-x))


def _layer_norm(x, g, b):
    mu = jnp.mean(x, axis=-1, keepdims=True)
    xc = x - mu
    var = jnp.mean(xc * xc, axis=-1, keepdims=True)
    return xc * lax.rsqrt(var + LN_EPS) * g + b


def _rms_norm(x, g):
    return x * lax.rsqrt(jnp.mean(x * x, axis=-1, keepdims=True) + RMS_EPS) * g


def _proj_kernel(x_ref, w_ref, o_ref):
    o_ref[...] = _dot(x_ref[...].astype(BF16), w_ref[...])


def _proj(x, w, tm):
    m, k = x.shape
    n = w.shape[1]
    return pl.pallas_call(
        _proj_kernel,
        out_shape=jax.ShapeDtypeStruct((m, n), F32),
        grid=(m // tm,),
        in_specs=[pl.BlockSpec((tm, k), lambda i: (i, 0)),
                  pl.BlockSpec((k, n), lambda i: (0, 0))],
        out_specs=pl.BlockSpec((tm, n), lambda i: (i, 0)),
        compiler_params=_cparams(1),
        name="proj",
    )(x, w)


def _mla_in_kernel(x_ref, wa_ref, qg_ref, kvg_ref, wq_ref, wuk_ref, cos_ref, sin_ref,
                   qlat_ref, qrope_ref, c_ref, cbf_ref, kr_ref, krbf_ref, dt_ref):
    xb = x_ref[...].astype(BF16)
    pa = _dot(xb, wa_ref[...])
    q_a = pa[:, 0:256]
    kv_a = pa[:, 256:512]
    kr_raw = pa[:, 512:640]
    kr_rot = pa[:, 640:768]
    dt_ref[...] = pa[:, 768:896]
    cos = cos_ref[...]
    sin = sin_ref[...]
    c = _rms_norm(kv_a, kvg_ref[...])
    c_ref[...] = c
    cbf_ref[...] = c.astype(BF16)
    kr = kr_raw * cos + kr_rot * sin
    kr_ref[...] = kr[:, 0:ROPE_DIM]
    krbf_ref[...] = kr[:, 0:ROPE_DIM].astype(BF16)
    qn = _rms_norm(q_a, qg_ref[...]).astype(BF16)
    qq = _dot(qn, wq_ref[...])
    cos4 = jnp.concatenate([cos] * 4, axis=1)
    sin4 = jnp.concatenate([sin] * 4, axis=1)
    qr = (qq[:, 1024:1536] * cos4 + qq[:, 1536:2048] * sin4) * ATTN_SCALE
    for h in range(MLA_HEADS):
        qrope_ref[h] = qr[:, h * ROPE_DIM:(h + 1) * ROPE_DIM].astype(BF16)
        nope = qq[:, h * QK_NOPE:(h + 1) * QK_NOPE].astype(BF16)
        qlat_ref[h] = (_dot(nope, wuk_ref[h]) * ATTN_SCALE).astype(BF16)


def _mla_in(x, wa, qg, kvg, wq, wuk, cos_t, sin_t, tm):
    m = x.shape[0]
    row = lambda w: pl.BlockSpec((tm, w), lambda i: (i, 0))
    full2 = lambda a: pl.BlockSpec(a.shape, lambda i: (0, 0))
    full3 = lambda a: pl.BlockSpec(a.shape, lambda i: (0, 0, 0))
    return pl.pallas_call(
        _mla_in_kernel,
        out_shape=(jax.ShapeDtypeStruct((MLA_HEADS, m, KV_LORA), BF16),
                   jax.ShapeDtypeStruct((MLA_HEADS, m, ROPE_DIM), BF16),
                   jax.ShapeDtypeStruct((m, KV_LORA), F32),
                   jax.ShapeDtypeStruct((m, KV_LORA), BF16),
                   jax.ShapeDtypeStruct((m, ROPE_DIM), F32),
                   jax.ShapeDtypeStruct((m, ROPE_DIM), BF16),
                   jax.ShapeDtypeStruct((m, LANES), F32)),
        grid=(m // tm,),
        in_specs=[row(D_MODEL), full2(wa), full2(qg), full2(kvg), full2(wq), full3(wuk),
                  row(LANES), row(LANES)],
        out_specs=(pl.BlockSpec((MLA_HEADS, tm, KV_LORA), lambda i: (0, i, 0)),
                   pl.BlockSpec((MLA_HEADS, tm, ROPE_DIM), lambda i: (0, i, 0)),
                   row(KV_LORA), row(KV_LORA), row(ROPE_DIM), row(ROPE_DIM), row(LANES)),
        compiler_params=_cparams(1),
        name="mla_in",
    )(x, wa, qg, kvg, wq, wuk, cos_t, sin_t)


def _ssd_kernel(*refs, t_in, has_h0):
    if has_h0:
        (xbc_ref, z_ref, dt_ref, conv0_ref, h0_ref, cw_ref, cb_ref, dtb_ref, a_ref, d_ref, ng_ref,
         y_ref, hfin_ref, ctail_ref, ext, xcv, ybuf, hT, acs_s, acsT_s, dtT_s, dtv_s, ea_s) = refs
    else:
        (xbc_ref, z_ref, dt_ref, conv0_ref, cw_ref, cb_ref, dtb_ref, a_ref, d_ref, ng_ref,
         y_ref, hfin_ref, ctail_ref, ext, xcv, ybuf, hT, acs_s, acsT_s, dtT_s, dtv_s, ea_s) = refs
        h0_ref = None
    L = SSD_CHUNK
    c = pl.program_id(1)
    nc = pl.num_programs(1)

    @pl.when(c == 0)
    def _():
        ext[0:SUBLANES, :] = conv0_ref[0]
        if t_in < L:
            ext[SUBLANES:SUBLANES + L, :] = jnp.zeros((L, CONV_DIM), F32)
        if has_h0:
            for j in range(D_INNER // LANES):
                hT[:, j * LANES:(j + 1) * LANES] = h0_ref[0, j * LANES:(j + 1) * LANES, :].T
        else:
            hT[...] = jnp.zeros_like(hT)

    ext[SUBLANES:SUBLANES + t_in, :] = xbc_ref[...]

    @pl.when(c == nc - 1)
    def _():
        ctail_ref[0] = ext[t_in:t_in + SUBLANES, :]

    for j in range(CONV_DIM // LANES):
        sl = slice(j * LANES, (j + 1) * LANES)
        acc = cb_ref[:, sl] + ext[pl.ds(SUBLANES - 3, L), sl] * cw_ref[0:1, sl]
        acc = acc + ext[pl.ds(SUBLANES - 2, L), sl] * cw_ref[1:2, sl]
        acc = acc + ext[pl.ds(SUBLANES - 1, L), sl] * cw_ref[2:3, sl]
        acc = acc + ext[pl.ds(SUBLANES, L), sl] * cw_ref[3:4, sl]
        xcv[:, sl] = acc * _sigmoid(acc)
    ext[0:SUBLANES, :] = ext[L:L + SUBLANES, :]

    row = lax.broadcasted_iota(jnp.int32, (L, L), 0)
    col = lax.broadcasted_iota(jnp.int32, (L, L), 1)
    tri = row >= col
    tri_b = jnp.where(tri, 1.0, 0.0).astype(BF16)
    if t_in < L:
        dt_raw = jnp.concatenate([dt_ref[...], jnp.zeros((L - t_in, LANES), F32)], axis=0)
    else:
        dt_raw = dt_ref[...]
    xs = dt_raw + dtb_ref[...]
    dtv = jnp.maximum(xs, 0.0) + jnp.log1p(jnp.exp(-jnp.abs(xs)))
    if t_in < L:
        dtv = jnp.where(lax.broadcasted_iota(jnp.int32, (L, LANES), 0) < t_in, dtv, 0.0)
    da = dtv * a_ref[...]
    hi, mid, lo = _split3(da)
    acs = _dot(tri_b, hi) + _dot(tri_b, mid) + _dot(tri_b, lo)
    acs_s[...] = acs
    acsT_s[...] = acs.T
    dtv_s[...] = dtv
    dtT_s[...] = dtv.T
    ea_s[...] = jnp.exp(acs)

    for g in range(SSD_GROUPS):
        bo = D_INNER + g * D_STATE
        co = D_INNER + SSD_GROUPS * D_STATE + g * D_STATE
        bm = xcv[:, bo:bo + D_STATE]
        cmb = xcv[:, co:co + D_STATE].astype(BF16)
        cb = _dot_nt(cmb, bm.astype(BF16))
        bmT = bm.T.astype(BF16)
        for hh in range(HEADS_PER_GROUP):
            idx = g * HEADS_PER_GROUP + hh
            sl = slice(idx * SSD_HEAD_DIM, (idx + 1) * SSD_HEAD_DIM)
            xh = xcv[:, sl]
            a_col = acs_s[:, idx:idx + 1]
            a_row = acsT_s[idx:idx + 1, :]
            dt_row = dtT_s[idx:idx + 1, :]
            dt_col = dtv_s[:, idx:idx + 1]
            w = cb * jnp.exp(jnp.where(tri, a_col - a_row, NEG)) * dt_row
            yd = _dot(w.astype(BF16), xh.astype(BF16))
            hp = hT[:, sl]
            yo = _dot(cmb, hp.astype(BF16))
            ybuf[:, sl] = yd + yo * ea_s[:, idx:idx + 1] + d_ref[:, idx:idx + 1] * xh
            a_last = acs_s[L - 1:L, idx:idx + 1]
            wcol = jnp.exp(a_last - a_col) * dt_col
            s = _dot(bmT, (xh * wcol).astype(BF16))
            hT[:, sl] = jnp.exp(a_last) * hp + s

    gw = D_INNER // SSD_GROUPS
    for g in range(SSD_GROUPS):
        sl = slice(g * gw, (g + 1) * gw)
        zg = z_ref[:, sl]
        v = ybuf[0:t_in, sl] * (zg * _sigmoid(zg))
        ms = jnp.mean(v * v, axis=-1, keepdims=True)
        y_ref[:, sl] = (v * lax.rsqrt(ms + RMS_EPS) * ng_ref[:, sl]).astype(y_ref.dtype)

    @pl.when(c == nc - 1)
    def _():
        for j in range(D_INNER // LANES):
            hfin_ref[0, j * LANES:(j + 1) * LANES, :] = hT[:, j * LANES:(j + 1) * LANES].T


def _ssd(xbc, z, dt, conv0, h0, cw, cb, dtb, a_neg, d_row, ng, *, nb, nc, t_in, row0, y_dtype):
    L = SSD_CHUNK
    blk0 = row0 // t_in
    has_h0 = h0 is not None
    rowspec = lambda w: pl.BlockSpec((t_in, w), lambda b, c: (blk0 + b * nc + c, 0))
    par = lambda a: pl.BlockSpec(a.shape, lambda b, c: (0, 0))
    in_specs = [rowspec(CONV_DIM), rowspec(D_INNER), rowspec(LANES),
                pl.BlockSpec((1, SUBLANES, CONV_DIM), lambda b, c: (b, 0, 0))]
    args = [xbc, z, dt, conv0]
    if has_h0:
        in_specs.append(pl.BlockSpec((1, D_INNER, D_STATE), lambda b, c: (b, 0, 0)))
        args.append(h0)
    params = [cw, cb, dtb, a_neg, d_row, ng]
    in_specs += [par(p) for p in params]
    args += params
    return pl.pallas_call(
        functools.partial(_ssd_kernel, t_in=t_in, has_h0=has_h0),
        out_shape=(jax.ShapeDtypeStruct((nb * nc * t_in, D_INNER), y_dtype),
                   jax.ShapeDtypeStruct((nb, D_INNER, D_STATE), F32),
                   jax.ShapeDtypeStruct((nb, SUBLANES, CONV_DIM), F32)),
        grid=(nb, nc),
        in_specs=in_specs,
        out_specs=(pl.BlockSpec((t_in, D_INNER), lambda b, c: (b * nc + c, 0)),
                   pl.BlockSpec((1, D_INNER, D_STATE), lambda b, c: (b, 0, 0)),
                   pl.BlockSpec((1, SUBLANES, CONV_DIM), lambda b, c: (b, 0, 0))),
        scratch_shapes=[pltpu.VMEM((L + 2 * SUBLANES, CONV_DIM), F32),
                        pltpu.VMEM((L, CONV_DIM), F32),
                        pltpu.VMEM((L, D_INNER), F32),
                        pltpu.VMEM((D_STATE, D_INNER), F32),
                        pltpu.VMEM((L, LANES), F32), pltpu.VMEM((LANES, L), F32),
                        pltpu.VMEM((LANES, L), F32), pltpu.VMEM((L, LANES), F32),
                        pltpu.VMEM((L, LANES), F32)],
        compiler_params=_cparams(2),
        name="ssd",
    )(*args)


def _softmax_step(s, kc, m_scr, l_scr, acc_scr):
    reps = s.shape[1] // LANES
    m_prev = m_scr[...]
    m_new = jnp.maximum(m_prev, jnp.max(s, axis=1, keepdims=True))
    alpha = jnp.exp(m_prev - m_new)
    p = jnp.exp(s - (jnp.concatenate([m_new] * reps, axis=1) if reps > 1 else m_new))
    l_scr[...] = alpha * l_scr[...] + jnp.sum(p, axis=1, keepdims=True)
    acc_scr[...] = acc_scr[...] * jnp.concatenate([alpha] * (KV_LORA // LANES), axis=1) + _dot(p.astype(BF16), kc)
    m_scr[...] = m_new


def _attn_prompt_kernel(qlat_ref, qrope_ref, c_ref, kr_ref, wuv_ref, o_ref, m_scr, l_scr, acc_scr, *, tq):
    qi = pl.program_id(1)
    R = MLA_HEADS * tq
    q = qlat_ref[...].reshape(R, KV_LORA)
    qr = qrope_ref[...].reshape(R, ROPE_DIM)
    m_scr[...] = jnp.full_like(m_scr, NEG)
    l_scr[...] = jnp.zeros_like(l_scr)
    acc_scr[...] = jnp.zeros_like(acc_scr)

    def scores(j):
        start = pl.multiple_of(j * tq, tq)
        kc = c_ref[pl.ds(start, tq), :]
        kr = kr_ref[pl.ds(start, tq), :]
        return _dot_nt(q, kc) + _dot_nt(qr, kr), kc

    def body(j, carry):
        s, kc = scores(j)
        _softmax_step(s, kc, m_scr, l_scr, acc_scr)
        return carry

    lax.fori_loop(0, qi, body, 0)
    s, kc = scores(qi)
    t = lax.broadcasted_iota(jnp.int32, (R, tq), 0) & (tq - 1)
    kcol = lax.broadcasted_iota(jnp.int32, (R, tq), 1)
    s = jnp.where(kcol <= t, s, NEG)
    _softmax_step(s, kc, m_scr, l_scr, acc_scr)

    inv = 1.0 / l_scr[...]
    o_lat = acc_scr[...] * jnp.concatenate([inv] * (KV_LORA // LANES), axis=1)
    for h in range(MLA_HEADS):
        oh = _dot(o_lat[h * tq:(h + 1) * tq, :].astype(BF16), wuv_ref[h])
        o_ref[:, h * V_HEAD:(h + 1) * V_HEAD] = oh.astype(o_ref.dtype)


def _attn_prompt(qlat, qrope, cbf, krbf, wuv, *, nb, t, tq):
    nq = t // tq
    R = MLA_HEADS * tq
    return pl.pallas_call(
        functools.partial(_attn_prompt_kernel, tq=tq),
        out_shape=jax.ShapeDtypeStruct((nb * t, MLA_HEADS * V_HEAD), BF16),
        grid=(nb, nq),
        in_specs=[pl.BlockSpec((MLA_HEADS, tq, KV_LORA), lambda b, i: (0, b * nq + i, 0)),
                  pl.BlockSpec((MLA_HEADS, tq, ROPE_DIM), lambda b, i: (0, b * nq + i, 0)),
                  pl.BlockSpec((t, KV_LORA), lambda b, i: (b, 0)),
                  pl.BlockSpec((t, ROPE_DIM), lambda b, i: (b, 0)),
                  pl.BlockSpec(wuv.shape, lambda b, i: (0, 0, 0))],
        out_specs=pl.BlockSpec((tq, MLA_HEADS * V_HEAD), lambda b, i: (b * nq + i, 0)),
        scratch_shapes=[pltpu.VMEM((R, LANES), F32), pltpu.VMEM((R, LANES), F32),
                        pltpu.VMEM((R, KV_LORA), F32)],
        compiler_params=_cparams(2),
        name="attn_prompt",
    )(qlat, qrope, cbf, krbf, wuv)


def _attn_sample_kernel(pt_ref, qlat_ref, qrope_ref, cnew_ref, krnew_ref, *rest, n_pg, t_new):
    kv_refs = rest[:n_pg]
    kr_refs = rest[n_pg:2 * n_pg]
    o_ref, m_scr, l_scr, acc_scr = rest[2 * n_pg:]
    j = pl.program_id(1)
    R = qlat_ref.shape[1]
    q = qlat_ref[0]
    qr = qrope_ref[0]

    @pl.when(j == 0)
    def _():
        m_scr[...] = jnp.full_like(m_scr, NEG)
        l_scr[...] = jnp.zeros_like(l_scr)
        acc_scr[...] = jnp.zeros_like(acc_scr)
        kc = cnew_ref[0].astype(BF16)
        kr = krnew_ref[0].astype(BF16)
        s = _dot_nt(q, kc) + _dot_nt(qr, kr)
        t = lax.broadcasted_iota(jnp.int32, (R, LANES), 0) & (t_new - 1)
        kcol = lax.broadcasted_iota(jnp.int32, (R, LANES), 1)
        s = jnp.where(kcol <= t, s, NEG)
        _softmax_step(s, kc, m_scr, l_scr, acc_scr)

    kcs = [kv_refs[g][...].astype(BF16) for g in range(n_pg)]
    ss = [_dot_nt(q, kcs[g]) + _dot_nt(qr, kr_refs[g][...].astype(BF16)) for g in range(n_pg)]
    s = jnp.concatenate(ss, axis=1)
    kc = jnp.concatenate(kcs, axis=0)
    _softmax_step(s, kc, m_scr, l_scr, acc_scr)

    @pl.when(j == pl.num_programs(1) - 1)
    def _():
        inv = 1.0 / l_scr[...]
        o_ref[0] = acc_scr[...] * jnp.concatenate([inv] * (KV_LORA // LANES), axis=1)


def _attn_sample(page_table, qlat, qrope, cnew, krnew, cache_kv, cache_kr, *, t_new):
    nb, n_pages = page_table.shape
    n_pg = PAGES_PER_STEP
    R = qlat.shape[1]
    page = cache_kv.shape[1]

    def pg_spec(g, w):
        return pl.BlockSpec((None, page, w), lambda b, j, pt: (pt[b, j * n_pg + g], 0, 0))

    per_b = lambda r, w: pl.BlockSpec((1, r, w), lambda b, j, pt: (b, 0, 0))
    grid_spec = pltpu.PrefetchScalarGridSpec(
        num_scalar_prefetch=1,
        grid=(nb, n_pages // n_pg),
        in_specs=[per_b(R, KV_LORA), per_b(R, ROPE_DIM), per_b(LANES, KV_LORA), per_b(LANES, ROPE_DIM)]
                 + [pg_spec(g, KV_LORA) for g in range(n_pg)]
                 + [pg_spec(g, ROPE_DIM) for g in range(n_pg)],
        out_specs=per_b(R, KV_LORA),
        scratch_shapes=[pltpu.VMEM((R, LANES), F32), pltpu.VMEM((R, LANES), F32),
                        pltpu.VMEM((R, KV_LORA), F32)],
    )
    return pl.pallas_call(
        functools.partial(_attn_sample_kernel, n_pg=n_pg, t_new=t_new),
        out_shape=jax.ShapeDtypeStruct((nb, R, KV_LORA), F32),
        grid_spec=grid_spec,
        compiler_params=_cparams(2),
        name="attn_sample",
    )(page_table, qlat, qrope, cnew, krnew, *([cache_kv] * n_pg), *([cache_kr] * n_pg))


def _head_proj_kernel(x_ref, w_ref, o_ref):
    o_ref[...] = _dot(x_ref[0].astype(BF16), w_ref[0]).astype(o_ref.dtype)


def _head_proj(x, w):
    h, m, k = x.shape
    n = w.shape[2]
    return pl.pallas_call(
        _head_proj_kernel,
        out_shape=jax.ShapeDtypeStruct((m, h * n), BF16),
        grid=(h,),
        in_specs=[pl.BlockSpec((1, m, k), lambda i: (i, 0, 0)),
                  pl.BlockSpec((1, k, n), lambda i: (i, 0, 0))],
        out_specs=pl.BlockSpec((m, n), lambda i: (0, i)),
        compiler_params=_cparams(1),
        name="head_proj",
    )(x, w)


def _merge_kernel(ya_p_ref, ya_s_ref, o_p_ref, o_s_ref, g_ref, u_ref, wa_ref, wb_ref, wo_ref,
                  lg_ref, lb_ref, rwh_ref, rwl_ref, rb_ref,
                  h_ref, h3_ref, ti_ref, tg_ref, *, n_prompt_tiles):
    i = pl.program_id(0)
    is_p = i < n_prompt_tiles
    yn = jnp.where(is_p, ya_p_ref[...], ya_s_ref[...])
    ob = jnp.where(is_p, o_p_ref[...], o_s_ref[...])
    ya = _dot(yn, wa_ref[...])
    yb = _dot(ob, wb_ref[...])
    g = g_ref[...]
    merged = _sigmoid(g[:, 0:D_MODEL]) * ya + _sigmoid(g[:, D_MODEL:2 * D_MODEL]) * yb
    mix = _dot(merged.astype(BF16), wo_ref[...])
    h = _layer_norm(DN_ALPHA * u_ref[...] + mix, lg_ref[...], lb_ref[...])
    h_ref[...] = h
    tm = h.shape[0]
    for j in range(D_MODEL // LANES):
        h3_ref[pl.ds(j, tm, stride=SUBLANES), :] = h[:, j * LANES:(j + 1) * LANES]
    hh = h.astype(BF16)
    hl = (h - hh.astype(F32)).astype(BF16)
    logits = _dot(hh, rwh_ref[...]) + _dot(hl, rwh_ref[...]) + _dot(hh, rwl_ref[...]) + rb_ref[...]
    lane = lax.broadcasted_iota(jnp.int32, logits.shape, 1).astype(F32)
    vals = jnp.zeros_like(logits)
    idxs = jnp.zeros_like(logits)
    cur = logits
    for k in range(TOP_K):
        mx = jnp.max(cur, axis=1, keepdims=True)
        ix = jnp.min(jnp.where(cur == mx, lane, float(LANES)), axis=1, keepdims=True)
        vals = jnp.where(lane == k, mx, vals)
        idxs = jnp.where(lane == k, ix, idxs)
        cur = jnp.where(lane == ix, NEG * 2.0, cur)
    v0 = vals[:, 0:1]
    e = jnp.where(lane < TOP_K, jnp.exp(vals - v0), 0.0)
    tg_ref[...] = e / jnp.sum(e, axis=1, keepdims=True)
    ti_ref[...] = idxs.astype(jnp.int32)


def _merge(ya_p, ya_s, o_p, o_s, gates, u, wa, wb, wo, lg, lb, rwh, rwl, rb, tm):
    m = u.shape[0]
    npt = ya_p.shape[0] // tm
    row = lambda w: pl.BlockSpec((tm, w), lambda i: (i, 0))
    prow = lambda w: pl.BlockSpec((tm, w), lambda i: (jnp.minimum(i, npt - 1), 0))
    srow = lambda w: pl.BlockSpec((tm, w), lambda i: (jnp.maximum(i - npt, 0), 0))
    par = lambda a: pl.BlockSpec(a.shape, lambda i: (0, 0))
    return pl.pallas_call(
        functools.partial(_merge_kernel, n_prompt_tiles=npt),
        out_shape=(jax.ShapeDtypeStruct((m, D_MODEL), F32),
                   jax.ShapeDtypeStruct((m * SUBLANES, LANES), F32),
                   jax.ShapeDtypeStruct((m, LANES), jnp.int32),
                   jax.ShapeDtypeStruct((m, LANES), F32)),
        grid=(m // tm,),
        in_specs=[prow(D_INNER), srow(D_INNER), prow(D_MODEL), srow(D_MODEL), row(2 * D_MODEL), row(D_MODEL),
                  par(wa), par(wb), par(wo), par(lg), par(lb), par(rwh), par(rwl), par(rb)],
        out_specs=(row(D_MODEL), pl.BlockSpec((tm * SUBLANES, LANES), lambda i: (i, 0)), row(LANES), row(LANES)),
        compiler_params=_cparams(1),
        name="merge",
    )(ya_p, ya_s, o_p, o_s, gates, u, wa, wb, wo, lg, lb, rwh, rwl, rb)


DEINT_IN = 2 * LANES
DEINT_COLS = 512


def _deint_kernel(w_ref, p_ref, g_ref, u_ref):
    for t in range(DEINT_COLS // DEINT_IN):
        r = _dot(w_ref[:, t * DEINT_IN:(t + 1) * DEINT_IN].astype(BF16), p_ref[...])
        g_ref[:, t * LANES:(t + 1) * LANES] = r[:, 0:LANES].astype(BF16)
        u_ref[:, t * LANES:(t + 1) * LANES] = r[:, LANES:DEINT_IN].astype(BF16)


def _deinterleave(w):
    e, k, f2 = w.shape
    perm = np.zeros((DEINT_IN, DEINT_IN), np.float32)
    perm[2 * np.arange(LANES), np.arange(LANES)] = 1.0
    perm[2 * np.arange(LANES) + 1, LANES + np.arange(LANES)] = 1.0
    half = DEINT_COLS // 2
    out = jax.ShapeDtypeStruct((e, k, f2 // 2), BF16)
    return pl.pallas_call(
        _deint_kernel,
        out_shape=(out, out),
        grid=(e, f2 // DEINT_COLS),
        in_specs=[pl.BlockSpec((None, k, DEINT_COLS), lambda i, c: (i, 0, c)),
                  pl.BlockSpec((DEINT_IN, DEINT_IN), lambda i, c: (0, 0))],
        out_specs=(pl.BlockSpec((None, k, half), lambda i, c: (i, 0, c)),
                   pl.BlockSpec((None, k, half), lambda i, c: (i, 0, c))),
        compiler_params=_cparams(2),
        name="deinterleave",
    )(w, jnp.asarray(perm, BF16))


def _row_fetch_step(i, n_active, idx_hbm, src_hbm, idx_smem, idx_sem, buf, row_sem, rows):
    slot = lax.rem(i, 2)

    def idx_copy(chunk, s):
        return pltpu.make_async_copy(idx_hbm.at[chunk], idx_smem.at[s], idx_sem.at[s])

    def issue_rows(s):
        def body(r, carry):
            src0 = pl.multiple_of(idx_smem[s, r] * SUBLANES, SUBLANES)
            dst0 = pl.multiple_of(r * SUBLANES, SUBLANES)
            pltpu.make_async_copy(src_hbm.at[pl.ds(src0, SUBLANES), :],
                                  buf.at[s, pl.ds(dst0, SUBLANES), :], row_sem.at[s]).start()
            return carry
        lax.fori_loop(0, rows, body, 0, unroll=8)

    @pl.when(i == 0)
    def _():
        idx_copy(0, 0).start()
        idx_copy(0, 0).wait()
        issue_rows(0)

        @pl.when(1 < n_active)
        def _():
            idx_copy(1, 1).start()

    @pl.when(i + 1 < n_active)
    def _():
        idx_copy(i + 1, 1 - slot).wait()
        issue_rows(1 - slot)

    @pl.when(i + 2 < n_active)
    def _():
        idx_copy(i + 2, slot).start()

    return slot


def _rows_wait(src_hbm, buf, row_sem, slot, rows):
    pltpu.make_async_copy(src_hbm.at[pl.ds(0, rows * SUBLANES), :], buf.at[slot], row_sem.at[slot]).wait()


def _fetch_scratch(rows):
    return [pltpu.SMEM((2, rows), jnp.int32), pltpu.SemaphoreType.DMA((2,)),
            pltpu.VMEM((2, rows * SUBLANES, LANES), F32), pltpu.SemaphoreType.DMA((2,))]


def _ffn_kernel(be_ref, nu_ref, tok_hbm, h3_hbm, wg_ref, wu_ref, bgu_ref, wdn_ref, bdn_ref, o_ref,
                idx_smem, idx_sem, xbuf, row_sem, *, tb):
    i = pl.program_id(0)
    nu = nu_ref[0]
    slot = _row_fetch_step(i, nu, tok_hbm, h3_hbm, idx_smem, idx_sem, xbuf, row_sem, tb)

    @pl.when(i < nu)
    def _():
        _rows_wait(h3_hbm, xbuf, row_sem, slot, tb)
        x = jnp.concatenate([xbuf[slot, pl.ds(j, tb, stride=SUBLANES), :].astype(BF16)
                             for j in range(D_MODEL // LANES)], axis=1)
        bgu = bgu_ref[...]
        g = jnp.minimum(_dot(x, wg_ref[...]) + bgu[:, 0:D_FF], SWIGLU_LIMIT)
        u = jnp.clip(_dot(x, wu_ref[...]) + bgu[:, D_FF:2 * D_FF], -SWIGLU_LIMIT, SWIGLU_LIMIT)
        hh = (u + 1.0) * (g * _sigmoid(SWIGLU_ALPHA * g))
        out = _dot(hh.astype(BF16), wdn_ref[...]) + bdn_ref[...]
        for j in range(D_MODEL // LANES):
            o_ref[pl.ds(j, tb, stride=SUBLANES), :] = out[:, j * LANES:(j + 1) * LANES]

    @pl.when(i >= nu)
    def _():
        o_ref[...] = jnp.zeros_like(o_ref)


def _ffn(blk_e, n_used, row_tok, h3, wg, wu, bgu, wdn, bdn, tb):
    cap = row_tok.shape[0]
    nblk = cap // tb
    wspec = lambda a: pl.BlockSpec((None,) + a.shape[1:], lambda i, be, nu: (be[i], 0, 0))
    grid_spec = pltpu.PrefetchScalarGridSpec(
        num_scalar_prefetch=2,
        grid=(nblk,),
        in_specs=[pl.BlockSpec(memory_space=pl.ANY), pl.BlockSpec(memory_space=pl.ANY),
                  wspec(wg), wspec(wu), wspec(bgu), wspec(wdn), wspec(bdn)],
        out_specs=pl.BlockSpec((tb * SUBLANES, LANES), lambda i, be, nu: (i, 0)),
        scratch_shapes=_fetch_scratch(tb),
    )
    return pl.pallas_call(
        functools.partial(_ffn_kernel, tb=tb),
        out_shape=jax.ShapeDtypeStruct((cap * SUBLANES, LANES), F32),
        grid_spec=grid_spec,
        compiler_params=_cparams(1),
        name="moe_ffn",
    )(blk_e, n_used, row_tok.reshape(nblk, tb), h3, wg, wu, bgu, wdn, bdn)


def _combine_kernel(dest_hbm, yr3_hbm, tg_ref, h_ref, lg_ref, lb_ref, op_ref, os_ref,
                    idx_smem, idx_sem, ybuf, row_sem, *, tm, n_prompt_tiles):
    i = pl.program_id(0)
    rows = tm * TOP_K
    slot = _row_fetch_step(i, pl.num_programs(0), dest_hbm, yr3_hbm, idx_smem, idx_sem, ybuf, row_sem, rows)
    _rows_wait(yr3_hbm, ybuf, row_sem, slot, rows)
    tg = tg_ref[...]
    stride = TOP_K * SUBLANES
    pieces = []
    for j in range(D_MODEL // LANES):
        acc = tg[:, 0:1] * ybuf[slot, pl.ds(j, tm, stride=stride), :]
        for k in range(1, TOP_K):
            acc = acc + tg[:, k:k + 1] * ybuf[slot, pl.ds(k * SUBLANES + j, tm, stride=stride), :]
        pieces.append(acc)
    y = jnp.concatenate(pieces, axis=1)
    out = _layer_norm(DN_ALPHA * h_ref[...] + y, lg_ref[...], lb_ref[...])

    @pl.when(i < n_prompt_tiles)
    def _():
        op_ref[...] = out

    @pl.when(i >= n_prompt_tiles)
    def _():
        os_ref[...] = out


def _combine(dest, yr3, tg, h, lg, lb, n_prompt, tm):
    m = h.shape[0]
    npt = n_prompt // tm
    row = lambda w: pl.BlockSpec((tm, w), lambda i: (i, 0))
    par = lambda a: pl.BlockSpec(a.shape, lambda i: (0, 0))
    return pl.pallas_call(
        functools.partial(_combine_kernel, tm=tm, n_prompt_tiles=npt),
        out_shape=(jax.ShapeDtypeStruct((n_prompt, D_MODEL), F32),
                   jax.ShapeDtypeStruct((m - n_prompt, D_MODEL), F32)),
        grid=(m // tm,),
        in_specs=[pl.BlockSpec(memory_space=pl.ANY), pl.BlockSpec(memory_space=pl.ANY),
                  row(LANES), row(D_MODEL), par(lg), par(lb)],
        out_specs=(pl.BlockSpec((tm, D_MODEL), lambda i: (jnp.minimum(i, npt - 1), 0)),
                   pl.BlockSpec((tm, D_MODEL), lambda i: (jnp.maximum(i - npt, 0), 0))),
        scratch_shapes=_fetch_scratch(tm * TOP_K),
        compiler_params=_cparams(1),
        name="combine",
    )(dest.reshape(m // tm, tm * TOP_K), yr3, tg, h, lg, lb)


def _rope_tables(pos):
    half = ROPE_DIM // 2
    inv = ROPE_THETA ** (-jnp.arange(half, dtype=F32) / half)
    ang = pos.astype(F32)[:, None] * inv[None, :]
    cos, sin = jnp.cos(ang), jnp.sin(ang)
    return jnp.concatenate([cos] * 4, axis=1), jnp.concatenate([-sin, sin] * 2, axis=1)


def _pad_cols(w, width):
    return jnp.pad(w, ((0, 0), (0, width - w.shape[1])))


def _routing(top_i, n_blocks, tb):
    n = top_i.shape[0]
    nk = n * TOP_K
    flat_e = top_i.reshape(-1)
    onehot = (flat_e[:, None] == jnp.arange(N_EXPERTS, dtype=jnp.int32)[None, :]).astype(jnp.int32)
    csum = jnp.cumsum(onehot, axis=0)
    rank = jnp.sum(onehot * csum, axis=1) - 1
    counts = csum[-1]
    nblk_e = (counts + tb - 1) // tb
    pend = jnp.cumsum(nblk_e)
    pstart = (pend - nblk_e) * tb
    dest = (pstart[flat_e] + rank).astype(jnp.int32)
    flat_t = jnp.arange(nk, dtype=jnp.int32) // TOP_K
    row_tok = jnp.zeros((n_blocks * tb,), jnp.int32).at[dest].set(flat_t, unique_indices=True)
    n_used = pend[-1].astype(jnp.int32)
    bidx = jnp.minimum(jnp.arange(n_blocks, dtype=jnp.int32), n_used - 1)
    blk_e = jnp.minimum(jnp.sum((pend[None, :] <= bidx[:, None]).astype(jnp.int32), axis=1), N_EXPERTS - 1)
    return row_tok, dest, blk_e, n_used.reshape(1)


def kernel(x_prompt, x_sample, cache_kv_latent, cache_k_rope, state_ssm, state_conv, page_table, w_in, ssd_conv_w, ssd_conv_b, ssd_dt_bias, ssd_a_log, ssd_d, ssd_norm_g, mla_q_norm_g, w_q_b, mla_kv_norm_g, w_kv_b, w_branch_a, w_branch_b, w_out, ln1_g, ln1_b, router_w, router_b, exp_w_gu, exp_b_gu, exp_w_down, exp_b_down, ln2_g, ln2_b):
    b_p, t_p, _ = x_prompt.shape
    b_s, t_s, _ = x_sample.shape
    n_pages = page_table.shape[1]
    page = cache_kv_latent.shape[2]
    past_len = n_pages * page
    n_p = b_p * t_p
    n_s = b_s * t_s
    n = n_p + n_s
    assert DEPTH == 1 and w_in.shape[0] == 1
    assert n_p % PROJ_TILE == 0 or n % PROJ_TILE == 0
    assert n_p % ROW_TILE == 0 and n_s % ROW_TILE == 0 and t_p % ATTN_TQ == 0 and t_p % SSD_CHUNK == 0
    assert t_s == SUBLANES and n_pages % PAGES_PER_STEP == 0 and page == LANES

    wi = w_in[0]
    seg = lambda k: wi[:, IN_OFFS[k]:IN_OFFS[k + 1]]
    half = ROPE_DIM // 2
    w_kr = seg(2)
    w_kr_rot = jnp.concatenate([w_kr[:, half:], w_kr[:, :half]], axis=1)
    w_mla = jnp.concatenate([seg(0), seg(1), _pad_cols(w_kr, LANES), _pad_cols(w_kr_rot, LANES),
                             _pad_cols(seg(5), LANES)], axis=1).astype(BF16)
    w_z = seg(3).astype(BF16)
    w_xbc = seg(4).astype(BF16)
    w_gates = jnp.concatenate([seg(6), seg(7)], axis=1).astype(BF16)
    wq3 = w_q_b[0].reshape(Q_LORA, MLA_HEADS, QK_NOPE + ROPE_DIM)
    wq_nope = wq3[:, :, :QK_NOPE].reshape(Q_LORA, MLA_HEADS * QK_NOPE)
    wq_rope = wq3[:, :, QK_NOPE:]
    wq_rope_rot = jnp.concatenate([wq_rope[..., half:], wq_rope[..., :half]], axis=-1)
    wq = jnp.concatenate([wq_nope, wq_rope.reshape(Q_LORA, -1), wq_rope_rot.reshape(Q_LORA, -1)],
                         axis=1).astype(BF16)
    wkv3 = w_kv_b[0].reshape(KV_LORA, MLA_HEADS, QK_NOPE + V_HEAD)
    wuk = jnp.transpose(wkv3[:, :, :QK_NOPE], (1, 2, 0)).astype(BF16)
    wuv = jnp.transpose(wkv3[:, :, QK_NOPE:], (1, 0, 2)).astype(BF16)
    qg = mla_q_norm_g[0].reshape(1, Q_LORA)
    kvg = mla_kv_norm_g[0].reshape(1, KV_LORA)
    cw = ssd_conv_w[0]
    cb = ssd_conv_b[0].reshape(1, CONV_DIM)
    dtb = _pad_cols(ssd_dt_bias[0].reshape(1, SSD_HEADS), LANES)
    a_neg = _pad_cols(-jnp.exp(ssd_a_log[0].astype(F32)).reshape(1, SSD_HEADS), LANES)
    d_row = _pad_cols(ssd_d[0].reshape(1, SSD_HEADS), LANES)
    ng = ssd_norm_g[0].reshape(1, D_INNER)
    wa = w_branch_a[0].astype(BF16)
    wb = w_branch_b[0].astype(BF16)
    wo = w_out[0].astype(BF16)
    rw = _pad_cols(router_w[0], LANES)
    rwh = rw.astype(BF16)
    rwl = (rw - rwh.astype(F32)).astype(BF16)
    rb = jnp.concatenate([router_b[0].reshape(1, N_EXPERTS),
                          jnp.full((1, LANES - N_EXPERTS), NEG, F32)], axis=1)
    wg, wu = _deinterleave(exp_w_gu[0])
    bgu = jnp.concatenate([exp_b_gu[0][:, 0::2], exp_b_gu[0][:, 1::2]], axis=1).reshape(N_EXPERTS, 1, 2 * D_FF)
    wdn = exp_w_down[0].astype(BF16)
    bdn = exp_b_down[0].reshape(N_EXPERTS, 1, D_MODEL)
    row2 = lambda v: v[0].reshape(1, D_MODEL)

    pos = jnp.concatenate([jnp.tile(jnp.arange(t_p, dtype=jnp.int32), b_p),
                           jnp.tile(past_len + jnp.arange(t_s, dtype=jnp.int32), b_s)])
    cos_t, sin_t = _rope_tables(pos)

    x = jnp.concatenate([x_prompt.reshape(n_p, D_MODEL), x_sample.reshape(n_s, D_MODEL)], axis=0)
    qlat, qrope, c_new, c_bf, kr_new, kr_bf, dt_raw = _mla_in(x, w_mla, qg, kvg, wq, wuk, cos_t, sin_t, ROW_TILE)
    z = _proj(x, w_z, PROJ_TILE)
    xbc = _proj(x, w_xbc, PROJ_TILE)
    gates = _proj(x, w_gates, PROJ_TILE)

    conv0_p = jnp.zeros((b_p, SUBLANES, CONV_DIM), F32)
    conv0_s = jnp.pad(state_conv[0], ((0, 0), (SUBLANES - (D_CONV - 1), 0), (0, 0)))
    ya_p, ssm_p, ctail_p = _ssd(xbc, z, dt_raw, conv0_p, None, cw, cb, dtb, a_neg, d_row, ng,
                                nb=b_p, nc=t_p // SSD_CHUNK, t_in=SSD_CHUNK, row0=0, y_dtype=BF16)
    ya_s, ssm_s, ctail_s = _ssd(xbc, z, dt_raw, conv0_s, state_ssm[0].reshape(b_s, D_INNER, D_STATE),
                                cw, cb, dtb, a_neg, d_row, ng,
                                nb=b_s, nc=1, t_in=t_s, row0=n_p, y_dtype=F32)
    conv_p = ctail_p[:, SUBLANES - (D_CONV - 1):, :]
    conv_s = ctail_s[:, SUBLANES - (D_CONV - 1):, :]

    o_p = _attn_prompt(qlat, qrope, c_bf, kr_bf, wuv, nb=b_p, t=t_p, tq=ATTN_TQ)
    ql_s = jnp.transpose(qlat[:, n_p:, :].reshape(MLA_HEADS, b_s, t_s, KV_LORA), (1, 0, 2, 3))
    qr_s = jnp.transpose(qrope[:, n_p:, :].reshape(MLA_HEADS, b_s, t_s, ROPE_DIM), (1, 0, 2, 3))
    cnew_s = jnp.pad(c_new[n_p:].reshape(b_s, t_s, KV_LORA), ((0, 0), (0, LANES - t_s), (0, 0)))
    krnew_s = jnp.pad(kr_new[n_p:].reshape(b_s, t_s, ROPE_DIM), ((0, 0), (0, LANES - t_s), (0, 0)))
    olat_s = _attn_sample(page_table, ql_s.reshape(b_s, MLA_HEADS * t_s, KV_LORA),
                          qr_s.reshape(b_s, MLA_HEADS * t_s, ROPE_DIM), cnew_s, krnew_s,
                          cache_kv_latent[0], cache_k_rope[0], t_new=t_s)
    olat_s = jnp.transpose(olat_s.reshape(b_s, MLA_HEADS, t_s, KV_LORA), (1, 0, 2, 3)).reshape(MLA_HEADS, n_s, KV_LORA)
    o_s = _head_proj(olat_s, wuv)

    h, h3, top_i, top_g = _merge(ya_p, ya_s.astype(BF16), o_p, o_s, gates, x, wa, wb, wo,
                             row2(ln1_g), row2(ln1_b), rwh, rwl, rb, ROW_TILE)

    tb = MOE_TB
    n_blocks = -(-(n * TOP_K + N_EXPERTS * (tb - 1)) // tb)
    row_tok, dest, blk_e, n_used = _routing(top_i[:, :TOP_K], n_blocks, tb)
    yr3 = _ffn(blk_e, n_used, row_tok, h3, wg, wu, bgu, wdn, bdn, tb)
    y_p, y_s = _combine(dest, yr3, top_g, h, row2(ln2_g), row2(ln2_b), n_p, COMBINE_TILE)

    return (y_p.reshape(b_p, t_p, D_MODEL), y_s.reshape(b_s, t_s, D_MODEL),
            c_new[:n_p].reshape(1, b_p, t_p, KV_LORA), kr_new[:n_p].reshape(1, b_p, t_p, ROPE_DIM),
            ssm_p.reshape(1, b_p, SSD_HEADS, SSD_HEAD_DIM, D_STATE), conv_p[None],
            c_new[n_p:].reshape(1, b_s, t_s, KV_LORA), kr_new[n_p:].reshape(1, b_s, t_s, ROPE_DIM),
            ssm_s.reshape(1, b_s, SSD_HEADS, SSD_HEAD_DIM, D_STATE), conv_s[None])
```

```python
import functools

import jax
import jax.numpy as jnp
import numpy as np
from jax import lax
from jax.experimental import pallas as pl
from jax.experimental.pallas import tpu as pltpu

F32 = jnp.float32
BF16 = jnp.bfloat16

D_MODEL = 1024
D_INNER = 2048
SSD_HEAD_DIM = 64
SSD_HEADS = 32
SSD_GROUPS = 4
HEADS_PER_GROUP = 8
D_STATE = 128
D_CONV = 4
CONV_DIM = D_INNER + 2 * SSD_GROUPS * D_STATE
SSD_CHUNK = 128
MLA_HEADS = 8
Q_LORA = 256
KV_LORA = 256
QK_NOPE = 128
ROPE_DIM = 64
V_HEAD = 128
ROPE_THETA = 10000.0
ATTN_SCALE = (QK_NOPE + ROPE_DIM) ** -0.5
N_EXPERTS = 32
TOP_K = 4
D_FF = 1024
SWIGLU_ALPHA = 1.702
SWIGLU_LIMIT = 7.0
DEPTH = 1
DN_ALPHA = (2 * DEPTH) ** 0.25
LN_EPS = 1e-5
RMS_EPS = 1e-6
IN_SIZES = (Q_LORA, KV_LORA, ROPE_DIM, D_INNER, CONV_DIM, SSD_HEADS, D_MODEL, D_MODEL)
IN_OFFS = tuple(int(v) for v in np.cumsum((0,) + IN_SIZES))

LANES = 128
SUBLANES = 8
VMEM_LIMIT = 56 * 1024 * 1024

NEG = -1e30

ROW_TILE = 256
PROJ_TILE = 640
ATTN_TQ = 256
PAGES_PER_STEP = 16
MOE_TB = 256
COMBINE_TILE = 128
MLA_A_COLS = 896


def _cparams(n_axes):
    return pltpu.CompilerParams(dimension_semantics=("arbitrary",) * n_axes,
                                vmem_limit_bytes=VMEM_LIMIT)


def _dot(a, b):
    return jnp.dot(a, b, preferred_element_type=F32)


def _dot_nt(a, b):
    return lax.dot_general(a, b, (((1,), (1,)), ((), ())), preferred_element_type=F32)


def _split3(x):
    hi = x.astype(BF16)
    r1 = x - hi.astype(F32)
    mid = r1.astype(BF16)
    lo = (r1 - mid.astype(F32)).astype(BF16)
    return hi, mid, lo


def _sigmoid(x):
    return 1.0 / (1.0 + jnp.exp(-x))


def _layer_norm(x, g, b):
    mu = jnp.mean(x, axis=-1, keepdims=True)
    xc = x - mu
    var = jnp.mean(xc * xc, axis=-1, keepdims=True)
    return xc * lax.rsqrt(var + LN_EPS) * g + b


def _rms_norm(x, g):
    return x * lax.rsqrt(jnp.mean(x * x, axis=-1, keepdims=True) + RMS_EPS) * g


def _proj_kernel(x_ref, w_ref, o_ref):
    o_ref[...] = _dot(x_ref[...].astype(BF16), w_ref[...])


def _proj(x, w, tm):
    m, k = x.shape
    n = w.shape[1]
    return pl.pallas_call(
        _proj_kernel,
        out_shape=jax.ShapeDtypeStruct((m, n), F32),
        grid=(m // tm,),
        in_specs=[pl.BlockSpec((tm, k), lambda i: (i, 0)),
                  pl.BlockSpec((k, n), lambda i: (0, 0))],
        out_specs=pl.BlockSpec((tm, n), lambda i: (i, 0)),
        compiler_params=_cparams(1),
        name="proj",
    )(x, w)


def _mla_in_kernel(x_ref, wa_ref, qg_ref, kvg_ref, wq_ref, wuk_ref, cos_ref, sin_ref,
                   qlat_ref, qrope_ref, c_ref, cbf_ref, kr_ref, krbf_ref, dt_ref):
    xb = x_ref[...].astype(BF16)
    pa = _dot(xb, wa_ref[...])
    q_a = pa[:, 0:256]
    kv_a = pa[:, 256:512]
    kr_raw = pa[:, 512:640]
    kr_rot = pa[:, 640:768]
    dt_ref[...] = pa[:, 768:896]
    cos = cos_ref[...]
    sin = sin_ref[...]
    c = _rms_norm(kv_a, kvg_ref[...])
    c_ref[...] = c
    cbf_ref[...] = c.astype(BF16)
    kr = kr_raw * cos + kr_rot * sin
    kr_ref[...] = kr[:, 0:ROPE_DIM]
    krbf_ref[...] = kr[:, 0:ROPE_DIM].astype(BF16)
    qn = _rms_norm(q_a, qg_ref[...]).astype(BF16)
    qq = _dot(qn, wq_ref[...])
    cos4 = jnp.concatenate([cos] * 4, axis=1)
    sin4 = jnp.concatenate([sin] * 4, axis=1)
    qr = (qq[:, 1024:1536] * cos4 + qq[:, 1536:2048] * sin4) * ATTN_SCALE
    for h in range(MLA_HEADS):
        qrope_ref[h] = qr[:, h * ROPE_DIM:(h + 1) * ROPE_DIM].astype(BF16)
        nope = qq[:, h * QK_NOPE:(h + 1) * QK_NOPE].astype(BF16)
        qlat_ref[h] = (_dot(nope, wuk_ref[h]) * ATTN_SCALE).astype(BF16)


def _mla_in(x, wa, qg, kvg, wq, wuk, cos_t, sin_t, tm):
    m = x.shape[0]
    row = lambda w: pl.BlockSpec((tm, w), lambda i: (i, 0))
    full2 = lambda a: pl.BlockSpec(a.shape, lambda i: (0, 0))
    full3 = lambda a: pl.BlockSpec(a.shape, lambda i: (0, 0, 0))
    return pl.pallas_call(
        _mla_in_kernel,
        out_shape=(jax.ShapeDtypeStruct((MLA_HEADS, m, KV_LORA), BF16),
                   jax.ShapeDtypeStruct((MLA_HEADS, m, ROPE_DIM), BF16),
                   jax.ShapeDtypeStruct((m, KV_LORA), F32),
                   jax.ShapeDtypeStruct((m, KV_LORA), BF16),
                   jax.ShapeDtypeStruct((m, ROPE_DIM), F32),
                   jax.ShapeDtypeStruct((m, ROPE_DIM), BF16),
                   jax.ShapeDtypeStruct((m, LANES), F32)),
        grid=(m // tm,),
        in_specs=[row(D_MODEL), full2(wa), full2(qg), full2(kvg), full2(wq), full3(wuk),
                  row(LANES), row(LANES)],
        out_specs=(pl.BlockSpec((MLA_HEADS, tm, KV_LORA), lambda i: (0, i, 0)),
                   pl.BlockSpec((MLA_HEADS, tm, ROPE_DIM), lambda i: (0, i, 0)),
                   row(KV_LORA), row(KV_LORA), row(ROPE_DIM), row(ROPE_DIM), row(LANES)),
        compiler_params=_cparams(1),
        name="mla_in",
    )(x, wa, qg, kvg, wq, wuk, cos_t, sin_t)


def _expand_heads(v, e_ref, terms):
    out = None
    rem = v
    for _ in range(terms):
        part = rem.astype(BF16)
        rem = rem - part.astype(F32)
        d = _dot(part, e_ref[...])
        out = d if out is None else out + d
    return out


def _ssd_kernel(*refs, t_in, has_h0):
    n_pairs = SSD_HEADS // 2
    refs = list(refs)
    hT = refs[-n_pairs:]
    refs = refs[:-n_pairs]
    if has_h0:
        (xbc_ref, z_ref, dt_ref, conv0_ref, h0_ref, cw_ref, cb_ref, dtb_ref, a_ref, dx_ref, ng_ref, e_ref,
         y_ref, hfin_ref, ctail_ref, ext, xcv, ybuf, acs_s, acsT_s, dtT_s, eax_s, wtx_s) = refs
    else:
        (xbc_ref, z_ref, dt_ref, conv0_ref, cw_ref, cb_ref, dtb_ref, a_ref, dx_ref, ng_ref, e_ref,
         y_ref, hfin_ref, ctail_ref, ext, xcv, ybuf, acs_s, acsT_s, dtT_s, eax_s, wtx_s) = refs
        h0_ref = None
    L = SSD_CHUNK
    c = pl.program_id(1)
    nc = pl.num_programs(1)

    @pl.when(c == 0)
    def _():
        ext[0:SUBLANES, :] = conv0_ref[0]
        if t_in < L:
            ext[SUBLANES:SUBLANES + L, :] = jnp.zeros((L, CONV_DIM), F32)
        for j in range(n_pairs):
            if has_h0:
                hT[j][...] = h0_ref[0, j * LANES:(j + 1) * LANES, :].T
            else:
                hT[j][...] = jnp.zeros((D_STATE, LANES), F32)

    ext[SUBLANES:SUBLANES + t_in, :] = xbc_ref[...]

    @pl.when(c == nc - 1)
    def _():
        ctail_ref[0] = ext[t_in:t_in + SUBLANES, :]

    for j in range(CONV_DIM // LANES):
        sl = slice(j * LANES, (j + 1) * LANES)
        acc = cb_ref[:, sl] + ext[pl.ds(SUBLANES - 3, L), sl] * cw_ref[0:1, sl]
        acc = acc + ext[pl.ds(SUBLANES - 2, L), sl] * cw_ref[1:2, sl]
        acc = acc + ext[pl.ds(SUBLANES - 1, L), sl] * cw_ref[2:3, sl]
        acc = acc + ext[pl.ds(SUBLANES, L), sl] * cw_ref[3:4, sl]
        xcv[:, sl] = acc * _sigmoid(acc)
    ext[0:SUBLANES, :] = ext[L:L + SUBLANES, :]

    row = lax.broadcasted_iota(jnp.int32, (L, L), 0)
    col = lax.broadcasted_iota(jnp.int32, (L, L), 1)
    tri = row >= col
    tri_b = jnp.where(tri, 1.0, 0.0).astype(BF16)
    if t_in < L:
        dt_raw = jnp.concatenate([dt_ref[...], jnp.zeros((L - t_in, LANES), F32)], axis=0)
    else:
        dt_raw = dt_ref[...]
    xs = dt_raw + dtb_ref[...]
    dtv = jnp.maximum(xs, 0.0) + jnp.log1p(jnp.exp(-jnp.abs(xs)))
    if t_in < L:
        dtv = jnp.where(lax.broadcasted_iota(jnp.int32, (L, LANES), 0) < t_in, dtv, 0.0)
    da = dtv * a_ref[...]
    hi, mid, lo = _split3(da)
    acs = _dot(tri_b, hi) + _dot(tri_b, mid) + _dot(tri_b, lo)
    acs_s[...] = acs
    acsT_s[...] = acs.T
    dtT_s[...] = dtv.T
    eax_s[...] = _expand_heads(jnp.exp(acs), e_ref, 3)
    wtx_s[...] = _expand_heads(jnp.exp(acs[L - 1:L, :] - acs) * dtv, e_ref, 2)

    low_half = lax.broadcasted_iota(jnp.int32, (L, LANES), 1) < SSD_HEAD_DIM
    pairs_per_group = HEADS_PER_GROUP // 2
    for g in range(SSD_GROUPS):
        bo = D_INNER + g * D_STATE
        co = D_INNER + SSD_GROUPS * D_STATE + g * D_STATE
        bm = xcv[:, bo:bo + D_STATE]
        cmb = xcv[:, co:co + D_STATE].astype(BF16)
        cb = _dot_nt(cmb, bm.astype(BF16))
        bmT = bm.T.astype(BF16)
        for k in range(pairs_per_group):
            kk = g * pairs_per_group + k
            tl = slice(kk * LANES, (kk + 1) * LANES)
            x2 = xcv[:, tl]
            ws = []
            for idx in (2 * kk, 2 * kk + 1):
                seg = acs_s[:, idx:idx + 1] - acsT_s[idx:idx + 1, :]
                w = cb * jnp.exp(jnp.where(tri, seg, NEG)) * dtT_s[idx:idx + 1, :]
                ws.append(w.astype(BF16))
            w2 = jnp.concatenate(ws, axis=1)
            xbd = jnp.concatenate([jnp.where(low_half, x2, 0.0).astype(BF16),
                                   jnp.where(low_half, 0.0, x2).astype(BF16)], axis=0)
            yd = _dot(w2, xbd)
            hp = hT[kk][...]
            yo = _dot(cmb, hp.astype(BF16))
            ybuf[:, tl] = yd + yo * eax_s[:, tl] + dx_ref[:, tl] * x2
            s = _dot(bmT, (x2 * wtx_s[:, tl]).astype(BF16))
            hT[kk][...] = eax_s[L - 1:L, tl] * hp + s

    gw = D_INNER // SSD_GROUPS
    for g in range(SSD_GROUPS):
        sl = slice(g * gw, (g + 1) * gw)
        zg = z_ref[:, sl]
        v = ybuf[0:t_in, sl] * (zg * _sigmoid(zg))
        ms = jnp.mean(v * v, axis=-1, keepdims=True)
        y_ref[:, sl] = (v * lax.rsqrt(ms + RMS_EPS) * ng_ref[:, sl]).astype(y_ref.dtype)

    @pl.when(c == nc - 1)
    def _():
        for j in range(n_pairs):
            hfin_ref[0, j * LANES:(j + 1) * LANES, :] = hT[j][...].T


def _ssd(xbc, z, dt, conv0, h0, cw, cb, dtb, a_neg, d_x, ng, e_mat, *, nb, nc, t_in, row0, y_dtype):
    L = SSD_CHUNK
    blk0 = row0 // t_in
    has_h0 = h0 is not None
    rowspec = lambda w: pl.BlockSpec((t_in, w), lambda b, c: (blk0 + b * nc + c, 0))
    par = lambda a: pl.BlockSpec(a.shape, lambda b, c: (0, 0))
    in_specs = [rowspec(CONV_DIM), rowspec(D_INNER), rowspec(LANES),
                pl.BlockSpec((1, SUBLANES, CONV_DIM), lambda b, c: (b, 0, 0))]
    args = [xbc, z, dt, conv0]
    if has_h0:
        in_specs.append(pl.BlockSpec((1, D_INNER, D_STATE), lambda b, c: (b, 0, 0)))
        args.append(h0)
    params = [cw, cb, dtb, a_neg, d_x, ng, e_mat]
    in_specs += [par(p) for p in params]
    args += params
    return pl.pallas_call(
        functools.partial(_ssd_kernel, t_in=t_in, has_h0=has_h0),
        out_shape=(jax.ShapeDtypeStruct((nb * nc * t_in, D_INNER), y_dtype),
                   jax.ShapeDtypeStruct((nb, D_INNER, D_STATE), F32),
                   jax.ShapeDtypeStruct((nb, SUBLANES, CONV_DIM), F32)),
        grid=(nb, nc),
        in_specs=in_specs,
        out_specs=(pl.BlockSpec((t_in, D_INNER), lambda b, c: (b * nc + c, 0)),
                   pl.BlockSpec((1, D_INNER, D_STATE), lambda b, c: (b, 0, 0)),
                   pl.BlockSpec((1, SUBLANES, CONV_DIM), lambda b, c: (b, 0, 0))),
        scratch_shapes=[pltpu.VMEM((L + 2 * SUBLANES, CONV_DIM), F32),
                        pltpu.VMEM((L, CONV_DIM), F32),
                        pltpu.VMEM((L, D_INNER), F32),
                        pltpu.VMEM((L, LANES), F32), pltpu.VMEM((LANES, L), F32), pltpu.VMEM((LANES, L), F32),
                        pltpu.VMEM((L, D_INNER), F32), pltpu.VMEM((L, D_INNER), F32)]
                       + [pltpu.VMEM((D_STATE, LANES), F32)] * (SSD_HEADS // 2),
        compiler_params=_cparams(2),
        name="ssd",
    )(*args)


def _softmax_step(s, kc, m_scr, l_scr, acc_scr):
    reps = s.shape[1] // LANES
    m_prev = m_scr[...]
    m_new = jnp.maximum(m_prev, jnp.max(s, axis=1, keepdims=True))
    alpha = jnp.exp(m_prev - m_new)
    p = jnp.exp(s - (jnp.concatenate([m_new] * reps, axis=1) if reps > 1 else m_new))
    l_scr[...] = alpha * l_scr[...] + jnp.sum(p, axis=1, keepdims=True)
    acc_scr[...] = acc_scr[...] * jnp.concatenate([alpha] * (KV_LORA // LANES), axis=1) + _dot(p.astype(BF16), kc)
    m_scr[...] = m_new


def _attn_prompt_kernel(qlat_ref, qrope_ref, c_ref, kr_ref, wuv_ref, o_ref, m_scr, l_scr, acc_scr, *, tq):
    qi = pl.program_id(1)
    R = MLA_HEADS * tq
    q = qlat_ref[...].reshape(R, KV_LORA)
    qr = qrope_ref[...].reshape(R, ROPE_DIM)
    m_scr[...] = jnp.full_like(m_scr, NEG)
    l_scr[...] = jnp.zeros_like(l_scr)
    acc_scr[...] = jnp.zeros_like(acc_scr)

    def scores(j):
        start = pl.multiple_of(j * tq, tq)
        kc = c_ref[pl.ds(start, tq), :]
        kr = kr_ref[pl.ds(start, tq), :]
        return _dot_nt(q, kc) + _dot_nt(qr, kr), kc

    def body(j, carry):
        s, kc = scores(j)
        _softmax_step(s, kc, m_scr, l_scr, acc_scr)
        return carry

    lax.fori_loop(0, qi, body, 0)
    s, kc = scores(qi)
    t = lax.broadcasted_iota(jnp.int32, (R, tq), 0) & (tq - 1)
    kcol = lax.broadcasted_iota(jnp.int32, (R, tq), 1)
    s = jnp.where(kcol <= t, s, NEG)
    _softmax_step(s, kc, m_scr, l_scr, acc_scr)

    inv = 1.0 / l_scr[...]
    o_lat = acc_scr[...] * jnp.concatenate([inv] * (KV_LORA // LANES), axis=1)
    for h in range(MLA_HEADS):
        oh = _dot(o_lat[h * tq:(h + 1) * tq, :].astype(BF16), wuv_ref[h])
        o_ref[:, h * V_HEAD:(h + 1) * V_HEAD] = oh.astype(o_ref.dtype)


def _attn_prompt(qlat, qrope, cbf, krbf, wuv, *, nb, t, tq):
    nq = t // tq
    R = MLA_HEADS * tq
    return pl.pallas_call(
        functools.partial(_attn_prompt_kernel, tq=tq),
        out_shape=jax.ShapeDtypeStruct((nb * t, MLA_HEADS * V_HEAD), BF16),
        grid=(nb, nq),
        in_specs=[pl.BlockSpec((MLA_HEADS, tq, KV_LORA), lambda b, i: (0, b * nq + i, 0)),
                  pl.BlockSpec((MLA_HEADS, tq, ROPE_DIM), lambda b, i: (0, b * nq + i, 0)),
                  pl.BlockSpec((t, KV_LORA), lambda b, i: (b, 0)),
                  pl.BlockSpec((t, ROPE_DIM), lambda b, i: (b, 0)),
                  pl.BlockSpec(wuv.shape, lambda b, i: (0, 0, 0))],
        out_specs=pl.BlockSpec((tq, MLA_HEADS * V_HEAD), lambda b, i: (b * nq + i, 0)),
        scratch_shapes=[pltpu.VMEM((R, LANES), F32), pltpu.VMEM((R, LANES), F32),
                        pltpu.VMEM((R, KV_LORA), F32)],
        compiler_params=_cparams(2),
        name="attn_prompt",
    )(qlat, qrope, cbf, krbf, wuv)


def _attn_sample_kernel(pt_ref, qlat_ref, qrope_ref, cnew_ref, krnew_ref, *rest, n_pg, t_new):
    kv_refs = rest[:n_pg]
    kr_refs = rest[n_pg:2 * n_pg]
    o_ref, m_scr, l_scr, acc_scr = rest[2 * n_pg:]
    j = pl.program_id(1)
    R = qlat_ref.shape[1]
    q = qlat_ref[0]
    qr = qrope_ref[0]

    @pl.when(j == 0)
    def _():
        m_scr[...] = jnp.full_like(m_scr, NEG)
        l_scr[...] = jnp.zeros_like(l_scr)
        acc_scr[...] = jnp.zeros_like(acc_scr)
        kc = cnew_ref[0].astype(BF16)
        kr = krnew_ref[0].astype(BF16)
        s = _dot_nt(q, kc) + _dot(qr, kr)
        t = lax.broadcasted_iota(jnp.int32, (R, LANES), 0) & (t_new - 1)
        kcol = lax.broadcasted_iota(jnp.int32, (R, LANES), 1)
        s = jnp.where(kcol <= t, s, NEG)
        _softmax_step(s, kc, m_scr, l_scr, acc_scr)

    kc = jnp.concatenate([kv_refs[g][...].astype(BF16) for g in range(n_pg)], axis=0)
    kr = jnp.concatenate([kr_refs[g][...].astype(BF16) for g in range(n_pg)], axis=1)
    s = _dot_nt(q, kc) + _dot(qr, kr)
    _softmax_step(s, kc, m_scr, l_scr, acc_scr)

    @pl.when(j == pl.num_programs(1) - 1)
    def _():
        inv = 1.0 / l_scr[...]
        o_ref[0] = acc_scr[...] * jnp.concatenate([inv] * (KV_LORA // LANES), axis=1)


def _attn_sample(page_table, qlat, qrope, cnew, krnew, cache_kv, cache_kr, *, t_new):
    nb, n_pages = page_table.shape
    n_pg = PAGES_PER_STEP
    R = qlat.shape[1]
    page = cache_kv.shape[1]

    def pg_spec(g, r, w):
        return pl.BlockSpec((None, r, w), lambda b, j, pt: (pt[b, j * n_pg + g], 0, 0))

    per_b = lambda r, w: pl.BlockSpec((1, r, w), lambda b, j, pt: (b, 0, 0))
    grid_spec = pltpu.PrefetchScalarGridSpec(
        num_scalar_prefetch=1,
        grid=(nb, n_pages // n_pg),
        in_specs=[per_b(R, KV_LORA), per_b(R, ROPE_DIM), per_b(LANES, KV_LORA), per_b(ROPE_DIM, LANES)]
                 + [pg_spec(g, page, KV_LORA) for g in range(n_pg)]
                 + [pg_spec(g, ROPE_DIM, page) for g in range(n_pg)],
        out_specs=per_b(R, KV_LORA),
        scratch_shapes=[pltpu.VMEM((R, LANES), F32), pltpu.VMEM((R, LANES), F32),
                        pltpu.VMEM((R, KV_LORA), F32)],
    )
    return pl.pallas_call(
        functools.partial(_attn_sample_kernel, n_pg=n_pg, t_new=t_new),
        out_shape=jax.ShapeDtypeStruct((nb, R, KV_LORA), F32),
        grid_spec=grid_spec,
        compiler_params=_cparams(2),
        name="attn_sample",
    )(page_table, qlat, qrope, cnew, krnew, *([cache_kv] * n_pg), *([cache_kr] * n_pg))


def _head_proj_kernel(x_ref, w_ref, o_ref):
    o_ref[...] = _dot(x_ref[0].astype(BF16), w_ref[0]).astype(o_ref.dtype)


def _head_proj(x, w):
    h, m, k = x.shape
    n = w.shape[2]
    return pl.pallas_call(
        _head_proj_kernel,
        out_shape=jax.ShapeDtypeStruct((m, h * n), BF16),
        grid=(h,),
        in_specs=[pl.BlockSpec((1, m, k), lambda i: (i, 0, 0)),
                  pl.BlockSpec((1, k, n), lambda i: (i, 0, 0))],
        out_specs=pl.BlockSpec((m, n), lambda i: (0, i)),
        compiler_params=_cparams(1),
        name="head_proj",
    )(x, w)


def _merge_kernel(ya_p_ref, ya_s_ref, o_p_ref, o_s_ref, g_ref, u_ref, wa_ref, wb_ref, wo_ref,
                  lg_ref, lb_ref, rwh_ref, rwl_ref, rb_ref,
                  h_ref, h3_ref, ti_ref, tg_ref, *, n_prompt_tiles):
    i = pl.program_id(0)
    is_p = i < n_prompt_tiles
    yn = jnp.where(is_p, ya_p_ref[...], ya_s_ref[...])
    ob = jnp.where(is_p, o_p_ref[...], o_s_ref[...])
    ya = _dot(yn, wa_ref[...])
    yb = _dot(ob, wb_ref[...])
    g = g_ref[...]
    merged = _sigmoid(g[:, 0:D_MODEL]) * ya + _sigmoid(g[:, D_MODEL:2 * D_MODEL]) * yb
    mix = _dot(merged.astype(BF16), wo_ref[...])
    h = _layer_norm(DN_ALPHA * u_ref[...] + mix, lg_ref[...], lb_ref[...])
    h_ref[...] = h
    tm = h.shape[0]
    for j in range(D_MODEL // LANES):
        h3_ref[pl.ds(j, tm, stride=SUBLANES), :] = h[:, j * LANES:(j + 1) * LANES]
    hh = h.astype(BF16)
    hl = (h - hh.astype(F32)).astype(BF16)
    logits = _dot(hh, rwh_ref[...]) + _dot(hl, rwh_ref[...]) + _dot(hh, rwl_ref[...]) + rb_ref[...]
    lane = lax.broadcasted_iota(jnp.int32, logits.shape, 1).astype(F32)
    vals = jnp.zeros_like(logits)
    idxs = jnp.zeros_like(logits)
    cur = logits
    for k in range(TOP_K):
        mx = jnp.max(cur, axis=1, keepdims=True)
        ix = jnp.min(jnp.where(cur == mx, lane, float(LANES)), axis=1, keepdims=True)
        vals = jnp.where(lane == k, mx, vals)
        idxs = jnp.where(lane == k, ix, idxs)
        cur = jnp.where(lane == ix, NEG * 2.0, cur)
    v0 = vals[:, 0:1]
    e = jnp.where(lane < TOP_K, jnp.exp(vals - v0), 0.0)
    tg_ref[...] = e / jnp.sum(e, axis=1, keepdims=True)
    ti_ref[...] = idxs.astype(jnp.int32)


def _merge(ya_p, ya_s, o_p, o_s, gates, u, wa, wb, wo, lg, lb, rwh, rwl, rb, tm):
    m = u.shape[0]
    npt = ya_p.shape[0] // tm
    row = lambda w: pl.BlockSpec((tm, w), lambda i: (i, 0))
    prow = lambda w: pl.BlockSpec((tm, w), lambda i: (jnp.minimum(i, npt - 1), 0))
    srow = lambda w: pl.BlockSpec((tm, w), lambda i: (jnp.maximum(i - npt, 0), 0))
    par = lambda a: pl.BlockSpec(a.shape, lambda i: (0, 0))
    return pl.pallas_call(
        functools.partial(_merge_kernel, n_prompt_tiles=npt),
        out_shape=(jax.ShapeDtypeStruct((m, D_MODEL), F32),
                   jax.ShapeDtypeStruct((m * SUBLANES, LANES), F32),
                   jax.ShapeDtypeStruct((m, LANES), jnp.int32),
                   jax.ShapeDtypeStruct((m, LANES), F32)),
        grid=(m // tm,),
        in_specs=[prow(D_INNER), srow(D_INNER), prow(D_MODEL), srow(D_MODEL), row(2 * D_MODEL), row(D_MODEL),
                  par(wa), par(wb), par(wo), par(lg), par(lb), par(rwh), par(rwl), par(rb)],
        out_specs=(row(D_MODEL), pl.BlockSpec((tm * SUBLANES, LANES), lambda i: (i, 0)), row(LANES), row(LANES)),
        compiler_params=_cparams(1),
        name="merge",
    )(ya_p, ya_s, o_p, o_s, gates, u, wa, wb, wo, lg, lb, rwh, rwl, rb)


DEINT_IN = 2 * LANES
DEINT_COLS = 512


def _deint_kernel(w_ref, p_ref, g_ref, u_ref):
    for t in range(DEINT_COLS // DEINT_IN):
        r = _dot(w_ref[:, t * DEINT_IN:(t + 1) * DEINT_IN].astype(BF16), p_ref[...])
        g_ref[:, t * LANES:(t + 1) * LANES] = r[:, 0:LANES].astype(BF16)
        u_ref[:, t * LANES:(t + 1) * LANES] = r[:, LANES:DEINT_IN].astype(BF16)


def _deinterleave(w):
    e, k, f2 = w.shape
    perm = np.zeros((DEINT_IN, DEINT_IN), np.float32)
    perm[2 * np.arange(LANES), np.arange(LANES)] = 1.0
    perm[2 * np.arange(LANES) + 1, LANES + np.arange(LANES)] = 1.0
    half = DEINT_COLS // 2
    out = jax.ShapeDtypeStruct((e, k, f2 // 2), BF16)
    return pl.pallas_call(
        _deint_kernel,
        out_shape=(out, out),
        grid=(e, f2 // DEINT_COLS),
        in_specs=[pl.BlockSpec((None, k, DEINT_COLS), lambda i, c: (i, 0, c)),
                  pl.BlockSpec((DEINT_IN, DEINT_IN), lambda i, c: (0, 0))],
        out_specs=(pl.BlockSpec((None, k, half), lambda i, c: (i, 0, c)),
                   pl.BlockSpec((None, k, half), lambda i, c: (i, 0, c))),
        compiler_params=_cparams(2),
        name="deinterleave",
    )(w, jnp.asarray(perm, BF16))


def _row_pipeline(i, n_active, idx_hbm, src_hbm, idx_smem, idx_sem, buf, row_sem, rows, compute):
    slot = lax.rem(i, 2)

    def idx_copy(chunk, s):
        return pltpu.make_async_copy(idx_hbm.at[chunk], idx_smem.at[s], idx_sem.at[s])

    def issue_row(s, r):
        src0 = pl.multiple_of(idx_smem[s, r] * SUBLANES, SUBLANES)
        dst0 = pl.multiple_of(r * SUBLANES, SUBLANES)
        pltpu.make_async_copy(src_hbm.at[pl.ds(src0, SUBLANES), :],
                              buf.at[s, pl.ds(dst0, SUBLANES), :], row_sem.at[s]).start()

    @pl.when(i == 0)
    def _():
        idx_copy(0, 0).start()
        idx_copy(0, 0).wait()

        def body(r, carry):
            issue_row(0, r)
            return carry
        lax.fori_loop(0, rows, body, 0, unroll=8)

        @pl.when(1 < n_active)
        def _():
            idx_copy(1, 1).start()

    has_next = i + 1 < n_active

    @pl.when(has_next)
    def _():
        idx_copy(i + 1, 1 - slot).wait()

    @pl.when(i + 2 < n_active)
    def _():
        idx_copy(i + 2, slot).start()

    @pl.when(i < n_active)
    def _():
        pltpu.make_async_copy(src_hbm.at[pl.ds(0, rows * SUBLANES), :], buf.at[slot], row_sem.at[slot]).wait()

    @pl.when(has_next)
    def _():
        for r in range(rows):
            issue_row(1 - slot, r)
        compute(slot)

    @pl.when(i + 1 == n_active)
    def _():
        compute(slot)


def _fetch_scratch(rows):
    return [pltpu.SMEM((2, rows), jnp.int32), pltpu.SemaphoreType.DMA((2,)),
            pltpu.VMEM((2, rows * SUBLANES, LANES), F32), pltpu.SemaphoreType.DMA((2,))]


def _ffn_kernel(be_ref, nu_ref, tok_hbm, h3_hbm, wg_ref, wu_ref, bgu_ref, wdn_ref, bdn_ref, o_ref,
                idx_smem, idx_sem, xbuf, row_sem, *, tb):
    i = pl.program_id(0)
    nu = nu_ref[0]

    def compute(slot):
        x = jnp.concatenate([xbuf[slot, pl.ds(j, tb, stride=SUBLANES), :].astype(BF16)
                             for j in range(D_MODEL // LANES)], axis=1)
        bgu = bgu_ref[...]
        g = jnp.minimum(_dot(x, wg_ref[...]) + bgu[:, 0:D_FF], SWIGLU_LIMIT)
        u = jnp.clip(_dot(x, wu_ref[...]) + bgu[:, D_FF:2 * D_FF], -SWIGLU_LIMIT, SWIGLU_LIMIT)
        hh = (u + 1.0) * (g * _sigmoid(SWIGLU_ALPHA * g))
        out = _dot(hh.astype(BF16), wdn_ref[...]) + bdn_ref[...]
        for j in range(D_MODEL // LANES):
            o_ref[pl.ds(j, tb, stride=SUBLANES), :] = out[:, j * LANES:(j + 1) * LANES]

    _row_pipeline(i, nu, tok_hbm, h3_hbm, idx_smem, idx_sem, xbuf, row_sem, tb, compute)

    @pl.when(i >= nu)
    def _():
        o_ref[...] = jnp.zeros_like(o_ref)


def _ffn(blk_e, n_used, row_tok, h3, wg, wu, bgu, wdn, bdn, tb):
    cap = row_tok.shape[0]
    nblk = cap // tb
    wspec = lambda a: pl.BlockSpec((None,) + a.shape[1:], lambda i, be, nu: (be[i], 0, 0))
    grid_spec = pltpu.PrefetchScalarGridSpec(
        num_scalar_prefetch=2,
        grid=(nblk,),
        in_specs=[pl.BlockSpec(memory_space=pl.ANY), pl.BlockSpec(memory_space=pl.ANY),
                  wspec(wg), wspec(wu), wspec(bgu), wspec(wdn), wspec(bdn)],
        out_specs=pl.BlockSpec((tb * SUBLANES, LANES), lambda i, be, nu: (i, 0)),
        scratch_shapes=_fetch_scratch(tb),
    )
    return pl.pallas_call(
        functools.partial(_ffn_kernel, tb=tb),
        out_shape=jax.ShapeDtypeStruct((cap * SUBLANES, LANES), F32),
        grid_spec=grid_spec,
        compiler_params=_cparams(1),
        name="moe_ffn",
    )(blk_e, n_used, row_tok.reshape(nblk, tb), h3, wg, wu, bgu, wdn, bdn)


def _combine_kernel(dest_hbm, yr3_hbm, tg_ref, h_ref, lg_ref, lb_ref, op_ref, os_ref,
                    idx_smem, idx_sem, ybuf, row_sem, *, tm, n_prompt_tiles):
    i = pl.program_id(0)
    rows = tm * TOP_K

    def compute(slot):
        tg = tg_ref[...]
        stride = TOP_K * SUBLANES
        pieces = []
        for j in range(D_MODEL // LANES):
            acc = tg[:, 0:1] * ybuf[slot, pl.ds(j, tm, stride=stride), :]
            for k in range(1, TOP_K):
                acc = acc + tg[:, k:k + 1] * ybuf[slot, pl.ds(k * SUBLANES + j, tm, stride=stride), :]
            pieces.append(acc)
        y = jnp.concatenate(pieces, axis=1)
        out = _layer_norm(DN_ALPHA * h_ref[...] + y, lg_ref[...], lb_ref[...])

        @pl.when(i < n_prompt_tiles)
        def _():
            op_ref[...] = out

        @pl.when(i >= n_prompt_tiles)
        def _():
            os_ref[...] = out

    _row_pipeline(i, pl.num_programs(0), dest_hbm, yr3_hbm, idx_smem, idx_sem, ybuf, row_sem, rows, compute)


def _combine(dest, yr3, tg, h, lg, lb, n_prompt, tm):
    m = h.shape[0]
    npt = n_prompt // tm
    row = lambda w: pl.BlockSpec((tm, w), lambda i: (i, 0))
    par = lambda a: pl.BlockSpec(a.shape, lambda i: (0, 0))
    return pl.pallas_call(
        functools.partial(_combine_kernel, tm=tm, n_prompt_tiles=npt),
        out_shape=(jax.ShapeDtypeStruct((n_prompt, D_MODEL), F32),
                   jax.ShapeDtypeStruct((m - n_prompt, D_MODEL), F32)),
        grid=(m // tm,),
        in_specs=[pl.BlockSpec(memory_space=pl.ANY), pl.BlockSpec(memory_space=pl.ANY),
                  row(LANES), row(D_MODEL), par(lg), par(lb)],
        out_specs=(pl.BlockSpec((tm, D_MODEL), lambda i: (jnp.minimum(i, npt - 1), 0)),
                   pl.BlockSpec((tm, D_MODEL), lambda i: (jnp.maximum(i - npt, 0), 0))),
        scratch_shapes=_fetch_scratch(tm * TOP_K),
        compiler_params=_cparams(1),
        name="combine",
    )(dest.reshape(m // tm, tm * TOP_K), yr3, tg, h, lg, lb)


def _rope_tables(pos):
    half = ROPE_DIM // 2
    inv = ROPE_THETA ** (-jnp.arange(half, dtype=F32) / half)
    ang = pos.astype(F32)[:, None] * inv[None, :]
    cos, sin = jnp.cos(ang), jnp.sin(ang)
    return jnp.concatenate([cos] * 4, axis=1), jnp.concatenate([-sin, sin] * 2, axis=1)


def _pad_cols(w, width):
    return jnp.pad(w, ((0, 0), (0, width - w.shape[1])))


def _routing(top_i, n_blocks, tb):
    n = top_i.shape[0]
    nk = n * TOP_K
    flat_e = top_i.reshape(-1)
    onehot = (flat_e[:, None] == jnp.arange(N_EXPERTS, dtype=jnp.int32)[None, :]).astype(jnp.int32)
    csum = jnp.cumsum(onehot, axis=0)
    rank = jnp.sum(onehot * csum, axis=1) - 1
    counts = csum[-1]
    nblk_e = (counts + tb - 1) // tb
    pend = jnp.cumsum(nblk_e)
    pstart = (pend - nblk_e) * tb
    dest = (pstart[flat_e] + rank).astype(jnp.int32)
    flat_t = jnp.arange(nk, dtype=jnp.int32) // TOP_K
    row_tok = jnp.zeros((n_blocks * tb,), jnp.int32).at[dest].set(flat_t, unique_indices=True)
    n_used = pend[-1].astype(jnp.int32)
    bidx = jnp.minimum(jnp.arange(n_blocks, dtype=jnp.int32), n_used - 1)
    blk_e = jnp.minimum(jnp.sum((pend[None, :] <= bidx[:, None]).astype(jnp.int32), axis=1), N_EXPERTS - 1)
    return row_tok, dest, blk_e, n_used.reshape(1)


def kernel(x_prompt, x_sample, cache_kv_latent, cache_k_rope, state_ssm, state_conv, page_table, w_in, ssd_conv_w, ssd_conv_b, ssd_dt_bias, ssd_a_log, ssd_d, ssd_norm_g, mla_q_norm_g, w_q_b, mla_kv_norm_g, w_kv_b, w_branch_a, w_branch_b, w_out, ln1_g, ln1_b, router_w, router_b, exp_w_gu, exp_b_gu, exp_w_down, exp_b_down, ln2_g, ln2_b):
    b_p, t_p, _ = x_prompt.shape
    b_s, t_s, _ = x_sample.shape
    n_pages = page_table.shape[1]
    page = cache_kv_latent.shape[2]
    past_len = n_pages * page
    n_p = b_p * t_p
    n_s = b_s * t_s
    n = n_p + n_s
    assert DEPTH == 1 and w_in.shape[0] == 1
    assert n_p % PROJ_TILE == 0 or n % PROJ_TILE == 0
    assert n_p % ROW_TILE == 0 and n_s % ROW_TILE == 0 and t_p % ATTN_TQ == 0 and t_p % SSD_CHUNK == 0
    assert t_s == SUBLANES and n_pages % PAGES_PER_STEP == 0 and page == LANES

    wi = w_in[0]
    seg = lambda k: wi[:, IN_OFFS[k]:IN_OFFS[k + 1]]
    half = ROPE_DIM // 2
    w_kr = seg(2)
    w_kr_rot = jnp.concatenate([w_kr[:, half:], w_kr[:, :half]], axis=1)
    w_mla = jnp.concatenate([seg(0), seg(1), _pad_cols(w_kr, LANES), _pad_cols(w_kr_rot, LANES),
                             _pad_cols(seg(5), LANES)], axis=1).astype(BF16)
    w_z = seg(3).astype(BF16)
    w_xbc = seg(4).astype(BF16)
    w_gates = jnp.concatenate([seg(6), seg(7)], axis=1).astype(BF16)
    wq3 = w_q_b[0].reshape(Q_LORA, MLA_HEADS, QK_NOPE + ROPE_DIM)
    wq_nope = wq3[:, :, :QK_NOPE].reshape(Q_LORA, MLA_HEADS * QK_NOPE)
    wq_rope = wq3[:, :, QK_NOPE:]
    wq_rope_rot = jnp.concatenate([wq_rope[..., half:], wq_rope[..., :half]], axis=-1)
    wq = jnp.concatenate([wq_nope, wq_rope.reshape(Q_LORA, -1), wq_rope_rot.reshape(Q_LORA, -1)],
                         axis=1).astype(BF16)
    wkv3 = w_kv_b[0].reshape(KV_LORA, MLA_HEADS, QK_NOPE + V_HEAD)
    wuk = jnp.transpose(wkv3[:, :, :QK_NOPE], (1, 2, 0)).astype(BF16)
    wuv = jnp.transpose(wkv3[:, :, QK_NOPE:], (1, 0, 2)).astype(BF16)
    qg = mla_q_norm_g[0].reshape(1, Q_LORA)
    kvg = mla_kv_norm_g[0].reshape(1, KV_LORA)
    cw = ssd_conv_w[0]
    cb = ssd_conv_b[0].reshape(1, CONV_DIM)
    dtb = _pad_cols(ssd_dt_bias[0].reshape(1, SSD_HEADS), LANES)
    a_neg = _pad_cols(-jnp.exp(ssd_a_log[0].astype(F32)).reshape(1, SSD_HEADS), LANES)
    d_x = jnp.repeat(ssd_d[0].astype(F32), SSD_HEAD_DIM).reshape(1, D_INNER)
    e_mat = jnp.asarray(np.arange(LANES)[:, None] == np.arange(D_INNER)[None, :] // SSD_HEAD_DIM, BF16)
    ng = ssd_norm_g[0].reshape(1, D_INNER)
    wa = w_branch_a[0].astype(BF16)
    wb = w_branch_b[0].astype(BF16)
    wo = w_out[0].astype(BF16)
    rw = _pad_cols(router_w[0], LANES)
    rwh = rw.astype(BF16)
    rwl = (rw - rwh.astype(F32)).astype(BF16)
    rb = jnp.concatenate([router_b[0].reshape(1, N_EXPERTS),
                          jnp.full((1, LANES - N_EXPERTS), NEG, F32)], axis=1)
    wg, wu = _deinterleave(exp_w_gu[0])
    bgu = jnp.concatenate([exp_b_gu[0][:, 0::2], exp_b_gu[0][:, 1::2]], axis=1).reshape(N_EXPERTS, 1, 2 * D_FF)
    wdn = exp_w_down[0].astype(BF16)
    bdn = exp_b_down[0].reshape(N_EXPERTS, 1, D_MODEL)
    row2 = lambda v: v[0].reshape(1, D_MODEL)

    pos = jnp.concatenate([jnp.tile(jnp.arange(t_p, dtype=jnp.int32), b_p),
                           jnp.tile(past_len + jnp.arange(t_s, dtype=jnp.int32), b_s)])
    cos_t, sin_t = _rope_tables(pos)

    x = jnp.concatenate([x_prompt.reshape(n_p, D_MODEL), x_sample.reshape(n_s, D_MODEL)], axis=0)
    qlat, qrope, c_new, c_bf, kr_new, kr_bf, dt_raw = _mla_in(x, w_mla, qg, kvg, wq, wuk, cos_t, sin_t, ROW_TILE)
    z = _proj(x, w_z, PROJ_TILE)
    xbc = _proj(x, w_xbc, PROJ_TILE)
    gates = _proj(x, w_gates, PROJ_TILE)

    conv0_p = jnp.zeros((b_p, SUBLANES, CONV_DIM), F32)
    conv0_s = jnp.pad(state_conv[0], ((0, 0), (SUBLANES - (D_CONV - 1), 0), (0, 0)))
    ya_p, ssm_p, ctail_p = _ssd(xbc, z, dt_raw, conv0_p, None, cw, cb, dtb, a_neg, d_x, ng, e_mat,
                                nb=b_p, nc=t_p // SSD_CHUNK, t_in=SSD_CHUNK, row0=0, y_dtype=BF16)
    ya_s, ssm_s, ctail_s = _ssd(xbc, z, dt_raw, conv0_s, state_ssm[0].reshape(b_s, D_INNER, D_STATE),
                                cw, cb, dtb, a_neg, d_x, ng, e_mat,
                                nb=b_s, nc=1, t_in=t_s, row0=n_p, y_dtype=F32)
    conv_p = ctail_p[:, SUBLANES - (D_CONV - 1):, :]
    conv_s = ctail_s[:, SUBLANES - (D_CONV - 1):, :]

    o_p = _attn_prompt(qlat, qrope, c_bf, kr_bf, wuv, nb=b_p, t=t_p, tq=ATTN_TQ)
    ql_s = jnp.transpose(qlat[:, n_p:, :].reshape(MLA_HEADS, b_s, t_s, KV_LORA), (1, 0, 2, 3))
    qr_s = jnp.transpose(qrope[:, n_p:, :].reshape(MLA_HEADS, b_s, t_s, ROPE_DIM), (1, 0, 2, 3))
    cnew_s = jnp.pad(c_new[n_p:].reshape(b_s, t_s, KV_LORA), ((0, 0), (0, LANES - t_s), (0, 0)))
    krnew_s = jnp.swapaxes(jnp.pad(kr_new[n_p:].reshape(b_s, t_s, ROPE_DIM), ((0, 0), (0, LANES - t_s), (0, 0))), 1, 2)
    olat_s = _attn_sample(page_table, ql_s.reshape(b_s, MLA_HEADS * t_s, KV_LORA),
                          qr_s.reshape(b_s, MLA_HEADS * t_s, ROPE_DIM), cnew_s, krnew_s,
                          cache_kv_latent[0], jnp.swapaxes(cache_k_rope[0], 1, 2), t_new=t_s)
    olat_s = jnp.transpose(olat_s.reshape(b_s, MLA_HEADS, t_s, KV_LORA), (1, 0, 2, 3)).reshape(MLA_HEADS, n_s, KV_LORA)
    o_s = _head_proj(olat_s, wuv)

    h, h3, top_i, top_g = _merge(ya_p, ya_s.astype(BF16), o_p, o_s, gates, x, wa, wb, wo,
                             row2(ln1_g), row2(ln1_b), rwh, rwl, rb, ROW_TILE)

    tb = MOE_TB
    n_blocks = -(-(n * TOP_K + N_EXPERTS * (tb - 1)) // tb)
    row_tok, dest, blk_e, n_used = _routing(top_i[:, :TOP_K], n_blocks, tb)
    yr3 = _ffn(blk_e, n_used, row_tok, h3, wg, wu, bgu, wdn, bdn, tb)
    y_p, y_s = _combine(dest, yr3, top_g, h, row2(ln2_g), row2(ln2_b), n_p, COMBINE_TILE)

    return (y_p.reshape(b_p, t_p, D_MODEL), y_s.reshape(b_s, t_s, D_MODEL),
            c_new[:n_p].reshape(1, b_p, t_p, KV_LORA), kr_new[:n_p].reshape(1, b_p, t_p, ROPE_DIM),
            ssm_p.reshape(1, b_p, SSD_HEADS, SSD_HEAD_DIM, D_STATE), conv_p[None],
            c_new[n_p:].reshape(1, b_s, t_s, KV_LORA), kr_new[n_p:].reshape(1, b_s, t_s, ROPE_DIM),
            ssm_s.reshape(1, b_s, SSD_HEADS, SSD_HEAD_DIM, D_STATE), conv_s[None])
```

```python
import functools

import jax
import jax.numpy as jnp
import numpy as np
from jax import lax
from jax.experimental import pallas as pl
from jax.experimental.pallas import tpu as pltpu

F32 = jnp.float32
BF16 = jnp.bfloat16

D_MODEL = 1024
D_INNER = 2048
SSD_HEAD_DIM = 64
SSD_HEADS = 32
SSD_GROUPS = 4
HEADS_PER_GROUP = 8
D_STATE = 128
D_CONV = 4
CONV_DIM = D_INNER + 2 * SSD_GROUPS * D_STATE
SSD_CHUNK = 128
MLA_HEADS = 8
Q_LORA = 256
KV_LORA = 256
QK_NOPE = 128
ROPE_DIM = 64
V_HEAD = 128
ROPE_THETA = 10000.0
ATTN_SCALE = (QK_NOPE + ROPE_DIM) ** -0.5
N_EXPERTS = 32
TOP_K = 4
D_FF = 1024
SWIGLU_ALPHA = 1.702
SWIGLU_LIMIT = 7.0
DEPTH = 1
DN_ALPHA = (2 * DEPTH) ** 0.25
LN_EPS = 1e-5
RMS_EPS = 1e-6
IN_SIZES = (Q_LORA, KV_LORA, ROPE_DIM, D_INNER, CONV_DIM, SSD_HEADS, D_MODEL, D_MODEL)
IN_OFFS = tuple(int(v) for v in np.cumsum((0,) + IN_SIZES))

LANES = 128
SUBLANES = 8
VMEM_LIMIT = 56 * 1024 * 1024

NEG = -1e30

ROW_TILE = 256
PROJ_TILE = 512
ATTN_TQ = 256
PAGE_GROUP = 32
MOE_TB = 256
COMBINE_TILE = 128
MLA_A_COLS = 896


def _cparams(n_axes):
    return pltpu.CompilerParams(dimension_semantics=("arbitrary",) * n_axes,
                                vmem_limit_bytes=VMEM_LIMIT)


def _dot(a, b):
    return jnp.dot(a, b, preferred_element_type=F32)


def _dot_nt(a, b):
    return lax.dot_general(a, b, (((1,), (1,)), ((), ())), preferred_element_type=F32)


def _split3(x):
    hi = x.astype(BF16)
    r1 = x - hi.astype(F32)
    mid = r1.astype(BF16)
    lo = (r1 - mid.astype(F32)).astype(BF16)
    return hi, mid, lo


def _sigmoid(x):
    return 1.0 / (1.0 + jnp.exp(-x))


def _layer_norm(x, g, b):
    mu = jnp.mean(x, axis=-1, keepdims=True)
    xc = x - mu
    var = jnp.mean(xc * xc, axis=-1, keepdims=True)
    return xc * lax.rsqrt(var + LN_EPS) * g + b


def _rms_norm(x, g):
    return x * lax.rsqrt(jnp.mean(x * x, axis=-1, keepdims=True) + RMS_EPS) * g


def _proj_kernel(x_ref, w_ref, o_ref):
    o_ref[...] = _dot(x_ref[...].astype(BF16), w_ref[...])


def _proj(x, w, tm):
    m, k = x.shape
    n = w.shape[1]
    return pl.pallas_call(
        _proj_kernel,
        out_shape=jax.ShapeDtypeStruct((m, n), F32),
        grid=(m // tm,),
        in_specs=[pl.BlockSpec((tm, k), lambda i: (i, 0)),
                  pl.BlockSpec((k, n), lambda i: (0, 0))],
        out_specs=pl.BlockSpec((tm, n), lambda i: (i, 0)),
        compiler_params=_cparams(1),
        name="proj",
    )(x, w)


def _mla_in_kernel(x_ref, wa_ref, qg_ref, kvg_ref, wq_ref, wuk_ref, cos_ref, sin_ref,
                   qlat_ref, qrope_ref, c_ref, cbf_ref, kr_ref, krbf_ref, dt_ref):
    xb = x_ref[...].astype(BF16)
    pa = _dot(xb, wa_ref[...])
    q_a = pa[:, 0:256]
    kv_a = pa[:, 256:512]
    kr_raw = pa[:, 512:640]
    kr_rot = pa[:, 640:768]
    dt_ref[...] = pa[:, 768:896]
    cos = cos_ref[...]
    sin = sin_ref[...]
    c = _rms_norm(kv_a, kvg_ref[...])
    c_ref[...] = c
    cbf_ref[...] = c.astype(BF16)
    kr = kr_raw * cos + kr_rot * sin
    kr_ref[...] = kr[:, 0:ROPE_DIM]
    krbf_ref[...] = kr[:, 0:ROPE_DIM].astype(BF16)
    qn = _rms_norm(q_a, qg_ref[...]).astype(BF16)
    qq = _dot(qn, wq_ref[...])
    cos4 = jnp.concatenate([cos] * 4, axis=1)
    sin4 = jnp.concatenate([sin] * 4, axis=1)
    qr = (qq[:, 1024:1536] * cos4 + qq[:, 1536:2048] * sin4) * ATTN_SCALE
    for h in range(MLA_HEADS):
        qrope_ref[h] = qr[:, h * ROPE_DIM:(h + 1) * ROPE_DIM].astype(BF16)
        nope = qq[:, h * QK_NOPE:(h + 1) * QK_NOPE].astype(BF16)
        qlat_ref[h] = (_dot(nope, wuk_ref[h]) * ATTN_SCALE).astype(BF16)


def _mla_in(x, wa, qg, kvg, wq, wuk, cos_t, sin_t, tm):
    m = x.shape[0]
    tab_blocks = cos_t.shape[0] // tm
    row = lambda w: pl.BlockSpec((tm, w), lambda i: (i, 0))
    tab = pl.BlockSpec((tm, LANES), lambda i: (i % tab_blocks, 0))
    full2 = lambda a: pl.BlockSpec(a.shape, lambda i: (0, 0))
    full3 = lambda a: pl.BlockSpec(a.shape, lambda i: (0, 0, 0))
    return pl.pallas_call(
        _mla_in_kernel,
        out_shape=(jax.ShapeDtypeStruct((MLA_HEADS, m, KV_LORA), BF16),
                   jax.ShapeDtypeStruct((MLA_HEADS, m, ROPE_DIM), BF16),
                   jax.ShapeDtypeStruct((m, KV_LORA), F32),
                   jax.ShapeDtypeStruct((m, KV_LORA), BF16),
                   jax.ShapeDtypeStruct((m, ROPE_DIM), F32),
                   jax.ShapeDtypeStruct((m, ROPE_DIM), BF16),
                   jax.ShapeDtypeStruct((m, LANES), F32)),
        grid=(m // tm,),
        in_specs=[row(D_MODEL), full2(wa), full2(qg), full2(kvg), full2(wq), full3(wuk), tab, tab],
        out_specs=(pl.BlockSpec((MLA_HEADS, tm, KV_LORA), lambda i: (0, i, 0)),
                   pl.BlockSpec((MLA_HEADS, tm, ROPE_DIM), lambda i: (0, i, 0)),
                   row(KV_LORA), row(KV_LORA), row(ROPE_DIM), row(ROPE_DIM), row(LANES)),
        compiler_params=_cparams(1),
        name="mla_in",
    )(x, wa, qg, kvg, wq, wuk, cos_t, sin_t)


def _expand_heads(v, e_ref, terms):
    out = None
    rem = v
    for _ in range(terms):
        part = rem.astype(BF16)
        rem = rem - part.astype(F32)
        d = _dot(part, e_ref[...])
        out = d if out is None else out + d
    return out


def _ssd_kernel(*refs, t_in, has_h0):
    n_pairs = SSD_HEADS // 2
    refs = list(refs)
    hT = refs[-n_pairs:]
    refs = refs[:-n_pairs]
    if has_h0:
        (xbc_ref, z_ref, dt_ref, conv0_ref, h0_ref, cw_ref, cb_ref, dtb_ref, a_ref, dx_ref, ng_ref, e_ref,
         y_ref, hfin_ref, ctail_ref, ext, xcv, ybuf, acs_s, acsT_s, dtT_s, eax_s, wtx_s) = refs
    else:
        (xbc_ref, z_ref, dt_ref, conv0_ref, cw_ref, cb_ref, dtb_ref, a_ref, dx_ref, ng_ref, e_ref,
         y_ref, hfin_ref, ctail_ref, ext, xcv, ybuf, acs_s, acsT_s, dtT_s, eax_s, wtx_s) = refs
        h0_ref = None
    L = SSD_CHUNK
    c = pl.program_id(1)
    nc = pl.num_programs(1)

    @pl.when(c == 0)
    def _():
        ext[0:SUBLANES, :] = conv0_ref[0]
        if t_in < L:
            ext[SUBLANES:SUBLANES + L, :] = jnp.zeros((L, CONV_DIM), F32)
        for j in range(n_pairs):
            if has_h0:
                hT[j][...] = h0_ref[0, j * LANES:(j + 1) * LANES, :].T
            else:
                hT[j][...] = jnp.zeros((D_STATE, LANES), F32)

    ext[SUBLANES:SUBLANES + t_in, :] = xbc_ref[...]

    @pl.when(c == nc - 1)
    def _():
        ctail_ref[0] = ext[t_in:t_in + SUBLANES, :]

    for j in range(CONV_DIM // LANES):
        sl = slice(j * LANES, (j + 1) * LANES)
        acc = cb_ref[:, sl] + ext[pl.ds(SUBLANES - 3, L), sl] * cw_ref[0:1, sl]
        acc = acc + ext[pl.ds(SUBLANES - 2, L), sl] * cw_ref[1:2, sl]
        acc = acc + ext[pl.ds(SUBLANES - 1, L), sl] * cw_ref[2:3, sl]
        acc = acc + ext[pl.ds(SUBLANES, L), sl] * cw_ref[3:4, sl]
        xcv[:, sl] = acc * _sigmoid(acc)
    ext[0:SUBLANES, :] = ext[L:L + SUBLANES, :]

    row = lax.broadcasted_iota(jnp.int32, (L, L), 0)
    col = lax.broadcasted_iota(jnp.int32, (L, L), 1)
    tri = row >= col
    tri_b = jnp.where(tri, 1.0, 0.0).astype(BF16)
    if t_in < L:
        dt_raw = jnp.concatenate([dt_ref[...], jnp.zeros((L - t_in, LANES), F32)], axis=0)
    else:
        dt_raw = dt_ref[...]
    xs = dt_raw + dtb_ref[...]
    dtv = jnp.maximum(xs, 0.0) + jnp.log1p(jnp.exp(-jnp.abs(xs)))
    if t_in < L:
        dtv = jnp.where(lax.broadcasted_iota(jnp.int32, (L, LANES), 0) < t_in, dtv, 0.0)
    da = dtv * a_ref[...]
    hi, mid, lo = _split3(da)
    acs = _dot(tri_b, hi) + _dot(tri_b, mid) + _dot(tri_b, lo)
    acs_s[...] = acs
    acsT_s[...] = acs.T
    dtT_s[...] = dtv.T
    eax_s[...] = _expand_heads(jnp.exp(acs), e_ref, 3)
    wtx_s[...] = _expand_heads(jnp.exp(acs[L - 1:L, :] - acs) * dtv, e_ref, 2)

    low_half = lax.broadcasted_iota(jnp.int32, (L, LANES), 1) < SSD_HEAD_DIM
    pairs_per_group = HEADS_PER_GROUP // 2
    for g in range(SSD_GROUPS):
        bo = D_INNER + g * D_STATE
        co = D_INNER + SSD_GROUPS * D_STATE + g * D_STATE
        bm = xcv[:, bo:bo + D_STATE]
        cmb = xcv[:, co:co + D_STATE].astype(BF16)
        cb = _dot_nt(cmb, bm.astype(BF16))
        bmT = bm.T.astype(BF16)
        for k in range(pairs_per_group):
            kk = g * pairs_per_group + k
            tl = slice(kk * LANES, (kk + 1) * LANES)
            x2 = xcv[:, tl]
            ws = []
            for idx in (2 * kk, 2 * kk + 1):
                seg = acs_s[:, idx:idx + 1] - acsT_s[idx:idx + 1, :]
                w = cb * jnp.exp(jnp.where(tri, seg, NEG)) * dtT_s[idx:idx + 1, :]
                ws.append(w.astype(BF16))
            w2 = jnp.concatenate(ws, axis=1)
            xbd = jnp.concatenate([jnp.where(low_half, x2, 0.0).astype(BF16),
                                   jnp.where(low_half, 0.0, x2).astype(BF16)], axis=0)
            yd = _dot(w2, xbd)
            hp = hT[kk][...]
            yo = _dot(cmb, hp.astype(BF16))
            ybuf[:, tl] = yd + yo * eax_s[:, tl] + dx_ref[:, tl] * x2
            s = _dot(bmT, (x2 * wtx_s[:, tl]).astype(BF16))
            hT[kk][...] = eax_s[L - 1:L, tl] * hp + s

    gw = D_INNER // SSD_GROUPS
    for g in range(SSD_GROUPS):
        sl = slice(g * gw, (g + 1) * gw)
        zg = z_ref[:, sl]
        v = ybuf[0:t_in, sl] * (zg * _sigmoid(zg))
        ms = jnp.mean(v * v, axis=-1, keepdims=True)
        y_ref[:, sl] = (v * lax.rsqrt(ms + RMS_EPS) * ng_ref[:, sl]).astype(y_ref.dtype)

    @pl.when(c == nc - 1)
    def _():
        for j in range(n_pairs):
            hfin_ref[0, j * LANES:(j + 1) * LANES, :] = hT[j][...].T


def _ssd(xbc, z, dt, conv0, h0, cw, cb, dtb, a_neg, d_x, ng, e_mat, *, nb, nc, t_in, row0, y_dtype):
    L = SSD_CHUNK
    blk0 = row0 // t_in
    has_h0 = h0 is not None
    rowspec = lambda w: pl.BlockSpec((t_in, w), lambda b, c: (blk0 + b * nc + c, 0))
    par = lambda a: pl.BlockSpec(a.shape, lambda b, c: (0, 0))
    in_specs = [rowspec(CONV_DIM), rowspec(D_INNER), rowspec(LANES),
                pl.BlockSpec((1, SUBLANES, CONV_DIM), lambda b, c: (b, 0, 0))]
    args = [xbc, z, dt, conv0]
    if has_h0:
        in_specs.append(pl.BlockSpec((1, D_INNER, D_STATE), lambda b, c: (b, 0, 0)))
        args.append(h0)
    params = [cw, cb, dtb, a_neg, d_x, ng, e_mat]
    in_specs += [par(p) for p in params]
    args += params
    return pl.pallas_call(
        functools.partial(_ssd_kernel, t_in=t_in, has_h0=has_h0),
        out_shape=(jax.ShapeDtypeStruct((nb * nc * t_in, D_INNER), y_dtype),
                   jax.ShapeDtypeStruct((nb, D_INNER, D_STATE), F32),
                   jax.ShapeDtypeStruct((nb, SUBLANES, CONV_DIM), F32)),
        grid=(nb, nc),
        in_specs=in_specs,
        out_specs=(pl.BlockSpec((t_in, D_INNER), lambda b, c: (b * nc + c, 0)),
                   pl.BlockSpec((1, D_INNER, D_STATE), lambda b, c: (b, 0, 0)),
                   pl.BlockSpec((1, SUBLANES, CONV_DIM), lambda b, c: (b, 0, 0))),
        scratch_shapes=[pltpu.VMEM((L + 2 * SUBLANES, CONV_DIM), F32),
                        pltpu.VMEM((L, CONV_DIM), F32),
                        pltpu.VMEM((L, D_INNER), F32),
                        pltpu.VMEM((L, LANES), F32), pltpu.VMEM((LANES, L), F32), pltpu.VMEM((LANES, L), F32),
                        pltpu.VMEM((L, D_INNER), F32), pltpu.VMEM((L, D_INNER), F32)]
                       + [pltpu.VMEM((D_STATE, LANES), F32)] * (SSD_HEADS // 2),
        compiler_params=_cparams(2),
        name="ssd",
    )(*args)


def _softmax_step(s, kc, m_scr, l_scr, acc_scr):
    reps = s.shape[1] // LANES
    m_prev = m_scr[...]
    m_new = jnp.maximum(m_prev, jnp.max(s, axis=1, keepdims=True))
    alpha = jnp.exp(m_prev - m_new)
    p = jnp.exp(s - (jnp.concatenate([m_new] * reps, axis=1) if reps > 1 else m_new))
    l_scr[...] = alpha * l_scr[...] + jnp.sum(p, axis=1, keepdims=True)
    acc_scr[...] = acc_scr[...] * jnp.concatenate([alpha] * (KV_LORA // LANES), axis=1) + _dot(p.astype(BF16), kc)
    m_scr[...] = m_new


def _attn_prompt_kernel(qlat_ref, qrope_ref, c_ref, kr_ref, wuv_ref, o_ref, m_scr, l_scr, acc_scr, *, tq):
    qi = pl.program_id(1)
    R = MLA_HEADS * tq

    def scores(block, width):
        start = pl.multiple_of(block * width, width)
        kc = c_ref[pl.ds(start, width), :]
        kr = kr_ref[pl.ds(start, width), :]
        q = qlat_ref[...].reshape(R, KV_LORA)
        qr = qrope_ref[...].reshape(R, ROPE_DIM)
        return _dot_nt(q, kc) + _dot_nt(qr, kr), kc

    s, kc = scores(qi, tq)
    t = lax.broadcasted_iota(jnp.int32, (R, tq), 0) & (tq - 1)
    kcol = lax.broadcasted_iota(jnp.int32, (R, tq), 1)
    s = jnp.where(kcol <= t, s, NEG)
    m0 = jnp.max(s, axis=1, keepdims=True)
    p = jnp.exp(s - m0)
    m_scr[...] = jnp.broadcast_to(m0, m_scr.shape)
    l_scr[...] = jnp.broadcast_to(jnp.sum(p, axis=1, keepdims=True), l_scr.shape)
    acc_scr[...] = _dot(p.astype(BF16), kc)

    def body(j, carry):
        s2, kc2 = scores(j, 2 * tq)
        _softmax_step(s2, kc2, m_scr, l_scr, acc_scr)
        return carry

    lax.fori_loop(0, qi // 2, body, 0)

    @pl.when(qi % 2 == 1)
    def _():
        s1, kc1 = scores(qi - 1, tq)
        _softmax_step(s1, kc1, m_scr, l_scr, acc_scr)

    inv = 1.0 / l_scr[...]
    o_lat = acc_scr[...] * jnp.concatenate([inv] * (KV_LORA // LANES), axis=1)
    for h in range(MLA_HEADS):
        oh = _dot(o_lat[h * tq:(h + 1) * tq, :].astype(BF16), wuv_ref[h])
        o_ref[:, h * V_HEAD:(h + 1) * V_HEAD] = oh.astype(o_ref.dtype)


def _attn_prompt(qlat, qrope, cbf, krbf, wuv, *, nb, t, tq):
    nq = t // tq
    R = MLA_HEADS * tq
    return pl.pallas_call(
        functools.partial(_attn_prompt_kernel, tq=tq),
        out_shape=jax.ShapeDtypeStruct((nb * t, MLA_HEADS * V_HEAD), BF16),
        grid=(nb, nq),
        in_specs=[pl.BlockSpec((MLA_HEADS, tq, KV_LORA), lambda b, i: (0, b * nq + i, 0)),
                  pl.BlockSpec((MLA_HEADS, tq, ROPE_DIM), lambda b, i: (0, b * nq + i, 0)),
                  pl.BlockSpec((t, KV_LORA), lambda b, i: (b, 0)),
                  pl.BlockSpec((t, ROPE_DIM), lambda b, i: (b, 0)),
                  pl.BlockSpec(wuv.shape, lambda b, i: (0, 0, 0))],
        out_specs=pl.BlockSpec((tq, MLA_HEADS * V_HEAD), lambda b, i: (b * nq + i, 0)),
        scratch_shapes=[pltpu.VMEM((R, LANES), F32), pltpu.VMEM((R, LANES), F32),
                        pltpu.VMEM((R, KV_LORA), F32)],
        compiler_params=_cparams(2),
        name="attn_prompt",
    )(qlat, qrope, cbf, krbf, wuv)


def _attn_sample_kernel(pt_ref, qlat_ref, qrope_ref, cnew_ref, krnew_ref, kv_hbm, kr_hbm, o_ref,
                        kvbuf, krbuf, kv_sem, kr_sem, m_scr, l_scr, acc_scr, *, n_groups, gp, t_new):
    b = pl.program_id(0)
    nb = pl.num_programs(0)
    R = qlat_ref.shape[1]
    page = kvbuf.shape[2]
    q = qlat_ref[0]
    qr = qrope_ref[0]

    def start_group(bb, g, slot):
        for p in range(gp):
            pid = pt_ref[bb, g * gp + p]
            pltpu.make_async_copy(kv_hbm.at[pid], kvbuf.at[slot, p], kv_sem.at[slot]).start()
            pltpu.make_async_copy(kr_hbm.at[pid], krbuf.at[slot, p], kr_sem.at[slot]).start()

    def wait_group(slot):
        pltpu.make_async_copy(kvbuf.at[slot], kvbuf.at[slot], kv_sem.at[slot]).wait()
        pltpu.make_async_copy(krbuf.at[slot], krbuf.at[slot], kr_sem.at[slot]).wait()

    @pl.when(b == 0)
    def _():
        start_group(0, 0, 0)

    kc = cnew_ref[0].astype(BF16)
    kr = krnew_ref[0].astype(BF16)
    s = _dot_nt(q, kc) + _dot(qr, kr)
    t = lax.broadcasted_iota(jnp.int32, (R, LANES), 0) & (t_new - 1)
    kcol = lax.broadcasted_iota(jnp.int32, (R, LANES), 1)
    s = jnp.where(kcol <= t, s, NEG)
    m0 = jnp.max(s, axis=1, keepdims=True)
    p0 = jnp.exp(s - m0)
    m_scr[...] = jnp.broadcast_to(m0, m_scr.shape)
    l_scr[...] = jnp.broadcast_to(jnp.sum(p0, axis=1, keepdims=True), l_scr.shape)
    acc_scr[...] = _dot(p0.astype(BF16), kc)

    for g in range(n_groups):
        slot = g % 2
        if g + 1 < n_groups:
            start_group(b, g + 1, 1 - slot)
        else:
            @pl.when(b + 1 < nb)
            def _():
                start_group(b + 1, 0, 1 - slot)
        wait_group(slot)
        kc = kvbuf[slot].reshape(gp * page, KV_LORA).astype(BF16)
        kr = jnp.concatenate([krbuf[slot, p].astype(BF16) for p in range(gp)], axis=1)
        s = _dot_nt(q, kc) + _dot(qr, kr)
        _softmax_step(s, kc, m_scr, l_scr, acc_scr)

    inv = 1.0 / l_scr[...]
    o_ref[0] = acc_scr[...] * jnp.concatenate([inv] * (KV_LORA // LANES), axis=1)


def _attn_sample(page_table, qlat, qrope, cnew, krnew, cache_kv, cache_kr, *, t_new, gp):
    nb, n_pages = page_table.shape
    n_groups = n_pages // gp
    R = qlat.shape[1]
    page = cache_kv.shape[1]
    per_b = lambda r, w: pl.BlockSpec((1, r, w), lambda b, pt: (b, 0, 0))
    grid_spec = pltpu.PrefetchScalarGridSpec(
        num_scalar_prefetch=1,
        grid=(nb,),
        in_specs=[per_b(R, KV_LORA), per_b(R, ROPE_DIM), per_b(LANES, KV_LORA), per_b(ROPE_DIM, LANES),
                  pl.BlockSpec(memory_space=pl.ANY), pl.BlockSpec(memory_space=pl.ANY)],
        out_specs=per_b(R, KV_LORA),
        scratch_shapes=[pltpu.VMEM((2, gp, page, KV_LORA), F32), pltpu.VMEM((2, gp, ROPE_DIM, page), F32),
                        pltpu.SemaphoreType.DMA((2,)), pltpu.SemaphoreType.DMA((2,)),
                        pltpu.VMEM((R, LANES), F32), pltpu.VMEM((R, LANES), F32),
                        pltpu.VMEM((R, KV_LORA), F32)],
    )
    return pl.pallas_call(
        functools.partial(_attn_sample_kernel, n_groups=n_groups, gp=gp, t_new=t_new),
        out_shape=jax.ShapeDtypeStruct((nb, R, KV_LORA), F32),
        grid_spec=grid_spec,
        compiler_params=_cparams(1),
        name="attn_sample",
    )(page_table, qlat, qrope, cnew, krnew, cache_kv, cache_kr)


def _head_proj_kernel(x_ref, w_ref, o_ref):
    o_ref[...] = _dot(x_ref[0].astype(BF16), w_ref[0]).astype(o_ref.dtype)


def _head_proj(x, w):
    h, m, k = x.shape
    n = w.shape[2]
    return pl.pallas_call(
        _head_proj_kernel,
        out_shape=jax.ShapeDtypeStruct((m, h * n), BF16),
        grid=(h,),
        in_specs=[pl.BlockSpec((1, m, k), lambda i: (i, 0, 0)),
                  pl.BlockSpec((1, k, n), lambda i: (i, 0, 0))],
        out_specs=pl.BlockSpec((m, n), lambda i: (0, i)),
        compiler_params=_cparams(1),
        name="head_proj",
    )(x, w)


def _merge_kernel(ya_p_ref, ya_s_ref, o_p_ref, o_s_ref, g_p_ref, g_s_ref, u_p_ref, u_s_ref, wa_ref, wb_ref, wo_ref,
                  lg_ref, lb_ref, rwh_ref, rwl_ref, rb_ref,
                  h_ref, h3_ref, ti_ref, tg_ref, *, n_prompt_tiles):
    i = pl.program_id(0)
    is_p = i < n_prompt_tiles
    yn = jnp.where(is_p, ya_p_ref[...], ya_s_ref[...])
    ob = jnp.where(is_p, o_p_ref[...], o_s_ref[...])
    g = jnp.where(is_p, g_p_ref[...], g_s_ref[...])
    u = jnp.where(is_p, u_p_ref[...], u_s_ref[...])
    ya = _dot(yn, wa_ref[...])
    yb = _dot(ob, wb_ref[...])
    merged = _sigmoid(g[:, 0:D_MODEL]) * ya + _sigmoid(g[:, D_MODEL:2 * D_MODEL]) * yb
    mix = _dot(merged.astype(BF16), wo_ref[...])
    h = _layer_norm(DN_ALPHA * u + mix, lg_ref[...], lb_ref[...])
    h_ref[...] = h
    tm = h.shape[0]
    for j in range(D_MODEL // LANES):
        h3_ref[pl.ds(j, tm, stride=SUBLANES), :] = h[:, j * LANES:(j + 1) * LANES]
    hh = h.astype(BF16)
    hl = (h - hh.astype(F32)).astype(BF16)
    logits = _dot(hh, rwh_ref[...]) + _dot(hl, rwh_ref[...]) + _dot(hh, rwl_ref[...]) + rb_ref[...]
    lane = lax.broadcasted_iota(jnp.int32, logits.shape, 1).astype(F32)
    vals = jnp.zeros_like(logits)
    idxs = jnp.zeros_like(logits)
    cur = logits
    for k in range(TOP_K):
        mx = jnp.max(cur, axis=1, keepdims=True)
        ix = jnp.min(jnp.where(cur == mx, lane, float(LANES)), axis=1, keepdims=True)
        vals = jnp.where(lane == k, mx, vals)
        idxs = jnp.where(lane == k, ix, idxs)
        cur = jnp.where(lane == ix, NEG * 2.0, cur)
    v0 = vals[:, 0:1]
    e = jnp.where(lane < TOP_K, jnp.exp(vals - v0), 0.0)
    tg_ref[...] = e / jnp.sum(e, axis=1, keepdims=True)
    ti_ref[...] = idxs.astype(jnp.int32)


def _merge(ya_p, ya_s, o_p, o_s, g_p, g_s, u_p, u_s, wa, wb, wo, lg, lb, rwh, rwl, rb, tm):
    m = u_p.shape[0] + u_s.shape[0]
    npt = ya_p.shape[0] // tm
    row = lambda w: pl.BlockSpec((tm, w), lambda i: (i, 0))
    prow = lambda w: pl.BlockSpec((tm, w), lambda i: (jnp.minimum(i, npt - 1), 0))
    srow = lambda w: pl.BlockSpec((tm, w), lambda i: (jnp.maximum(i - npt, 0), 0))
    par = lambda a: pl.BlockSpec(a.shape, lambda i: (0, 0))
    return pl.pallas_call(
        functools.partial(_merge_kernel, n_prompt_tiles=npt),
        out_shape=(jax.ShapeDtypeStruct((m, D_MODEL), F32),
                   jax.ShapeDtypeStruct((m * SUBLANES, LANES), F32),
                   jax.ShapeDtypeStruct((m, LANES), jnp.int32),
                   jax.ShapeDtypeStruct((m, LANES), F32)),
        grid=(m // tm,),
        in_specs=[prow(D_INNER), srow(D_INNER), prow(D_MODEL), srow(D_MODEL), prow(2 * D_MODEL), srow(2 * D_MODEL),
                  prow(D_MODEL), srow(D_MODEL),
                  par(wa), par(wb), par(wo), par(lg), par(lb), par(rwh), par(rwl), par(rb)],
        out_specs=(row(D_MODEL), pl.BlockSpec((tm * SUBLANES, LANES), lambda i: (i, 0)), row(LANES), row(LANES)),
        compiler_params=_cparams(1),
        name="merge",
    )(ya_p, ya_s, o_p, o_s, g_p, g_s, u_p, u_s, wa, wb, wo, lg, lb, rwh, rwl, rb)


DEINT_IN = 2 * LANES


def _deint_perm():
    perm = np.zeros((DEINT_IN, DEINT_IN), np.float32)
    perm[2 * np.arange(LANES), np.arange(LANES)] = 1.0
    perm[2 * np.arange(LANES) + 1, LANES + np.arange(LANES)] = 1.0
    return jnp.asarray(perm, BF16)


def _row_pipeline(i, n_active, idx_hbm, src_hbm, idx_smem, idx_sem, buf, row_sem, rows, compute):
    slot = lax.rem(i, 2)

    def idx_copy(chunk, s):
        return pltpu.make_async_copy(idx_hbm.at[chunk], idx_smem.at[s], idx_sem.at[s])

    def issue_row(s, r):
        src0 = pl.multiple_of(idx_smem[s, r] * SUBLANES, SUBLANES)
        dst0 = pl.multiple_of(r * SUBLANES, SUBLANES)
        pltpu.make_async_copy(src_hbm.at[pl.ds(src0, SUBLANES), :],
                              buf.at[s, pl.ds(dst0, SUBLANES), :], row_sem.at[s]).start()

    @pl.when(i == 0)
    def _():
        idx_copy(0, 0).start()
        idx_copy(0, 0).wait()

        def body(r, carry):
            issue_row(0, r)
            return carry
        lax.fori_loop(0, rows, body, 0, unroll=8)

        @pl.when(1 < n_active)
        def _():
            idx_copy(1, 1).start()

    has_next = i + 1 < n_active

    @pl.when(has_next)
    def _():
        idx_copy(i + 1, 1 - slot).wait()

    @pl.when(i + 2 < n_active)
    def _():
        idx_copy(i + 2, slot).start()

    @pl.when(i < n_active)
    def _():
        pltpu.make_async_copy(src_hbm.at[pl.ds(0, rows * SUBLANES), :], buf.at[slot], row_sem.at[slot]).wait()

    @pl.when(has_next)
    def _():
        for r in range(rows):
            issue_row(1 - slot, r)
        compute(slot)

    @pl.when(i + 1 == n_active)
    def _():
        compute(slot)


def _fetch_scratch(rows):
    return [pltpu.SMEM((2, rows), jnp.int32), pltpu.SemaphoreType.DMA((2,)),
            pltpu.VMEM((2, rows * SUBLANES, LANES), F32), pltpu.SemaphoreType.DMA((2,))]


def _ffn_kernel(be_ref, nu_ref, tok_hbm, h3_hbm, wgu_ref, bgu_ref, wdn_ref, bdn_ref, perm_ref, o_ref,
                idx_smem, idx_sem, xbuf, row_sem, wg_s, wu_s, wd_s, *, tb):
    i = pl.program_id(0)
    nu = nu_ref[0]
    new_expert = jnp.logical_or(i == 0, be_ref[i] != be_ref[jnp.maximum(i - 1, 0)])

    @pl.when(jnp.logical_and(new_expert, i < nu))
    def _():
        for t in range(2 * D_FF // DEINT_IN):
            r = _dot(wgu_ref[:, t * DEINT_IN:(t + 1) * DEINT_IN].astype(BF16), perm_ref[...])
            wg_s[:, t * LANES:(t + 1) * LANES] = r[:, 0:LANES].astype(BF16)
            wu_s[:, t * LANES:(t + 1) * LANES] = r[:, LANES:DEINT_IN].astype(BF16)
        wd_s[...] = wdn_ref[...].astype(BF16)

    def compute(slot):
        x = jnp.concatenate([xbuf[slot, pl.ds(j, tb, stride=SUBLANES), :].astype(BF16)
                             for j in range(D_MODEL // LANES)], axis=1)
        bgu = bgu_ref[...]
        g = jnp.minimum(_dot(x, wg_s[...]) + bgu[:, 0:D_FF], SWIGLU_LIMIT)
        u = jnp.clip(_dot(x, wu_s[...]) + bgu[:, D_FF:2 * D_FF], -SWIGLU_LIMIT, SWIGLU_LIMIT)
        hh = (u + 1.0) * (g * _sigmoid(SWIGLU_ALPHA * g))
        out = _dot(hh.astype(BF16), wd_s[...]) + bdn_ref[...]
        for j in range(D_MODEL // LANES):
            o_ref[pl.ds(j, tb, stride=SUBLANES), :] = out[:, j * LANES:(j + 1) * LANES]

    _row_pipeline(i, nu, tok_hbm, h3_hbm, idx_smem, idx_sem, xbuf, row_sem, tb, compute)

    @pl.when(i >= nu)
    def _():
        o_ref[...] = jnp.zeros_like(o_ref)


def _ffn(blk_e, n_used, row_tok, h3, wgu, bgu, wdn, bdn, tb):
    cap = row_tok.shape[0]
    nblk = cap // tb
    perm = _deint_perm()
    wspec = lambda a: pl.BlockSpec((None,) + a.shape[1:], lambda i, be, nu: (be[i], 0, 0))
    grid_spec = pltpu.PrefetchScalarGridSpec(
        num_scalar_prefetch=2,
        grid=(nblk,),
        in_specs=[pl.BlockSpec(memory_space=pl.ANY), pl.BlockSpec(memory_space=pl.ANY),
                  wspec(wgu), wspec(bgu), wspec(wdn), wspec(bdn),
                  pl.BlockSpec(perm.shape, lambda i, be, nu: (0, 0))],
        out_specs=pl.BlockSpec((tb * SUBLANES, LANES), lambda i, be, nu: (i, 0)),
        scratch_shapes=_fetch_scratch(tb) + [pltpu.VMEM((D_MODEL, D_FF), BF16), pltpu.VMEM((D_MODEL, D_FF), BF16),
                                             pltpu.VMEM((D_FF, D_MODEL), BF16)],
    )
    return pl.pallas_call(
        functools.partial(_ffn_kernel, tb=tb),
        out_shape=jax.ShapeDtypeStruct((cap * SUBLANES, LANES), F32),
        grid_spec=grid_spec,
        compiler_params=_cparams(1),
        name="moe_ffn",
    )(blk_e, n_used, row_tok.reshape(nblk, tb), h3, wgu, bgu, wdn, bdn, perm)


def _combine_kernel(dest_hbm, yr3_hbm, tg_ref, h_ref, lg_ref, lb_ref, op_ref, os_ref,
                    idx_smem, idx_sem, ybuf, row_sem, *, tm, n_prompt_tiles):
    i = pl.program_id(0)
    rows = tm * TOP_K

    def compute(slot):
        tg = tg_ref[...]
        stride = TOP_K * SUBLANES
        pieces = []
        for j in range(D_MODEL // LANES):
            acc = tg[:, 0:1] * ybuf[slot, pl.ds(j, tm, stride=stride), :]
            for k in range(1, TOP_K):
                acc = acc + tg[:, k:k + 1] * ybuf[slot, pl.ds(k * SUBLANES + j, tm, stride=stride), :]
            pieces.append(acc)
        y = jnp.concatenate(pieces, axis=1)
        out = _layer_norm(DN_ALPHA * h_ref[...] + y, lg_ref[...], lb_ref[...])

        @pl.when(i < n_prompt_tiles)
        def _():
            op_ref[...] = out

        @pl.when(i >= n_prompt_tiles)
        def _():
            os_ref[...] = out

    _row_pipeline(i, pl.num_programs(0), dest_hbm, yr3_hbm, idx_smem, idx_sem, ybuf, row_sem, rows, compute)


def _combine(dest, yr3, tg, h, lg, lb, n_prompt, tm):
    m = h.shape[0]
    npt = n_prompt // tm
    row = lambda w: pl.BlockSpec((tm, w), lambda i: (i, 0))
    par = lambda a: pl.BlockSpec(a.shape, lambda i: (0, 0))
    return pl.pallas_call(
        functools.partial(_combine_kernel, tm=tm, n_prompt_tiles=npt),
        out_shape=(jax.ShapeDtypeStruct((n_prompt, D_MODEL), F32),
                   jax.ShapeDtypeStruct((m - n_prompt, D_MODEL), F32)),
        grid=(m // tm,),
        in_specs=[pl.BlockSpec(memory_space=pl.ANY), pl.BlockSpec(memory_space=pl.ANY),
                  row(LANES), row(D_MODEL), par(lg), par(lb)],
        out_specs=(pl.BlockSpec((tm, D_MODEL), lambda i: (jnp.minimum(i, npt - 1), 0)),
                   pl.BlockSpec((tm, D_MODEL), lambda i: (jnp.maximum(i - npt, 0), 0))),
        scratch_shapes=_fetch_scratch(tm * TOP_K),
        compiler_params=_cparams(1),
        name="combine",
    )(dest.reshape(m // tm, tm * TOP_K), yr3, tg, h, lg, lb)


def _rope_tables(pos):
    half = ROPE_DIM // 2
    inv = ROPE_THETA ** (-jnp.arange(half, dtype=F32) / half)
    ang = pos.astype(F32)[:, None] * inv[None, :]
    cos, sin = jnp.cos(ang), jnp.sin(ang)
    return jnp.concatenate([cos] * 4, axis=1), jnp.concatenate([-sin, sin] * 2, axis=1)


def _pad_cols(w, width):
    return jnp.pad(w, ((0, 0), (0, width - w.shape[1])))


def _routing(top_i, n_blocks, tb):
    n = top_i.shape[0]
    nk = n * TOP_K
    flat_e = top_i.reshape(-1)
    onehot = (flat_e[:, None] == jnp.arange(N_EXPERTS, dtype=jnp.int32)[None, :]).astype(jnp.int32)
    csum = jnp.cumsum(onehot, axis=0)
    rank = jnp.sum(onehot * csum, axis=1) - 1
    counts = csum[-1]
    nblk_e = (counts + tb - 1) // tb
    pend = jnp.cumsum(nblk_e)
    pstart = (pend - nblk_e) * tb
    dest = (pstart[flat_e] + rank).astype(jnp.int32)
    flat_t = jnp.arange(nk, dtype=jnp.int32) // TOP_K
    row_tok = jnp.zeros((n_blocks * tb,), jnp.int32).at[dest].set(flat_t, unique_indices=True)
    n_used = pend[-1].astype(jnp.int32)
    bidx = jnp.minimum(jnp.arange(n_blocks, dtype=jnp.int32), n_used - 1)
    blk_e = jnp.minimum(jnp.sum((pend[None, :] <= bidx[:, None]).astype(jnp.int32), axis=1), N_EXPERTS - 1)
    return row_tok, dest, blk_e, n_used.reshape(1)


def kernel(x_prompt, x_sample, cache_kv_latent, cache_k_rope, state_ssm, state_conv, page_table, w_in, ssd_conv_w, ssd_conv_b, ssd_dt_bias, ssd_a_log, ssd_d, ssd_norm_g, mla_q_norm_g, w_q_b, mla_kv_norm_g, w_kv_b, w_branch_a, w_branch_b, w_out, ln1_g, ln1_b, router_w, router_b, exp_w_gu, exp_b_gu, exp_w_down, exp_b_down, ln2_g, ln2_b):
    b_p, t_p, _ = x_prompt.shape
    b_s, t_s, _ = x_sample.shape
    n_pages = page_table.shape[1]
    page = cache_kv_latent.shape[2]
    past_len = n_pages * page
    n_p = b_p * t_p
    n_s = b_s * t_s
    n = n_p + n_s
    assert DEPTH == 1 and w_in.shape[0] == 1
    assert n_p % PROJ_TILE == 0 and t_p % ROW_TILE == 0 and n_s % ROW_TILE == 0
    assert t_p % (2 * ATTN_TQ) == 0 and t_p % SSD_CHUNK == 0
    assert t_s == SUBLANES and n_pages % (2 * PAGE_GROUP) == 0 and page == LANES

    wi = w_in[0]
    seg = lambda k: wi[:, IN_OFFS[k]:IN_OFFS[k + 1]]
    half = ROPE_DIM // 2
    w_kr = seg(2)
    w_kr_rot = jnp.concatenate([w_kr[:, half:], w_kr[:, :half]], axis=1)
    w_mla = jnp.concatenate([seg(0), seg(1), _pad_cols(w_kr, LANES), _pad_cols(w_kr_rot, LANES),
                             _pad_cols(seg(5), LANES)], axis=1).astype(BF16)
    w_z = seg(3).astype(BF16)
    w_xbc = seg(4).astype(BF16)
    w_gates = jnp.concatenate([seg(6), seg(7)], axis=1).astype(BF16)
    wq3 = w_q_b[0].reshape(Q_LORA, MLA_HEADS, QK_NOPE + ROPE_DIM)
    wq_nope = wq3[:, :, :QK_NOPE].reshape(Q_LORA, MLA_HEADS * QK_NOPE)
    wq_rope = wq3[:, :, QK_NOPE:]
    wq_rope_rot = jnp.concatenate([wq_rope[..., half:], wq_rope[..., :half]], axis=-1)
    wq = jnp.concatenate([wq_nope, wq_rope.reshape(Q_LORA, -1), wq_rope_rot.reshape(Q_LORA, -1)],
                         axis=1).astype(BF16)
    wkv3 = w_kv_b[0].reshape(KV_LORA, MLA_HEADS, QK_NOPE + V_HEAD)
    wuk = jnp.transpose(wkv3[:, :, :QK_NOPE], (1, 2, 0)).astype(BF16)
    wuv = jnp.transpose(wkv3[:, :, QK_NOPE:], (1, 0, 2)).astype(BF16)
    qg = mla_q_norm_g[0].reshape(1, Q_LORA)
    kvg = mla_kv_norm_g[0].reshape(1, KV_LORA)
    cw = ssd_conv_w[0]
    cb = ssd_conv_b[0].reshape(1, CONV_DIM)
    dtb = _pad_cols(ssd_dt_bias[0].reshape(1, SSD_HEADS), LANES)
    a_neg = _pad_cols(-jnp.exp(ssd_a_log[0].astype(F32)).reshape(1, SSD_HEADS), LANES)
    d_x = jnp.repeat(ssd_d[0].astype(F32), SSD_HEAD_DIM).reshape(1, D_INNER)
    e_mat = jnp.asarray(np.arange(LANES)[:, None] == np.arange(D_INNER)[None, :] // SSD_HEAD_DIM, BF16)
    ng = ssd_norm_g[0].reshape(1, D_INNER)
    wa = w_branch_a[0].astype(BF16)
    wb = w_branch_b[0].astype(BF16)
    wo = w_out[0].astype(BF16)
    rw = _pad_cols(router_w[0], LANES)
    rwh = rw.astype(BF16)
    rwl = (rw - rwh.astype(F32)).astype(BF16)
    rb = jnp.concatenate([router_b[0].reshape(1, N_EXPERTS),
                          jnp.full((1, LANES - N_EXPERTS), NEG, F32)], axis=1)
    bgu = jnp.concatenate([exp_b_gu[0][:, 0::2], exp_b_gu[0][:, 1::2]], axis=1).reshape(N_EXPERTS, 1, 2 * D_FF)
    bdn = exp_b_down[0].reshape(N_EXPERTS, 1, D_MODEL)
    row2 = lambda v: v[0].reshape(1, D_MODEL)

    cos_p, sin_p = _rope_tables(jnp.arange(t_p, dtype=jnp.int32))
    cos_s, sin_s = _rope_tables(jnp.tile(past_len + jnp.arange(t_s, dtype=jnp.int32), b_s))

    xp = x_prompt.reshape(n_p, D_MODEL)
    xs = x_sample.reshape(n_s, D_MODEL)
    qlat_p, qrope_p, c_p, cbf_p, kr_p, krbf_p, dt_p = _mla_in(xp, w_mla, qg, kvg, wq, wuk, cos_p, sin_p, ROW_TILE)
    qlat_s, qrope_s, c_s, _, kr_s, _, dt_s = _mla_in(xs, w_mla, qg, kvg, wq, wuk, cos_s, sin_s, ROW_TILE)
    z_p, xbc_p, gates_p = (_proj(xp, w, PROJ_TILE) for w in (w_z, w_xbc, w_gates))
    z_s, xbc_s, gates_s = (_proj(xs, w, ROW_TILE) for w in (w_z, w_xbc, w_gates))

    conv0_p = jnp.zeros((b_p, SUBLANES, CONV_DIM), F32)
    conv0_s = jnp.pad(state_conv[0], ((0, 0), (SUBLANES - (D_CONV - 1), 0), (0, 0)))
    ya_p, ssm_p, ctail_p = _ssd(xbc_p, z_p, dt_p, conv0_p, None, cw, cb, dtb, a_neg, d_x, ng, e_mat,
                                nb=b_p, nc=t_p // SSD_CHUNK, t_in=SSD_CHUNK, row0=0, y_dtype=BF16)
    ya_s, ssm_s, ctail_s = _ssd(xbc_s, z_s, dt_s, conv0_s, state_ssm[0].reshape(b_s, D_INNER, D_STATE),
                                cw, cb, dtb, a_neg, d_x, ng, e_mat,
                                nb=b_s, nc=1, t_in=t_s, row0=0, y_dtype=F32)
    conv_p = ctail_p[:, SUBLANES - (D_CONV - 1):, :]
    conv_s = ctail_s[:, SUBLANES - (D_CONV - 1):, :]

    o_p = _attn_prompt(qlat_p, qrope_p, cbf_p, krbf_p, wuv, nb=b_p, t=t_p, tq=ATTN_TQ)
    ql_s = jnp.transpose(qlat_s.reshape(MLA_HEADS, b_s, t_s, KV_LORA), (1, 0, 2, 3))
    qr_s = jnp.transpose(qrope_s.reshape(MLA_HEADS, b_s, t_s, ROPE_DIM), (1, 0, 2, 3))
    cnew_s = jnp.pad(c_s.reshape(b_s, t_s, KV_LORA), ((0, 0), (0, LANES - t_s), (0, 0)))
    krnew_s = jnp.swapaxes(jnp.pad(kr_s.reshape(b_s, t_s, ROPE_DIM), ((0, 0), (0, LANES - t_s), (0, 0))), 1, 2)
    olat_s = _attn_sample(page_table, ql_s.reshape(b_s, MLA_HEADS * t_s, KV_LORA),
                          qr_s.reshape(b_s, MLA_HEADS * t_s, ROPE_DIM), cnew_s, krnew_s,
                          cache_kv_latent[0], jnp.swapaxes(cache_k_rope[0], 1, 2),
                          t_new=t_s, gp=PAGE_GROUP)
    olat_s = jnp.transpose(olat_s.reshape(b_s, MLA_HEADS, t_s, KV_LORA), (1, 0, 2, 3)).reshape(MLA_HEADS, n_s, KV_LORA)
    o_s = _head_proj(olat_s, wuv)

    h, h3, top_i, top_g = _merge(ya_p, ya_s.astype(BF16), o_p, o_s, gates_p, gates_s, xp, xs, wa, wb, wo,
                                 row2(ln1_g), row2(ln1_b), rwh, rwl, rb, ROW_TILE)

    tb = MOE_TB
    n_blocks = -(-(n * TOP_K + N_EXPERTS * (tb - 1)) // tb)
    row_tok, dest, blk_e, n_used = _routing(top_i[:, :TOP_K], n_blocks, tb)
    yr3 = _ffn(blk_e, n_used, row_tok, h3, exp_w_gu[0], bgu, exp_w_down[0], bdn, tb)
    y_p, y_s = _combine(dest, yr3, top_g, h, row2(ln2_g), row2(ln2_b), n_p, COMBINE_TILE)

    return (y_p.reshape(b_p, t_p, D_MODEL), y_s.reshape(b_s, t_s, D_MODEL),
            c_p.reshape(1, b_p, t_p, KV_LORA), kr_p.reshape(1, b_p, t_p, ROPE_DIM),
            ssm_p.reshape(1, b_p, SSD_HEADS, SSD_HEAD_DIM, D_STATE), conv_p[None],
            c_s.reshape(1, b_s, t_s, KV_LORA), kr_s.reshape(1, b_s, t_s, ROPE_DIM),
            ssm_s.reshape(1, b_s, SSD_HEADS, SSD_HEAD_DIM, D_STATE), conv_s[None])
```

```python
import functools

import jax
import jax.numpy as jnp
import numpy as np
from jax import lax
from jax.experimental import pallas as pl
from jax.experimental.pallas import tpu as pltpu

F32 = jnp.float32
BF16 = jnp.bfloat16

D_MODEL = 1024
D_INNER = 2048
SSD_HEAD_DIM = 64
SSD_HEADS = 32
SSD_GROUPS = 4
HEADS_PER_GROUP = 8
D_STATE = 128
D_CONV = 4
CONV_DIM = D_INNER + 2 * SSD_GROUPS * D_STATE
SSD_CHUNK = 128
MLA_HEADS = 8
Q_LORA = 256
KV_LORA = 256
QK_NOPE = 128
ROPE_DIM = 64
V_HEAD = 128
ROPE_THETA = 10000.0
ATTN_SCALE = (QK_NOPE + ROPE_DIM) ** -0.5
N_EXPERTS = 32
TOP_K = 4
D_FF = 1024
SWIGLU_ALPHA = 1.702
SWIGLU_LIMIT = 7.0
DEPTH = 1
DN_ALPHA = (2 * DEPTH) ** 0.25
LN_EPS = 1e-5
RMS_EPS = 1e-6
IN_SIZES = (Q_LORA, KV_LORA, ROPE_DIM, D_INNER, CONV_DIM, SSD_HEADS, D_MODEL, D_MODEL)
IN_OFFS = tuple(int(v) for v in np.cumsum((0,) + IN_SIZES))

LANES = 128
SUBLANES = 8
VMEM_LIMIT = 56 * 1024 * 1024

NEG = -1e30

ROW_TILE = 256
PROJ_TILE = 512
ATTN_TQ = 256
PAGE_GROUP = 32
MOE_TB = 256
COMBINE_TILE = 128
MLA_A_COLS = 896


def _cparams(n_axes):
    return pltpu.CompilerParams(dimension_semantics=("arbitrary",) * n_axes,
                                vmem_limit_bytes=VMEM_LIMIT)


def _dot(a, b):
    return jnp.dot(a, b, preferred_element_type=F32)


def _dot_nt(a, b):
    return lax.dot_general(a, b, (((1,), (1,)), ((), ())), preferred_element_type=F32)


def _split3(x):
    hi = x.astype(BF16)
    r1 = x - hi.astype(F32)
    mid = r1.astype(BF16)
    lo = (r1 - mid.astype(F32)).astype(BF16)
    return hi, mid, lo


def _sigmoid(x):
    return 1.0 / (1.0 + jnp.exp(-x))


def _layer_norm(x, g, b):
    mu = jnp.mean(x, axis=-1, keepdims=True)
    xc = x - mu
    var = jnp.mean(xc * xc, axis=-1, keepdims=True)
    return xc * lax.rsqrt(var + LN_EPS) * g + b


def _rms_norm(x, g):
    return x * lax.rsqrt(jnp.mean(x * x, axis=-1, keepdims=True) + RMS_EPS) * g


def _proj_kernel(x_ref, w_ref, o_ref):
    o_ref[...] = _dot(x_ref[...].astype(BF16), w_ref[...])


def _proj(x, w, tm):
    m, k = x.shape
    n = w.shape[1]
    return pl.pallas_call(
        _proj_kernel,
        out_shape=jax.ShapeDtypeStruct((m, n), F32),
        grid=(m // tm,),
        in_specs=[pl.BlockSpec((tm, k), lambda i: (i, 0)),
                  pl.BlockSpec((k, n), lambda i: (0, 0))],
        out_specs=pl.BlockSpec((tm, n), lambda i: (i, 0)),
        compiler_params=_cparams(1),
        name="proj",
    )(x, w)


def _mla_in_kernel(x_ref, wa_ref, qg_ref, kvg_ref, wq_ref, wuk_ref, cos_ref, sin_ref,
                   qlat_ref, qrope_ref, c_ref, cbf_ref, kr_ref, krbf_ref, dt_ref):
    xb = x_ref[...].astype(BF16)
    pa = _dot(xb, wa_ref[...])
    q_a = pa[:, 0:256]
    kv_a = pa[:, 256:512]
    kr_raw = pa[:, 512:640]
    kr_rot = pa[:, 640:768]
    dt_ref[...] = pa[:, 768:896]
    cos = cos_ref[...]
    sin = sin_ref[...]
    c = _rms_norm(kv_a, kvg_ref[...])
    c_ref[...] = c
    cbf_ref[...] = c.astype(BF16)
    kr = kr_raw * cos + kr_rot * sin
    kr_ref[...] = kr[:, 0:ROPE_DIM]
    krbf_ref[...] = kr[:, 0:ROPE_DIM].astype(BF16)
    qn = _rms_norm(q_a, qg_ref[...]).astype(BF16)
    qq = _dot(qn, wq_ref[...])
    cos4 = jnp.concatenate([cos] * 4, axis=1)
    sin4 = jnp.concatenate([sin] * 4, axis=1)
    qr = (qq[:, 1024:1536] * cos4 + qq[:, 1536:2048] * sin4) * ATTN_SCALE
    for h in range(MLA_HEADS):
        qrope_ref[h] = qr[:, h * ROPE_DIM:(h + 1) * ROPE_DIM].astype(BF16)
        nope = qq[:, h * QK_NOPE:(h + 1) * QK_NOPE].astype(BF16)
        qlat_ref[h] = (_dot(nope, wuk_ref[h]) * ATTN_SCALE).astype(BF16)


def _mla_in(x, wa, qg, kvg, wq, wuk, cos_t, sin_t, tm):
    m = x.shape[0]
    tab_blocks = cos_t.shape[0] // tm
    row = lambda w: pl.BlockSpec((tm, w), lambda i: (i, 0))
    tab = pl.BlockSpec((tm, LANES), lambda i: (i % tab_blocks, 0))
    full2 = lambda a: pl.BlockSpec(a.shape, lambda i: (0, 0))
    full3 = lambda a: pl.BlockSpec(a.shape, lambda i: (0, 0, 0))
    return pl.pallas_call(
        _mla_in_kernel,
        out_shape=(jax.ShapeDtypeStruct((MLA_HEADS, m, KV_LORA), BF16),
                   jax.ShapeDtypeStruct((MLA_HEADS, m, ROPE_DIM), BF16),
                   jax.ShapeDtypeStruct((m, KV_LORA), F32),
                   jax.ShapeDtypeStruct((m, KV_LORA), BF16),
                   jax.ShapeDtypeStruct((m, ROPE_DIM), F32),
                   jax.ShapeDtypeStruct((m, ROPE_DIM), BF16),
                   jax.ShapeDtypeStruct((m, LANES), F32)),
        grid=(m // tm,),
        in_specs=[row(D_MODEL), full2(wa), full2(qg), full2(kvg), full2(wq), full3(wuk), tab, tab],
        out_specs=(pl.BlockSpec((MLA_HEADS, tm, KV_LORA), lambda i: (0, i, 0)),
                   pl.BlockSpec((MLA_HEADS, tm, ROPE_DIM), lambda i: (0, i, 0)),
                   row(KV_LORA), row(KV_LORA), row(ROPE_DIM), row(ROPE_DIM), row(LANES)),
        compiler_params=_cparams(1),
        name="mla_in",
    )(x, wa, qg, kvg, wq, wuk, cos_t, sin_t)


def _expand_heads(v, e_ref, terms):
    out = None
    rem = v
    for _ in range(terms):
        part = rem.astype(BF16)
        rem = rem - part.astype(F32)
        d = _dot(part, e_ref[...])
        out = d if out is None else out + d
    return out


def _ssd_kernel(*refs, t_in, has_h0):
    n_pairs = SSD_HEADS // 2
    refs = list(refs)
    hT = refs[-n_pairs:]
    refs = refs[:-n_pairs]
    if has_h0:
        (xbc_ref, z_ref, dt_ref, conv0_ref, h0_ref, cw_ref, cb_ref, dtb_ref, a_ref, dx_ref, ng_ref, e_ref,
         y_ref, hfin_ref, ctail_ref, ext, xcv, ybuf, acs_s, acsT_s, dtT_s, eax_s, wtx_s) = refs
    else:
        (xbc_ref, z_ref, dt_ref, conv0_ref, cw_ref, cb_ref, dtb_ref, a_ref, dx_ref, ng_ref, e_ref,
         y_ref, hfin_ref, ctail_ref, ext, xcv, ybuf, acs_s, acsT_s, dtT_s, eax_s, wtx_s) = refs
        h0_ref = None
    L = SSD_CHUNK
    c = pl.program_id(1)
    nc = pl.num_programs(1)

    @pl.when(c == 0)
    def _():
        ext[0:SUBLANES, :] = conv0_ref[0]
        if t_in < L:
            ext[SUBLANES:SUBLANES + L, :] = jnp.zeros((L, CONV_DIM), F32)
        for j in range(n_pairs):
            if has_h0:
                hT[j][...] = h0_ref[0, j * LANES:(j + 1) * LANES, :].T
            else:
                hT[j][...] = jnp.zeros((D_STATE, LANES), F32)

    ext[SUBLANES:SUBLANES + t_in, :] = xbc_ref[...]

    @pl.when(c == nc - 1)
    def _():
        ctail_ref[0] = ext[t_in:t_in + SUBLANES, :]

    for j in range(CONV_DIM // LANES):
        sl = slice(j * LANES, (j + 1) * LANES)
        acc = cb_ref[:, sl] + ext[pl.ds(SUBLANES - 3, L), sl] * cw_ref[0:1, sl]
        acc = acc + ext[pl.ds(SUBLANES - 2, L), sl] * cw_ref[1:2, sl]
        acc = acc + ext[pl.ds(SUBLANES - 1, L), sl] * cw_ref[2:3, sl]
        acc = acc + ext[pl.ds(SUBLANES, L), sl] * cw_ref[3:4, sl]
        xcv[:, sl] = acc * _sigmoid(acc)
    ext[0:SUBLANES, :] = ext[L:L + SUBLANES, :]

    row = lax.broadcasted_iota(jnp.int32, (L, L), 0)
    col = lax.broadcasted_iota(jnp.int32, (L, L), 1)
    tri = row >= col
    tri_b = jnp.where(tri, 1.0, 0.0).astype(BF16)
    if t_in < L:
        dt_raw = jnp.concatenate([dt_ref[...], jnp.zeros((L - t_in, LANES), F32)], axis=0)
    else:
        dt_raw = dt_ref[...]
    xs = dt_raw + dtb_ref[...]
    dtv = jnp.maximum(xs, 0.0) + jnp.log1p(jnp.exp(-jnp.abs(xs)))
    if t_in < L:
        dtv = jnp.where(lax.broadcasted_iota(jnp.int32, (L, LANES), 0) < t_in, dtv, 0.0)
    da = dtv * a_ref[...]
    hi, mid, lo = _split3(da)
    acs = _dot(tri_b, hi) + _dot(tri_b, mid) + _dot(tri_b, lo)
    acs_s[...] = acs
    acsT_s[...] = acs.T
    dtT_s[...] = dtv.T
    eax_s[...] = _expand_heads(jnp.exp(acs), e_ref, 3)
    wtx_s[...] = _expand_heads(jnp.exp(acs[L - 1:L, :] - acs) * dtv, e_ref, 2)

    low_half = lax.broadcasted_iota(jnp.int32, (L, LANES), 1) < SSD_HEAD_DIM
    pairs_per_group = HEADS_PER_GROUP // 2
    for g in range(SSD_GROUPS):
        bo = D_INNER + g * D_STATE
        co = D_INNER + SSD_GROUPS * D_STATE + g * D_STATE
        bm = xcv[:, bo:bo + D_STATE]
        cmb = xcv[:, co:co + D_STATE].astype(BF16)
        cb = _dot_nt(cmb, bm.astype(BF16))
        bmT = bm.T.astype(BF16)
        for k in range(pairs_per_group):
            kk = g * pairs_per_group + k
            tl = slice(kk * LANES, (kk + 1) * LANES)
            x2 = xcv[:, tl]
            ws = []
            for idx in (2 * kk, 2 * kk + 1):
                seg = acs_s[:, idx:idx + 1] - acsT_s[idx:idx + 1, :]
                w = cb * jnp.exp(jnp.where(tri, seg, NEG)) * dtT_s[idx:idx + 1, :]
                ws.append(w.astype(BF16))
            w2 = jnp.concatenate(ws, axis=1)
            xbd = jnp.concatenate([jnp.where(low_half, x2, 0.0).astype(BF16),
                                   jnp.where(low_half, 0.0, x2).astype(BF16)], axis=0)
            yd = _dot(w2, xbd)
            hp = hT[kk][...]
            yo = _dot(cmb, hp.astype(BF16))
            ybuf[:, tl] = yd + yo * eax_s[:, tl] + dx_ref[:, tl] * x2
            s = _dot(bmT, (x2 * wtx_s[:, tl]).astype(BF16))
            hT[kk][...] = eax_s[L - 1:L, tl] * hp + s

    gw = D_INNER // SSD_GROUPS
    for g in range(SSD_GROUPS):
        sl = slice(g * gw, (g + 1) * gw)
        zg = z_ref[:, sl]
        v = ybuf[0:t_in, sl] * (zg * _sigmoid(zg))
        ms = jnp.mean(v * v, axis=-1, keepdims=True)
        y_ref[:, sl] = (v * lax.rsqrt(ms + RMS_EPS) * ng_ref[:, sl]).astype(y_ref.dtype)

    @pl.when(c == nc - 1)
    def _():
        for j in range(n_pairs):
            hfin_ref[0, j * LANES:(j + 1) * LANES, :] = hT[j][...].T


def _ssd(xbc, z, dt, conv0, h0, cw, cb, dtb, a_neg, d_x, ng, e_mat, *, nb, nc, t_in, row0, y_dtype):
    L = SSD_CHUNK
    blk0 = row0 // t_in
    has_h0 = h0 is not None
    rowspec = lambda w: pl.BlockSpec((t_in, w), lambda b, c: (blk0 + b * nc + c, 0))
    par = lambda a: pl.BlockSpec(a.shape, lambda b, c: (0, 0))
    in_specs = [rowspec(CONV_DIM), rowspec(D_INNER), rowspec(LANES),
                pl.BlockSpec((1, SUBLANES, CONV_DIM), lambda b, c: (b, 0, 0))]
    args = [xbc, z, dt, conv0]
    if has_h0:
        in_specs.append(pl.BlockSpec((1, D_INNER, D_STATE), lambda b, c: (b, 0, 0)))
        args.append(h0)
    params = [cw, cb, dtb, a_neg, d_x, ng, e_mat]
    in_specs += [par(p) for p in params]
    args += params
    return pl.pallas_call(
        functools.partial(_ssd_kernel, t_in=t_in, has_h0=has_h0),
        out_shape=(jax.ShapeDtypeStruct((nb * nc * t_in, D_INNER), y_dtype),
                   jax.ShapeDtypeStruct((nb, D_INNER, D_STATE), F32),
                   jax.ShapeDtypeStruct((nb, SUBLANES, CONV_DIM), F32)),
        grid=(nb, nc),
        in_specs=in_specs,
        out_specs=(pl.BlockSpec((t_in, D_INNER), lambda b, c: (b * nc + c, 0)),
                   pl.BlockSpec((1, D_INNER, D_STATE), lambda b, c: (b, 0, 0)),
                   pl.BlockSpec((1, SUBLANES, CONV_DIM), lambda b, c: (b, 0, 0))),
        scratch_shapes=[pltpu.VMEM((L + 2 * SUBLANES, CONV_DIM), F32),
                        pltpu.VMEM((L, CONV_DIM), F32),
                        pltpu.VMEM((L, D_INNER), F32),
                        pltpu.VMEM((L, LANES), F32), pltpu.VMEM((LANES, L), F32), pltpu.VMEM((LANES, L), F32),
                        pltpu.VMEM((L, D_INNER), F32), pltpu.VMEM((L, D_INNER), F32)]
                       + [pltpu.VMEM((D_STATE, LANES), F32)] * (SSD_HEADS // 2),
        compiler_params=_cparams(2),
        name="ssd",
    )(*args)


def _softmax_step(s, kc, m_scr, l_scr, acc_scr):
    reps = s.shape[1] // LANES
    m_prev = m_scr[...]
    m_new = jnp.maximum(m_prev, jnp.max(s, axis=1, keepdims=True))
    alpha = jnp.exp(m_prev - m_new)
    p = jnp.exp(s - (jnp.concatenate([m_new] * reps, axis=1) if reps > 1 else m_new))
    l_scr[...] = alpha * l_scr[...] + jnp.sum(p, axis=1, keepdims=True)
    acc_scr[...] = acc_scr[...] * jnp.concatenate([alpha] * (KV_LORA // LANES), axis=1) + _dot(p.astype(BF16), kc)
    m_scr[...] = m_new


def _attn_prompt_kernel(qlat_ref, qrope_ref, c_ref, kr_ref, wuv_ref, o_ref, m_scr, l_scr, acc_scr, *, tq):
    qi = pl.program_id(1)
    R = MLA_HEADS * tq

    def scores(block, width):
        start = pl.multiple_of(block * width, width)
        kc = c_ref[pl.ds(start, width), :]
        kr = kr_ref[pl.ds(start, width), :]
        q = qlat_ref[...].reshape(R, KV_LORA)
        qr = qrope_ref[...].reshape(R, ROPE_DIM)
        return _dot_nt(q, kc) + _dot_nt(qr, kr), kc

    s, kc = scores(qi, tq)
    t = lax.broadcasted_iota(jnp.int32, (R, tq), 0) & (tq - 1)
    kcol = lax.broadcasted_iota(jnp.int32, (R, tq), 1)
    s = jnp.where(kcol <= t, s, NEG)
    m0 = jnp.max(s, axis=1, keepdims=True)
    p = jnp.exp(s - m0)
    m_scr[...] = jnp.broadcast_to(m0, m_scr.shape)
    l_scr[...] = jnp.broadcast_to(jnp.sum(p, axis=1, keepdims=True), l_scr.shape)
    acc_scr[...] = _dot(p.astype(BF16), kc)

    def body(j, carry):
        s2, kc2 = scores(j, 2 * tq)
        _softmax_step(s2, kc2, m_scr, l_scr, acc_scr)
        return carry

    lax.fori_loop(0, qi // 2, body, 0)

    @pl.when(qi % 2 == 1)
    def _():
        s1, kc1 = scores(qi - 1, tq)
        _softmax_step(s1, kc1, m_scr, l_scr, acc_scr)

    inv = 1.0 / l_scr[...]
    o_lat = acc_scr[...] * jnp.concatenate([inv] * (KV_LORA // LANES), axis=1)
    for h in range(MLA_HEADS):
        oh = _dot(o_lat[h * tq:(h + 1) * tq, :].astype(BF16), wuv_ref[h])
        o_ref[:, h * V_HEAD:(h + 1) * V_HEAD] = oh.astype(o_ref.dtype)


def _attn_prompt(qlat, qrope, cbf, krbf, wuv, *, nb, t, tq):
    nq = t // tq
    R = MLA_HEADS * tq
    return pl.pallas_call(
        functools.partial(_attn_prompt_kernel, tq=tq),
        out_shape=jax.ShapeDtypeStruct((nb * t, MLA_HEADS * V_HEAD), BF16),
        grid=(nb, nq),
        in_specs=[pl.BlockSpec((MLA_HEADS, tq, KV_LORA), lambda b, i: (0, b * nq + i, 0)),
                  pl.BlockSpec((MLA_HEADS, tq, ROPE_DIM), lambda b, i: (0, b * nq + i, 0)),
                  pl.BlockSpec((t, KV_LORA), lambda b, i: (b, 0)),
                  pl.BlockSpec((t, ROPE_DIM), lambda b, i: (b, 0)),
                  pl.BlockSpec(wuv.shape, lambda b, i: (0, 0, 0))],
        out_specs=pl.BlockSpec((tq, MLA_HEADS * V_HEAD), lambda b, i: (b * nq + i, 0)),
        scratch_shapes=[pltpu.VMEM((R, LANES), F32), pltpu.VMEM((R, LANES), F32),
                        pltpu.VMEM((R, KV_LORA), F32)],
        compiler_params=_cparams(2),
        name="attn_prompt",
    )(qlat, qrope, cbf, krbf, wuv)


def _attn_sample_kernel(pt_ref, qlat_ref, qrope_ref, cnew_ref, krnew_ref, kv_hbm, kr_hbm, o_ref,
                        kvbuf, krbuf, kv_sem, kr_sem, m_scr, l_scr, acc_scr, *, n_groups, gp, t_new):
    b = pl.program_id(0)
    nb = pl.num_programs(0)
    R = qlat_ref.shape[1]
    page = kvbuf.shape[2]
    q = qlat_ref[0]
    qr = qrope_ref[0]

    def start_group(bb, g, slot):
        for p in range(gp):
            pid = pt_ref[bb, g * gp + p]
            pltpu.make_async_copy(kv_hbm.at[pid], kvbuf.at[slot, p], kv_sem.at[slot]).start()
            pltpu.make_async_copy(kr_hbm.at[pid], krbuf.at[slot, p], kr_sem.at[slot]).start()

    def wait_group(slot):
        pltpu.make_async_copy(kvbuf.at[slot], kvbuf.at[slot], kv_sem.at[slot]).wait()
        pltpu.make_async_copy(krbuf.at[slot], krbuf.at[slot], kr_sem.at[slot]).wait()

    @pl.when(b == 0)
    def _():
        start_group(0, 0, 0)

    kc = cnew_ref[0].astype(BF16)
    kr = krnew_ref[0].astype(BF16)
    s = _dot_nt(q, kc) + _dot(qr, kr)
    t = lax.broadcasted_iota(jnp.int32, (R, LANES), 0) & (t_new - 1)
    kcol = lax.broadcasted_iota(jnp.int32, (R, LANES), 1)
    s = jnp.where(kcol <= t, s, NEG)
    m0 = jnp.max(s, axis=1, keepdims=True)
    p0 = jnp.exp(s - m0)
    m_scr[...] = jnp.broadcast_to(m0, m_scr.shape)
    l_scr[...] = jnp.broadcast_to(jnp.sum(p0, axis=1, keepdims=True), l_scr.shape)
    acc_scr[...] = _dot(p0.astype(BF16), kc)

    for g in range(n_groups):
        slot = g % 2
        if g + 1 < n_groups:
            start_group(b, g + 1, 1 - slot)
        else:
            @pl.when(b + 1 < nb)
            def _():
                start_group(b + 1, 0, 1 - slot)
        wait_group(slot)
        kc = kvbuf[slot].reshape(gp * page, KV_LORA).astype(BF16)
        kr = jnp.concatenate([krbuf[slot, p].astype(BF16) for p in range(gp)], axis=1)
        s = _dot_nt(q, kc) + _dot(qr, kr)
        _softmax_step(s, kc, m_scr, l_scr, acc_scr)

    inv = 1.0 / l_scr[...]
    o_ref[0] = acc_scr[...] * jnp.concatenate([inv] * (KV_LORA // LANES), axis=1)


def _attn_sample(page_table, qlat, qrope, cnew, krnew, cache_kv, cache_kr, *, t_new, gp):
    nb, n_pages = page_table.shape
    n_groups = n_pages // gp
    R = qlat.shape[1]
    page = cache_kv.shape[1]
    per_b = lambda r, w: pl.BlockSpec((1, r, w), lambda b, pt: (b, 0, 0))
    grid_spec = pltpu.PrefetchScalarGridSpec(
        num_scalar_prefetch=1,
        grid=(nb,),
        in_specs=[per_b(R, KV_LORA), per_b(R, ROPE_DIM), per_b(LANES, KV_LORA), per_b(ROPE_DIM, LANES),
                  pl.BlockSpec(memory_space=pl.ANY), pl.BlockSpec(memory_space=pl.ANY)],
        out_specs=per_b(R, KV_LORA),
        scratch_shapes=[pltpu.VMEM((2, gp, page, KV_LORA), F32), pltpu.VMEM((2, gp, ROPE_DIM, page), F32),
                        pltpu.SemaphoreType.DMA((2,)), pltpu.SemaphoreType.DMA((2,)),
                        pltpu.VMEM((R, LANES), F32), pltpu.VMEM((R, LANES), F32),
                        pltpu.VMEM((R, KV_LORA), F32)],
    )
    return pl.pallas_call(
        functools.partial(_attn_sample_kernel, n_groups=n_groups, gp=gp, t_new=t_new),
        out_shape=jax.ShapeDtypeStruct((nb, R, KV_LORA), F32),
        grid_spec=grid_spec,
        compiler_params=_cparams(1),
        name="attn_sample",
    )(page_table, qlat, qrope, cnew, krnew, cache_kv, cache_kr)


def _head_proj_kernel(x_ref, w_ref, o_ref):
    o_ref[...] = _dot(x_ref[0].astype(BF16), w_ref[0]).astype(o_ref.dtype)


def _head_proj(x, w):
    h, m, k = x.shape
    n = w.shape[2]
    return pl.pallas_call(
        _head_proj_kernel,
        out_shape=jax.ShapeDtypeStruct((m, h * n), BF16),
        grid=(h,),
        in_specs=[pl.BlockSpec((1, m, k), lambda i: (i, 0, 0)),
                  pl.BlockSpec((1, k, n), lambda i: (i, 0, 0))],
        out_specs=pl.BlockSpec((m, n), lambda i: (0, i)),
        compiler_params=_cparams(1),
        name="head_proj",
    )(x, w)


def _merge_kernel(ya_p_ref, ya_s_ref, o_p_ref, o_s_ref, g_p_ref, g_s_ref, u_p_ref, u_s_ref, wa_ref, wb_ref, wo_ref,
                  lg_ref, lb_ref, rwh_ref, rwl_ref, rb_ref,
                  h_ref, h3_ref, ti_ref, tg_ref, *, n_prompt_tiles):
    i = pl.program_id(0)
    is_p = i < n_prompt_tiles
    yn = jnp.where(is_p, ya_p_ref[...], ya_s_ref[...])
    ob = jnp.where(is_p, o_p_ref[...], o_s_ref[...])
    g = jnp.where(is_p, g_p_ref[...], g_s_ref[...])
    u = jnp.where(is_p, u_p_ref[...], u_s_ref[...])
    ya = _dot(yn, wa_ref[...])
    yb = _dot(ob, wb_ref[...])
    merged = _sigmoid(g[:, 0:D_MODEL]) * ya + _sigmoid(g[:, D_MODEL:2 * D_MODEL]) * yb
    mix = _dot(merged.astype(BF16), wo_ref[...])
    h = _layer_norm(DN_ALPHA * u + mix, lg_ref[...], lb_ref[...])
    h_ref[...] = h
    tm = h.shape[0]
    for j in range(D_MODEL // LANES):
        h3_ref[pl.ds(j, tm, stride=SUBLANES), :] = h[:, j * LANES:(j + 1) * LANES]
    hh = h.astype(BF16)
    hl = (h - hh.astype(F32)).astype(BF16)
    logits = _dot(hh, rwh_ref[...]) + _dot(hl, rwh_ref[...]) + _dot(hh, rwl_ref[...]) + rb_ref[...]
    lane = lax.broadcasted_iota(jnp.int32, logits.shape, 1).astype(F32)
    vals = jnp.zeros_like(logits)
    idxs = jnp.zeros_like(logits)
    cur = logits
    for k in range(TOP_K):
        mx = jnp.max(cur, axis=1, keepdims=True)
        ix = jnp.min(jnp.where(cur == mx, lane, float(LANES)), axis=1, keepdims=True)
        vals = jnp.where(lane == k, mx, vals)
        idxs = jnp.where(lane == k, ix, idxs)
        cur = jnp.where(lane == ix, NEG * 2.0, cur)
    v0 = vals[:, 0:1]
    e = jnp.where(lane < TOP_K, jnp.exp(vals - v0), 0.0)
    tg_ref[...] = e / jnp.sum(e, axis=1, keepdims=True)
    ti_ref[...] = idxs.astype(jnp.int32)


def _merge(ya_p, ya_s, o_p, o_s, g_p, g_s, u_p, u_s, wa, wb, wo, lg, lb, rwh, rwl, rb, tm):
    m = u_p.shape[0] + u_s.shape[0]
    npt = ya_p.shape[0] // tm
    row = lambda w: pl.BlockSpec((tm, w), lambda i: (i, 0))
    prow = lambda w: pl.BlockSpec((tm, w), lambda i: (jnp.minimum(i, npt - 1), 0))
    srow = lambda w: pl.BlockSpec((tm, w), lambda i: (jnp.maximum(i - npt, 0), 0))
    par = lambda a: pl.BlockSpec(a.shape, lambda i: (0, 0))
    return pl.pallas_call(
        functools.partial(_merge_kernel, n_prompt_tiles=npt),
        out_shape=(jax.ShapeDtypeStruct((m, D_MODEL), F32),
                   jax.ShapeDtypeStruct((m * SUBLANES, LANES), F32),
                   jax.ShapeDtypeStruct((m, LANES), jnp.int32),
                   jax.ShapeDtypeStruct((m, LANES), F32)),
        grid=(m // tm,),
        in_specs=[prow(D_INNER), srow(D_INNER), prow(D_MODEL), srow(D_MODEL), prow(2 * D_MODEL), srow(2 * D_MODEL),
                  prow(D_MODEL), srow(D_MODEL),
                  par(wa), par(wb), par(wo), par(lg), par(lb), par(rwh), par(rwl), par(rb)],
        out_specs=(row(D_MODEL), pl.BlockSpec((tm * SUBLANES, LANES), lambda i: (i, 0)), row(LANES), row(LANES)),
        compiler_params=_cparams(1),
        name="merge",
    )(ya_p, ya_s, o_p, o_s, g_p, g_s, u_p, u_s, wa, wb, wo, lg, lb, rwh, rwl, rb)


DEINT_IN = 2 * LANES


def _deint_perm():
    perm = np.zeros((DEINT_IN, DEINT_IN), np.float32)
    perm[2 * np.arange(LANES), np.arange(LANES)] = 1.0
    perm[2 * np.arange(LANES) + 1, LANES + np.arange(LANES)] = 1.0
    return jnp.asarray(perm, BF16)


def _row_pipeline(i, n_active, idx_hbm, src_hbm, idx_smem, idx_sem, buf, row_sem, rows, compute):
    slot = lax.rem(i, 2)

    def idx_copy(chunk, s):
        return pltpu.make_async_copy(idx_hbm.at[chunk], idx_smem.at[s], idx_sem.at[s])

    def issue_row(s, r):
        src0 = pl.multiple_of(idx_smem[s, r] * SUBLANES, SUBLANES)
        dst0 = pl.multiple_of(r * SUBLANES, SUBLANES)
        pltpu.make_async_copy(src_hbm.at[pl.ds(src0, SUBLANES), :],
                              buf.at[s, pl.ds(dst0, SUBLANES), :], row_sem.at[s]).start()

    @pl.when(i == 0)
    def _():
        idx_copy(0, 0).start()
        idx_copy(0, 0).wait()

        def body(r, carry):
            issue_row(0, r)
            return carry
        lax.fori_loop(0, rows, body, 0, unroll=8)

        @pl.when(1 < n_active)
        def _():
            idx_copy(1, 1).start()

    has_next = i + 1 < n_active

    @pl.when(has_next)
    def _():
        idx_copy(i + 1, 1 - slot).wait()

    @pl.when(i + 2 < n_active)
    def _():
        idx_copy(i + 2, slot).start()

    @pl.when(i < n_active)
    def _():
        pltpu.make_async_copy(src_hbm.at[pl.ds(0, rows * SUBLANES), :], buf.at[slot], row_sem.at[slot]).wait()

    @pl.when(has_next)
    def _():
        for r in range(rows):
            issue_row(1 - slot, r)
        compute(slot)

    @pl.when(i + 1 == n_active)
    def _():
        compute(slot)


def _fetch_scratch(rows):
    return [pltpu.SMEM((2, rows), jnp.int32), pltpu.SemaphoreType.DMA((2,)),
            pltpu.VMEM((2, rows * SUBLANES, LANES), F32), pltpu.SemaphoreType.DMA((2,))]


def _dispatch_kernel(pad_ref, dest_hbm, h3_ref, xr_hbm, idx_smem, idx_sem, stage, row_sem, zbuf, z_sem,
                     *, tm, tb, min_used):
    i = pl.program_id(0)
    n = pl.num_programs(0)
    rows = tm * TOP_K
    slot = lax.rem(i, 2)

    def idx_copy(chunk, s):
        return pltpu.make_async_copy(dest_hbm.at[chunk], idx_smem.at[s], idx_sem.at[s])

    def rows_wait(s):
        span = xr_hbm.at[pl.ds(0, rows * SUBLANES), :]
        pltpu.make_async_copy(span, span, row_sem.at[s]).wait()

    @pl.when(i == 0)
    def _():
        zbuf[...] = jnp.zeros_like(zbuf)
        zero_to = lambda start: pltpu.make_async_copy(zbuf, xr_hbm.at[pl.ds(start, tb * SUBLANES), :], z_sem)
        for e in range(N_EXPERTS):
            zero_to(pl.multiple_of(pad_ref[e] * SUBLANES, SUBLANES)).start()
        for e in range(N_EXPERTS):
            zero_to(0).wait()
        n_used = pad_ref[N_EXPERTS]
        for blk in range(min_used, xr_hbm.shape[0] // (tb * SUBLANES)):
            @pl.when(blk >= n_used)
            def _():
                zero_to(blk * tb * SUBLANES).start()
                zero_to(0).wait()
        idx_copy(0, 0).start()

    idx_copy(i, slot).wait()

    @pl.when(i + 1 < n)
    def _():
        idx_copy(i + 1, 1 - slot).start()

    stage[slot] = h3_ref[...]
    for r in range(rows):
        dst0 = pl.multiple_of(idx_smem[slot, r] * SUBLANES, SUBLANES)
        pltpu.make_async_copy(stage.at[slot, pl.ds((r // TOP_K) * SUBLANES, SUBLANES), :],
                              xr_hbm.at[pl.ds(dst0, SUBLANES), :], row_sem.at[slot]).start()

    @pl.when(i >= 1)
    def _():
        rows_wait(1 - slot)

    @pl.when(i == n - 1)
    def _():
        rows_wait(slot)


def _dispatch(pad_start, dest, h3, cap_rows, tm, tb):
    n = h3.shape[0] // SUBLANES
    grid_spec = pltpu.PrefetchScalarGridSpec(
        num_scalar_prefetch=1,
        grid=(n // tm,),
        in_specs=[pl.BlockSpec(memory_space=pl.ANY),
                  pl.BlockSpec((tm * SUBLANES, LANES), lambda i, pad: (i, 0))],
        out_specs=pl.BlockSpec(memory_space=pl.ANY),
        scratch_shapes=[pltpu.SMEM((2, tm * TOP_K), jnp.int32), pltpu.SemaphoreType.DMA((2,)),
                        pltpu.VMEM((2, tm * SUBLANES, LANES), F32), pltpu.SemaphoreType.DMA((2,)),
                        pltpu.VMEM((tb * SUBLANES, LANES), F32), pltpu.SemaphoreType.DMA(())],
    )
    return pl.pallas_call(
        functools.partial(_dispatch_kernel, tm=tm, tb=tb, min_used=-(-(n * TOP_K) // tb)),
        out_shape=jax.ShapeDtypeStruct((cap_rows * SUBLANES, LANES), F32),
        grid_spec=grid_spec,
        compiler_params=_cparams(1),
        name="dispatch",
    )(pad_start, dest.reshape(n // tm, tm * TOP_K), h3)


def _ffn_kernel(be_ref, nu_ref, x_ref, wgu_ref, bgu_ref, wdn_ref, bdn_ref, perm_ref, o_ref,
                wg_s, wu_s, wd_s, *, tb):
    i = pl.program_id(0)
    nu = nu_ref[0]
    new_expert = jnp.logical_or(i == 0, be_ref[i] != be_ref[jnp.maximum(i - 1, 0)])

    @pl.when(jnp.logical_and(new_expert, i < nu))
    def _():
        for t in range(2 * D_FF // DEINT_IN):
            r = _dot(wgu_ref[:, t * DEINT_IN:(t + 1) * DEINT_IN].astype(BF16), perm_ref[...])
            wg_s[:, t * LANES:(t + 1) * LANES] = r[:, 0:LANES].astype(BF16)
            wu_s[:, t * LANES:(t + 1) * LANES] = r[:, LANES:DEINT_IN].astype(BF16)
        wd_s[...] = wdn_ref[...].astype(BF16)

    @pl.when(i < nu)
    def _():
        x = jnp.concatenate([x_ref[pl.ds(j, tb, stride=SUBLANES), :].astype(BF16)
                             for j in range(D_MODEL // LANES)], axis=1)
        bgu = bgu_ref[...]
        g = jnp.minimum(_dot(x, wg_s[...]) + bgu[:, 0:D_FF], SWIGLU_LIMIT)
        u = jnp.clip(_dot(x, wu_s[...]) + bgu[:, D_FF:2 * D_FF], -SWIGLU_LIMIT, SWIGLU_LIMIT)
        hh = (u + 1.0) * (g * _sigmoid(SWIGLU_ALPHA * g))
        out = _dot(hh.astype(BF16), wd_s[...]) + bdn_ref[...]
        for j in range(D_MODEL // LANES):
            o_ref[pl.ds(j, tb, stride=SUBLANES), :] = out[:, j * LANES:(j + 1) * LANES]

    @pl.when(i >= nu)
    def _():
        o_ref[...] = jnp.zeros_like(o_ref)


def _ffn(blk_e, n_used, xr3, nblk, wgu, bgu, wdn, bdn, tb):
    perm = _deint_perm()
    wspec = lambda a: pl.BlockSpec((None,) + a.shape[1:], lambda i, be, nu: (be[i], 0, 0))
    grid_spec = pltpu.PrefetchScalarGridSpec(
        num_scalar_prefetch=2,
        grid=(nblk,),
        in_specs=[pl.BlockSpec((tb * SUBLANES, LANES), lambda i, be, nu: (jnp.minimum(i, nu[0] - 1), 0)),
                  wspec(wgu), wspec(bgu), wspec(wdn), wspec(bdn),
                  pl.BlockSpec(perm.shape, lambda i, be, nu: (0, 0))],
        out_specs=pl.BlockSpec((tb * SUBLANES, LANES), lambda i, be, nu: (i, 0)),
        scratch_shapes=[pltpu.VMEM((D_MODEL, D_FF), BF16), pltpu.VMEM((D_MODEL, D_FF), BF16),
                        pltpu.VMEM((D_FF, D_MODEL), BF16)],
    )
    return pl.pallas_call(
        functools.partial(_ffn_kernel, tb=tb),
        out_shape=jax.ShapeDtypeStruct((nblk * tb * SUBLANES, LANES), F32),
        grid_spec=grid_spec,
        compiler_params=_cparams(1),
        name="moe_ffn",
    )(blk_e, n_used, xr3, wgu, bgu, wdn, bdn, perm)


def _combine_kernel(dest_hbm, yr3_hbm, tg_ref, h_ref, lg_ref, lb_ref, op_ref, os_ref,
                    idx_smem, idx_sem, ybuf, row_sem, *, tm, n_prompt_tiles):
    i = pl.program_id(0)
    rows = tm * TOP_K

    def compute(slot):
        tg = tg_ref[...]
        stride = TOP_K * SUBLANES
        pieces = []
        for j in range(D_MODEL // LANES):
            acc = tg[:, 0:1] * ybuf[slot, pl.ds(j, tm, stride=stride), :]
            for k in range(1, TOP_K):
                acc = acc + tg[:, k:k + 1] * ybuf[slot, pl.ds(k * SUBLANES + j, tm, stride=stride), :]
            pieces.append(acc)
        y = jnp.concatenate(pieces, axis=1)
        out = _layer_norm(DN_ALPHA * h_ref[...] + y, lg_ref[...], lb_ref[...])

        @pl.when(i < n_prompt_tiles)
        def _():
            op_ref[...] = out

        @pl.when(i >= n_prompt_tiles)
        def _():
            os_ref[...] = out

    _row_pipeline(i, pl.num_programs(0), dest_hbm, yr3_hbm, idx_smem, idx_sem, ybuf, row_sem, rows, compute)


def _combine(dest, yr3, tg, h, lg, lb, n_prompt, tm):
    m = h.shape[0]
    npt = n_prompt // tm
    row = lambda w: pl.BlockSpec((tm, w), lambda i: (i, 0))
    par = lambda a: pl.BlockSpec(a.shape, lambda i: (0, 0))
    return pl.pallas_call(
        functools.partial(_combine_kernel, tm=tm, n_prompt_tiles=npt),
        out_shape=(jax.ShapeDtypeStruct((n_prompt, D_MODEL), F32),
                   jax.ShapeDtypeStruct((m - n_prompt, D_MODEL), F32)),
        grid=(m // tm,),
        in_specs=[pl.BlockSpec(memory_space=pl.ANY), pl.BlockSpec(memory_space=pl.ANY),
                  row(LANES), row(D_MODEL), par(lg), par(lb)],
        out_specs=(pl.BlockSpec((tm, D_MODEL), lambda i: (jnp.minimum(i, npt - 1), 0)),
                   pl.BlockSpec((tm, D_MODEL), lambda i: (jnp.maximum(i - npt, 0), 0))),
        scratch_shapes=_fetch_scratch(tm * TOP_K),
        compiler_params=_cparams(1),
        name="combine",
    )(dest.reshape(m // tm, tm * TOP_K), yr3, tg, h, lg, lb)


def _rope_tables(pos):
    half = ROPE_DIM // 2
    inv = ROPE_THETA ** (-jnp.arange(half, dtype=F32) / half)
    ang = pos.astype(F32)[:, None] * inv[None, :]
    cos, sin = jnp.cos(ang), jnp.sin(ang)
    return jnp.concatenate([cos] * 4, axis=1), jnp.concatenate([-sin, sin] * 2, axis=1)


def _pad_cols(w, width):
    return jnp.pad(w, ((0, 0), (0, width - w.shape[1])))


def _routing(top_i, n_blocks, tb):
    n = top_i.shape[0]
    nk = n * TOP_K
    flat_e = top_i.reshape(-1)
    onehot = (flat_e[:, None] == jnp.arange(N_EXPERTS, dtype=jnp.int32)[None, :]).astype(jnp.int32)
    csum = jnp.cumsum(onehot, axis=0)
    rank = jnp.sum(onehot * csum, axis=1) - 1
    counts = csum[-1]
    nblk_e = (counts + tb - 1) // tb
    pend = jnp.cumsum(nblk_e)
    pstart = (pend - nblk_e) * tb
    dest = (pstart[flat_e] + rank).astype(jnp.int32)
    n_used = pend[-1].astype(jnp.int32)
    pad_start = jnp.concatenate([(pstart + counts).astype(jnp.int32), n_used.reshape(1)])
    bidx = jnp.minimum(jnp.arange(n_blocks, dtype=jnp.int32), n_used - 1)
    blk_e = jnp.minimum(jnp.sum((pend[None, :] <= bidx[:, None]).astype(jnp.int32), axis=1), N_EXPERTS - 1)
    return dest, pad_start, blk_e, n_used.reshape(1)


def kernel(x_prompt, x_sample, cache_kv_latent, cache_k_rope, state_ssm, state_conv, page_table, w_in, ssd_conv_w, ssd_conv_b, ssd_dt_bias, ssd_a_log, ssd_d, ssd_norm_g, mla_q_norm_g, w_q_b, mla_kv_norm_g, w_kv_b, w_branch_a, w_branch_b, w_out, ln1_g, ln1_b, router_w, router_b, exp_w_gu, exp_b_gu, exp_w_down, exp_b_down, ln2_g, ln2_b):
    b_p, t_p, _ = x_prompt.shape
    b_s, t_s, _ = x_sample.shape
    n_pages = page_table.shape[1]
    page = cache_kv_latent.shape[2]
    past_len = n_pages * page
    n_p = b_p * t_p
    n_s = b_s * t_s
    n = n_p + n_s
    assert DEPTH == 1 and w_in.shape[0] == 1
    assert n_p % PROJ_TILE == 0 and t_p % ROW_TILE == 0 and n_s % ROW_TILE == 0
    assert t_p % (2 * ATTN_TQ) == 0 and t_p % SSD_CHUNK == 0
    assert t_s == SUBLANES and n_pages % (2 * PAGE_GROUP) == 0 and page == LANES

    wi = w_in[0]
    seg = lambda k: wi[:, IN_OFFS[k]:IN_OFFS[k + 1]]
    half = ROPE_DIM // 2
    w_kr = seg(2)
    w_kr_rot = jnp.concatenate([w_kr[:, half:], w_kr[:, :half]], axis=1)
    w_mla = jnp.concatenate([seg(0), seg(1), _pad_cols(w_kr, LANES), _pad_cols(w_kr_rot, LANES),
                             _pad_cols(seg(5), LANES)], axis=1).astype(BF16)
    w_z = seg(3).astype(BF16)
    w_xbc = seg(4).astype(BF16)
    w_gates = jnp.concatenate([seg(6), seg(7)], axis=1).astype(BF16)
    wq3 = w_q_b[0].reshape(Q_LORA, MLA_HEADS, QK_NOPE + ROPE_DIM)
    wq_nope = wq3[:, :, :QK_NOPE].reshape(Q_LORA, MLA_HEADS * QK_NOPE)
    wq_rope = wq3[:, :, QK_NOPE:]
    wq_rope_rot = jnp.concatenate([wq_rope[..., half:], wq_rope[..., :half]], axis=-1)
    wq = jnp.concatenate([wq_nope, wq_rope.reshape(Q_LORA, -1), wq_rope_rot.reshape(Q_LORA, -1)],
                         axis=1).astype(BF16)
    wkv3 = w_kv_b[0].reshape(KV_LORA, MLA_HEADS, QK_NOPE + V_HEAD)
    wuk = jnp.transpose(wkv3[:, :, :QK_NOPE], (1, 2, 0)).astype(BF16)
    wuv = jnp.transpose(wkv3[:, :, QK_NOPE:], (1, 0, 2)).astype(BF16)
    qg = mla_q_norm_g[0].reshape(1, Q_LORA)
    kvg = mla_kv_norm_g[0].reshape(1, KV_LORA)
    cw = ssd_conv_w[0]
    cb = ssd_conv_b[0].reshape(1, CONV_DIM)
    dtb = _pad_cols(ssd_dt_bias[0].reshape(1, SSD_HEADS), LANES)
    a_neg = _pad_cols(-jnp.exp(ssd_a_log[0].astype(F32)).reshape(1, SSD_HEADS), LANES)
    d_x = jnp.repeat(ssd_d[0].astype(F32), SSD_HEAD_DIM).reshape(1, D_INNER)
    e_mat = jnp.asarray(np.arange(LANES)[:, None] == np.arange(D_INNER)[None, :] // SSD_HEAD_DIM, BF16)
    ng = ssd_norm_g[0].reshape(1, D_INNER)
    wa = w_branch_a[0].astype(BF16)
    wb = w_branch_b[0].astype(BF16)
    wo = w_out[0].astype(BF16)
    rw = _pad_cols(router_w[0], LANES)
    rwh = rw.astype(BF16)
    rwl = (rw - rwh.astype(F32)).astype(BF16)
    rb = jnp.concatenate([router_b[0].reshape(1, N_EXPERTS),
                          jnp.full((1, LANES - N_EXPERTS), NEG, F32)], axis=1)
    bgu = jnp.concatenate([exp_b_gu[0][:, 0::2], exp_b_gu[0][:, 1::2]], axis=1).reshape(N_EXPERTS, 1, 2 * D_FF)
    bdn = exp_b_down[0].reshape(N_EXPERTS, 1, D_MODEL)
    row2 = lambda v: v[0].reshape(1, D_MODEL)

    cos_p, sin_p = _rope_tables(jnp.arange(t_p, dtype=jnp.int32))
    cos_s, sin_s = _rope_tables(jnp.tile(past_len + jnp.arange(t_s, dtype=jnp.int32), b_s))

    xp = x_prompt.reshape(n_p, D_MODEL)
    xs = x_sample.reshape(n_s, D_MODEL)
    qlat_p, qrope_p, c_p, cbf_p, kr_p, krbf_p, dt_p = _mla_in(xp, w_mla, qg, kvg, wq, wuk, cos_p, sin_p, ROW_TILE)
    qlat_s, qrope_s, c_s, _, kr_s, _, dt_s = _mla_in(xs, w_mla, qg, kvg, wq, wuk, cos_s, sin_s, ROW_TILE)
    z_p, xbc_p, gates_p = (_proj(xp, w, PROJ_TILE) for w in (w_z, w_xbc, w_gates))
    z_s, xbc_s, gates_s = (_proj(xs, w, ROW_TILE) for w in (w_z, w_xbc, w_gates))

    conv0_p = jnp.zeros((b_p, SUBLANES, CONV_DIM), F32)
    conv0_s = jnp.pad(state_conv[0], ((0, 0), (SUBLANES - (D_CONV - 1), 0), (0, 0)))
    ya_p, ssm_p, ctail_p = _ssd(xbc_p, z_p, dt_p, conv0_p, None, cw, cb, dtb, a_neg, d_x, ng, e_mat,
                                nb=b_p, nc=t_p // SSD_CHUNK, t_in=SSD_CHUNK, row0=0, y_dtype=BF16)
    ya_s, ssm_s, ctail_s = _ssd(xbc_s, z_s, dt_s, conv0_s, state_ssm[0].reshape(b_s, D_INNER, D_STATE),
                                cw, cb, dtb, a_neg, d_x, ng, e_mat,
                                nb=b_s, nc=1, t_in=t_s, row0=0, y_dtype=F32)
    conv_p = ctail_p[:, SUBLANES - (D_CONV - 1):, :]
    conv_s = ctail_s[:, SUBLANES - (D_CONV - 1):, :]

    o_p = _attn_prompt(qlat_p, qrope_p, cbf_p, krbf_p, wuv, nb=b_p, t=t_p, tq=ATTN_TQ)
    ql_s = jnp.transpose(qlat_s.reshape(MLA_HEADS, b_s, t_s, KV_LORA), (1, 0, 2, 3))
    qr_s = jnp.transpose(qrope_s.reshape(MLA_HEADS, b_s, t_s, ROPE_DIM), (1, 0, 2, 3))
    cnew_s = jnp.pad(c_s.reshape(b_s, t_s, KV_LORA), ((0, 0), (0, LANES - t_s), (0, 0)))
    krnew_s = jnp.swapaxes(jnp.pad(kr_s.reshape(b_s, t_s, ROPE_DIM), ((0, 0), (0, LANES - t_s), (0, 0))), 1, 2)
    olat_s = _attn_sample(page_table, ql_s.reshape(b_s, MLA_HEADS * t_s, KV_LORA),
                          qr_s.reshape(b_s, MLA_HEADS * t_s, ROPE_DIM), cnew_s, krnew_s,
                          cache_kv_latent[0], jnp.swapaxes(cache_k_rope[0], 1, 2),
                          t_new=t_s, gp=PAGE_GROUP)
    olat_s = jnp.transpose(olat_s.reshape(b_s, MLA_HEADS, t_s, KV_LORA), (1, 0, 2, 3)).reshape(MLA_HEADS, n_s, KV_LORA)
    o_s = _head_proj(olat_s, wuv)

    h, h3, top_i, top_g = _merge(ya_p, ya_s.astype(BF16), o_p, o_s, gates_p, gates_s, xp, xs, wa, wb, wo,
                                 row2(ln1_g), row2(ln1_b), rwh, rwl, rb, ROW_TILE)

    tb = MOE_TB
    n_blocks = -(-(n * TOP_K + N_EXPERTS * (tb - 1)) // tb)
    dest, pad_start, blk_e, n_used = _routing(top_i[:, :TOP_K], n_blocks, tb)
    xr3 = _dispatch(pad_start, dest, h3, (n_blocks + 1) * tb, COMBINE_TILE, tb)
    yr3 = _ffn(blk_e, n_used, xr3, n_blocks, exp_w_gu[0], bgu, exp_w_down[0], bdn, tb)
    y_p, y_s = _combine(dest, yr3, top_g, h, row2(ln2_g), row2(ln2_b), n_p, COMBINE_TILE)

    return (y_p.reshape(b_p, t_p, D_MODEL), y_s.reshape(b_s, t_s, D_MODEL),
            c_p.reshape(1, b_p, t_p, KV_LORA), kr_p.reshape(1, b_p, t_p, ROPE_DIM),
            ssm_p.reshape(1, b_p, SSD_HEADS, SSD_HEAD_DIM, D_STATE), conv_p[None],
            c_s.reshape(1, b_s, t_s, KV_LORA), kr_s.reshape(1, b_s, t_s, ROPE_DIM),
            ssm_s.reshape(1, b_s, SSD_HEADS, SSD_HEAD_DIM, D_STATE), conv_s[None])
```

```python
import functools

import jax
import jax.numpy as jnp
import numpy as np
from jax import lax
from jax.experimental import pallas as pl
from jax.experimental.pallas import tpu as pltpu

F32 = jnp.float32
BF16 = jnp.bfloat16

D_MODEL = 1024
D_INNER = 2048
SSD_HEAD_DIM = 64
SSD_HEADS = 32
SSD_GROUPS = 4
HEADS_PER_GROUP = 8
D_STATE = 128
D_CONV = 4
CONV_DIM = D_INNER + 2 * SSD_GROUPS * D_STATE
SSD_CHUNK = 128
MLA_HEADS = 8
Q_LORA = 256
KV_LORA = 256
QK_NOPE = 128
ROPE_DIM = 64
V_HEAD = 128
ROPE_THETA = 10000.0
ATTN_SCALE = (QK_NOPE + ROPE_DIM) ** -0.5
N_EXPERTS = 32
TOP_K = 4
D_FF = 1024
SWIGLU_ALPHA = 1.702
SWIGLU_LIMIT = 7.0
DEPTH = 1
DN_ALPHA = (2 * DEPTH) ** 0.25
LN_EPS = 1e-5
RMS_EPS = 1e-6
IN_SIZES = (Q_LORA, KV_LORA, ROPE_DIM, D_INNER, CONV_DIM, SSD_HEADS, D_MODEL, D_MODEL)
IN_OFFS = tuple(int(v) for v in np.cumsum((0,) + IN_SIZES))

LANES = 128
SUBLANES = 8
VMEM_LIMIT = 56 * 1024 * 1024

NEG = -1e30

ROW_TILE = 256
PROJ_TILE = 512
ATTN_TQ = 256
PAGE_GROUP = 32
MOE_TB = 256
COMBINE_TILE = 128
MLA_A_COLS = 896


def _cparams(n_axes):
    return pltpu.CompilerParams(dimension_semantics=("arbitrary",) * n_axes,
                                vmem_limit_bytes=VMEM_LIMIT)


def _dot(a, b):
    return jnp.dot(a, b, preferred_element_type=F32)


def _dot_nt(a, b):
    return lax.dot_general(a, b, (((1,), (1,)), ((), ())), preferred_element_type=F32)


def _split3(x):
    hi = x.astype(BF16)
    r1 = x - hi.astype(F32)
    mid = r1.astype(BF16)
    lo = (r1 - mid.astype(F32)).astype(BF16)
    return hi, mid, lo


def _sigmoid(x):
    return 1.0 / (1.0 + jnp.exp(-x))


def _layer_norm(x, g, b):
    mu = jnp.mean(x, axis=-1, keepdims=True)
    xc = x - mu
    var = jnp.mean(xc * xc, axis=-1, keepdims=True)
    return xc * lax.rsqrt(var + LN_EPS) * g + b


def _rms_norm(x, g):
    return x * lax.rsqrt(jnp.mean(x * x, axis=-1, keepdims=True) + RMS_EPS) * g


def _proj_kernel(x_ref, w_ref, o_ref):
    o_ref[...] = _dot(x_ref[...].astype(BF16), w_ref[...])


def _proj(x, w, tm):
    m, k = x.shape
    n = w.shape[1]
    return pl.pallas_call(
        _proj_kernel,
        out_shape=jax.ShapeDtypeStruct((m, n), F32),
        grid=(m // tm,),
        in_specs=[pl.BlockSpec((tm, k), lambda i: (i, 0)),
                  pl.BlockSpec((k, n), lambda i: (0, 0))],
        out_specs=pl.BlockSpec((tm, n), lambda i: (i, 0)),
        compiler_params=_cparams(1),
        name="proj",
    )(x, w)


def _mla_in_kernel(x_ref, wa_ref, qg_ref, kvg_ref, wq_ref, wuk_ref, cos_ref, sin_ref,
                   qlat_ref, qrope_ref, c_ref, cbf_ref, kr_ref, krbf_ref, dt_ref):
    xb = x_ref[...].astype(BF16)
    pa = _dot(xb, wa_ref[...])
    q_a = pa[:, 0:256]
    kv_a = pa[:, 256:512]
    kr_raw = pa[:, 512:640]
    kr_rot = pa[:, 640:768]
    dt_ref[...] = pa[:, 768:896]
    cos = cos_ref[...]
    sin = sin_ref[...]
    c = _rms_norm(kv_a, kvg_ref[...])
    c_ref[...] = c
    cbf_ref[...] = c.astype(BF16)
    kr = kr_raw * cos + kr_rot * sin
    kr_ref[...] = kr[:, 0:ROPE_DIM]
    krbf_ref[...] = kr[:, 0:ROPE_DIM].astype(BF16)
    qn = _rms_norm(q_a, qg_ref[...]).astype(BF16)
    qq = _dot(qn, wq_ref[...])
    cos4 = jnp.concatenate([cos] * 4, axis=1)
    sin4 = jnp.concatenate([sin] * 4, axis=1)
    qr = (qq[:, 1024:1536] * cos4 + qq[:, 1536:2048] * sin4) * ATTN_SCALE
    for h in range(MLA_HEADS):
        qrope_ref[h] = qr[:, h * ROPE_DIM:(h + 1) * ROPE_DIM].astype(BF16)
        nope = qq[:, h * QK_NOPE:(h + 1) * QK_NOPE].astype(BF16)
        qlat_ref[h] = (_dot(nope, wuk_ref[h]) * ATTN_SCALE).astype(BF16)


def _mla_in(x, wa, qg, kvg, wq, wuk, cos_t, sin_t, tm):
    m = x.shape[0]
    tab_blocks = cos_t.shape[0] // tm
    row = lambda w: pl.BlockSpec((tm, w), lambda i: (i, 0))
    tab = pl.BlockSpec((tm, LANES), lambda i: (i % tab_blocks, 0))
    full2 = lambda a: pl.BlockSpec(a.shape, lambda i: (0, 0))
    full3 = lambda a: pl.BlockSpec(a.shape, lambda i: (0, 0, 0))
    return pl.pallas_call(
        _mla_in_kernel,
        out_shape=(jax.ShapeDtypeStruct((MLA_HEADS, m, KV_LORA), BF16),
                   jax.ShapeDtypeStruct((MLA_HEADS, m, ROPE_DIM), BF16),
                   jax.ShapeDtypeStruct((m, KV_LORA), F32),
                   jax.ShapeDtypeStruct((m, KV_LORA), BF16),
                   jax.ShapeDtypeStruct((m, ROPE_DIM), F32),
                   jax.ShapeDtypeStruct((m, ROPE_DIM), BF16),
                   jax.ShapeDtypeStruct((m, LANES), F32)),
        grid=(m // tm,),
        in_specs=[row(D_MODEL), full2(wa), full2(qg), full2(kvg), full2(wq), full3(wuk), tab, tab],
        out_specs=(pl.BlockSpec((MLA_HEADS, tm, KV_LORA), lambda i: (0, i, 0)),
                   pl.BlockSpec((MLA_HEADS, tm, ROPE_DIM), lambda i: (0, i, 0)),
                   row(KV_LORA), row(KV_LORA), row(ROPE_DIM), row(ROPE_DIM), row(LANES)),
        compiler_params=_cparams(1),
        name="mla_in",
    )(x, wa, qg, kvg, wq, wuk, cos_t, sin_t)


def _expand_heads(v, e_ref, terms):
    out = None
    rem = v
    for _ in range(terms):
        part = rem.astype(BF16)
        rem = rem - part.astype(F32)
        d = _dot(part, e_ref[...])
        out = d if out is None else out + d
    return out


def _ssd_kernel(*refs, t_in, has_h0):
    n_pairs = SSD_HEADS // 2
    refs = list(refs)
    hT = refs[-n_pairs:]
    refs = refs[:-n_pairs]
    if has_h0:
        (xbc_ref, z_ref, dt_ref, conv0_ref, h0_ref, cw_ref, cb_ref, dtb_ref, a_ref, dx_ref, ng_ref, e_ref,
         y_ref, hfin_ref, ctail_ref, ext, xcv, ybuf, acs_s, acsT_s, dtT_s, eax_s, wtx_s) = refs
    else:
        (xbc_ref, z_ref, dt_ref, conv0_ref, cw_ref, cb_ref, dtb_ref, a_ref, dx_ref, ng_ref, e_ref,
         y_ref, hfin_ref, ctail_ref, ext, xcv, ybuf, acs_s, acsT_s, dtT_s, eax_s, wtx_s) = refs
        h0_ref = None
    L = SSD_CHUNK
    c = pl.program_id(1)
    nc = pl.num_programs(1)

    @pl.when(c == 0)
    def _():
        ext[0:SUBLANES, :] = conv0_ref[0]
        if t_in < L:
            ext[SUBLANES:SUBLANES + L, :] = jnp.zeros((L, CONV_DIM), F32)
        for j in range(n_pairs):
            if has_h0:
                hT[j][...] = h0_ref[0, j * LANES:(j + 1) * LANES, :].T
            else:
                hT[j][...] = jnp.zeros((D_STATE, LANES), F32)

    ext[SUBLANES:SUBLANES + t_in, :] = xbc_ref[...]

    @pl.when(c == nc - 1)
    def _():
        ctail_ref[0] = ext[t_in:t_in + SUBLANES, :]

    for j in range(CONV_DIM // LANES):
        sl = slice(j * LANES, (j + 1) * LANES)
        acc = cb_ref[:, sl] + ext[pl.ds(SUBLANES - 3, L), sl] * cw_ref[0:1, sl]
        acc = acc + ext[pl.ds(SUBLANES - 2, L), sl] * cw_ref[1:2, sl]
        acc = acc + ext[pl.ds(SUBLANES - 1, L), sl] * cw_ref[2:3, sl]
        acc = acc + ext[pl.ds(SUBLANES, L), sl] * cw_ref[3:4, sl]
        xcv[:, sl] = acc * _sigmoid(acc)
    ext[0:SUBLANES, :] = ext[L:L + SUBLANES, :]

    row = lax.broadcasted_iota(jnp.int32, (L, L), 0)
    col = lax.broadcasted_iota(jnp.int32, (L, L), 1)
    tri = row >= col
    tri_b = jnp.where(tri, 1.0, 0.0).astype(BF16)
    if t_in < L:
        dt_raw = jnp.concatenate([dt_ref[...], jnp.zeros((L - t_in, LANES), F32)], axis=0)
    else:
        dt_raw = dt_ref[...]
    xs = dt_raw + dtb_ref[...]
    dtv = jnp.maximum(xs, 0.0) + jnp.log1p(jnp.exp(-jnp.abs(xs)))
    if t_in < L:
        dtv = jnp.where(lax.broadcasted_iota(jnp.int32, (L, LANES), 0) < t_in, dtv, 0.0)
    da = dtv * a_ref[...]
    hi, mid, lo = _split3(da)
    acs = _dot(tri_b, hi) + _dot(tri_b, mid) + _dot(tri_b, lo)
    acs_s[...] = acs
    acsT_s[...] = acs.T
    dtT_s[...] = dtv.T
    eax_s[...] = _expand_heads(jnp.exp(acs), e_ref, 3)
    wtx_s[...] = _expand_heads(jnp.exp(acs[L - 1:L, :] - acs) * dtv, e_ref, 2)

    low_half = lax.broadcasted_iota(jnp.int32, (L, LANES), 1) < SSD_HEAD_DIM
    pairs_per_group = HEADS_PER_GROUP // 2
    for g in range(SSD_GROUPS):
        bo = D_INNER + g * D_STATE
        co = D_INNER + SSD_GROUPS * D_STATE + g * D_STATE
        bm = xcv[:, bo:bo + D_STATE]
        cmb = xcv[:, co:co + D_STATE].astype(BF16)
        cb = _dot_nt(cmb, bm.astype(BF16))
        bmT = bm.T.astype(BF16)
        for k in range(pairs_per_group):
            kk = g * pairs_per_group + k
            tl = slice(kk * LANES, (kk + 1) * LANES)
            x2 = xcv[:, tl]
            ws = []
            for idx in (2 * kk, 2 * kk + 1):
                seg = acs_s[:, idx:idx + 1] - acsT_s[idx:idx + 1, :]
                w = cb * jnp.exp(jnp.where(tri, seg, NEG)) * dtT_s[idx:idx + 1, :]
                ws.append(w.astype(BF16))
            w2 = jnp.concatenate(ws, axis=1)
            xbd = jnp.concatenate([jnp.where(low_half, x2, 0.0).astype(BF16),
                                   jnp.where(low_half, 0.0, x2).astype(BF16)], axis=0)
            yd = _dot(w2, xbd)
            hp = hT[kk][...]
            yo = _dot(cmb, hp.astype(BF16))
            ybuf[:, tl] = yd + yo * eax_s[:, tl] + dx_ref[:, tl] * x2
            s = _dot(bmT, (x2 * wtx_s[:, tl]).astype(BF16))
            hT[kk][...] = eax_s[L - 1:L, tl] * hp + s

    gw = D_INNER // SSD_GROUPS
    for g in range(SSD_GROUPS):
        sl = slice(g * gw, (g + 1) * gw)
        zg = z_ref[:, sl]
        v = ybuf[0:t_in, sl] * (zg * _sigmoid(zg))
        ms = jnp.mean(v * v, axis=-1, keepdims=True)
        y_ref[:, sl] = (v * lax.rsqrt(ms + RMS_EPS) * ng_ref[:, sl]).astype(y_ref.dtype)

    @pl.when(c == nc - 1)
    def _():
        for j in range(n_pairs):
            hfin_ref[0, j * LANES:(j + 1) * LANES, :] = hT[j][...].T


def _ssd(xbc, z, dt, conv0, h0, cw, cb, dtb, a_neg, d_x, ng, e_mat, *, nb, nc, t_in, row0, y_dtype):
    L = SSD_CHUNK
    blk0 = row0 // t_in
    has_h0 = h0 is not None
    rowspec = lambda w: pl.BlockSpec((t_in, w), lambda b, c: (blk0 + b * nc + c, 0))
    par = lambda a: pl.BlockSpec(a.shape, lambda b, c: (0, 0))
    in_specs = [rowspec(CONV_DIM), rowspec(D_INNER), rowspec(LANES),
                pl.BlockSpec((1, SUBLANES, CONV_DIM), lambda b, c: (b, 0, 0))]
    args = [xbc, z, dt, conv0]
    if has_h0:
        in_specs.append(pl.BlockSpec((1, D_INNER, D_STATE), lambda b, c: (b, 0, 0)))
        args.append(h0)
    params = [cw, cb, dtb, a_neg, d_x, ng, e_mat]
    in_specs += [par(p) for p in params]
    args += params
    return pl.pallas_call(
        functools.partial(_ssd_kernel, t_in=t_in, has_h0=has_h0),
        out_shape=(jax.ShapeDtypeStruct((nb * nc * t_in, D_INNER), y_dtype),
                   jax.ShapeDtypeStruct((nb, D_INNER, D_STATE), F32),
                   jax.ShapeDtypeStruct((nb, SUBLANES, CONV_DIM), F32)),
        grid=(nb, nc),
        in_specs=in_specs,
        out_specs=(pl.BlockSpec((t_in, D_INNER), lambda b, c: (b * nc + c, 0)),
                   pl.BlockSpec((1, D_INNER, D_STATE), lambda b, c: (b, 0, 0)),
                   pl.BlockSpec((1, SUBLANES, CONV_DIM), lambda b, c: (b, 0, 0))),
        scratch_shapes=[pltpu.VMEM((L + 2 * SUBLANES, CONV_DIM), F32),
                        pltpu.VMEM((L, CONV_DIM), F32),
                        pltpu.VMEM((L, D_INNER), F32),
                        pltpu.VMEM((L, LANES), F32), pltpu.VMEM((LANES, L), F32), pltpu.VMEM((LANES, L), F32),
                        pltpu.VMEM((L, D_INNER), F32), pltpu.VMEM((L, D_INNER), F32)]
                       + [pltpu.VMEM((D_STATE, LANES), F32)] * (SSD_HEADS // 2),
        compiler_params=_cparams(2),
        name="ssd",
    )(*args)


def _softmax_step(s, kc, m_scr, l_scr, acc_scr):
    reps = s.shape[1] // LANES
    m_prev = m_scr[...]
    m_new = jnp.maximum(m_prev, jnp.max(s, axis=1, keepdims=True))
    alpha = jnp.exp(m_prev - m_new)
    p = jnp.exp(s - (jnp.concatenate([m_new] * reps, axis=1) if reps > 1 else m_new))
    l_scr[...] = alpha * l_scr[...] + jnp.sum(p, axis=1, keepdims=True)
    acc_scr[...] = acc_scr[...] * jnp.concatenate([alpha] * (KV_LORA // LANES), axis=1) + _dot(p.astype(BF16), kc)
    m_scr[...] = m_new


def _attn_prompt_kernel(qlat_ref, qrope_ref, c_ref, kr_ref, wuv_ref, o_ref, m_scr, l_scr, acc_scr, *, tq):
    qi = pl.program_id(1)
    R = MLA_HEADS * tq

    def scores(block, width):
        start = pl.multiple_of(block * width, width)
        kc = c_ref[pl.ds(start, width), :]
        kr = kr_ref[pl.ds(start, width), :]
        q = qlat_ref[...].reshape(R, KV_LORA)
        qr = qrope_ref[...].reshape(R, ROPE_DIM)
        return _dot_nt(q, kc) + _dot_nt(qr, kr), kc

    s, kc = scores(qi, tq)
    t = lax.broadcasted_iota(jnp.int32, (R, tq), 0) & (tq - 1)
    kcol = lax.broadcasted_iota(jnp.int32, (R, tq), 1)
    s = jnp.where(kcol <= t, s, NEG)
    m0 = jnp.max(s, axis=1, keepdims=True)
    p = jnp.exp(s - m0)
    m_scr[...] = jnp.broadcast_to(m0, m_scr.shape)
    l_scr[...] = jnp.broadcast_to(jnp.sum(p, axis=1, keepdims=True), l_scr.shape)
    acc_scr[...] = _dot(p.astype(BF16), kc)

    def body(j, carry):
        s2, kc2 = scores(j, 2 * tq)
        _softmax_step(s2, kc2, m_scr, l_scr, acc_scr)
        return carry

    lax.fori_loop(0, qi // 2, body, 0)

    @pl.when(qi % 2 == 1)
    def _():
        s1, kc1 = scores(qi - 1, tq)
        _softmax_step(s1, kc1, m_scr, l_scr, acc_scr)

    inv = 1.0 / l_scr[...]
    o_lat = acc_scr[...] * jnp.concatenate([inv] * (KV_LORA // LANES), axis=1)
    for h in range(MLA_HEADS):
        oh = _dot(o_lat[h * tq:(h + 1) * tq, :].astype(BF16), wuv_ref[h])
        o_ref[:, h * V_HEAD:(h + 1) * V_HEAD] = oh.astype(o_ref.dtype)


def _attn_prompt(qlat, qrope, cbf, krbf, wuv, *, nb, t, tq):
    nq = t // tq
    R = MLA_HEADS * tq
    return pl.pallas_call(
        functools.partial(_attn_prompt_kernel, tq=tq),
        out_shape=jax.ShapeDtypeStruct((nb * t, MLA_HEADS * V_HEAD), BF16),
        grid=(nb, nq),
        in_specs=[pl.BlockSpec((MLA_HEADS, tq, KV_LORA), lambda b, i: (0, b * nq + i, 0)),
                  pl.BlockSpec((MLA_HEADS, tq, ROPE_DIM), lambda b, i: (0, b * nq + i, 0)),
                  pl.BlockSpec((t, KV_LORA), lambda b, i: (b, 0)),
                  pl.BlockSpec((t, ROPE_DIM), lambda b, i: (b, 0)),
                  pl.BlockSpec(wuv.shape, lambda b, i: (0, 0, 0))],
        out_specs=pl.BlockSpec((tq, MLA_HEADS * V_HEAD), lambda b, i: (b * nq + i, 0)),
        scratch_shapes=[pltpu.VMEM((R, LANES), F32), pltpu.VMEM((R, LANES), F32),
                        pltpu.VMEM((R, KV_LORA), F32)],
        compiler_params=_cparams(2),
        name="attn_prompt",
    )(qlat, qrope, cbf, krbf, wuv)


def _attn_sample_kernel(pt_ref, qlat_ref, qrope_ref, cnew_ref, krnew_ref, kv_hbm, kr_hbm, o_ref,
                        kvbuf, krbuf, kv_sem, kr_sem, m_scr, l_scr, acc_scr, *, n_groups, gp, t_new):
    b = pl.program_id(0)
    nb = pl.num_programs(0)
    R = qlat_ref.shape[1]
    page = kvbuf.shape[2]
    q = qlat_ref[0]
    qr = qrope_ref[0]

    def start_group(bb, g, slot):
        for p in range(gp):
            pid = pt_ref[bb, g * gp + p]
            pltpu.make_async_copy(kv_hbm.at[pid], kvbuf.at[slot, p], kv_sem.at[slot]).start()
            pltpu.make_async_copy(kr_hbm.at[pid], krbuf.at[slot, p], kr_sem.at[slot]).start()

    def wait_group(slot):
        pltpu.make_async_copy(kvbuf.at[slot], kvbuf.at[slot], kv_sem.at[slot]).wait()
        pltpu.make_async_copy(krbuf.at[slot], krbuf.at[slot], kr_sem.at[slot]).wait()

    @pl.when(b == 0)
    def _():
        start_group(0, 0, 0)

    kc = cnew_ref[0].astype(BF16)
    kr = krnew_ref[0].astype(BF16)
    s = _dot_nt(q, kc) + _dot(qr, kr)
    t = lax.broadcasted_iota(jnp.int32, (R, LANES), 0) & (t_new - 1)
    kcol = lax.broadcasted_iota(jnp.int32, (R, LANES), 1)
    s = jnp.where(kcol <= t, s, NEG)
    m0 = jnp.max(s, axis=1, keepdims=True)
    p0 = jnp.exp(s - m0)
    m_scr[...] = jnp.broadcast_to(m0, m_scr.shape)
    l_scr[...] = jnp.broadcast_to(jnp.sum(p0, axis=1, keepdims=True), l_scr.shape)
    acc_scr[...] = _dot(p0.astype(BF16), kc)

    for g in range(n_groups):
        slot = g % 2
        if g + 1 < n_groups:
            start_group(b, g + 1, 1 - slot)
        else:
            @pl.when(b + 1 < nb)
            def _():
                start_group(b + 1, 0, 1 - slot)
        wait_group(slot)
        kc = kvbuf[slot].reshape(gp * page, KV_LORA).astype(BF16)
        kr = jnp.concatenate([krbuf[slot, p].astype(BF16) for p in range(gp)], axis=1)
        s = _dot_nt(q, kc) + _dot(qr, kr)
        _softmax_step(s, kc, m_scr, l_scr, acc_scr)

    inv = 1.0 / l_scr[...]
    o_ref[0] = acc_scr[...] * jnp.concatenate([inv] * (KV_LORA // LANES), axis=1)


def _attn_sample(page_table, qlat, qrope, cnew, krnew, cache_kv, cache_kr, *, t_new, gp):
    nb, n_pages = page_table.shape
    n_groups = n_pages // gp
    R = qlat.shape[1]
    page = cache_kv.shape[1]
    per_b = lambda r, w: pl.BlockSpec((1, r, w), lambda b, pt: (b, 0, 0))
    grid_spec = pltpu.PrefetchScalarGridSpec(
        num_scalar_prefetch=1,
        grid=(nb,),
        in_specs=[per_b(R, KV_LORA), per_b(R, ROPE_DIM), per_b(LANES, KV_LORA), per_b(ROPE_DIM, LANES),
                  pl.BlockSpec(memory_space=pl.ANY), pl.BlockSpec(memory_space=pl.ANY)],
        out_specs=per_b(R, KV_LORA),
        scratch_shapes=[pltpu.VMEM((2, gp, page, KV_LORA), F32), pltpu.VMEM((2, gp, ROPE_DIM, page), F32),
                        pltpu.SemaphoreType.DMA((2,)), pltpu.SemaphoreType.DMA((2,)),
                        pltpu.VMEM((R, LANES), F32), pltpu.VMEM((R, LANES), F32),
                        pltpu.VMEM((R, KV_LORA), F32)],
    )
    return pl.pallas_call(
        functools.partial(_attn_sample_kernel, n_groups=n_groups, gp=gp, t_new=t_new),
        out_shape=jax.ShapeDtypeStruct((nb, R, KV_LORA), F32),
        grid_spec=grid_spec,
        compiler_params=_cparams(1),
        name="attn_sample",
    )(page_table, qlat, qrope, cnew, krnew, cache_kv, cache_kr)


def _head_proj_kernel(x_ref, w_ref, o_ref):
    o_ref[...] = _dot(x_ref[0].astype(BF16), w_ref[0]).astype(o_ref.dtype)


def _head_proj(x, w):
    h, m, k = x.shape
    n = w.shape[2]
    return pl.pallas_call(
        _head_proj_kernel,
        out_shape=jax.ShapeDtypeStruct((m, h * n), BF16),
        grid=(h,),
        in_specs=[pl.BlockSpec((1, m, k), lambda i: (i, 0, 0)),
                  pl.BlockSpec((1, k, n), lambda i: (i, 0, 0))],
        out_specs=pl.BlockSpec((m, n), lambda i: (0, i)),
        compiler_params=_cparams(1),
        name="head_proj",
    )(x, w)


def _merge_kernel(ya_p_ref, ya_s_ref, o_p_ref, o_s_ref, g_p_ref, g_s_ref, u_p_ref, u_s_ref, wa_ref, wb_ref, wo_ref,
                  lg_ref, lb_ref, rwh_ref, rwl_ref, rb_ref,
                  h_ref, h3_ref, ti_ref, tg_ref, *, n_prompt_tiles):
    i = pl.program_id(0)
    is_p = i < n_prompt_tiles
    yn = jnp.where(is_p, ya_p_ref[...], ya_s_ref[...])
    ob = jnp.where(is_p, o_p_ref[...], o_s_ref[...])
    g = jnp.where(is_p, g_p_ref[...], g_s_ref[...])
    u = jnp.where(is_p, u_p_ref[...], u_s_ref[...])
    ya = _dot(yn, wa_ref[...])
    yb = _dot(ob, wb_ref[...])
    merged = _sigmoid(g[:, 0:D_MODEL]) * ya + _sigmoid(g[:, D_MODEL:2 * D_MODEL]) * yb
    mix = _dot(merged.astype(BF16), wo_ref[...])
    h = _layer_norm(DN_ALPHA * u + mix, lg_ref[...], lb_ref[...])
    h_ref[...] = h
    tm = h.shape[0]
    for j in range(D_MODEL // LANES):
        h3_ref[pl.ds(j, tm, stride=SUBLANES), :] = h[:, j * LANES:(j + 1) * LANES]
    hh = h.astype(BF16)
    hl = (h - hh.astype(F32)).astype(BF16)
    logits = _dot(hh, rwh_ref[...]) + _dot(hl, rwh_ref[...]) + _dot(hh, rwl_ref[...]) + rb_ref[...]
    lane = lax.broadcasted_iota(jnp.int32, logits.shape, 1).astype(F32)
    vals = jnp.zeros_like(logits)
    idxs = jnp.zeros_like(logits)
    cur = logits
    for k in range(TOP_K):
        mx = jnp.max(cur, axis=1, keepdims=True)
        ix = jnp.min(jnp.where(cur == mx, lane, float(LANES)), axis=1, keepdims=True)
        vals = jnp.where(lane == k, mx, vals)
        idxs = jnp.where(lane == k, ix, idxs)
        cur = jnp.where(lane == ix, NEG * 2.0, cur)
    v0 = vals[:, 0:1]
    e = jnp.where(lane < TOP_K, jnp.exp(vals - v0), 0.0)
    tg_ref[...] = e / jnp.sum(e, axis=1, keepdims=True)
    ti_ref[...] = idxs.astype(jnp.int32)


def _merge(ya_p, ya_s, o_p, o_s, g_p, g_s, u_p, u_s, wa, wb, wo, lg, lb, rwh, rwl, rb, tm):
    m = u_p.shape[0] + u_s.shape[0]
    npt = ya_p.shape[0] // tm
    row = lambda w: pl.BlockSpec((tm, w), lambda i: (i, 0))
    prow = lambda w: pl.BlockSpec((tm, w), lambda i: (jnp.minimum(i, npt - 1), 0))
    srow = lambda w: pl.BlockSpec((tm, w), lambda i: (jnp.maximum(i - npt, 0), 0))
    par = lambda a: pl.BlockSpec(a.shape, lambda i: (0, 0))
    return pl.pallas_call(
        functools.partial(_merge_kernel, n_prompt_tiles=npt),
        out_shape=(jax.ShapeDtypeStruct((m, D_MODEL), F32),
                   jax.ShapeDtypeStruct((m * SUBLANES, LANES), F32),
                   jax.ShapeDtypeStruct((m, LANES), jnp.int32),
                   jax.ShapeDtypeStruct((m, LANES), F32)),
        grid=(m // tm,),
        in_specs=[prow(D_INNER), srow(D_INNER), prow(D_MODEL), srow(D_MODEL), prow(2 * D_MODEL), srow(2 * D_MODEL),
                  prow(D_MODEL), srow(D_MODEL),
                  par(wa), par(wb), par(wo), par(lg), par(lb), par(rwh), par(rwl), par(rb)],
        out_specs=(row(D_MODEL), pl.BlockSpec((tm * SUBLANES, LANES), lambda i: (i, 0)), row(LANES), row(LANES)),
        compiler_params=_cparams(1),
        name="merge",
    )(ya_p, ya_s, o_p, o_s, g_p, g_s, u_p, u_s, wa, wb, wo, lg, lb, rwh, rwl, rb)


DEINT_IN = 2 * LANES


def _deint_perm():
    perm = np.zeros((DEINT_IN, DEINT_IN), np.float32)
    perm[2 * np.arange(LANES), np.arange(LANES)] = 1.0
    perm[2 * np.arange(LANES) + 1, LANES + np.arange(LANES)] = 1.0
    return jnp.asarray(perm, BF16)


def _row_pipeline(i, n_steps, idx_hbm, src_hbm, idx_smem, idx_sem, buf, row_sem, rows, compute):
    def idx_copy(chunk, s):
        return pltpu.make_async_copy(idx_hbm.at[chunk], idx_smem.at[s], idx_sem.at[s])

    def issue_row(s, r):
        src0 = pl.multiple_of(idx_smem[s, r] * SUBLANES, SUBLANES)
        dst0 = pl.multiple_of(r * SUBLANES, SUBLANES)
        pltpu.make_async_copy(src_hbm.at[pl.ds(src0, SUBLANES), :],
                              buf.at[s, pl.ds(dst0, SUBLANES), :], row_sem.at[s]).start()

    @pl.when(i == 0)
    def _():
        idx_copy(0, 0).start()
        idx_copy(0, 0).wait()

        def body(r, carry):
            issue_row(0, r)
            return carry
        lax.fori_loop(0, rows, body, 0, unroll=8)
        if n_steps > 1:
            idx_copy(1, 1).start()

    def step(slot):
        has_next = i + 1 < n_steps

        @pl.when(has_next)
        def _():
            idx_copy(i + 1, 1 - slot).wait()

        @pl.when(i + 2 < n_steps)
        def _():
            idx_copy(i + 2, slot).start()

        pltpu.make_async_copy(src_hbm.at[pl.ds(0, rows * SUBLANES), :], buf.at[slot], row_sem.at[slot]).wait()

        @pl.when(has_next)
        def _():
            for r in range(rows):
                issue_row(1 - slot, r)
            compute(slot)

        if (n_steps - 1) % 2 == slot:
            @pl.when(i + 1 == n_steps)
            def _():
                compute(slot)

    for slot in (0, 1):
        pl.when(lax.rem(i, 2) == slot)(functools.partial(step, slot))


def _fetch_scratch(rows):
    return [pltpu.SMEM((2, rows), jnp.int32), pltpu.SemaphoreType.DMA((2,)),
            pltpu.VMEM((2, rows * SUBLANES, LANES), F32), pltpu.SemaphoreType.DMA((2,))]


def _dispatch_kernel(pad_ref, dest_hbm, h3_ref, xr_hbm, idx_smem, idx_sem, stage, row_sem, zbuf, z_sem,
                     *, tm, tb, min_used):
    i = pl.program_id(0)
    n = pl.num_programs(0)
    rows = tm * TOP_K

    def idx_copy(chunk, s):
        return pltpu.make_async_copy(dest_hbm.at[chunk], idx_smem.at[s], idx_sem.at[s])

    def rows_wait(s):
        span = xr_hbm.at[pl.ds(0, rows * SUBLANES), :]
        pltpu.make_async_copy(span, span, row_sem.at[s]).wait()

    @pl.when(i == 0)
    def _():
        zbuf[...] = jnp.zeros_like(zbuf)
        zero_to = lambda start: pltpu.make_async_copy(zbuf, xr_hbm.at[pl.ds(start, tb * SUBLANES), :], z_sem)
        for e in range(N_EXPERTS):
            zero_to(pl.multiple_of(pad_ref[e] * SUBLANES, SUBLANES)).start()
        for e in range(N_EXPERTS):
            zero_to(0).wait()
        n_used = pad_ref[N_EXPERTS]
        for blk in range(min_used, xr_hbm.shape[0] // (tb * SUBLANES)):
            @pl.when(blk >= n_used)
            def _():
                zero_to(blk * tb * SUBLANES).start()
                zero_to(0).wait()
        idx_copy(0, 0).start()

    def step(slot):
        idx_copy(i, slot).wait()

        @pl.when(i + 1 < n)
        def _():
            idx_copy(i + 1, 1 - slot).start()

        stage[slot] = h3_ref[...]
        for r in range(rows):
            dst0 = pl.multiple_of(idx_smem[slot, r] * SUBLANES, SUBLANES)
            pltpu.make_async_copy(stage.at[slot, pl.ds((r // TOP_K) * SUBLANES, SUBLANES), :],
                                  xr_hbm.at[pl.ds(dst0, SUBLANES), :], row_sem.at[slot]).start()

        @pl.when(i >= 1)
        def _():
            rows_wait(1 - slot)

        @pl.when(i == n - 1)
        def _():
            rows_wait(slot)

    for slot in (0, 1):
        pl.when(lax.rem(i, 2) == slot)(functools.partial(step, slot))


def _dispatch(pad_start, dest, h3, cap_rows, tm, tb):
    n = h3.shape[0] // SUBLANES
    grid_spec = pltpu.PrefetchScalarGridSpec(
        num_scalar_prefetch=1,
        grid=(n // tm,),
        in_specs=[pl.BlockSpec(memory_space=pl.ANY),
                  pl.BlockSpec((tm * SUBLANES, LANES), lambda i, pad: (i, 0))],
        out_specs=pl.BlockSpec(memory_space=pl.ANY),
        scratch_shapes=[pltpu.SMEM((2, tm * TOP_K), jnp.int32), pltpu.SemaphoreType.DMA((2,)),
                        pltpu.VMEM((2, tm * SUBLANES, LANES), F32), pltpu.SemaphoreType.DMA((2,)),
                        pltpu.VMEM((tb * SUBLANES, LANES), F32), pltpu.SemaphoreType.DMA(())],
    )
    return pl.pallas_call(
        functools.partial(_dispatch_kernel, tm=tm, tb=tb, min_used=-(-(n * TOP_K) // tb)),
        out_shape=jax.ShapeDtypeStruct((cap_rows * SUBLANES, LANES), F32),
        grid_spec=grid_spec,
        compiler_params=_cparams(1),
        name="dispatch",
    )(pad_start, dest.reshape(n // tm, tm * TOP_K), h3)


def _ffn_kernel(be_ref, nu_ref, x_ref, wgu_ref, bgu_ref, wdn_ref, bdn_ref, perm_ref, o_ref,
                wg_s, wu_s, wd_s, *, tb):
    i = pl.program_id(0)
    nu = nu_ref[0]
    new_expert = jnp.logical_or(i == 0, be_ref[i] != be_ref[jnp.maximum(i - 1, 0)])

    @pl.when(jnp.logical_and(new_expert, i < nu))
    def _():
        for t in range(2 * D_FF // DEINT_IN):
            r = _dot(wgu_ref[:, t * DEINT_IN:(t + 1) * DEINT_IN].astype(BF16), perm_ref[...])
            wg_s[:, t * LANES:(t + 1) * LANES] = r[:, 0:LANES].astype(BF16)
            wu_s[:, t * LANES:(t + 1) * LANES] = r[:, LANES:DEINT_IN].astype(BF16)
        wd_s[...] = wdn_ref[...].astype(BF16)

    @pl.when(i < nu)
    def _():
        x = jnp.concatenate([x_ref[pl.ds(j, tb, stride=SUBLANES), :].astype(BF16)
                             for j in range(D_MODEL // LANES)], axis=1)
        bgu = bgu_ref[...]
        g = jnp.minimum(_dot(x, wg_s[...]) + bgu[:, 0:D_FF], SWIGLU_LIMIT)
        u = jnp.clip(_dot(x, wu_s[...]) + bgu[:, D_FF:2 * D_FF], -SWIGLU_LIMIT, SWIGLU_LIMIT)
        hh = (u + 1.0) * (g * _sigmoid(SWIGLU_ALPHA * g))
        out = _dot(hh.astype(BF16), wd_s[...]) + bdn_ref[...]
        for j in range(D_MODEL // LANES):
            o_ref[pl.ds(j, tb, stride=SUBLANES), :] = out[:, j * LANES:(j + 1) * LANES]

    @pl.when(i >= nu)
    def _():
        o_ref[...] = jnp.zeros_like(o_ref)


def _ffn(blk_e, n_used, xr3, nblk, wgu, bgu, wdn, bdn, tb):
    perm = _deint_perm()
    wspec = lambda a: pl.BlockSpec((None,) + a.shape[1:], lambda i, be, nu: (be[i], 0, 0))
    grid_spec = pltpu.PrefetchScalarGridSpec(
        num_scalar_prefetch=2,
        grid=(nblk,),
        in_specs=[pl.BlockSpec((tb * SUBLANES, LANES), lambda i, be, nu: (jnp.minimum(i, nu[0] - 1), 0)),
                  wspec(wgu), wspec(bgu), wspec(wdn), wspec(bdn),
                  pl.BlockSpec(perm.shape, lambda i, be, nu: (0, 0))],
        out_specs=pl.BlockSpec((tb * SUBLANES, LANES), lambda i, be, nu: (i, 0)),
        scratch_shapes=[pltpu.VMEM((D_MODEL, D_FF), BF16), pltpu.VMEM((D_MODEL, D_FF), BF16),
                        pltpu.VMEM((D_FF, D_MODEL), BF16)],
    )
    return pl.pallas_call(
        functools.partial(_ffn_kernel, tb=tb),
        out_shape=jax.ShapeDtypeStruct((nblk * tb * SUBLANES, LANES), F32),
        grid_spec=grid_spec,
        compiler_params=_cparams(1),
        name="moe_ffn",
    )(blk_e, n_used, xr3, wgu, bgu, wdn, bdn, perm)


def _combine_kernel(dest_hbm, yr3_hbm, tg_ref, h_ref, lg_ref, lb_ref, op_ref, os_ref,
                    idx_smem, idx_sem, ybuf, row_sem, *, tm, n_prompt_tiles, n_steps):
    i = pl.program_id(0)
    rows = tm * TOP_K

    def compute(slot):
        tg = tg_ref[...]
        stride = TOP_K * SUBLANES
        pieces = []
        for j in range(D_MODEL // LANES):
            acc = tg[:, 0:1] * ybuf[slot, pl.ds(j, tm, stride=stride), :]
            for k in range(1, TOP_K):
                acc = acc + tg[:, k:k + 1] * ybuf[slot, pl.ds(k * SUBLANES + j, tm, stride=stride), :]
            pieces.append(acc)
        y = jnp.concatenate(pieces, axis=1)
        out = _layer_norm(DN_ALPHA * h_ref[...] + y, lg_ref[...], lb_ref[...])

        @pl.when(i < n_prompt_tiles)
        def _():
            op_ref[...] = out

        @pl.when(i >= n_prompt_tiles)
        def _():
            os_ref[...] = out

    _row_pipeline(i, n_steps, dest_hbm, yr3_hbm, idx_smem, idx_sem, ybuf, row_sem, rows, compute)


def _combine(dest, yr3, tg, h, lg, lb, n_prompt, tm):
    m = h.shape[0]
    npt = n_prompt // tm
    row = lambda w: pl.BlockSpec((tm, w), lambda i: (i, 0))
    par = lambda a: pl.BlockSpec(a.shape, lambda i: (0, 0))
    return pl.pallas_call(
        functools.partial(_combine_kernel, tm=tm, n_prompt_tiles=npt, n_steps=m // tm),
        out_shape=(jax.ShapeDtypeStruct((n_prompt, D_MODEL), F32),
                   jax.ShapeDtypeStruct((m - n_prompt, D_MODEL), F32)),
        grid=(m // tm,),
        in_specs=[pl.BlockSpec(memory_space=pl.ANY), pl.BlockSpec(memory_space=pl.ANY),
                  row(LANES), row(D_MODEL), par(lg), par(lb)],
        out_specs=(pl.BlockSpec((tm, D_MODEL), lambda i: (jnp.minimum(i, npt - 1), 0)),
                   pl.BlockSpec((tm, D_MODEL), lambda i: (jnp.maximum(i - npt, 0), 0))),
        scratch_shapes=_fetch_scratch(tm * TOP_K),
        compiler_params=_cparams(1),
        name="combine",
    )(dest.reshape(m // tm, tm * TOP_K), yr3, tg, h, lg, lb)


def _rope_tables(pos):
    half = ROPE_DIM // 2
    inv = ROPE_THETA ** (-jnp.arange(half, dtype=F32) / half)
    ang = pos.astype(F32)[:, None] * inv[None, :]
    cos, sin = jnp.cos(ang), jnp.sin(ang)
    return jnp.concatenate([cos] * 4, axis=1), jnp.concatenate([-sin, sin] * 2, axis=1)


def _pad_cols(w, width):
    return jnp.pad(w, ((0, 0), (0, width - w.shape[1])))


def _routing(top_i, n_blocks, tb):
    n = top_i.shape[0]
    nk = n * TOP_K
    flat_e = top_i.reshape(-1)
    onehot = (flat_e[:, None] == jnp.arange(N_EXPERTS, dtype=jnp.int32)[None, :]).astype(jnp.int32)
    csum = jnp.cumsum(onehot, axis=0)
    rank = jnp.sum(onehot * csum, axis=1) - 1
    counts = csum[-1]
    nblk_e = (counts + tb - 1) // tb
    pend = jnp.cumsum(nblk_e)
    pstart = (pend - nblk_e) * tb
    dest = (pstart[flat_e] + rank).astype(jnp.int32)
    n_used = pend[-1].astype(jnp.int32)
    pad_start = jnp.concatenate([(pstart + counts).astype(jnp.int32), n_used.reshape(1)])
    bidx = jnp.minimum(jnp.arange(n_blocks, dtype=jnp.int32), n_used - 1)
    blk_e = jnp.minimum(jnp.sum((pend[None, :] <= bidx[:, None]).astype(jnp.int32), axis=1), N_EXPERTS - 1)
    return dest, pad_start, blk_e, n_used.reshape(1)


def kernel(x_prompt, x_sample, cache_kv_latent, cache_k_rope, state_ssm, state_conv, page_table, w_in, ssd_conv_w, ssd_conv_b, ssd_dt_bias, ssd_a_log, ssd_d, ssd_norm_g, mla_q_norm_g, w_q_b, mla_kv_norm_g, w_kv_b, w_branch_a, w_branch_b, w_out, ln1_g, ln1_b, router_w, router_b, exp_w_gu, exp_b_gu, exp_w_down, exp_b_down, ln2_g, ln2_b):
    b_p, t_p, _ = x_prompt.shape
    b_s, t_s, _ = x_sample.shape
    n_pages = page_table.shape[1]
    page = cache_kv_latent.shape[2]
    past_len = n_pages * page
    n_p = b_p * t_p
    n_s = b_s * t_s
    n = n_p + n_s
    assert DEPTH == 1 and w_in.shape[0] == 1
    assert n_p % PROJ_TILE == 0 and t_p % ROW_TILE == 0 and n_s % ROW_TILE == 0
    assert t_p % (2 * ATTN_TQ) == 0 and t_p % SSD_CHUNK == 0
    assert t_s == SUBLANES and n_pages % (2 * PAGE_GROUP) == 0 and page == LANES

    wi = w_in[0]
    seg = lambda k: wi[:, IN_OFFS[k]:IN_OFFS[k + 1]]
    half = ROPE_DIM // 2
    w_kr = seg(2)
    w_kr_rot = jnp.concatenate([w_kr[:, half:], w_kr[:, :half]], axis=1)
    w_mla = jnp.concatenate([seg(0), seg(1), _pad_cols(w_kr, LANES), _pad_cols(w_kr_rot, LANES),
                             _pad_cols(seg(5), LANES)], axis=1).astype(BF16)
    w_z = seg(3).astype(BF16)
    w_xbc = seg(4).astype(BF16)
    w_gates = jnp.concatenate([seg(6), seg(7)], axis=1).astype(BF16)
    wq3 = w_q_b[0].reshape(Q_LORA, MLA_HEADS, QK_NOPE + ROPE_DIM)
    wq_nope = wq3[:, :, :QK_NOPE].reshape(Q_LORA, MLA_HEADS * QK_NOPE)
    wq_rope = wq3[:, :, QK_NOPE:]
    wq_rope_rot = jnp.concatenate([wq_rope[..., half:], wq_rope[..., :half]], axis=-1)
    wq = jnp.concatenate([wq_nope, wq_rope.reshape(Q_LORA, -1), wq_rope_rot.reshape(Q_LORA, -1)],
                         axis=1).astype(BF16)
    wkv3 = w_kv_b[0].reshape(KV_LORA, MLA_HEADS, QK_NOPE + V_HEAD)
    wuk = jnp.transpose(wkv3[:, :, :QK_NOPE], (1, 2, 0)).astype(BF16)
    wuv = jnp.transpose(wkv3[:, :, QK_NOPE:], (1, 0, 2)).astype(BF16)
    qg = mla_q_norm_g[0].reshape(1, Q_LORA)
    kvg = mla_kv_norm_g[0].reshape(1, KV_LORA)
    cw = ssd_conv_w[0]
    cb = ssd_conv_b[0].reshape(1, CONV_DIM)
    dtb = _pad_cols(ssd_dt_bias[0].reshape(1, SSD_HEADS), LANES)
    a_neg = _pad_cols(-jnp.exp(ssd_a_log[0].astype(F32)).reshape(1, SSD_HEADS), LANES)
    d_x = jnp.repeat(ssd_d[0].astype(F32), SSD_HEAD_DIM).reshape(1, D_INNER)
    e_mat = jnp.asarray(np.arange(LANES)[:, None] == np.arange(D_INNER)[None, :] // SSD_HEAD_DIM, BF16)
    ng = ssd_norm_g[0].reshape(1, D_INNER)
    wa = w_branch_a[0].astype(BF16)
    wb = w_branch_b[0].astype(BF16)
    wo = w_out[0].astype(BF16)
    rw = _pad_cols(router_w[0], LANES)
    rwh = rw.astype(BF16)
    rwl = (rw - rwh.astype(F32)).astype(BF16)
    rb = jnp.concatenate([router_b[0].reshape(1, N_EXPERTS),
                          jnp.full((1, LANES - N_EXPERTS), NEG, F32)], axis=1)
    bgu = jnp.concatenate([exp_b_gu[0][:, 0::2], exp_b_gu[0][:, 1::2]], axis=1).reshape(N_EXPERTS, 1, 2 * D_FF)
    bdn = exp_b_down[0].reshape(N_EXPERTS, 1, D_MODEL)
    row2 = lambda v: v[0].reshape(1, D_MODEL)

    cos_p, sin_p = _rope_tables(jnp.arange(t_p, dtype=jnp.int32))
    cos_s, sin_s = _rope_tables(jnp.tile(past_len + jnp.arange(t_s, dtype=jnp.int32), b_s))

    xp = x_prompt.reshape(n_p, D_MODEL)
    xs = x_sample.reshape(n_s, D_MODEL)
    qlat_p, qrope_p, c_p, cbf_p, kr_p, krbf_p, dt_p = _mla_in(xp, w_mla, qg, kvg, wq, wuk, cos_p, sin_p, ROW_TILE)
    qlat_s, qrope_s, c_s, _, kr_s, _, dt_s = _mla_in(xs, w_mla, qg, kvg, wq, wuk, cos_s, sin_s, ROW_TILE)
    z_p, xbc_p, gates_p = (_proj(xp, w, PROJ_TILE) for w in (w_z, w_xbc, w_gates))
    z_s, xbc_s, gates_s = (_proj(xs, w, ROW_TILE) for w in (w_z, w_xbc, w_gates))

    conv0_p = jnp.zeros((b_p, SUBLANES, CONV_DIM), F32)
    conv0_s = jnp.pad(state_conv[0], ((0, 0), (SUBLANES - (D_CONV - 1), 0), (0, 0)))
    ya_p, ssm_p, ctail_p = _ssd(xbc_p, z_p, dt_p, conv0_p, None, cw, cb, dtb, a_neg, d_x, ng, e_mat,
                                nb=b_p, nc=t_p // SSD_CHUNK, t_in=SSD_CHUNK, row0=0, y_dtype=BF16)
    ya_s, ssm_s, ctail_s = _ssd(xbc_s, z_s, dt_s, conv0_s, state_ssm[0].reshape(b_s, D_INNER, D_STATE),
                                cw, cb, dtb, a_neg, d_x, ng, e_mat,
                                nb=b_s, nc=1, t_in=t_s, row0=0, y_dtype=F32)
    conv_p = ctail_p[:, SUBLANES - (D_CONV - 1):, :]
    conv_s = ctail_s[:, SUBLANES - (D_CONV - 1):, :]

    o_p = _attn_prompt(qlat_p, qrope_p, cbf_p, krbf_p, wuv, nb=b_p, t=t_p, tq=ATTN_TQ)
    ql_s = jnp.transpose(qlat_s.reshape(MLA_HEADS, b_s, t_s, KV_LORA), (1, 0, 2, 3))
    qr_s = jnp.transpose(qrope_s.reshape(MLA_HEADS, b_s, t_s, ROPE_DIM), (1, 0, 2, 3))
    cnew_s = jnp.pad(c_s.reshape(b_s, t_s, KV_LORA), ((0, 0), (0, LANES - t_s), (0, 0)))
    krnew_s = jnp.swapaxes(jnp.pad(kr_s.reshape(b_s, t_s, ROPE_DIM), ((0, 0), (0, LANES - t_s), (0, 0))), 1, 2)
    olat_s = _attn_sample(page_table, ql_s.reshape(b_s, MLA_HEADS * t_s, KV_LORA),
                          qr_s.reshape(b_s, MLA_HEADS * t_s, ROPE_DIM), cnew_s, krnew_s,
                          cache_kv_latent[0], jnp.swapaxes(cache_k_rope[0], 1, 2),
                          t_new=t_s, gp=PAGE_GROUP)
    olat_s = jnp.transpose(olat_s.reshape(b_s, MLA_HEADS, t_s, KV_LORA), (1, 0, 2, 3)).reshape(MLA_HEADS, n_s, KV_LORA)
    o_s = _head_proj(olat_s, wuv)

    h, h3, top_i, top_g = _merge(ya_p, ya_s.astype(BF16), o_p, o_s, gates_p, gates_s, xp, xs, wa, wb, wo,
                                 row2(ln1_g), row2(ln1_b), rwh, rwl, rb, ROW_TILE)

    tb = MOE_TB
    n_blocks = -(-(n * TOP_K + N_EXPERTS * (tb - 1)) // tb)
    dest, pad_start, blk_e, n_used = _routing(top_i[:, :TOP_K], n_blocks, tb)
    xr3 = _dispatch(pad_start, dest, h3, (n_blocks + 1) * tb, COMBINE_TILE, tb)
    yr3 = _ffn(blk_e, n_used, xr3, n_blocks, exp_w_gu[0], bgu, exp_w_down[0], bdn, tb)
    y_p, y_s = _combine(dest, yr3, top_g, h, row2(ln2_g), row2(ln2_b), n_p, COMBINE_TILE)

    return (y_p.reshape(b_p, t_p, D_MODEL), y_s.reshape(b_s, t_s, D_MODEL),
            c_p.reshape(1, b_p, t_p, KV_LORA), kr_p.reshape(1, b_p, t_p, ROPE_DIM),
            ssm_p.reshape(1, b_p, SSD_HEADS, SSD_HEAD_DIM, D_STATE), conv_p[None],
            c_s.reshape(1, b_s, t_s, KV_LORA), kr_s.reshape(1, b_s, t_s, ROPE_DIM),
            ssm_s.reshape(1, b_s, SSD_HEADS, SSD_HEAD_DIM, D_STATE), conv_s[None])
```

```python
import functools

import jax
import jax.numpy as jnp
import numpy as np
from jax import lax
from jax.experimental import pallas as pl
from jax.experimental.pallas import tpu as pltpu

F32 = jnp.float32
BF16 = jnp.bfloat16

D_MODEL = 1024
D_INNER = 2048
SSD_HEAD_DIM = 64
SSD_HEADS = 32
SSD_GROUPS = 4
HEADS_PER_GROUP = 8
D_STATE = 128
D_CONV = 4
CONV_DIM = D_INNER + 2 * SSD_GROUPS * D_STATE
SSD_CHUNK = 128
MLA_HEADS = 8
Q_LORA = 256
KV_LORA = 256
QK_NOPE = 128
ROPE_DIM = 64
V_HEAD = 128
ROPE_THETA = 10000.0
ATTN_SCALE = (QK_NOPE + ROPE_DIM) ** -0.5
N_EXPERTS = 32
TOP_K = 4
D_FF = 1024
SWIGLU_ALPHA = 1.702
SWIGLU_LIMIT = 7.0
DEPTH = 1
DN_ALPHA = (2 * DEPTH) ** 0.25
LN_EPS = 1e-5
RMS_EPS = 1e-6
IN_SIZES = (Q_LORA, KV_LORA, ROPE_DIM, D_INNER, CONV_DIM, SSD_HEADS, D_MODEL, D_MODEL)
IN_OFFS = tuple(int(v) for v in np.cumsum((0,) + IN_SIZES))

LANES = 128
SUBLANES = 8
VMEM_LIMIT = 56 * 1024 * 1024

NEG = -1e30

ROW_TILE = 256
PROJ_TILE = 512
ATTN_TQ = 256
PAGE_GROUP = 32
SAMPLE_CHUNK = 32
MOE_TB = 256
COMBINE_TILE = 128
MLA_A_COLS = 896


def _cparams(n_axes):
    return pltpu.CompilerParams(dimension_semantics=("arbitrary",) * n_axes,
                                vmem_limit_bytes=VMEM_LIMIT)


def _dot(a, b):
    return jnp.dot(a, b, preferred_element_type=F32)


def _dot_nt(a, b):
    return lax.dot_general(a, b, (((1,), (1,)), ((), ())), preferred_element_type=F32)


def _split3(x):
    hi = x.astype(BF16)
    r1 = x - hi.astype(F32)
    mid = r1.astype(BF16)
    lo = (r1 - mid.astype(F32)).astype(BF16)
    return hi, mid, lo


def _sigmoid(x):
    return 1.0 / (1.0 + jnp.exp(-x))


def _layer_norm(x, g, b):
    mu = jnp.mean(x, axis=-1, keepdims=True)
    xc = x - mu
    var = jnp.mean(xc * xc, axis=-1, keepdims=True)
    return xc * lax.rsqrt(var + LN_EPS) * g + b


def _rms_norm(x, g):
    return x * lax.rsqrt(jnp.mean(x * x, axis=-1, keepdims=True) + RMS_EPS) * g


def _proj_kernel(x_ref, w_ref, o_ref):
    o_ref[...] = _dot(x_ref[...].astype(BF16), w_ref[...])


def _proj(x, w, tm):
    m, k = x.shape
    n = w.shape[1]
    return pl.pallas_call(
        _proj_kernel,
        out_shape=jax.ShapeDtypeStruct((m, n), F32),
        grid=(m // tm,),
        in_specs=[pl.BlockSpec((tm, k), lambda i: (i, 0)),
                  pl.BlockSpec((k, n), lambda i: (0, 0))],
        out_specs=pl.BlockSpec((tm, n), lambda i: (i, 0)),
        compiler_params=_cparams(1),
        name="proj",
    )(x, w)


def _mla_in_kernel(x_ref, wa_ref, qg_ref, kvg_ref, wq_ref, wuk_ref, cos_ref, sin_ref,
                   qlat_ref, qrope_ref, c_ref, cbf_ref, kr_ref, krbf_ref, dt_ref):
    xb = x_ref[...].astype(BF16)
    pa = _dot(xb, wa_ref[...])
    q_a = pa[:, 0:256]
    kv_a = pa[:, 256:512]
    kr_raw = pa[:, 512:640]
    kr_rot = pa[:, 640:768]
    dt_ref[...] = pa[:, 768:896]
    cos = cos_ref[...]
    sin = sin_ref[...]
    c = _rms_norm(kv_a, kvg_ref[...])
    c_ref[...] = c
    cbf_ref[...] = c.astype(BF16)
    kr = kr_raw * cos + kr_rot * sin
    kr_ref[...] = kr[:, 0:ROPE_DIM]
    krbf_ref[...] = kr[:, 0:ROPE_DIM].astype(BF16)
    qn = _rms_norm(q_a, qg_ref[...]).astype(BF16)
    qq = _dot(qn, wq_ref[...])
    cos4 = jnp.concatenate([cos] * 4, axis=1)
    sin4 = jnp.concatenate([sin] * 4, axis=1)
    qr = (qq[:, 1024:1536] * cos4 + qq[:, 1536:2048] * sin4) * ATTN_SCALE
    for h in range(MLA_HEADS):
        qrope_ref[h] = qr[:, h * ROPE_DIM:(h + 1) * ROPE_DIM].astype(BF16)
        nope = qq[:, h * QK_NOPE:(h + 1) * QK_NOPE].astype(BF16)
        qlat_ref[h] = (_dot(nope, wuk_ref[h]) * ATTN_SCALE).astype(BF16)


def _mla_in(x, wa, qg, kvg, wq, wuk, cos_t, sin_t, tm):
    m = x.shape[0]
    tab_blocks = cos_t.shape[0] // tm
    row = lambda w: pl.BlockSpec((tm, w), lambda i: (i, 0))
    tab = pl.BlockSpec((tm, LANES), lambda i: (i % tab_blocks, 0))
    full2 = lambda a: pl.BlockSpec(a.shape, lambda i: (0, 0))
    full3 = lambda a: pl.BlockSpec(a.shape, lambda i: (0, 0, 0))
    return pl.pallas_call(
        _mla_in_kernel,
        out_shape=(jax.ShapeDtypeStruct((MLA_HEADS, m, KV_LORA), BF16),
                   jax.ShapeDtypeStruct((MLA_HEADS, m, ROPE_DIM), BF16),
                   jax.ShapeDtypeStruct((m, KV_LORA), F32),
                   jax.ShapeDtypeStruct((m, KV_LORA), BF16),
                   jax.ShapeDtypeStruct((m, ROPE_DIM), F32),
                   jax.ShapeDtypeStruct((m, ROPE_DIM), BF16),
                   jax.ShapeDtypeStruct((m, LANES), F32)),
        grid=(m // tm,),
        in_specs=[row(D_MODEL), full2(wa), full2(qg), full2(kvg), full2(wq), full3(wuk), tab, tab],
        out_specs=(pl.BlockSpec((MLA_HEADS, tm, KV_LORA), lambda i: (0, i, 0)),
                   pl.BlockSpec((MLA_HEADS, tm, ROPE_DIM), lambda i: (0, i, 0)),
                   row(KV_LORA), row(KV_LORA), row(ROPE_DIM), row(ROPE_DIM), row(LANES)),
        compiler_params=_cparams(1),
        name="mla_in",
    )(x, wa, qg, kvg, wq, wuk, cos_t, sin_t)


def _expand_heads(v, e_ref, terms):
    out = None
    rem = v
    for _ in range(terms):
        part = rem.astype(BF16)
        rem = rem - part.astype(F32)
        d = _dot(part, e_ref[...])
        out = d if out is None else out + d
    return out


def _conv_silu(ext, r0, rows, cw_ref, cb_ref, out, out_r0):
    for j in range(CONV_DIM // LANES):
        sl = slice(j * LANES, (j + 1) * LANES)
        acc = cb_ref[:, sl]
        for k in range(D_CONV):
            acc = acc + ext[pl.ds(r0 + SUBLANES - (D_CONV - 1) + k, rows), sl] * cw_ref[k:k + 1, sl]
        out[out_r0:out_r0 + rows, sl] = acc * _sigmoid(acc)


def _ssd_kernel(*refs, t_in, L, has_h0):
    n_pairs = SSD_HEADS // 2
    refs = list(refs)
    hT = refs[-n_pairs:]
    it = iter(refs[:-n_pairs])
    xbc_ref, z_ref, dt_ref, conv0_ref = next(it), next(it), next(it), next(it)
    h0_ref = next(it) if has_h0 else None
    (cw_ref, cb_ref, dtb_ref, a_ref, dx_ref, ng_ref, e_ref, y_ref, hfin_ref, ctail_ref,
     ext, xcv, ybuf, acs_s, acsT_s, dtT_s, eax_s, wtx_s) = it
    c = pl.program_id(1)
    nc = pl.num_programs(1)

    @pl.when(c == 0)
    def _():
        ext[0:SUBLANES, :] = conv0_ref[0]
        if t_in < L:
            ext[SUBLANES:SUBLANES + L, :] = jnp.zeros((L, CONV_DIM), F32)
        for j in range(n_pairs):
            if has_h0:
                hT[j][...] = h0_ref[0, j * LANES:(j + 1) * LANES, :].T
            else:
                hT[j][...] = jnp.zeros((D_STATE, LANES), F32)

    ext[SUBLANES:SUBLANES + t_in, :] = xbc_ref[...]

    @pl.when(c == nc - 1)
    def _():
        ctail_ref[0] = ext[t_in:t_in + SUBLANES, :]

    _conv_silu(ext, 0, L, cw_ref, cb_ref, xcv, 0)
    ext[0:SUBLANES, :] = ext[L:L + SUBLANES, :]

    row = lax.broadcasted_iota(jnp.int32, (L, L), 0)
    col = lax.broadcasted_iota(jnp.int32, (L, L), 1)
    tri = row >= col
    tri_b = jnp.where(tri, 1.0, 0.0).astype(BF16)
    if t_in < L:
        dt_raw = jnp.concatenate([dt_ref[...], jnp.zeros((L - t_in, LANES), F32)], axis=0)
    else:
        dt_raw = dt_ref[...]
    xs = dt_raw + dtb_ref[...]
    dtv = jnp.maximum(xs, 0.0) + jnp.log1p(jnp.exp(-jnp.abs(xs)))
    if t_in < L:
        dtv = jnp.where(lax.broadcasted_iota(jnp.int32, (L, LANES), 0) < t_in, dtv, 0.0)
    da = dtv * a_ref[...]
    hi, mid, lo = _split3(da)
    acs = _dot(tri_b, hi) + _dot(tri_b, mid) + _dot(tri_b, lo)
    acs_s[...] = acs
    acsT_s[...] = acs.T
    dtT_s[...] = dtv.T
    eax_s[...] = _expand_heads(jnp.exp(acs), e_ref, 3)
    wtx_s[...] = _expand_heads(jnp.exp(acs[L - 1:L, :] - acs) * dtv, e_ref, 2)

    low_half = lax.broadcasted_iota(jnp.int32, (L, LANES), 1) < SSD_HEAD_DIM
    pairs_per_group = HEADS_PER_GROUP // 2
    for g in range(SSD_GROUPS):
        bo = D_INNER + g * D_STATE
        co = D_INNER + SSD_GROUPS * D_STATE + g * D_STATE
        bm = xcv[:, bo:bo + D_STATE]
        cmb = xcv[:, co:co + D_STATE].astype(BF16)
        cb = _dot_nt(cmb, bm.astype(BF16))
        bmT = bm.T.astype(BF16)
        for k in range(pairs_per_group):
            kk = g * pairs_per_group + k
            tl = slice(kk * LANES, (kk + 1) * LANES)
            x2 = xcv[:, tl]
            ws = []
            for idx in (2 * kk, 2 * kk + 1):
                seg = acs_s[:, idx:idx + 1] - acsT_s[idx:idx + 1, :]
                w = cb * jnp.exp(jnp.where(tri, seg, NEG)) * dtT_s[idx:idx + 1, :]
                ws.append(w.astype(BF16))
            w2 = jnp.concatenate(ws, axis=1)
            xbd = jnp.concatenate([jnp.where(low_half, x2, 0.0).astype(BF16),
                                   jnp.where(low_half, 0.0, x2).astype(BF16)], axis=0)
            yd = _dot(w2, xbd)
            hp = hT[kk][...]
            yo = _dot(cmb, hp.astype(BF16))
            ybuf[:, tl] = yd + yo * eax_s[:, tl] + dx_ref[:, tl] * x2
            s = _dot(bmT, (x2 * wtx_s[:, tl]).astype(BF16))
            hT[kk][...] = eax_s[L - 1:L, tl] * hp + s

    gw = D_INNER // SSD_GROUPS
    for g in range(SSD_GROUPS):
        sl = slice(g * gw, (g + 1) * gw)
        zg = z_ref[:, sl]
        v = ybuf[0:t_in, sl] * (zg * _sigmoid(zg))
        ms = jnp.mean(v * v, axis=-1, keepdims=True)
        y_ref[:, sl] = (v * lax.rsqrt(ms + RMS_EPS) * ng_ref[:, sl]).astype(y_ref.dtype)

    @pl.when(c == nc - 1)
    def _():
        for j in range(n_pairs):
            hfin_ref[0, j * LANES:(j + 1) * LANES, :] = hT[j][...].T


def _ssd(xbc, z, dt, conv0, h0, cw, cb, dtb, a_neg, d_x, ng, e_mat, *, nb, nc, t_in, chunk, y_dtype):
    L = chunk
    has_h0 = h0 is not None
    rowspec = lambda w: pl.BlockSpec((t_in, w), lambda b, c: (b * nc + c, 0))
    par = lambda a: pl.BlockSpec(a.shape, lambda b, c: (0, 0))
    per_seq = lambda r, w: pl.BlockSpec((1, r, w), lambda b, c: (b, 0, 0))
    in_specs = [rowspec(CONV_DIM), rowspec(D_INNER), rowspec(LANES), per_seq(SUBLANES, CONV_DIM)]
    args = [xbc, z, dt, conv0]
    if has_h0:
        in_specs.append(per_seq(D_INNER, D_STATE))
        args.append(h0)
    params = [cw, cb, dtb, a_neg, d_x, ng, e_mat]
    in_specs += [par(p) for p in params]
    args += params
    return pl.pallas_call(
        functools.partial(_ssd_kernel, t_in=t_in, L=L, has_h0=has_h0),
        out_shape=(jax.ShapeDtypeStruct((nb * nc * t_in, D_INNER), y_dtype),
                   jax.ShapeDtypeStruct((nb, D_INNER, D_STATE), F32),
                   jax.ShapeDtypeStruct((nb, SUBLANES, CONV_DIM), F32)),
        grid=(nb, nc),
        in_specs=in_specs,
        out_specs=(rowspec(D_INNER), per_seq(D_INNER, D_STATE), per_seq(SUBLANES, CONV_DIM)),
        scratch_shapes=[pltpu.VMEM((L + 2 * SUBLANES, CONV_DIM), F32),
                        pltpu.VMEM((L, CONV_DIM), F32),
                        pltpu.VMEM((L, D_INNER), F32),
                        pltpu.VMEM((L, LANES), F32), pltpu.VMEM((LANES, L), F32), pltpu.VMEM((LANES, L), F32),
                        pltpu.VMEM((L, D_INNER), F32), pltpu.VMEM((L, D_INNER), F32)]
                       + [pltpu.VMEM((D_STATE, LANES), F32)] * (SSD_HEADS // 2),
        compiler_params=_cparams(2),
        name="ssd",
    )(*args)


def _softmax_step(s, kc, m_scr, l_scr, acc_scr):
    reps = s.shape[1] // LANES
    m_prev = m_scr[...]
    m_new = jnp.maximum(m_prev, jnp.max(s, axis=1, keepdims=True))
    alpha = jnp.exp(m_prev - m_new)
    p = jnp.exp(s - (jnp.concatenate([m_new] * reps, axis=1) if reps > 1 else m_new))
    l_scr[...] = alpha * l_scr[...] + jnp.sum(p, axis=1, keepdims=True)
    acc_scr[...] = acc_scr[...] * jnp.concatenate([alpha] * (KV_LORA // LANES), axis=1) + _dot(p.astype(BF16), kc)
    m_scr[...] = m_new


def _attn_prompt_kernel(qlat_ref, qrope_ref, c_ref, kr_ref, wuv_ref, o_ref, m_scr, l_scr, acc_scr, *, tq):
    qi = pl.program_id(1)
    R = MLA_HEADS * tq

    def scores(block, width):
        start = pl.multiple_of(block * width, width)
        kc = c_ref[pl.ds(start, width), :]
        kr = kr_ref[pl.ds(start, width), :]
        q = qlat_ref[...].reshape(R, KV_LORA)
        qr = qrope_ref[...].reshape(R, ROPE_DIM)
        return _dot_nt(q, kc) + _dot_nt(qr, kr), kc

    s, kc = scores(qi, tq)
    t = lax.broadcasted_iota(jnp.int32, (R, tq), 0) & (tq - 1)
    kcol = lax.broadcasted_iota(jnp.int32, (R, tq), 1)
    s = jnp.where(kcol <= t, s, NEG)
    m0 = jnp.max(s, axis=1, keepdims=True)
    p = jnp.exp(s - m0)
    m_scr[...] = jnp.broadcast_to(m0, m_scr.shape)
    l_scr[...] = jnp.broadcast_to(jnp.sum(p, axis=1, keepdims=True), l_scr.shape)
    acc_scr[...] = _dot(p.astype(BF16), kc)

    def body(j, carry):
        s2, kc2 = scores(j, 2 * tq)
        _softmax_step(s2, kc2, m_scr, l_scr, acc_scr)
        return carry

    lax.fori_loop(0, qi // 2, body, 0)

    @pl.when(qi % 2 == 1)
    def _():
        s1, kc1 = scores(qi - 1, tq)
        _softmax_step(s1, kc1, m_scr, l_scr, acc_scr)

    inv = 1.0 / l_scr[...]
    o_lat = acc_scr[...] * jnp.concatenate([inv] * (KV_LORA // LANES), axis=1)
    for h in range(MLA_HEADS):
        oh = _dot(o_lat[h * tq:(h + 1) * tq, :].astype(BF16), wuv_ref[h])
        o_ref[:, h * V_HEAD:(h + 1) * V_HEAD] = oh.astype(o_ref.dtype)


def _attn_prompt(qlat, qrope, cbf, krbf, wuv, *, nb, t, tq):
    nq = t // tq
    R = MLA_HEADS * tq
    return pl.pallas_call(
        functools.partial(_attn_prompt_kernel, tq=tq),
        out_shape=jax.ShapeDtypeStruct((nb * t, MLA_HEADS * V_HEAD), BF16),
        grid=(nb, nq),
        in_specs=[pl.BlockSpec((MLA_HEADS, tq, KV_LORA), lambda b, i: (0, b * nq + i, 0)),
                  pl.BlockSpec((MLA_HEADS, tq, ROPE_DIM), lambda b, i: (0, b * nq + i, 0)),
                  pl.BlockSpec((t, KV_LORA), lambda b, i: (b, 0)),
                  pl.BlockSpec((t, ROPE_DIM), lambda b, i: (b, 0)),
                  pl.BlockSpec(wuv.shape, lambda b, i: (0, 0, 0))],
        out_specs=pl.BlockSpec((tq, MLA_HEADS * V_HEAD), lambda b, i: (b * nq + i, 0)),
        scratch_shapes=[pltpu.VMEM((R, LANES), F32), pltpu.VMEM((R, LANES), F32),
                        pltpu.VMEM((R, KV_LORA), F32)],
        compiler_params=_cparams(2),
        name="attn_prompt",
    )(qlat, qrope, cbf, krbf, wuv)


def _attn_sample_kernel(pt_ref, qlat_ref, qrope_ref, cnew_ref, krnew_ref, kv_hbm, kr_hbm, o_ref,
                        kvbuf, krbuf, kv_sem, kr_sem, m_scr, l_scr, acc_scr, *, n_groups, gp, t_new):
    b = pl.program_id(0)
    nb = pl.num_programs(0)
    R = qlat_ref.shape[1]
    page = kvbuf.shape[2]
    q = qlat_ref[0]
    qr = qrope_ref[0]

    def start_group(bb, g, slot):
        for p in range(gp):
            pid = pt_ref[bb, g * gp + p]
            pltpu.make_async_copy(kv_hbm.at[pid], kvbuf.at[slot, p], kv_sem.at[slot]).start()
            pltpu.make_async_copy(kr_hbm.at[pid], krbuf.at[slot, p], kr_sem.at[slot]).start()

    def wait_group(slot):
        pltpu.make_async_copy(kvbuf.at[slot], kvbuf.at[slot], kv_sem.at[slot]).wait()
        pltpu.make_async_copy(krbuf.at[slot], krbuf.at[slot], kr_sem.at[slot]).wait()

    @pl.when(b == 0)
    def _():
        start_group(0, 0, 0)

    kc = cnew_ref[0].astype(BF16)
    kr = krnew_ref[0].astype(BF16)
    s = _dot_nt(q, kc) + _dot(qr, kr)
    t = lax.broadcasted_iota(jnp.int32, (R, LANES), 0) & (t_new - 1)
    kcol = lax.broadcasted_iota(jnp.int32, (R, LANES), 1)
    s = jnp.where(kcol <= t, s, NEG)
    m0 = jnp.max(s, axis=1, keepdims=True)
    p0 = jnp.exp(s - m0)
    m_scr[...] = jnp.broadcast_to(m0, m_scr.shape)
    l_scr[...] = jnp.broadcast_to(jnp.sum(p0, axis=1, keepdims=True), l_scr.shape)
    acc_scr[...] = _dot(p0.astype(BF16), kc)

    for g in range(n_groups):
        slot = g % 2
        if g + 1 < n_groups:
            start_group(b, g + 1, 1 - slot)
        else:
            @pl.when(b + 1 < nb)
            def _():
                start_group(b + 1, 0, 1 - slot)
        wait_group(slot)
        kc = kvbuf[slot].reshape(gp * page, KV_LORA).astype(BF16)
        kr = jnp.concatenate([krbuf[slot, p].astype(BF16) for p in range(gp)], axis=1)
        s = _dot_nt(q, kc) + _dot(qr, kr)
        _softmax_step(s, kc, m_scr, l_scr, acc_scr)

    inv = 1.0 / l_scr[...]
    o_ref[0] = acc_scr[...] * jnp.concatenate([inv] * (KV_LORA // LANES), axis=1)


def _attn_sample(page_table, qlat, qrope, cnew, krnew, cache_kv, cache_kr, *, t_new, gp):
    nb, n_pages = page_table.shape
    n_groups = n_pages // gp
    R = qlat.shape[1]
    page = cache_kv.shape[1]
    per_b = lambda r, w: pl.BlockSpec((1, r, w), lambda b, pt: (b, 0, 0))
    grid_spec = pltpu.PrefetchScalarGridSpec(
        num_scalar_prefetch=1,
        grid=(nb,),
        in_specs=[per_b(R, KV_LORA), per_b(R, ROPE_DIM), per_b(LANES, KV_LORA), per_b(ROPE_DIM, LANES),
                  pl.BlockSpec(memory_space=pl.ANY), pl.BlockSpec(memory_space=pl.ANY)],
        out_specs=per_b(R, KV_LORA),
        scratch_shapes=[pltpu.VMEM((2, gp, page, KV_LORA), F32), pltpu.VMEM((2, gp, ROPE_DIM, page), F32),
                        pltpu.SemaphoreType.DMA((2,)), pltpu.SemaphoreType.DMA((2,)),
                        pltpu.VMEM((R, LANES), F32), pltpu.VMEM((R, LANES), F32),
                        pltpu.VMEM((R, KV_LORA), F32)],
    )
    return pl.pallas_call(
        functools.partial(_attn_sample_kernel, n_groups=n_groups, gp=gp, t_new=t_new),
        out_shape=jax.ShapeDtypeStruct((nb, R, KV_LORA), F32),
        grid_spec=grid_spec,
        compiler_params=_cparams(1),
        name="attn_sample",
    )(page_table, qlat, qrope, cnew, krnew, cache_kv, cache_kr)


def _head_proj_kernel(x_ref, w_ref, o_ref):
    o_ref[...] = _dot(x_ref[0].astype(BF16), w_ref[0]).astype(o_ref.dtype)


def _head_proj(x, w):
    h, m, k = x.shape
    n = w.shape[2]
    return pl.pallas_call(
        _head_proj_kernel,
        out_shape=jax.ShapeDtypeStruct((m, h * n), BF16),
        grid=(h,),
        in_specs=[pl.BlockSpec((1, m, k), lambda i: (i, 0, 0)),
                  pl.BlockSpec((1, k, n), lambda i: (i, 0, 0))],
        out_specs=pl.BlockSpec((m, n), lambda i: (0, i)),
        compiler_params=_cparams(1),
        name="head_proj",
    )(x, w)


def _merge_kernel(ya_p_ref, ya_s_ref, o_p_ref, o_s_ref, g_p_ref, g_s_ref, u_p_ref, u_s_ref, wa_ref, wb_ref, wo_ref,
                  lg_ref, lb_ref, rwh_ref, rwl_ref, rb_ref,
                  h_ref, h3_ref, ti_ref, tg_ref, rank_ref, cnt_ref, run_ref, *, n_prompt_tiles):
    i = pl.program_id(0)
    is_p = i < n_prompt_tiles
    yn = jnp.where(is_p, ya_p_ref[...], ya_s_ref[...])
    ob = jnp.where(is_p, o_p_ref[...], o_s_ref[...])
    g = jnp.where(is_p, g_p_ref[...], g_s_ref[...])
    u = jnp.where(is_p, u_p_ref[...], u_s_ref[...])
    ya = _dot(yn, wa_ref[...])
    yb = _dot(ob, wb_ref[...])
    merged = _sigmoid(g[:, 0:D_MODEL]) * ya + _sigmoid(g[:, D_MODEL:2 * D_MODEL]) * yb
    mix = _dot(merged.astype(BF16), wo_ref[...])
    h = _layer_norm(DN_ALPHA * u + mix, lg_ref[...], lb_ref[...])
    h_ref[...] = h
    tm = h.shape[0]
    for j in range(D_MODEL // LANES):
        h3_ref[pl.ds(j, tm, stride=SUBLANES), :] = h[:, j * LANES:(j + 1) * LANES]
    hh = h.astype(BF16)
    hl = (h - hh.astype(F32)).astype(BF16)
    logits = _dot(hh, rwh_ref[...]) + _dot(hl, rwh_ref[...]) + _dot(hh, rwl_ref[...]) + rb_ref[...]
    lane = lax.broadcasted_iota(jnp.int32, logits.shape, 1).astype(F32)
    vals = jnp.zeros_like(logits)
    idxs = jnp.zeros_like(logits)
    cur = logits
    picked = []
    for k in range(TOP_K):
        mx = jnp.max(cur, axis=1, keepdims=True)
        ix = jnp.min(jnp.where(cur == mx, lane, float(LANES)), axis=1, keepdims=True)
        vals = jnp.where(lane == k, mx, vals)
        idxs = jnp.where(lane == k, ix, idxs)
        picked.append(lane == ix)
        cur = jnp.where(picked[k], NEG * 2.0, cur)
    v0 = vals[:, 0:1]
    e = jnp.where(lane < TOP_K, jnp.exp(vals - v0), 0.0)
    tg_ref[...] = e / jnp.sum(e, axis=1, keepdims=True)
    ti_ref[...] = idxs.astype(jnp.int32)

    @pl.when(i == 0)
    def _():
        run_ref[...] = jnp.zeros_like(run_ref)

    cnt = jnp.zeros_like(logits)
    for k in range(TOP_K):
        cnt = cnt + jnp.where(picked[k], 1.0, 0.0)
    r_i = lax.broadcasted_iota(jnp.int32, (tm, tm), 0)
    c_i = lax.broadcasted_iota(jnp.int32, (tm, tm), 1)
    before = _dot(jnp.where(r_i > c_i, 1.0, 0.0).astype(BF16), cnt.astype(BF16)) + run_ref[0:1, :]
    ranks = jnp.zeros_like(logits)
    for k in range(TOP_K):
        rk = jnp.sum(jnp.where(picked[k], before, 0.0), axis=1, keepdims=True)
        ranks = jnp.where(lane == k, rk, ranks)
    rank_ref[...] = ranks.astype(jnp.int32)
    run_ref[0:1, :] = run_ref[0:1, :] + jnp.sum(cnt, axis=0, keepdims=True)
    cnt_ref[...] = run_ref[...].astype(jnp.int32)


def _merge(ya_p, ya_s, o_p, o_s, g_p, g_s, u_p, u_s, wa, wb, wo, lg, lb, rwh, rwl, rb, tm):
    m = u_p.shape[0] + u_s.shape[0]
    npt = ya_p.shape[0] // tm
    row = lambda w: pl.BlockSpec((tm, w), lambda i: (i, 0))
    prow = lambda w: pl.BlockSpec((tm, w), lambda i: (jnp.minimum(i, npt - 1), 0))
    srow = lambda w: pl.BlockSpec((tm, w), lambda i: (jnp.maximum(i - npt, 0), 0))
    par = lambda a: pl.BlockSpec(a.shape, lambda i: (0, 0))
    return pl.pallas_call(
        functools.partial(_merge_kernel, n_prompt_tiles=npt),
        out_shape=(jax.ShapeDtypeStruct((m, D_MODEL), F32),
                   jax.ShapeDtypeStruct((m * SUBLANES, LANES), F32),
                   jax.ShapeDtypeStruct((m, LANES), jnp.int32),
                   jax.ShapeDtypeStruct((m, LANES), F32),
                   jax.ShapeDtypeStruct((m, LANES), jnp.int32),
                   jax.ShapeDtypeStruct((SUBLANES, LANES), jnp.int32)),
        grid=(m // tm,),
        in_specs=[prow(D_INNER), srow(D_INNER), prow(D_MODEL), srow(D_MODEL), prow(2 * D_MODEL), srow(2 * D_MODEL),
                  prow(D_MODEL), srow(D_MODEL),
                  par(wa), par(wb), par(wo), par(lg), par(lb), par(rwh), par(rwl), par(rb)],
        out_specs=(row(D_MODEL), pl.BlockSpec((tm * SUBLANES, LANES), lambda i: (i, 0)), row(LANES), row(LANES),
                   row(LANES), pl.BlockSpec((SUBLANES, LANES), lambda i: (0, 0))),
        scratch_shapes=[pltpu.VMEM((SUBLANES, LANES), F32)],
        compiler_params=_cparams(1),
        name="merge",
    )(ya_p, ya_s, o_p, o_s, g_p, g_s, u_p, u_s, wa, wb, wo, lg, lb, rwh, rwl, rb)


DEINT_IN = 2 * LANES


def _deint_perm():
    perm = np.zeros((DEINT_IN, DEINT_IN), np.float32)
    perm[2 * np.arange(LANES), np.arange(LANES)] = 1.0
    perm[2 * np.arange(LANES) + 1, LANES + np.arange(LANES)] = 1.0
    return jnp.asarray(perm, BF16)


def _row_pipeline(i, n_steps, idx_hbm, src_hbm, idx_smem, idx_sem, buf, row_sem, rows, compute):
    def idx_copy(chunk, s):
        return pltpu.make_async_copy(idx_hbm.at[chunk], idx_smem.at[s], idx_sem.at[s])

    def issue_row(s, r):
        src0 = pl.multiple_of(idx_smem[s, r] * SUBLANES, SUBLANES)
        dst0 = pl.multiple_of(r * SUBLANES, SUBLANES)
        pltpu.make_async_copy(src_hbm.at[pl.ds(src0, SUBLANES), :],
                              buf.at[s, pl.ds(dst0, SUBLANES), :], row_sem.at[s]).start()

    @pl.when(i == 0)
    def _():
        idx_copy(0, 0).start()
        idx_copy(0, 0).wait()

        def body(r, carry):
            issue_row(0, r)
            return carry
        lax.fori_loop(0, rows, body, 0, unroll=8)
        if n_steps > 1:
            idx_copy(1, 1).start()

    def step(slot):
        has_next = i + 1 < n_steps

        @pl.when(has_next)
        def _():
            idx_copy(i + 1, 1 - slot).wait()

        @pl.when(i + 2 < n_steps)
        def _():
            idx_copy(i + 2, slot).start()

        pltpu.make_async_copy(src_hbm.at[pl.ds(0, rows * SUBLANES), :], buf.at[slot], row_sem.at[slot]).wait()

        @pl.when(has_next)
        def _():
            for r in range(rows):
                issue_row(1 - slot, r)
            compute(slot)

        if (n_steps - 1) % 2 == slot:
            @pl.when(i + 1 == n_steps)
            def _():
                compute(slot)

    for slot in (0, 1):
        pl.when(lax.rem(i, 2) == slot)(functools.partial(step, slot))


def _fetch_scratch(rows):
    return [pltpu.SMEM((2, rows), jnp.int32), pltpu.SemaphoreType.DMA((2,)),
            pltpu.VMEM((2, rows * SUBLANES, LANES), F32), pltpu.SemaphoreType.DMA((2,))]


def _dispatch_kernel(pad_ref, dest_hbm, h3_ref, xr_hbm, idx_smem, idx_sem, stage, row_sem, zbuf, z_sem,
                     *, tm, tb, min_used):
    i = pl.program_id(0)
    n = pl.num_programs(0)
    rows = tm * TOP_K

    def idx_copy(chunk, s):
        return pltpu.make_async_copy(dest_hbm.at[chunk], idx_smem.at[s], idx_sem.at[s])

    def rows_wait(s):
        span = xr_hbm.at[pl.ds(0, rows * SUBLANES), :]
        pltpu.make_async_copy(span, span, row_sem.at[s]).wait()

    @pl.when(i == 0)
    def _():
        zbuf[...] = jnp.zeros_like(zbuf)
        zero_to = lambda start: pltpu.make_async_copy(zbuf, xr_hbm.at[pl.ds(start, tb * SUBLANES), :], z_sem)
        for e in range(N_EXPERTS):
            zero_to(pl.multiple_of(pad_ref[e] * SUBLANES, SUBLANES)).start()
        for e in range(N_EXPERTS):
            zero_to(0).wait()
        n_used = pad_ref[N_EXPERTS]
        for blk in range(min_used, xr_hbm.shape[0] // (tb * SUBLANES)):
            @pl.when(blk >= n_used)
            def _():
                zero_to(blk * tb * SUBLANES).start()
                zero_to(0).wait()
        idx_copy(0, 0).start()

    def step(slot):
        idx_copy(i, slot).wait()

        @pl.when(i + 1 < n)
        def _():
            idx_copy(i + 1, 1 - slot).start()

        stage[slot] = h3_ref[...]
        for r in range(rows):
            dst0 = pl.multiple_of(idx_smem[slot, r] * SUBLANES, SUBLANES)
            pltpu.make_async_copy(stage.at[slot, pl.ds((r // TOP_K) * SUBLANES, SUBLANES), :],
                                  xr_hbm.at[pl.ds(dst0, SUBLANES), :], row_sem.at[slot]).start()

        @pl.when(i >= 1)
        def _():
            rows_wait(1 - slot)

        @pl.when(i == n - 1)
        def _():
            rows_wait(slot)

    for slot in (0, 1):
        pl.when(lax.rem(i, 2) == slot)(functools.partial(step, slot))


def _dispatch(pad_start, dest, h3, cap_rows, tm, tb):
    n = h3.shape[0] // SUBLANES
    grid_spec = pltpu.PrefetchScalarGridSpec(
        num_scalar_prefetch=1,
        grid=(n // tm,),
        in_specs=[pl.BlockSpec(memory_space=pl.ANY),
                  pl.BlockSpec((tm * SUBLANES, LANES), lambda i, pad: (i, 0))],
        out_specs=pl.BlockSpec(memory_space=pl.ANY),
        scratch_shapes=[pltpu.SMEM((2, tm * TOP_K), jnp.int32), pltpu.SemaphoreType.DMA((2,)),
                        pltpu.VMEM((2, tm * SUBLANES, LANES), F32), pltpu.SemaphoreType.DMA((2,)),
                        pltpu.VMEM((tb * SUBLANES, LANES), F32), pltpu.SemaphoreType.DMA(())],
    )
    return pl.pallas_call(
        functools.partial(_dispatch_kernel, tm=tm, tb=tb, min_used=-(-(n * TOP_K) // tb)),
        out_shape=jax.ShapeDtypeStruct((cap_rows * SUBLANES, LANES), F32),
        grid_spec=grid_spec,
        compiler_params=_cparams(1),
        name="dispatch",
    )(pad_start, dest.reshape(n // tm, tm * TOP_K), h3)


def _ffn_kernel(be_ref, nu_ref, x_ref, wgu_ref, bgu_ref, wdn_ref, bdn_ref, perm_ref, o_ref,
                wg_s, wu_s, wd_s, *, tb):
    i = pl.program_id(0)
    nu = nu_ref[0]
    new_expert = jnp.logical_or(i == 0, be_ref[i] != be_ref[jnp.maximum(i - 1, 0)])

    @pl.when(jnp.logical_and(new_expert, i < nu))
    def _():
        for t in range(2 * D_FF // DEINT_IN):
            r = _dot(wgu_ref[:, t * DEINT_IN:(t + 1) * DEINT_IN].astype(BF16), perm_ref[...])
            wg_s[:, t * LANES:(t + 1) * LANES] = r[:, 0:LANES].astype(BF16)
            wu_s[:, t * LANES:(t + 1) * LANES] = r[:, LANES:DEINT_IN].astype(BF16)
        wd_s[...] = wdn_ref[...].astype(BF16)

    @pl.when(i < nu)
    def _():
        x = jnp.concatenate([x_ref[pl.ds(j, tb, stride=SUBLANES), :].astype(BF16)
                             for j in range(D_MODEL // LANES)], axis=1)
        bgu = bgu_ref[...]
        g = jnp.minimum(_dot(x, wg_s[...]) + bgu[:, 0:D_FF], SWIGLU_LIMIT)
        u = jnp.clip(_dot(x, wu_s[...]) + bgu[:, D_FF:2 * D_FF], -SWIGLU_LIMIT, SWIGLU_LIMIT)
        hh = (u + 1.0) * (g * _sigmoid(SWIGLU_ALPHA * g))
        out = _dot(hh.astype(BF16), wd_s[...]) + bdn_ref[...]
        for j in range(D_MODEL // LANES):
            o_ref[pl.ds(j, tb, stride=SUBLANES), :] = out[:, j * LANES:(j + 1) * LANES]

    @pl.when(i >= nu)
    def _():
        o_ref[...] = jnp.zeros_like(o_ref)


def _ffn(blk_e, n_used, xr3, nblk, wgu, bgu, wdn, bdn, tb):
    perm = _deint_perm()
    wspec = lambda a: pl.BlockSpec((None,) + a.shape[1:], lambda i, be, nu: (be[i], 0, 0))
    grid_spec = pltpu.PrefetchScalarGridSpec(
        num_scalar_prefetch=2,
        grid=(nblk,),
        in_specs=[pl.BlockSpec((tb * SUBLANES, LANES), lambda i, be, nu: (jnp.minimum(i, nu[0] - 1), 0)),
                  wspec(wgu), wspec(bgu), wspec(wdn), wspec(bdn),
                  pl.BlockSpec(perm.shape, lambda i, be, nu: (0, 0))],
        out_specs=pl.BlockSpec((tb * SUBLANES, LANES), lambda i, be, nu: (i, 0)),
        scratch_shapes=[pltpu.VMEM((D_MODEL, D_FF), BF16), pltpu.VMEM((D_MODEL, D_FF), BF16),
                        pltpu.VMEM((D_FF, D_MODEL), BF16)],
    )
    return pl.pallas_call(
        functools.partial(_ffn_kernel, tb=tb),
        out_shape=jax.ShapeDtypeStruct((nblk * tb * SUBLANES, LANES), F32),
        grid_spec=grid_spec,
        compiler_params=_cparams(1),
        name="moe_ffn",
    )(blk_e, n_used, xr3, wgu, bgu, wdn, bdn, perm)


def _combine_kernel(dest_hbm, yr3_hbm, tg_ref, h_ref, lg_ref, lb_ref, op_ref, os_ref,
                    idx_smem, idx_sem, ybuf, row_sem, *, tm, n_prompt_tiles, n_steps):
    i = pl.program_id(0)
    rows = tm * TOP_K

    def compute(slot):
        tg = tg_ref[...]
        stride = TOP_K * SUBLANES
        pieces = []
        for j in range(D_MODEL // LANES):
            acc = tg[:, 0:1] * ybuf[slot, pl.ds(j, tm, stride=stride), :]
            for k in range(1, TOP_K):
                acc = acc + tg[:, k:k + 1] * ybuf[slot, pl.ds(k * SUBLANES + j, tm, stride=stride), :]
            pieces.append(acc)
        y = jnp.concatenate(pieces, axis=1)
        out = _layer_norm(DN_ALPHA * h_ref[...] + y, lg_ref[...], lb_ref[...])

        @pl.when(i < n_prompt_tiles)
        def _():
            op_ref[...] = out

        @pl.when(i >= n_prompt_tiles)
        def _():
            os_ref[...] = out

    _row_pipeline(i, n_steps, dest_hbm, yr3_hbm, idx_smem, idx_sem, ybuf, row_sem, rows, compute)


def _combine(dest, yr3, tg, h, lg, lb, n_prompt, tm):
    m = h.shape[0]
    npt = n_prompt // tm
    row = lambda w: pl.BlockSpec((tm, w), lambda i: (i, 0))
    par = lambda a: pl.BlockSpec(a.shape, lambda i: (0, 0))
    return pl.pallas_call(
        functools.partial(_combine_kernel, tm=tm, n_prompt_tiles=npt, n_steps=m // tm),
        out_shape=(jax.ShapeDtypeStruct((n_prompt, D_MODEL), F32),
                   jax.ShapeDtypeStruct((m - n_prompt, D_MODEL), F32)),
        grid=(m // tm,),
        in_specs=[pl.BlockSpec(memory_space=pl.ANY), pl.BlockSpec(memory_space=pl.ANY),
                  row(LANES), row(D_MODEL), par(lg), par(lb)],
        out_specs=(pl.BlockSpec((tm, D_MODEL), lambda i: (jnp.minimum(i, npt - 1), 0)),
                   pl.BlockSpec((tm, D_MODEL), lambda i: (jnp.maximum(i - npt, 0), 0))),
        scratch_shapes=_fetch_scratch(tm * TOP_K),
        compiler_params=_cparams(1),
        name="combine",
    )(dest.reshape(m // tm, tm * TOP_K), yr3, tg, h, lg, lb)


def _rope_tables(pos):
    half = ROPE_DIM // 2
    inv = ROPE_THETA ** (-jnp.arange(half, dtype=F32) / half)
    ang = pos.astype(F32)[:, None] * inv[None, :]
    cos, sin = jnp.cos(ang), jnp.sin(ang)
    return jnp.concatenate([cos] * 4, axis=1), jnp.concatenate([-sin, sin] * 2, axis=1)


def _pad_cols(w, width):
    return jnp.pad(w, ((0, 0), (0, width - w.shape[1])))


def _routing(top_i, rank, counts, n_blocks, tb):
    flat_e = top_i.reshape(-1)
    nblk_e = (counts + tb - 1) // tb
    pend = jnp.cumsum(nblk_e)
    pstart = (pend - nblk_e) * tb
    onehot = flat_e[:, None] == jnp.arange(N_EXPERTS, dtype=jnp.int32)[None, :]
    dest = (jnp.sum(jnp.where(onehot, pstart[None, :], 0), axis=1) + rank.reshape(-1)).astype(jnp.int32)
    n_used = pend[-1].astype(jnp.int32)
    pad_start = jnp.concatenate([(pstart + counts).astype(jnp.int32), n_used.reshape(1)])
    bidx = jnp.minimum(jnp.arange(n_blocks, dtype=jnp.int32), n_used - 1)
    blk_e = jnp.minimum(jnp.sum((pend[None, :] <= bidx[:, None]).astype(jnp.int32), axis=1), N_EXPERTS - 1)
    return dest, pad_start, blk_e, n_used.reshape(1)


def kernel(x_prompt, x_sample, cache_kv_latent, cache_k_rope, state_ssm, state_conv, page_table, w_in, ssd_conv_w, ssd_conv_b, ssd_dt_bias, ssd_a_log, ssd_d, ssd_norm_g, mla_q_norm_g, w_q_b, mla_kv_norm_g, w_kv_b, w_branch_a, w_branch_b, w_out, ln1_g, ln1_b, router_w, router_b, exp_w_gu, exp_b_gu, exp_w_down, exp_b_down, ln2_g, ln2_b):
    b_p, t_p, _ = x_prompt.shape
    b_s, t_s, _ = x_sample.shape
    n_pages = page_table.shape[1]
    page = cache_kv_latent.shape[2]
    past_len = n_pages * page
    n_p = b_p * t_p
    n_s = b_s * t_s
    n = n_p + n_s
    assert DEPTH == 1 and w_in.shape[0] == 1
    assert n_p % PROJ_TILE == 0 and t_p % ROW_TILE == 0 and n_s % ROW_TILE == 0
    assert t_p % (2 * ATTN_TQ) == 0 and t_p % SSD_CHUNK == 0
    assert t_s == SUBLANES and n_pages % (2 * PAGE_GROUP) == 0 and page == LANES

    wi = w_in[0]
    seg = lambda k: wi[:, IN_OFFS[k]:IN_OFFS[k + 1]]
    half = ROPE_DIM // 2
    w_kr = seg(2)
    w_kr_rot = jnp.concatenate([w_kr[:, half:], w_kr[:, :half]], axis=1)
    w_mla = jnp.concatenate([seg(0), seg(1), _pad_cols(w_kr, LANES), _pad_cols(w_kr_rot, LANES),
                             _pad_cols(seg(5), LANES)], axis=1).astype(BF16)
    w_z = seg(3).astype(BF16)
    w_xbc = seg(4).astype(BF16)
    w_gates = jnp.concatenate([seg(6), seg(7)], axis=1).astype(BF16)
    wq3 = w_q_b[0].reshape(Q_LORA, MLA_HEADS, QK_NOPE + ROPE_DIM)
    wq_nope = wq3[:, :, :QK_NOPE].reshape(Q_LORA, MLA_HEADS * QK_NOPE)
    wq_rope = wq3[:, :, QK_NOPE:]
    wq_rope_rot = jnp.concatenate([wq_rope[..., half:], wq_rope[..., :half]], axis=-1)
    wq = jnp.concatenate([wq_nope, wq_rope.reshape(Q_LORA, -1), wq_rope_rot.reshape(Q_LORA, -1)],
                         axis=1).astype(BF16)
    wkv3 = w_kv_b[0].reshape(KV_LORA, MLA_HEADS, QK_NOPE + V_HEAD)
    wuk = jnp.transpose(wkv3[:, :, :QK_NOPE], (1, 2, 0)).astype(BF16)
    wuv = jnp.transpose(wkv3[:, :, QK_NOPE:], (1, 0, 2)).astype(BF16)
    qg = mla_q_norm_g[0].reshape(1, Q_LORA)
    kvg = mla_kv_norm_g[0].reshape(1, KV_LORA)
    cw = ssd_conv_w[0]
    cb = ssd_conv_b[0].reshape(1, CONV_DIM)
    dtb = _pad_cols(ssd_dt_bias[0].reshape(1, SSD_HEADS), LANES)
    a_neg = _pad_cols(-jnp.exp(ssd_a_log[0].astype(F32)).reshape(1, SSD_HEADS), LANES)
    d_x = jnp.repeat(ssd_d[0].astype(F32), SSD_HEAD_DIM).reshape(1, D_INNER)
    e_mat = jnp.asarray(np.arange(LANES)[:, None] == np.arange(D_INNER)[None, :] // SSD_HEAD_DIM, BF16)
    ng = ssd_norm_g[0].reshape(1, D_INNER)
    wa = w_branch_a[0].astype(BF16)
    wb = w_branch_b[0].astype(BF16)
    wo = w_out[0].astype(BF16)
    rw = _pad_cols(router_w[0], LANES)
    rwh = rw.astype(BF16)
    rwl = (rw - rwh.astype(F32)).astype(BF16)
    rb = jnp.concatenate([router_b[0].reshape(1, N_EXPERTS),
                          jnp.full((1, LANES - N_EXPERTS), NEG, F32)], axis=1)
    bgu = jnp.concatenate([exp_b_gu[0][:, 0::2], exp_b_gu[0][:, 1::2]], axis=1).reshape(N_EXPERTS, 1, 2 * D_FF)
    bdn = exp_b_down[0].reshape(N_EXPERTS, 1, D_MODEL)
    row2 = lambda v: v[0].reshape(1, D_MODEL)

    cos_p, sin_p = _rope_tables(jnp.arange(t_p, dtype=jnp.int32))
    cos_s, sin_s = _rope_tables(jnp.tile(past_len + jnp.arange(t_s, dtype=jnp.int32), b_s))

    xp = x_prompt.reshape(n_p, D_MODEL)
    xs = x_sample.reshape(n_s, D_MODEL)
    qlat_p, qrope_p, c_p, cbf_p, kr_p, krbf_p, dt_p = _mla_in(xp, w_mla, qg, kvg, wq, wuk, cos_p, sin_p, ROW_TILE)
    qlat_s, qrope_s, c_s, _, kr_s, _, dt_s = _mla_in(xs, w_mla, qg, kvg, wq, wuk, cos_s, sin_s, ROW_TILE)
    z_p, xbc_p, gates_p = (_proj(xp, w, PROJ_TILE) for w in (w_z, w_xbc, w_gates))
    z_s, xbc_s, gates_s = (_proj(xs, w, ROW_TILE) for w in (w_z, w_xbc, w_gates))

    conv0_p = jnp.zeros((b_p, SUBLANES, CONV_DIM), F32)
    conv0_s = jnp.pad(state_conv[0], ((0, 0), (SUBLANES - (D_CONV - 1), 0), (0, 0)))
    ya_p, ssm_p, ctail_p = _ssd(xbc_p, z_p, dt_p, conv0_p, None, cw, cb, dtb, a_neg, d_x, ng, e_mat,
                                nb=b_p, nc=t_p // SSD_CHUNK, t_in=SSD_CHUNK, chunk=SSD_CHUNK, y_dtype=BF16)
    ya_s, ssm_s, ctail_s = _ssd(xbc_s, z_s, dt_s, conv0_s, state_ssm[0].reshape(b_s, D_INNER, D_STATE),
                                cw, cb, dtb, a_neg, d_x, ng, e_mat,
                                nb=b_s, nc=1, t_in=t_s, chunk=SAMPLE_CHUNK, y_dtype=F32)
    conv_p = ctail_p[:, SUBLANES - (D_CONV - 1):, :]
    conv_s = ctail_s[:, SUBLANES - (D_CONV - 1):, :]

    o_p = _attn_prompt(qlat_p, qrope_p, cbf_p, krbf_p, wuv, nb=b_p, t=t_p, tq=ATTN_TQ)
    ql_s = jnp.transpose(qlat_s.reshape(MLA_HEADS, b_s, t_s, KV_LORA), (1, 0, 2, 3))
    qr_s = jnp.transpose(qrope_s.reshape(MLA_HEADS, b_s, t_s, ROPE_DIM), (1, 0, 2, 3))
    cnew_s = jnp.pad(c_s.reshape(b_s, t_s, KV_LORA), ((0, 0), (0, LANES - t_s), (0, 0)))
    krnew_s = jnp.swapaxes(jnp.pad(kr_s.reshape(b_s, t_s, ROPE_DIM), ((0, 0), (0, LANES - t_s), (0, 0))), 1, 2)
    olat_s = _attn_sample(page_table, ql_s.reshape(b_s, MLA_HEADS * t_s, KV_LORA),
                          qr_s.reshape(b_s, MLA_HEADS * t_s, ROPE_DIM), cnew_s, krnew_s,
                          cache_kv_latent[0], jnp.swapaxes(cache_k_rope[0], 1, 2),
                          t_new=t_s, gp=PAGE_GROUP)
    olat_s = jnp.transpose(olat_s.reshape(b_s, MLA_HEADS, t_s, KV_LORA), (1, 0, 2, 3)).reshape(MLA_HEADS, n_s, KV_LORA)
    o_s = _head_proj(olat_s, wuv)

    h, h3, top_i, top_g, rank, counts = _merge(ya_p, ya_s.astype(BF16), o_p, o_s, gates_p, gates_s, xp, xs, wa, wb, wo,
                                 row2(ln1_g), row2(ln1_b), rwh, rwl, rb, ROW_TILE)

    tb = MOE_TB
    n_blocks = -(-(n * TOP_K + N_EXPERTS * (tb - 1)) // tb)
    dest, pad_start, blk_e, n_used = _routing(top_i[:, :TOP_K], rank[:, :TOP_K], counts[0, :N_EXPERTS], n_blocks, tb)
    xr3 = _dispatch(pad_start, dest, h3, (n_blocks + 1) * tb, COMBINE_TILE, tb)
    yr3 = _ffn(blk_e, n_used, xr3, n_blocks, exp_w_gu[0], bgu, exp_w_down[0], bdn, tb)
    y_p, y_s = _combine(dest, yr3, top_g, h, row2(ln2_g), row2(ln2_b), n_p, COMBINE_TILE)

    return (y_p.reshape(b_p, t_p, D_MODEL), y_s.reshape(b_s, t_s, D_MODEL),
            c_p.reshape(1, b_p, t_p, KV_LORA), kr_p.reshape(1, b_p, t_p, ROPE_DIM),
            ssm_p.reshape(1, b_p, SSD_HEADS, SSD_HEAD_DIM, D_STATE), conv_p[None],
            c_s.reshape(1, b_s, t_s, KV_LORA), kr_s.reshape(1, b_s, t_s, ROPE_DIM),
            ssm_s.reshape(1, b_s, SSD_HEADS, SSD_HEAD_DIM, D_STATE), conv_s[None])
```

```python
import functools

import jax
import jax.numpy as jnp
import numpy as np
from jax import lax
from jax.experimental import pallas as pl
from jax.experimental.pallas import tpu as pltpu

F32 = jnp.float32
BF16 = jnp.bfloat16

D_MODEL = 1024
D_INNER = 2048
SSD_HEAD_DIM = 64
SSD_HEADS = 32
SSD_GROUPS = 4
HEADS_PER_GROUP = 8
D_STATE = 128
D_CONV = 4
CONV_DIM = D_INNER + 2 * SSD_GROUPS * D_STATE
SSD_CHUNK = 128
MLA_HEADS = 8
Q_LORA = 256
KV_LORA = 256
QK_NOPE = 128
ROPE_DIM = 64
V_HEAD = 128
ROPE_THETA = 10000.0
ATTN_SCALE = (QK_NOPE + ROPE_DIM) ** -0.5
N_EXPERTS = 32
TOP_K = 4
D_FF = 1024
SWIGLU_ALPHA = 1.702
SWIGLU_LIMIT = 7.0
DEPTH = 1
DN_ALPHA = (2 * DEPTH) ** 0.25
LN_EPS = 1e-5
RMS_EPS = 1e-6
IN_SIZES = (Q_LORA, KV_LORA, ROPE_DIM, D_INNER, CONV_DIM, SSD_HEADS, D_MODEL, D_MODEL)
IN_OFFS = tuple(int(v) for v in np.cumsum((0,) + IN_SIZES))

LANES = 128
SUBLANES = 8
VMEM_LIMIT = 56 * 1024 * 1024

NEG = -1e30

ROW_TILE = 256
PROJ_TILE = 512
ATTN_TQ = 256
PAGE_GROUP = 32
SAMPLE_CHUNK = 32
MOE_TB = 256
COMBINE_TILE = 128
MLA_A_COLS = 896


def _cparams(n_axes):
    return pltpu.CompilerParams(dimension_semantics=("arbitrary",) * n_axes,
                                vmem_limit_bytes=VMEM_LIMIT)


def _dot(a, b):
    return jnp.dot(a, b, preferred_element_type=F32)


def _dot_nt(a, b):
    return lax.dot_general(a, b, (((1,), (1,)), ((), ())), preferred_element_type=F32)


def _split3(x):
    hi = x.astype(BF16)
    r1 = x - hi.astype(F32)
    mid = r1.astype(BF16)
    lo = (r1 - mid.astype(F32)).astype(BF16)
    return hi, mid, lo


def _sigmoid(x):
    return 1.0 / (1.0 + jnp.exp(-x))


def _layer_norm(x, g, b):
    mu = jnp.mean(x, axis=-1, keepdims=True)
    xc = x - mu
    var = jnp.mean(xc * xc, axis=-1, keepdims=True)
    return xc * lax.rsqrt(var + LN_EPS) * g + b


def _rms_norm(x, g):
    return x * lax.rsqrt(jnp.mean(x * x, axis=-1, keepdims=True) + RMS_EPS) * g


def _proj_kernel(x_ref, w_ref, o_ref):
    o_ref[...] = _dot(x_ref[...].astype(BF16), w_ref[...])


def _proj(x, w, tm):
    m, k = x.shape
    n = w.shape[1]
    return pl.pallas_call(
        _proj_kernel,
        out_shape=jax.ShapeDtypeStruct((m, n), F32),
        grid=(m // tm,),
        in_specs=[pl.BlockSpec((tm, k), lambda i: (i, 0)),
                  pl.BlockSpec((k, n), lambda i: (0, 0))],
        out_specs=pl.BlockSpec((tm, n), lambda i: (i, 0)),
        compiler_params=_cparams(1),
        name="proj",
    )(x, w)


def _mla_in_kernel(x_ref, wa_ref, qg_ref, kvg_ref, wq_ref, wuk_ref, cos_ref, sin_ref,
                   qlat_ref, qrope_ref, c_ref, cbf_ref, kr_ref, krbf_ref, dt_ref):
    xb = x_ref[...].astype(BF16)
    pa = _dot(xb, wa_ref[...])
    q_a = pa[:, 0:256]
    kv_a = pa[:, 256:512]
    kr_raw = pa[:, 512:640]
    kr_rot = pa[:, 640:768]
    dt_ref[...] = pa[:, 768:896]
    cos = cos_ref[...]
    sin = sin_ref[...]
    c = _rms_norm(kv_a, kvg_ref[...])
    c_ref[...] = c
    cbf_ref[...] = c.astype(BF16)
    kr = kr_raw * cos + kr_rot * sin
    kr_ref[...] = kr[:, 0:ROPE_DIM]
    krbf_ref[...] = kr[:, 0:ROPE_DIM].astype(BF16)
    qn = _rms_norm(q_a, qg_ref[...]).astype(BF16)
    qq = _dot(qn, wq_ref[...])
    cos4 = jnp.concatenate([cos] * 4, axis=1)
    sin4 = jnp.concatenate([sin] * 4, axis=1)
    qr = (qq[:, 1024:1536] * cos4 + qq[:, 1536:2048] * sin4) * ATTN_SCALE
    for h in range(MLA_HEADS):
        qrope_ref[h] = qr[:, h * ROPE_DIM:(h + 1) * ROPE_DIM].astype(BF16)
        nope = qq[:, h * QK_NOPE:(h + 1) * QK_NOPE].astype(BF16)
        qlat_ref[h] = (_dot(nope, wuk_ref[h]) * ATTN_SCALE).astype(BF16)


def _mla_in(x, wa, qg, kvg, wq, wuk, cos_t, sin_t, tm):
    m = x.shape[0]
    tab_blocks = cos_t.shape[0] // tm
    row = lambda w: pl.BlockSpec((tm, w), lambda i: (i, 0))
    tab = pl.BlockSpec((tm, LANES), lambda i: (i % tab_blocks, 0))
    full2 = lambda a: pl.BlockSpec(a.shape, lambda i: (0, 0))
    full3 = lambda a: pl.BlockSpec(a.shape, lambda i: (0, 0, 0))
    return pl.pallas_call(
        _mla_in_kernel,
        out_shape=(jax.ShapeDtypeStruct((MLA_HEADS, m, KV_LORA), BF16),
                   jax.ShapeDtypeStruct((MLA_HEADS, m, ROPE_DIM), BF16),
                   jax.ShapeDtypeStruct((m, KV_LORA), F32),
                   jax.ShapeDtypeStruct((m, KV_LORA), BF16),
                   jax.ShapeDtypeStruct((m, ROPE_DIM), F32),
                   jax.ShapeDtypeStruct((m, ROPE_DIM), BF16),
                   jax.ShapeDtypeStruct((m, LANES), F32)),
        grid=(m // tm,),
        in_specs=[row(D_MODEL), full2(wa), full2(qg), full2(kvg), full2(wq), full3(wuk), tab, tab],
        out_specs=(pl.BlockSpec((MLA_HEADS, tm, KV_LORA), lambda i: (0, i, 0)),
                   pl.BlockSpec((MLA_HEADS, tm, ROPE_DIM), lambda i: (0, i, 0)),
                   row(KV_LORA), row(KV_LORA), row(ROPE_DIM), row(ROPE_DIM), row(LANES)),
        compiler_params=_cparams(1),
        name="mla_in",
    )(x, wa, qg, kvg, wq, wuk, cos_t, sin_t)


def _expand_heads(v, e_ref, terms):
    out = None
    rem = v
    for _ in range(terms):
        part = rem.astype(BF16)
        rem = rem - part.astype(F32)
        d = _dot(part, e_ref[...])
        out = d if out is None else out + d
    return out


def _conv_silu(ext, r0, rows, cw_ref, cb_ref, out, out_r0, strips=range(CONV_DIM // LANES)):
    for j in strips:
        sl = slice(j * LANES, (j + 1) * LANES)
        acc = cb_ref[:, sl]
        for k in range(D_CONV):
            acc = acc + ext[pl.ds(r0 + SUBLANES - (D_CONV - 1) + k, rows), sl] * cw_ref[k:k + 1, sl]
        out[out_r0:out_r0 + rows, sl] = acc * _sigmoid(acc)


def _ssd_kernel(*refs, t_in, L, has_h0, pre_conv):
    n_pairs = SSD_HEADS // 2
    refs = list(refs)
    hT = refs[-n_pairs:]
    it = iter(refs[:-n_pairs])
    xbc_ref, z_ref, dt_ref = next(it), next(it), next(it)
    conv0_ref = None if pre_conv else next(it)
    h0_ref = next(it) if has_h0 else None
    cw_ref, cb_ref = (None, None) if pre_conv else (next(it), next(it))
    dtb_ref, a_ref, dx_ref, ng_ref, e_ref, y_ref, hfin_ref = (next(it) for _ in range(7))
    ctail_ref = None if pre_conv else next(it)
    ext, xcv = (None, xbc_ref) if pre_conv else (next(it), next(it))
    ybuf, acs_s, acsT_s, dtT_s, eax_s, wtx_s = it
    c = pl.program_id(1)
    nc = pl.num_programs(1)

    @pl.when(c == 0)
    def _():
        if not pre_conv:
            ext[0:SUBLANES, :] = conv0_ref[0]
            if t_in < L:
                ext[SUBLANES:SUBLANES + L, :] = jnp.zeros((L, CONV_DIM), F32)
        for j in range(n_pairs):
            if has_h0:
                hT[j][...] = h0_ref[0, j * LANES:(j + 1) * LANES, :].T
            else:
                hT[j][...] = jnp.zeros((D_STATE, LANES), F32)

    if not pre_conv:
        ext[SUBLANES:SUBLANES + t_in, :] = xbc_ref[...]

        @pl.when(c == nc - 1)
        def _():
            ctail_ref[0] = ext[t_in:t_in + SUBLANES, :]

        _conv_silu(ext, 0, L, cw_ref, cb_ref, xcv, 0)
        ext[0:SUBLANES, :] = ext[L:L + SUBLANES, :]

    row = lax.broadcasted_iota(jnp.int32, (L, L), 0)
    col = lax.broadcasted_iota(jnp.int32, (L, L), 1)
    tri = row >= col
    tri_b = jnp.where(tri, 1.0, 0.0).astype(BF16)
    if t_in < L:
        dt_raw = jnp.concatenate([dt_ref[...], jnp.zeros((L - t_in, LANES), F32)], axis=0)
    else:
        dt_raw = dt_ref[...]
    xs = dt_raw + dtb_ref[...]
    dtv = jnp.maximum(xs, 0.0) + jnp.log1p(jnp.exp(-jnp.abs(xs)))
    if t_in < L:
        dtv = jnp.where(lax.broadcasted_iota(jnp.int32, (L, LANES), 0) < t_in, dtv, 0.0)
    da = dtv * a_ref[...]
    hi, mid, lo = _split3(da)
    acs = _dot(tri_b, hi) + _dot(tri_b, mid) + _dot(tri_b, lo)
    acs_s[...] = acs
    acsT_s[...] = acs.T
    dtT_s[...] = dtv.T
    eax_s[...] = _expand_heads(jnp.exp(acs), e_ref, 3)
    wtx_s[...] = _expand_heads(jnp.exp(acs[L - 1:L, :] - acs) * dtv, e_ref, 2)

    low_half = lax.broadcasted_iota(jnp.int32, (L, LANES), 1) < SSD_HEAD_DIM
    pairs_per_group = HEADS_PER_GROUP // 2
    for g in range(SSD_GROUPS):
        bo = D_INNER + g * D_STATE
        co = D_INNER + SSD_GROUPS * D_STATE + g * D_STATE
        bm = xcv[:, bo:bo + D_STATE]
        cmb = xcv[:, co:co + D_STATE].astype(BF16)
        cb = _dot_nt(cmb, bm.astype(BF16))
        bmT = bm.T.astype(BF16)
        for k in range(pairs_per_group):
            kk = g * pairs_per_group + k
            tl = slice(kk * LANES, (kk + 1) * LANES)
            x2 = xcv[:, tl]
            ws = []
            for idx in (2 * kk, 2 * kk + 1):
                seg = acs_s[:, idx:idx + 1] - acsT_s[idx:idx + 1, :]
                w = cb * jnp.exp(jnp.where(tri, seg, NEG)) * dtT_s[idx:idx + 1, :]
                ws.append(w.astype(BF16))
            w2 = jnp.concatenate(ws, axis=1)
            xbd = jnp.concatenate([jnp.where(low_half, x2, 0.0).astype(BF16),
                                   jnp.where(low_half, 0.0, x2).astype(BF16)], axis=0)
            yd = _dot(w2, xbd)
            hp = hT[kk][...]
            yo = _dot(cmb, hp.astype(BF16))
            ybuf[:, tl] = yd + yo * eax_s[:, tl] + dx_ref[:, tl] * x2
            s = _dot(bmT, (x2 * wtx_s[:, tl]).astype(BF16))
            hT[kk][...] = eax_s[L - 1:L, tl] * hp + s

    gw = D_INNER // SSD_GROUPS
    for g in range(SSD_GROUPS):
        sl = slice(g * gw, (g + 1) * gw)
        zg = z_ref[:, sl]
        v = ybuf[0:t_in, sl] * (zg * _sigmoid(zg))
        ms = jnp.mean(v * v, axis=-1, keepdims=True)
        y_ref[:, sl] = (v * lax.rsqrt(ms + RMS_EPS) * ng_ref[:, sl]).astype(y_ref.dtype)

    @pl.when(c == nc - 1)
    def _():
        for j in range(n_pairs):
            hfin_ref[0, j * LANES:(j + 1) * LANES, :] = hT[j][...].T


def _ssd(xbc, z, dt, conv0, h0, cw, cb, dtb, a_neg, d_x, ng, e_mat, *, nb, nc, t_in, chunk, y_dtype):
    L = chunk
    has_h0 = h0 is not None
    pre_conv = conv0 is None
    assert not pre_conv or t_in == L
    rowspec = lambda w: pl.BlockSpec((t_in, w), lambda b, c: (b * nc + c, 0))
    par = lambda a: pl.BlockSpec(a.shape, lambda b, c: (0, 0))
    per_seq = lambda r, w: pl.BlockSpec((1, r, w), lambda b, c: (b, 0, 0))
    in_specs = [rowspec(CONV_DIM), rowspec(D_INNER), rowspec(LANES)]
    args = [xbc, z, dt]
    if not pre_conv:
        in_specs.append(per_seq(SUBLANES, CONV_DIM))
        args.append(conv0)
    if has_h0:
        in_specs.append(per_seq(D_INNER, D_STATE))
        args.append(h0)
    params = ([] if pre_conv else [cw, cb]) + [dtb, a_neg, d_x, ng, e_mat]
    in_specs += [par(p) for p in params]
    args += params
    out_shape = [jax.ShapeDtypeStruct((nb * nc * t_in, D_INNER), y_dtype),
                 jax.ShapeDtypeStruct((nb, D_INNER, D_STATE), F32)]
    out_specs = [rowspec(D_INNER), per_seq(D_INNER, D_STATE)]
    scratch = []
    if not pre_conv:
        out_shape.append(jax.ShapeDtypeStruct((nb, SUBLANES, CONV_DIM), F32))
        out_specs.append(per_seq(SUBLANES, CONV_DIM))
        scratch += [pltpu.VMEM((L + 2 * SUBLANES, CONV_DIM), F32),
                    pltpu.VMEM((L, CONV_DIM), F32)]
    scratch += [pltpu.VMEM((L, D_INNER), F32),
                pltpu.VMEM((L, LANES), F32), pltpu.VMEM((LANES, L), F32), pltpu.VMEM((LANES, L), F32),
                pltpu.VMEM((L, D_INNER), F32), pltpu.VMEM((L, D_INNER), F32)]
    scratch += [pltpu.VMEM((D_STATE, LANES), F32)] * (SSD_HEADS // 2)
    return pl.pallas_call(
        functools.partial(_ssd_kernel, t_in=t_in, L=L, has_h0=has_h0, pre_conv=pre_conv),
        out_shape=tuple(out_shape),
        grid=(nb, nc),
        in_specs=in_specs,
        out_specs=tuple(out_specs),
        scratch_shapes=scratch,
        compiler_params=_cparams(2),
        name="ssd",
    )(*args)


CONV_CHUNK = 256


def _proj_conv_kernel(x_ref, w_ref, cw_ref, cb_ref, o_ref, ctail_ref, ext_a, ext_b, *, tm, tiles_per_seq):
    i = pl.program_id(0)
    starts_seq = lax.rem(i, tiles_per_seq) == 0

    @pl.when(i == 0)
    def _():
        ext_b[...] = jnp.zeros_like(ext_b)

    def step(cur, prev):
        xb = x_ref[...].astype(BF16)
        for j in range(CONV_DIM // CONV_CHUNK):
            cs = slice(j * CONV_CHUNK, (j + 1) * CONV_CHUNK)
            cur[SUBLANES:SUBLANES + tm, cs] = _dot(xb, w_ref[:, cs])
            cur[0:SUBLANES, cs] = jnp.where(starts_seq, 0.0, prev[tm:tm + SUBLANES, cs])
            strips = range(j * CONV_CHUNK // LANES, (j + 1) * CONV_CHUNK // LANES)
            for r0 in range(0, tm, SSD_CHUNK):
                _conv_silu(prev, r0, SSD_CHUNK, cw_ref, cb_ref, o_ref, r0, strips)

        @pl.when(starts_seq)
        def _():
            ctail_ref[0] = prev[tm:tm + SUBLANES, :]

    pl.when(lax.rem(i, 2) == 0)(functools.partial(step, ext_a, ext_b))
    pl.when(lax.rem(i, 2) == 1)(functools.partial(step, ext_b, ext_a))


def _proj_conv(x, w, cw, cb, tm, t_seq):
    m, k = x.shape
    n_tiles = m // tm
    tiles_per_seq = t_seq // tm
    par = lambda a: pl.BlockSpec(a.shape, lambda i: (0, 0))
    lag = lambda i: jnp.maximum(i - 1, 0)
    return pl.pallas_call(
        functools.partial(_proj_conv_kernel, tm=tm, tiles_per_seq=tiles_per_seq),
        out_shape=(jax.ShapeDtypeStruct((m, CONV_DIM), F32),
                   jax.ShapeDtypeStruct((m // t_seq, SUBLANES, CONV_DIM), F32)),
        grid=(n_tiles + 1,),
        in_specs=[pl.BlockSpec((tm, k), lambda i: (jnp.minimum(i, n_tiles - 1), 0)), par(w), par(cw), par(cb)],
        out_specs=(pl.BlockSpec((tm, CONV_DIM), lambda i: (lag(i), 0)),
                   pl.BlockSpec((1, SUBLANES, CONV_DIM), lambda i: (lag(i) // tiles_per_seq, 0, 0))),
        scratch_shapes=[pltpu.VMEM((tm + 2 * SUBLANES, CONV_DIM), F32)] * 2,
        compiler_params=_cparams(1),
        name="proj_conv",
    )(x, w, cw, cb)


def _softmax_step(s, kc, m_scr, l_scr, acc_scr):
    reps = s.shape[1] // LANES
    m_prev = m_scr[...]
    m_new = jnp.maximum(m_prev, jnp.max(s, axis=1, keepdims=True))
    alpha = jnp.exp(m_prev - m_new)
    p = jnp.exp(s - (jnp.concatenate([m_new] * reps, axis=1) if reps > 1 else m_new))
    l_scr[...] = alpha * l_scr[...] + jnp.sum(p, axis=1, keepdims=True)
    acc_scr[...] = acc_scr[...] * jnp.concatenate([alpha] * (KV_LORA // LANES), axis=1) + _dot(p.astype(BF16), kc)
    m_scr[...] = m_new


def _attn_prompt_kernel(qlat_ref, qrope_ref, c_ref, kr_ref, wuv_ref, o_ref, m_scr, l_scr, acc_scr, *, tq):
    qi = pl.program_id(1)
    R = MLA_HEADS * tq

    def scores(block, width):
        start = pl.multiple_of(block * width, width)
        kc = c_ref[pl.ds(start, width), :]
        kr = kr_ref[pl.ds(start, width), :]
        q = qlat_ref[...].reshape(R, KV_LORA)
        qr = qrope_ref[...].reshape(R, ROPE_DIM)
        return _dot_nt(q, kc) + _dot_nt(qr, kr), kc

    s, kc = scores(qi, tq)
    t = lax.broadcasted_iota(jnp.int32, (R, tq), 0) & (tq - 1)
    kcol = lax.broadcasted_iota(jnp.int32, (R, tq), 1)
    s = jnp.where(kcol <= t, s, NEG)
    m0 = jnp.max(s, axis=1, keepdims=True)
    p = jnp.exp(s - m0)
    m_scr[...] = jnp.broadcast_to(m0, m_scr.shape)
    l_scr[...] = jnp.broadcast_to(jnp.sum(p, axis=1, keepdims=True), l_scr.shape)
    acc_scr[...] = _dot(p.astype(BF16), kc)

    def body(j, carry):
        s2, kc2 = scores(j, 2 * tq)
        _softmax_step(s2, kc2, m_scr, l_scr, acc_scr)
        return carry

    lax.fori_loop(0, qi // 2, body, 0)

    @pl.when(qi % 2 == 1)
    def _():
        s1, kc1 = scores(qi - 1, tq)
        _softmax_step(s1, kc1, m_scr, l_scr, acc_scr)

    inv = 1.0 / l_scr[...]
    o_lat = acc_scr[...] * jnp.concatenate([inv] * (KV_LORA // LANES), axis=1)
    for h in range(MLA_HEADS):
        oh = _dot(o_lat[h * tq:(h + 1) * tq, :].astype(BF16), wuv_ref[h])
        o_ref[:, h * V_HEAD:(h + 1) * V_HEAD] = oh.astype(o_ref.dtype)


def _attn_prompt(qlat, qrope, cbf, krbf, wuv, *, nb, t, tq):
    nq = t // tq
    R = MLA_HEADS * tq
    return pl.pallas_call(
        functools.partial(_attn_prompt_kernel, tq=tq),
        out_shape=jax.ShapeDtypeStruct((nb * t, MLA_HEADS * V_HEAD), BF16),
        grid=(nb, nq),
        in_specs=[pl.BlockSpec((MLA_HEADS, tq, KV_LORA), lambda b, i: (0, b * nq + i, 0)),
                  pl.BlockSpec((MLA_HEADS, tq, ROPE_DIM), lambda b, i: (0, b * nq + i, 0)),
                  pl.BlockSpec((t, KV_LORA), lambda b, i: (b, 0)),
                  pl.BlockSpec((t, ROPE_DIM), lambda b, i: (b, 0)),
                  pl.BlockSpec(wuv.shape, lambda b, i: (0, 0, 0))],
        out_specs=pl.BlockSpec((tq, MLA_HEADS * V_HEAD), lambda b, i: (b * nq + i, 0)),
        scratch_shapes=[pltpu.VMEM((R, LANES), F32), pltpu.VMEM((R, LANES), F32),
                        pltpu.VMEM((R, KV_LORA), F32)],
        compiler_params=_cparams(2),
        name="attn_prompt",
    )(qlat, qrope, cbf, krbf, wuv)


def _attn_sample_kernel(pt_ref, qlat_ref, qrope_ref, cnew_ref, krnew_ref, kv_hbm, kr_hbm, o_ref,
                        kvbuf, krbuf, kv_sem, kr_sem, m_scr, l_scr, acc_scr, *, n_groups, gp, t_new):
    b = pl.program_id(0)
    nb = pl.num_programs(0)
    R = qlat_ref.shape[1]
    page = kvbuf.shape[2]
    q = qlat_ref[0]
    qr = qrope_ref[0]

    def start_group(bb, g, slot):
        for p in range(gp):
            pid = pt_ref[bb, g * gp + p]
            pltpu.make_async_copy(kv_hbm.at[pid], kvbuf.at[slot, p], kv_sem.at[slot]).start()
            pltpu.make_async_copy(kr_hbm.at[pid], krbuf.at[slot, p], kr_sem.at[slot]).start()

    def wait_group(slot):
        pltpu.make_async_copy(kvbuf.at[slot], kvbuf.at[slot], kv_sem.at[slot]).wait()
        pltpu.make_async_copy(krbuf.at[slot], krbuf.at[slot], kr_sem.at[slot]).wait()

    @pl.when(b == 0)
    def _():
        start_group(0, 0, 0)

    kc = cnew_ref[0].astype(BF16)
    kr = krnew_ref[0].astype(BF16)
    s = _dot_nt(q, kc) + _dot(qr, kr)
    t = lax.broadcasted_iota(jnp.int32, (R, LANES), 0) & (t_new - 1)
    kcol = lax.broadcasted_iota(jnp.int32, (R, LANES), 1)
    s = jnp.where(kcol <= t, s, NEG)
    m0 = jnp.max(s, axis=1, keepdims=True)
    p0 = jnp.exp(s - m0)
    m_scr[...] = jnp.broadcast_to(m0, m_scr.shape)
    l_scr[...] = jnp.broadcast_to(jnp.sum(p0, axis=1, keepdims=True), l_scr.shape)
    acc_scr[...] = _dot(p0.astype(BF16), kc)

    for g in range(n_groups):
        slot = g % 2
        if g + 1 < n_groups:
            start_group(b, g + 1, 1 - slot)
        else:
            @pl.when(b + 1 < nb)
            def _():
                start_group(b + 1, 0, 1 - slot)
        wait_group(slot)
        kc = kvbuf[slot].reshape(gp * page, KV_LORA).astype(BF16)
        kr = jnp.concatenate([krbuf[slot, p].astype(BF16) for p in range(gp)], axis=1)
        s = _dot_nt(q, kc) + _dot(qr, kr)
        _softmax_step(s, kc, m_scr, l_scr, acc_scr)

    inv = 1.0 / l_scr[...]
    o_ref[0] = acc_scr[...] * jnp.concatenate([inv] * (KV_LORA // LANES), axis=1)


def _attn_sample(page_table, qlat, qrope, cnew, krnew, cache_kv, cache_kr, *, t_new, gp):
    nb, n_pages = page_table.shape
    n_groups = n_pages // gp
    R = qlat.shape[1]
    page = cache_kv.shape[1]
    per_b = lambda r, w: pl.BlockSpec((1, r, w), lambda b, pt: (b, 0, 0))
    grid_spec = pltpu.PrefetchScalarGridSpec(
        num_scalar_prefetch=1,
        grid=(nb,),
        in_specs=[per_b(R, KV_LORA), per_b(R, ROPE_DIM), per_b(LANES, KV_LORA), per_b(ROPE_DIM, LANES),
                  pl.BlockSpec(memory_space=pl.ANY), pl.BlockSpec(memory_space=pl.ANY)],
        out_specs=per_b(R, KV_LORA),
        scratch_shapes=[pltpu.VMEM((2, gp, page, KV_LORA), F32), pltpu.VMEM((2, gp, ROPE_DIM, page), F32),
                        pltpu.SemaphoreType.DMA((2,)), pltpu.SemaphoreType.DMA((2,)),
                        pltpu.VMEM((R, LANES), F32), pltpu.VMEM((R, LANES), F32),
                        pltpu.VMEM((R, KV_LORA), F32)],
    )
    return pl.pallas_call(
        functools.partial(_attn_sample_kernel, n_groups=n_groups, gp=gp, t_new=t_new),
        out_shape=jax.ShapeDtypeStruct((nb, R, KV_LORA), F32),
        grid_spec=grid_spec,
        compiler_params=_cparams(1),
        name="attn_sample",
    )(page_table, qlat, qrope, cnew, krnew, cache_kv, cache_kr)


def _head_proj_kernel(x_ref, w_ref, o_ref):
    o_ref[...] = _dot(x_ref[0].astype(BF16), w_ref[0]).astype(o_ref.dtype)


def _head_proj(x, w):
    h, m, k = x.shape
    n = w.shape[2]
    return pl.pallas_call(
        _head_proj_kernel,
        out_shape=jax.ShapeDtypeStruct((m, h * n), BF16),
        grid=(h,),
        in_specs=[pl.BlockSpec((1, m, k), lambda i: (i, 0, 0)),
                  pl.BlockSpec((1, k, n), lambda i: (i, 0, 0))],
        out_specs=pl.BlockSpec((m, n), lambda i: (0, i)),
        compiler_params=_cparams(1),
        name="head_proj",
    )(x, w)


def _merge_kernel(ya_p_ref, ya_s_ref, o_p_ref, o_s_ref, g_p_ref, g_s_ref, u_p_ref, u_s_ref, wa_ref, wb_ref, wo_ref,
                  lg_ref, lb_ref, rwh_ref, rwl_ref, rb_ref,
                  h_ref, h3_ref, ti_ref, tg_ref, rank_ref, cnt_ref, run_ref, *, n_prompt_tiles):
    i = pl.program_id(0)
    is_p = i < n_prompt_tiles
    yn = jnp.where(is_p, ya_p_ref[...], ya_s_ref[...])
    ob = jnp.where(is_p, o_p_ref[...], o_s_ref[...])
    g = jnp.where(is_p, g_p_ref[...], g_s_ref[...])
    u = jnp.where(is_p, u_p_ref[...], u_s_ref[...])
    ya = _dot(yn, wa_ref[...])
    yb = _dot(ob, wb_ref[...])
    merged = _sigmoid(g[:, 0:D_MODEL]) * ya + _sigmoid(g[:, D_MODEL:2 * D_MODEL]) * yb
    mix = _dot(merged.astype(BF16), wo_ref[...])
    h = _layer_norm(DN_ALPHA * u + mix, lg_ref[...], lb_ref[...])
    h_ref[...] = h
    tm = h.shape[0]
    for j in range(D_MODEL // LANES):
        h3_ref[pl.ds(j, tm, stride=SUBLANES), :] = h[:, j * LANES:(j + 1) * LANES]
    hh = h.astype(BF16)
    hl = (h - hh.astype(F32)).astype(BF16)
    logits = _dot(hh, rwh_ref[...]) + _dot(hl, rwh_ref[...]) + _dot(hh, rwl_ref[...]) + rb_ref[...]
    lane = lax.broadcasted_iota(jnp.int32, logits.shape, 1).astype(F32)
    vals = jnp.zeros_like(logits)
    idxs = jnp.zeros_like(logits)
    cur = logits
    picked = []
    for k in range(TOP_K):
        mx = jnp.max(cur, axis=1, keepdims=True)
        ix = jnp.min(jnp.where(cur == mx, lane, float(LANES)), axis=1, keepdims=True)
        vals = jnp.where(lane == k, mx, vals)
        idxs = jnp.where(lane == k, ix, idxs)
        picked.append(lane == ix)
        cur = jnp.where(picked[k], NEG * 2.0, cur)
    v0 = vals[:, 0:1]
    e = jnp.where(lane < TOP_K, jnp.exp(vals - v0), 0.0)
    tg_ref[...] = e / jnp.sum(e, axis=1, keepdims=True)
    ti_ref[...] = idxs.astype(jnp.int32)

    @pl.when(i == 0)
    def _():
        run_ref[...] = jnp.zeros_like(run_ref)

    cnt = jnp.zeros_like(logits)
    for k in range(TOP_K):
        cnt = cnt + jnp.where(picked[k], 1.0, 0.0)
    r_i = lax.broadcasted_iota(jnp.int32, (tm, tm), 0)
    c_i = lax.broadcasted_iota(jnp.int32, (tm, tm), 1)
    before = _dot(jnp.where(r_i > c_i, 1.0, 0.0).astype(BF16), cnt.astype(BF16)) + run_ref[0:1, :]
    ranks = jnp.zeros_like(logits)
    for k in range(TOP_K):
        rk = jnp.sum(jnp.where(picked[k], before, 0.0), axis=1, keepdims=True)
        ranks = jnp.where(lane == k, rk, ranks)
    rank_ref[...] = ranks.astype(jnp.int32)
    run_ref[0:1, :] = run_ref[0:1, :] + jnp.sum(cnt, axis=0, keepdims=True)
    cnt_ref[...] = run_ref[...].astype(jnp.int32)


def _merge(ya_p, ya_s, o_p, o_s, g_p, g_s, u_p, u_s, wa, wb, wo, lg, lb, rwh, rwl, rb, tm):
    m = u_p.shape[0] + u_s.shape[0]
    npt = ya_p.shape[0] // tm
    row = lambda w: pl.BlockSpec((tm, w), lambda i: (i, 0))
    prow = lambda w: pl.BlockSpec((tm, w), lambda i: (jnp.minimum(i, npt - 1), 0))
    srow = lambda w: pl.BlockSpec((tm, w), lambda i: (jnp.maximum(i - npt, 0), 0))
    par = lambda a: pl.BlockSpec(a.shape, lambda i: (0, 0))
    return pl.pallas_call(
        functools.partial(_merge_kernel, n_prompt_tiles=npt),
        out_shape=(jax.ShapeDtypeStruct((m, D_MODEL), F32),
                   jax.ShapeDtypeStruct((m * SUBLANES, LANES), F32),
                   jax.ShapeDtypeStruct((m, LANES), jnp.int32),
                   jax.ShapeDtypeStruct((m, LANES), F32),
                   jax.ShapeDtypeStruct((m, LANES), jnp.int32),
                   jax.ShapeDtypeStruct((SUBLANES, LANES), jnp.int32)),
        grid=(m // tm,),
        in_specs=[prow(D_INNER), srow(D_INNER), prow(D_MODEL), srow(D_MODEL), prow(2 * D_MODEL), srow(2 * D_MODEL),
                  prow(D_MODEL), srow(D_MODEL),
                  par(wa), par(wb), par(wo), par(lg), par(lb), par(rwh), par(rwl), par(rb)],
        out_specs=(row(D_MODEL), pl.BlockSpec((tm * SUBLANES, LANES), lambda i: (i, 0)), row(LANES), row(LANES),
                   row(LANES), pl.BlockSpec((SUBLANES, LANES), lambda i: (0, 0))),
        scratch_shapes=[pltpu.VMEM((SUBLANES, LANES), F32)],
        compiler_params=_cparams(1),
        name="merge",
    )(ya_p, ya_s, o_p, o_s, g_p, g_s, u_p, u_s, wa, wb, wo, lg, lb, rwh, rwl, rb)


DEINT_IN = 2 * LANES


def _deint_perm():
    perm = np.zeros((DEINT_IN, DEINT_IN), np.float32)
    perm[2 * np.arange(LANES), np.arange(LANES)] = 1.0
    perm[2 * np.arange(LANES) + 1, LANES + np.arange(LANES)] = 1.0
    return jnp.asarray(perm, BF16)


def _row_pipeline(i, n_steps, idx_hbm, src_hbm, idx_smem, idx_sem, buf, row_sem, rows, compute):
    def idx_copy(chunk, s):
        return pltpu.make_async_copy(idx_hbm.at[chunk], idx_smem.at[s], idx_sem.at[s])

    def issue_row(s, r):
        src0 = pl.multiple_of(idx_smem[s, r] * SUBLANES, SUBLANES)
        dst0 = pl.multiple_of(r * SUBLANES, SUBLANES)
        pltpu.make_async_copy(src_hbm.at[pl.ds(src0, SUBLANES), :],
                              buf.at[s, pl.ds(dst0, SUBLANES), :], row_sem.at[s]).start()

    @pl.when(i == 0)
    def _():
        idx_copy(0, 0).start()
        idx_copy(0, 0).wait()

        def body(r, carry):
            issue_row(0, r)
            return carry
        lax.fori_loop(0, rows, body, 0, unroll=8)
        if n_steps > 1:
            idx_copy(1, 1).start()

    def step(slot):
        has_next = i + 1 < n_steps

        @pl.when(has_next)
        def _():
            idx_copy(i + 1, 1 - slot).wait()

        @pl.when(i + 2 < n_steps)
        def _():
            idx_copy(i + 2, slot).start()

        pltpu.make_async_copy(src_hbm.at[pl.ds(0, rows * SUBLANES), :], buf.at[slot], row_sem.at[slot]).wait()

        @pl.when(has_next)
        def _():
            for r in range(rows):
                issue_row(1 - slot, r)
            compute(slot)

        if (n_steps - 1) % 2 == slot:
            @pl.when(i + 1 == n_steps)
            def _():
                compute(slot)

    for slot in (0, 1):
        pl.when(lax.rem(i, 2) == slot)(functools.partial(step, slot))


def _fetch_scratch(rows):
    return [pltpu.SMEM((2, rows), jnp.int32), pltpu.SemaphoreType.DMA((2,)),
            pltpu.VMEM((2, rows * SUBLANES, LANES), F32), pltpu.SemaphoreType.DMA((2,))]


def _dispatch_kernel(pad_ref, dest_hbm, h3_ref, xr_hbm, idx_smem, idx_sem, stage, row_sem, zbuf, z_sem,
                     *, tm, tb, min_used):
    i = pl.program_id(0)
    n = pl.num_programs(0)
    rows = tm * TOP_K

    def idx_copy(chunk, s):
        return pltpu.make_async_copy(dest_hbm.at[chunk], idx_smem.at[s], idx_sem.at[s])

    def rows_wait(s):
        span = xr_hbm.at[pl.ds(0, rows * SUBLANES), :]
        pltpu.make_async_copy(span, span, row_sem.at[s]).wait()

    @pl.when(i == 0)
    def _():
        zbuf[...] = jnp.zeros_like(zbuf)
        zero_to = lambda start: pltpu.make_async_copy(zbuf, xr_hbm.at[pl.ds(start, tb * SUBLANES), :], z_sem)
        for e in range(N_EXPERTS):
            zero_to(pl.multiple_of(pad_ref[e] * SUBLANES, SUBLANES)).start()
        for e in range(N_EXPERTS):
            zero_to(0).wait()
        n_used = pad_ref[N_EXPERTS]
        for blk in range(min_used, xr_hbm.shape[0] // (tb * SUBLANES)):
            @pl.when(blk >= n_used)
            def _():
                zero_to(blk * tb * SUBLANES).start()
                zero_to(0).wait()
        idx_copy(0, 0).start()

    def step(slot):
        idx_copy(i, slot).wait()

        @pl.when(i + 1 < n)
        def _():
            idx_copy(i + 1, 1 - slot).start()

        stage[slot] = h3_ref[...]
        for r in range(rows):
            dst0 = pl.multiple_of(idx_smem[slot, r] * SUBLANES, SUBLANES)
            pltpu.make_async_copy(stage.at[slot, pl.ds((r // TOP_K) * SUBLANES, SUBLANES), :],
                                  xr_hbm.at[pl.ds(dst0, SUBLANES), :], row_sem.at[slot]).start()

        @pl.when(i >= 1)
        def _():
            rows_wait(1 - slot)

        @pl.when(i == n - 1)
        def _():
            rows_wait(slot)

    for slot in (0, 1):
        pl.when(lax.rem(i, 2) == slot)(functools.partial(step, slot))


def _dispatch(pad_start, dest, h3, cap_rows, tm, tb):
    n = h3.shape[0] // SUBLANES
    grid_spec = pltpu.PrefetchScalarGridSpec(
        num_scalar_prefetch=1,
        grid=(n // tm,),
        in_specs=[pl.BlockSpec(memory_space=pl.ANY),
                  pl.BlockSpec((tm * SUBLANES, LANES), lambda i, pad: (i, 0))],
        out_specs=pl.BlockSpec(memory_space=pl.ANY),
        scratch_shapes=[pltpu.SMEM((2, tm * TOP_K), jnp.int32), pltpu.SemaphoreType.DMA((2,)),
                        pltpu.VMEM((2, tm * SUBLANES, LANES), F32), pltpu.SemaphoreType.DMA((2,)),
                        pltpu.VMEM((tb * SUBLANES, LANES), F32), pltpu.SemaphoreType.DMA(())],
    )
    return pl.pallas_call(
        functools.partial(_dispatch_kernel, tm=tm, tb=tb, min_used=-(-(n * TOP_K) // tb)),
        out_shape=jax.ShapeDtypeStruct((cap_rows * SUBLANES, LANES), F32),
        grid_spec=grid_spec,
        compiler_params=_cparams(1),
        name="dispatch",
    )(pad_start, dest.reshape(n // tm, tm * TOP_K), h3)


def _ffn_kernel(be_ref, nu_ref, x_ref, wgu_ref, bgu_ref, wdn_ref, bdn_ref, perm_ref, o_ref,
                wg_s, wu_s, wd_s, *, tb):
    i = pl.program_id(0)
    nu = nu_ref[0]
    new_expert = jnp.logical_or(i == 0, be_ref[i] != be_ref[jnp.maximum(i - 1, 0)])

    @pl.when(jnp.logical_and(new_expert, i < nu))
    def _():
        for t in range(2 * D_FF // DEINT_IN):
            r = _dot(wgu_ref[:, t * DEINT_IN:(t + 1) * DEINT_IN].astype(BF16), perm_ref[...])
            wg_s[:, t * LANES:(t + 1) * LANES] = r[:, 0:LANES].astype(BF16)
            wu_s[:, t * LANES:(t + 1) * LANES] = r[:, LANES:DEINT_IN].astype(BF16)
        wd_s[...] = wdn_ref[...].astype(BF16)

    @pl.when(i < nu)
    def _():
        x = jnp.concatenate([x_ref[pl.ds(j, tb, stride=SUBLANES), :].astype(BF16)
                             for j in range(D_MODEL // LANES)], axis=1)
        bgu = bgu_ref[...]
        g = jnp.minimum(_dot(x, wg_s[...]) + bgu[:, 0:D_FF], SWIGLU_LIMIT)
        u = jnp.clip(_dot(x, wu_s[...]) + bgu[:, D_FF:2 * D_FF], -SWIGLU_LIMIT, SWIGLU_LIMIT)
        hh = (u + 1.0) * (g * _sigmoid(SWIGLU_ALPHA * g))
        out = _dot(hh.astype(BF16), wd_s[...]) + bdn_ref[...]
        for j in range(D_MODEL // LANES):
            o_ref[pl.ds(j, tb, stride=SUBLANES), :] = out[:, j * LANES:(j + 1) * LANES]

    @pl.when(i >= nu)
    def _():
        o_ref[...] = jnp.zeros_like(o_ref)


def _ffn(blk_e, n_used, xr3, nblk, wgu, bgu, wdn, bdn, tb):
    perm = _deint_perm()
    wspec = lambda a: pl.BlockSpec((None,) + a.shape[1:], lambda i, be, nu: (be[i], 0, 0))
    grid_spec = pltpu.PrefetchScalarGridSpec(
        num_scalar_prefetch=2,
        grid=(nblk,),
        in_specs=[pl.BlockSpec((tb * SUBLANES, LANES), lambda i, be, nu: (jnp.minimum(i, nu[0] - 1), 0)),
                  wspec(wgu), wspec(bgu), wspec(wdn), wspec(bdn),
                  pl.BlockSpec(perm.shape, lambda i, be, nu: (0, 0))],
        out_specs=pl.BlockSpec((tb * SUBLANES, LANES), lambda i, be, nu: (i, 0)),
        scratch_shapes=[pltpu.VMEM((D_MODEL, D_FF), BF16), pltpu.VMEM((D_MODEL, D_FF), BF16),
                        pltpu.VMEM((D_FF, D_MODEL), BF16)],
    )
    return pl.pallas_call(
        functools.partial(_ffn_kernel, tb=tb),
        out_shape=jax.ShapeDtypeStruct((nblk * tb * SUBLANES, LANES), F32),
        grid_spec=grid_spec,
        compiler_params=_cparams(1),
        name="moe_ffn",
    )(blk_e, n_used, xr3, wgu, bgu, wdn, bdn, perm)


def _combine_kernel(dest_hbm, yr3_hbm, tg_ref, h_ref, lg_ref, lb_ref, op_ref, os_ref,
                    idx_smem, idx_sem, ybuf, row_sem, *, tm, n_prompt_tiles, n_steps):
    i = pl.program_id(0)
    rows = tm * TOP_K

    def compute(slot):
        tg = tg_ref[...]
        stride = TOP_K * SUBLANES
        pieces = []
        for j in range(D_MODEL // LANES):
            acc = tg[:, 0:1] * ybuf[slot, pl.ds(j, tm, stride=stride), :]
            for k in range(1, TOP_K):
                acc = acc + tg[:, k:k + 1] * ybuf[slot, pl.ds(k * SUBLANES + j, tm, stride=stride), :]
            pieces.append(acc)
        y = jnp.concatenate(pieces, axis=1)
        out = _layer_norm(DN_ALPHA * h_ref[...] + y, lg_ref[...], lb_ref[...])

        @pl.when(i < n_prompt_tiles)
        def _():
            op_ref[...] = out

        @pl.when(i >= n_prompt_tiles)
        def _():
            os_ref[...] = out

    _row_pipeline(i, n_steps, dest_hbm, yr3_hbm, idx_smem, idx_sem, ybuf, row_sem, rows, compute)


def _combine(dest, yr3, tg, h, lg, lb, n_prompt, tm):
    m = h.shape[0]
    npt = n_prompt // tm
    row = lambda w: pl.BlockSpec((tm, w), lambda i: (i, 0))
    par = lambda a: pl.BlockSpec(a.shape, lambda i: (0, 0))
    return pl.pallas_call(
        functools.partial(_combine_kernel, tm=tm, n_prompt_tiles=npt, n_steps=m // tm),
        out_shape=(jax.ShapeDtypeStruct((n_prompt, D_MODEL), F32),
                   jax.ShapeDtypeStruct((m - n_prompt, D_MODEL), F32)),
        grid=(m // tm,),
        in_specs=[pl.BlockSpec(memory_space=pl.ANY), pl.BlockSpec(memory_space=pl.ANY),
                  row(LANES), row(D_MODEL), par(lg), par(lb)],
        out_specs=(pl.BlockSpec((tm, D_MODEL), lambda i: (jnp.minimum(i, npt - 1), 0)),
                   pl.BlockSpec((tm, D_MODEL), lambda i: (jnp.maximum(i - npt, 0), 0))),
        scratch_shapes=_fetch_scratch(tm * TOP_K),
        compiler_params=_cparams(1),
        name="combine",
    )(dest.reshape(m // tm, tm * TOP_K), yr3, tg, h, lg, lb)


def _rope_tables(pos):
    half = ROPE_DIM // 2
    inv = ROPE_THETA ** (-jnp.arange(half, dtype=F32) / half)
    ang = pos.astype(F32)[:, None] * inv[None, :]
    cos, sin = jnp.cos(ang), jnp.sin(ang)
    return jnp.concatenate([cos] * 4, axis=1), jnp.concatenate([-sin, sin] * 2, axis=1)


def _pad_cols(w, width):
    return jnp.pad(w, ((0, 0), (0, width - w.shape[1])))


def _routing(top_i, rank, counts, n_blocks, tb):
    flat_e = top_i.reshape(-1)
    nblk_e = (counts + tb - 1) // tb
    pend = jnp.cumsum(nblk_e)
    pstart = (pend - nblk_e) * tb
    onehot = flat_e[:, None] == jnp.arange(N_EXPERTS, dtype=jnp.int32)[None, :]
    dest = (jnp.sum(jnp.where(onehot, pstart[None, :], 0), axis=1) + rank.reshape(-1)).astype(jnp.int32)
    n_used = pend[-1].astype(jnp.int32)
    pad_start = jnp.concatenate([(pstart + counts).astype(jnp.int32), n_used.reshape(1)])
    bidx = jnp.minimum(jnp.arange(n_blocks, dtype=jnp.int32), n_used - 1)
    blk_e = jnp.minimum(jnp.sum((pend[None, :] <= bidx[:, None]).astype(jnp.int32), axis=1), N_EXPERTS - 1)
    return dest, pad_start, blk_e, n_used.reshape(1)


def kernel(x_prompt, x_sample, cache_kv_latent, cache_k_rope, state_ssm, state_conv, page_table, w_in, ssd_conv_w, ssd_conv_b, ssd_dt_bias, ssd_a_log, ssd_d, ssd_norm_g, mla_q_norm_g, w_q_b, mla_kv_norm_g, w_kv_b, w_branch_a, w_branch_b, w_out, ln1_g, ln1_b, router_w, router_b, exp_w_gu, exp_b_gu, exp_w_down, exp_b_down, ln2_g, ln2_b):
    b_p, t_p, _ = x_prompt.shape
    b_s, t_s, _ = x_sample.shape
    n_pages = page_table.shape[1]
    page = cache_kv_latent.shape[2]
    past_len = n_pages * page
    n_p = b_p * t_p
    n_s = b_s * t_s
    n = n_p + n_s
    assert DEPTH == 1 and w_in.shape[0] == 1
    assert n_p % PROJ_TILE == 0 and t_p % ROW_TILE == 0 and n_s % ROW_TILE == 0
    assert t_p % (2 * ATTN_TQ) == 0 and t_p % SSD_CHUNK == 0
    assert t_s == SUBLANES and n_pages % (2 * PAGE_GROUP) == 0 and page == LANES

    wi = w_in[0]
    seg = lambda k: wi[:, IN_OFFS[k]:IN_OFFS[k + 1]]
    half = ROPE_DIM // 2
    w_kr = seg(2)
    w_kr_rot = jnp.concatenate([w_kr[:, half:], w_kr[:, :half]], axis=1)
    w_mla = jnp.concatenate([seg(0), seg(1), _pad_cols(w_kr, LANES), _pad_cols(w_kr_rot, LANES),
                             _pad_cols(seg(5), LANES)], axis=1).astype(BF16)
    w_z = seg(3).astype(BF16)
    w_xbc = seg(4).astype(BF16)
    w_gates = jnp.concatenate([seg(6), seg(7)], axis=1).astype(BF16)
    wq3 = w_q_b[0].reshape(Q_LORA, MLA_HEADS, QK_NOPE + ROPE_DIM)
    wq_nope = wq3[:, :, :QK_NOPE].reshape(Q_LORA, MLA_HEADS * QK_NOPE)
    wq_rope = wq3[:, :, QK_NOPE:]
    wq_rope_rot = jnp.concatenate([wq_rope[..., half:], wq_rope[..., :half]], axis=-1)
    wq = jnp.concatenate([wq_nope, wq_rope.reshape(Q_LORA, -1), wq_rope_rot.reshape(Q_LORA, -1)],
                         axis=1).astype(BF16)
    wkv3 = w_kv_b[0].reshape(KV_LORA, MLA_HEADS, QK_NOPE + V_HEAD)
    wuk = jnp.transpose(wkv3[:, :, :QK_NOPE], (1, 2, 0)).astype(BF16)
    wuv = jnp.transpose(wkv3[:, :, QK_NOPE:], (1, 0, 2)).astype(BF16)
    qg = mla_q_norm_g[0].reshape(1, Q_LORA)
    kvg = mla_kv_norm_g[0].reshape(1, KV_LORA)
    cw = ssd_conv_w[0]
    cb = ssd_conv_b[0].reshape(1, CONV_DIM)
    dtb = _pad_cols(ssd_dt_bias[0].reshape(1, SSD_HEADS), LANES)
    a_neg = _pad_cols(-jnp.exp(ssd_a_log[0].astype(F32)).reshape(1, SSD_HEADS), LANES)
    d_x = jnp.repeat(ssd_d[0].astype(F32), SSD_HEAD_DIM).reshape(1, D_INNER)
    e_mat = jnp.asarray(np.arange(LANES)[:, None] == np.arange(D_INNER)[None, :] // SSD_HEAD_DIM, BF16)
    ng = ssd_norm_g[0].reshape(1, D_INNER)
    wa = w_branch_a[0].astype(BF16)
    wb = w_branch_b[0].astype(BF16)
    wo = w_out[0].astype(BF16)
    rw = _pad_cols(router_w[0], LANES)
    rwh = rw.astype(BF16)
    rwl = (rw - rwh.astype(F32)).astype(BF16)
    rb = jnp.concatenate([router_b[0].reshape(1, N_EXPERTS),
                          jnp.full((1, LANES - N_EXPERTS), NEG, F32)], axis=1)
    bgu = jnp.concatenate([exp_b_gu[0][:, 0::2], exp_b_gu[0][:, 1::2]], axis=1).reshape(N_EXPERTS, 1, 2 * D_FF)
    bdn = exp_b_down[0].reshape(N_EXPERTS, 1, D_MODEL)
    row2 = lambda v: v[0].reshape(1, D_MODEL)

    cos_p, sin_p = _rope_tables(jnp.arange(t_p, dtype=jnp.int32))
    cos_s, sin_s = _rope_tables(jnp.tile(past_len + jnp.arange(t_s, dtype=jnp.int32), b_s))

    xp = x_prompt.reshape(n_p, D_MODEL)
    xs = x_sample.reshape(n_s, D_MODEL)
    qlat_p, qrope_p, c_p, cbf_p, kr_p, krbf_p, dt_p = _mla_in(xp, w_mla, qg, kvg, wq, wuk, cos_p, sin_p, ROW_TILE)
    qlat_s, qrope_s, c_s, _, kr_s, _, dt_s = _mla_in(xs, w_mla, qg, kvg, wq, wuk, cos_s, sin_s, ROW_TILE)
    z_p, gates_p = (_proj(xp, w, PROJ_TILE) for w in (w_z, w_gates))
    xcv_p, ctail_p = _proj_conv(xp, w_xbc, cw, cb, PROJ_TILE, t_p)
    z_s, xbc_s, gates_s = (_proj(xs, w, ROW_TILE) for w in (w_z, w_xbc, w_gates))

    conv0_s = jnp.pad(state_conv[0], ((0, 0), (SUBLANES - (D_CONV - 1), 0), (0, 0)))
    ya_p, ssm_p = _ssd(xcv_p, z_p, dt_p, None, None, cw, cb, dtb, a_neg, d_x, ng, e_mat,
                       nb=b_p, nc=t_p // SSD_CHUNK, t_in=SSD_CHUNK, chunk=SSD_CHUNK, y_dtype=BF16)
    ya_s, ssm_s, ctail_s = _ssd(xbc_s, z_s, dt_s, conv0_s, state_ssm[0].reshape(b_s, D_INNER, D_STATE),
                                cw, cb, dtb, a_neg, d_x, ng, e_mat,
                                nb=b_s, nc=1, t_in=t_s, chunk=SAMPLE_CHUNK, y_dtype=F32)
    conv_p = ctail_p[:, SUBLANES - (D_CONV - 1):, :]
    conv_s = ctail_s[:, SUBLANES - (D_CONV - 1):, :]

    o_p = _attn_prompt(qlat_p, qrope_p, cbf_p, krbf_p, wuv, nb=b_p, t=t_p, tq=ATTN_TQ)
    ql_s = jnp.transpose(qlat_s.reshape(MLA_HEADS, b_s, t_s, KV_LORA), (1, 0, 2, 3))
    qr_s = jnp.transpose(qrope_s.reshape(MLA_HEADS, b_s, t_s, ROPE_DIM), (1, 0, 2, 3))
    cnew_s = jnp.pad(c_s.reshape(b_s, t_s, KV_LORA), ((0, 0), (0, LANES - t_s), (0, 0)))
    krnew_s = jnp.swapaxes(jnp.pad(kr_s.reshape(b_s, t_s, ROPE_DIM), ((0, 0), (0, LANES - t_s), (0, 0))), 1, 2)
    olat_s = _attn_sample(page_table, ql_s.reshape(b_s, MLA_HEADS * t_s, KV_LORA),
                          qr_s.reshape(b_s, MLA_HEADS * t_s, ROPE_DIM), cnew_s, krnew_s,
                          cache_kv_latent[0], jnp.swapaxes(cache_k_rope[0], 1, 2),
                          t_new=t_s, gp=PAGE_GROUP)
    olat_s = jnp.transpose(olat_s.reshape(b_s, MLA_HEADS, t_s, KV_LORA), (1, 0, 2, 3)).reshape(MLA_HEADS, n_s, KV_LORA)
    o_s = _head_proj(olat_s, wuv)

    h, h3, top_i, top_g, rank, counts = _merge(ya_p, ya_s.astype(BF16), o_p, o_s, gates_p, gates_s, xp, xs, wa, wb, wo,
                                 row2(ln1_g), row2(ln1_b), rwh, rwl, rb, ROW_TILE)

    tb = MOE_TB
    n_blocks = -(-(n * TOP_K + N_EXPERTS * (tb - 1)) // tb)
    dest, pad_start, blk_e, n_used = _routing(top_i[:, :TOP_K], rank[:, :TOP_K], counts[0, :N_EXPERTS], n_blocks, tb)
    xr3 = _dispatch(pad_start, dest, h3, (n_blocks + 1) * tb, COMBINE_TILE, tb)
    yr3 = _ffn(blk_e, n_used, xr3, n_blocks, exp_w_gu[0], bgu, exp_w_down[0], bdn, tb)
    y_p, y_s = _combine(dest, yr3, top_g, h, row2(ln2_g), row2(ln2_b), n_p, COMBINE_TILE)

    return (y_p.reshape(b_p, t_p, D_MODEL), y_s.reshape(b_s, t_s, D_MODEL),
            c_p.reshape(1, b_p, t_p, KV_LORA), kr_p.reshape(1, b_p, t_p, ROPE_DIM),
            ssm_p.reshape(1, b_p, SSD_HEADS, SSD_HEAD_DIM, D_STATE), conv_p[None],
            c_s.reshape(1, b_s, t_s, KV_LORA), kr_s.reshape(1, b_s, t_s, ROPE_DIM),
            ssm_s.reshape(1, b_s, SSD_HEADS, SSD_HEAD_DIM, D_STATE), conv_s[None])
```

```python
import functools

import jax
import jax.numpy as jnp
import numpy as np
from jax import lax
from jax.experimental import pallas as pl
from jax.experimental.pallas import tpu as pltpu

F32 = jnp.float32
BF16 = jnp.bfloat16

D_MODEL = 1024
D_INNER = 2048
SSD_HEAD_DIM = 64
SSD_HEADS = 32
SSD_GROUPS = 4
HEADS_PER_GROUP = 8
D_STATE = 128
D_CONV = 4
CONV_DIM = D_INNER + 2 * SSD_GROUPS * D_STATE
SSD_CHUNK = 128
MLA_HEADS = 8
Q_LORA = 256
KV_LORA = 256
QK_NOPE = 128
ROPE_DIM = 64
V_HEAD = 128
ROPE_THETA = 10000.0
ATTN_SCALE = (QK_NOPE + ROPE_DIM) ** -0.5
N_EXPERTS = 32
TOP_K = 4
D_FF = 1024
SWIGLU_ALPHA = 1.702
SWIGLU_LIMIT = 7.0
DEPTH = 1
DN_ALPHA = (2 * DEPTH) ** 0.25
LN_EPS = 1e-5
RMS_EPS = 1e-6
IN_SIZES = (Q_LORA, KV_LORA, ROPE_DIM, D_INNER, CONV_DIM, SSD_HEADS, D_MODEL, D_MODEL)
IN_OFFS = tuple(int(v) for v in np.cumsum((0,) + IN_SIZES))

LANES = 128
SUBLANES = 8
VMEM_LIMIT = 56 * 1024 * 1024

NEG = -1e30

ROW_TILE = 256
PROJ_TILE = 512
ATTN_TQ = 256
PAGE_GROUP = 64
SAMPLE_CHUNK = 32
MOE_TB = 256
COMBINE_TILE = 128
MLA_A_COLS = 896


def _cparams(n_axes):
    return pltpu.CompilerParams(dimension_semantics=("arbitrary",) * n_axes,
                                vmem_limit_bytes=VMEM_LIMIT)


def _dot(a, b):
    return jnp.dot(a, b, preferred_element_type=F32)


def _dot_nt(a, b):
    return lax.dot_general(a, b, (((1,), (1,)), ((), ())), preferred_element_type=F32)


def _split3(x):
    hi = x.astype(BF16)
    r1 = x - hi.astype(F32)
    mid = r1.astype(BF16)
    lo = (r1 - mid.astype(F32)).astype(BF16)
    return hi, mid, lo


def _sigmoid(x):
    return 1.0 / (1.0 + jnp.exp(-x))


def _layer_norm(x, g, b):
    mu = jnp.mean(x, axis=-1, keepdims=True)
    xc = x - mu
    var = jnp.mean(xc * xc, axis=-1, keepdims=True)
    return xc * lax.rsqrt(var + LN_EPS) * g + b


def _rms_norm(x, g):
    return x * lax.rsqrt(jnp.mean(x * x, axis=-1, keepdims=True) + RMS_EPS) * g


def _proj_kernel(x_ref, w_ref, o_ref):
    o_ref[...] = _dot(x_ref[...].astype(BF16), w_ref[...])


def _proj(x, w, tm):
    m, k = x.shape
    n = w.shape[1]
    return pl.pallas_call(
        _proj_kernel,
        out_shape=jax.ShapeDtypeStruct((m, n), F32),
        grid=(m // tm,),
        in_specs=[pl.BlockSpec((tm, k), lambda i: (i, 0)),
                  pl.BlockSpec((k, n), lambda i: (0, 0))],
        out_specs=pl.BlockSpec((tm, n), lambda i: (i, 0)),
        compiler_params=_cparams(1),
        name="proj",
    )(x, w)


def _mla_in_kernel(x_ref, wa_ref, qg_ref, kvg_ref, wq_ref, wuk_ref, cos_ref, sin_ref,
                   qlat_ref, qrope_ref, c_ref, cbf_ref, kr_ref, krbf_ref, dt_ref):
    xb = x_ref[...].astype(BF16)
    pa = _dot(xb, wa_ref[...])
    q_a = pa[:, 0:256]
    kv_a = pa[:, 256:512]
    kr_raw = pa[:, 512:640]
    kr_rot = pa[:, 640:768]
    dt_ref[...] = pa[:, 768:896]
    cos = cos_ref[...]
    sin = sin_ref[...]
    c = _rms_norm(kv_a, kvg_ref[...])
    c_ref[...] = c
    cbf_ref[...] = c.astype(BF16)
    kr = kr_raw * cos + kr_rot * sin
    kr_ref[...] = kr[:, 0:ROPE_DIM]
    krbf_ref[...] = kr[:, 0:ROPE_DIM].astype(BF16)
    qn = _rms_norm(q_a, qg_ref[...]).astype(BF16)
    qq = _dot(qn, wq_ref[...])
    cos4 = jnp.concatenate([cos] * 4, axis=1)
    sin4 = jnp.concatenate([sin] * 4, axis=1)
    qr = (qq[:, 1024:1536] * cos4 + qq[:, 1536:2048] * sin4) * ATTN_SCALE
    for h in range(MLA_HEADS):
        qrope_ref[h] = qr[:, h * ROPE_DIM:(h + 1) * ROPE_DIM].astype(BF16)
        nope = qq[:, h * QK_NOPE:(h + 1) * QK_NOPE].astype(BF16)
        qlat_ref[h] = (_dot(nope, wuk_ref[h]) * ATTN_SCALE).astype(BF16)


def _mla_in(x, wa, qg, kvg, wq, wuk, cos_t, sin_t, tm):
    m = x.shape[0]
    tab_blocks = cos_t.shape[0] // tm
    row = lambda w: pl.BlockSpec((tm, w), lambda i: (i, 0))
    tab = pl.BlockSpec((tm, LANES), lambda i: (i % tab_blocks, 0))
    full2 = lambda a: pl.BlockSpec(a.shape, lambda i: (0, 0))
    full3 = lambda a: pl.BlockSpec(a.shape, lambda i: (0, 0, 0))
    return pl.pallas_call(
        _mla_in_kernel,
        out_shape=(jax.ShapeDtypeStruct((MLA_HEADS, m, KV_LORA), BF16),
                   jax.ShapeDtypeStruct((MLA_HEADS, m, ROPE_DIM), BF16),
                   jax.ShapeDtypeStruct((m, KV_LORA), F32),
                   jax.ShapeDtypeStruct((m, KV_LORA), BF16),
                   jax.ShapeDtypeStruct((m, ROPE_DIM), F32),
                   jax.ShapeDtypeStruct((m, ROPE_DIM), BF16),
                   jax.ShapeDtypeStruct((m, LANES), F32)),
        grid=(m // tm,),
        in_specs=[row(D_MODEL), full2(wa), full2(qg), full2(kvg), full2(wq), full3(wuk), tab, tab],
        out_specs=(pl.BlockSpec((MLA_HEADS, tm, KV_LORA), lambda i: (0, i, 0)),
                   pl.BlockSpec((MLA_HEADS, tm, ROPE_DIM), lambda i: (0, i, 0)),
                   row(KV_LORA), row(KV_LORA), row(ROPE_DIM), row(ROPE_DIM), row(LANES)),
        compiler_params=_cparams(1),
        name="mla_in",
    )(x, wa, qg, kvg, wq, wuk, cos_t, sin_t)


def _expand_heads(v, e_ref, terms):
    out = None
    rem = v
    for _ in range(terms):
        part = rem.astype(BF16)
        rem = rem - part.astype(F32)
        d = _dot(part, e_ref[...])
        out = d if out is None else out + d
    return out


def _conv_silu(ext, r0, rows, cw_ref, cb_ref, out, out_r0):
    for j in range(CONV_DIM // LANES):
        sl = slice(j * LANES, (j + 1) * LANES)
        acc = cb_ref[:, sl]
        for k in range(D_CONV):
            acc = acc + ext[pl.ds(r0 + SUBLANES - (D_CONV - 1) + k, rows), sl] * cw_ref[k:k + 1, sl]
        out[out_r0:out_r0 + rows, sl] = acc * _sigmoid(acc)


def _ssd_kernel(*refs, t_in, L, has_h0):
    n_pairs = SSD_HEADS // 2
    refs = list(refs)
    hT = refs[-n_pairs:]
    it = iter(refs[:-n_pairs])
    xbc_ref, z_ref, dt_ref, conv0_ref = next(it), next(it), next(it), next(it)
    h0_ref = next(it) if has_h0 else None
    (cw_ref, cb_ref, dtb_ref, a_ref, dx_ref, ng_ref, e_ref, y_ref, hfin_ref, ctail_ref,
     ext, xcv, ybuf, acs_s, acsT_s, dtT_s, eax_s, wtx_s) = it
    c = pl.program_id(1)
    nc = pl.num_programs(1)

    @pl.when(c == 0)
    def _():
        ext[0:SUBLANES, :] = conv0_ref[0]
        if t_in < L:
            ext[SUBLANES:SUBLANES + L, :] = jnp.zeros((L, CONV_DIM), F32)
        for j in range(n_pairs):
            if has_h0:
                hT[j][...] = h0_ref[0, j * LANES:(j + 1) * LANES, :].T
            else:
                hT[j][...] = jnp.zeros((D_STATE, LANES), F32)

    ext[SUBLANES:SUBLANES + t_in, :] = xbc_ref[...]

    @pl.when(c == nc - 1)
    def _():
        ctail_ref[0] = ext[t_in:t_in + SUBLANES, :]

    _conv_silu(ext, 0, L, cw_ref, cb_ref, xcv, 0)
    ext[0:SUBLANES, :] = ext[L:L + SUBLANES, :]

    row = lax.broadcasted_iota(jnp.int32, (L, L), 0)
    col = lax.broadcasted_iota(jnp.int32, (L, L), 1)
    tri = row >= col
    tri_b = jnp.where(tri, 1.0, 0.0).astype(BF16)
    if t_in < L:
        dt_raw = jnp.concatenate([dt_ref[...], jnp.zeros((L - t_in, LANES), F32)], axis=0)
    else:
        dt_raw = dt_ref[...]
    xs = dt_raw + dtb_ref[...]
    dtv = jnp.maximum(xs, 0.0) + jnp.log1p(jnp.exp(-jnp.abs(xs)))
    if t_in < L:
        dtv = jnp.where(lax.broadcasted_iota(jnp.int32, (L, LANES), 0) < t_in, dtv, 0.0)
    da = dtv * a_ref[...]
    hi, mid, lo = _split3(da)
    acs = _dot(tri_b, hi) + _dot(tri_b, mid) + _dot(tri_b, lo)
    acs_s[...] = acs
    acsT_s[...] = acs.T
    dtT_s[...] = dtv.T
    eax_s[...] = _expand_heads(jnp.exp(acs), e_ref, 3)
    wtx_s[...] = _expand_heads(jnp.exp(acs[L - 1:L, :] - acs) * dtv, e_ref, 2)

    low_half = lax.broadcasted_iota(jnp.int32, (L, LANES), 1) < SSD_HEAD_DIM
    pairs_per_group = HEADS_PER_GROUP // 2
    for g in range(SSD_GROUPS):
        bo = D_INNER + g * D_STATE
        co = D_INNER + SSD_GROUPS * D_STATE + g * D_STATE
        bm = xcv[:, bo:bo + D_STATE]
        cmb = xcv[:, co:co + D_STATE].astype(BF16)
        cb = _dot_nt(cmb, bm.astype(BF16))
        bmT = bm.T.astype(BF16)
        for k in range(pairs_per_group):
            kk = g * pairs_per_group + k
            tl = slice(kk * LANES, (kk + 1) * LANES)
            x2 = xcv[:, tl]
            ws = []
            for idx in (2 * kk, 2 * kk + 1):
                seg = acs_s[:, idx:idx + 1] - acsT_s[idx:idx + 1, :]
                w = cb * jnp.exp(jnp.where(tri, seg, NEG)) * dtT_s[idx:idx + 1, :]
                ws.append(w.astype(BF16))
            w2 = jnp.concatenate(ws, axis=1)
            xbd = jnp.concatenate([jnp.where(low_half, x2, 0.0).astype(BF16),
                                   jnp.where(low_half, 0.0, x2).astype(BF16)], axis=0)
            yd = _dot(w2, xbd)
            hp = hT[kk][...]
            yo = _dot(cmb, hp.astype(BF16))
            ybuf[:, tl] = yd + yo * eax_s[:, tl] + dx_ref[:, tl] * x2
            s = _dot(bmT, (x2 * wtx_s[:, tl]).astype(BF16))
            hT[kk][...] = eax_s[L - 1:L, tl] * hp + s

    gw = D_INNER // SSD_GROUPS
    for g in range(SSD_GROUPS):
        sl = slice(g * gw, (g + 1) * gw)
        zg = z_ref[:, sl]
        v = ybuf[0:t_in, sl] * (zg * _sigmoid(zg))
        ms = jnp.mean(v * v, axis=-1, keepdims=True)
        y_ref[:, sl] = (v * lax.rsqrt(ms + RMS_EPS) * ng_ref[:, sl]).astype(y_ref.dtype)

    @pl.when(c == nc - 1)
    def _():
        for j in range(n_pairs):
            hfin_ref[0, j * LANES:(j + 1) * LANES, :] = hT[j][...].T


def _ssd(xbc, z, dt, conv0, h0, cw, cb, dtb, a_neg, d_x, ng, e_mat, *, nb, nc, t_in, chunk, y_dtype):
    L = chunk
    has_h0 = h0 is not None
    rowspec = lambda w: pl.BlockSpec((t_in, w), lambda b, c: (b * nc + c, 0))
    par = lambda a: pl.BlockSpec(a.shape, lambda b, c: (0, 0))
    per_seq = lambda r, w: pl.BlockSpec((1, r, w), lambda b, c: (b, 0, 0))
    in_specs = [rowspec(CONV_DIM), rowspec(D_INNER), rowspec(LANES), per_seq(SUBLANES, CONV_DIM)]
    args = [xbc, z, dt, conv0]
    if has_h0:
        in_specs.append(per_seq(D_INNER, D_STATE))
        args.append(h0)
    params = [cw, cb, dtb, a_neg, d_x, ng, e_mat]
    in_specs += [par(p) for p in params]
    args += params
    return pl.pallas_call(
        functools.partial(_ssd_kernel, t_in=t_in, L=L, has_h0=has_h0),
        out_shape=(jax.ShapeDtypeStruct((nb * nc * t_in, D_INNER), y_dtype),
                   jax.ShapeDtypeStruct((nb, D_INNER, D_STATE), F32),
                   jax.ShapeDtypeStruct((nb, SUBLANES, CONV_DIM), F32)),
        grid=(nb, nc),
        in_specs=in_specs,
        out_specs=(rowspec(D_INNER), per_seq(D_INNER, D_STATE), per_seq(SUBLANES, CONV_DIM)),
        scratch_shapes=[pltpu.VMEM((L + 2 * SUBLANES, CONV_DIM), F32),
                        pltpu.VMEM((L, CONV_DIM), F32),
                        pltpu.VMEM((L, D_INNER), F32),
                        pltpu.VMEM((L, LANES), F32), pltpu.VMEM((LANES, L), F32), pltpu.VMEM((LANES, L), F32),
                        pltpu.VMEM((L, D_INNER), F32), pltpu.VMEM((L, D_INNER), F32)]
                       + [pltpu.VMEM((D_STATE, LANES), F32)] * (SSD_HEADS // 2),
        compiler_params=_cparams(2),
        name="ssd",
    )(*args)


def _softmax_step(s, kc, m_scr, l_scr, acc_scr):
    reps = s.shape[1] // LANES
    m_prev = m_scr[...]
    m_new = jnp.maximum(m_prev, jnp.max(s, axis=1, keepdims=True))
    alpha = jnp.exp(m_prev - m_new)
    p = jnp.exp(s - (jnp.concatenate([m_new] * reps, axis=1) if reps > 1 else m_new))
    l_scr[...] = alpha * l_scr[...] + jnp.sum(p, axis=1, keepdims=True)
    acc_scr[...] = acc_scr[...] * jnp.concatenate([alpha] * (KV_LORA // LANES), axis=1) + _dot(p.astype(BF16), kc)
    m_scr[...] = m_new


def _attn_prompt_kernel(qlat_ref, qrope_ref, c_ref, kr_ref, wuv_ref, o_ref, m_scr, l_scr, acc_scr, *, tq):
    qi = pl.program_id(1)
    R = MLA_HEADS * tq

    def scores(block, width):
        start = pl.multiple_of(block * width, width)
        kc = c_ref[pl.ds(start, width), :]
        kr = kr_ref[pl.ds(start, width), :]
        q = qlat_ref[...].reshape(R, KV_LORA)
        qr = qrope_ref[...].reshape(R, ROPE_DIM)
        return _dot_nt(q, kc) + _dot_nt(qr, kr), kc

    s, kc = scores(qi, tq)
    t = lax.broadcasted_iota(jnp.int32, (R, tq), 0) & (tq - 1)
    kcol = lax.broadcasted_iota(jnp.int32, (R, tq), 1)
    s = jnp.where(kcol <= t, s, NEG)
    m0 = jnp.max(s, axis=1, keepdims=True)
    p = jnp.exp(s - m0)
    m_scr[...] = jnp.broadcast_to(m0, m_scr.shape)
    l_scr[...] = jnp.broadcast_to(jnp.sum(p, axis=1, keepdims=True), l_scr.shape)
    acc_scr[...] = _dot(p.astype(BF16), kc)

    def body(j, carry):
        s2, kc2 = scores(j, 2 * tq)
        _softmax_step(s2, kc2, m_scr, l_scr, acc_scr)
        return carry

    lax.fori_loop(0, qi // 2, body, 0)

    @pl.when(qi % 2 == 1)
    def _():
        s1, kc1 = scores(qi - 1, tq)
        _softmax_step(s1, kc1, m_scr, l_scr, acc_scr)

    inv = 1.0 / l_scr[...]
    o_lat = acc_scr[...] * jnp.concatenate([inv] * (KV_LORA // LANES), axis=1)
    for h in range(MLA_HEADS):
        oh = _dot(o_lat[h * tq:(h + 1) * tq, :].astype(BF16), wuv_ref[h])
        o_ref[:, h * V_HEAD:(h + 1) * V_HEAD] = oh.astype(o_ref.dtype)


def _attn_prompt(qlat, qrope, cbf, krbf, wuv, *, nb, t, tq):
    nq = t // tq
    R = MLA_HEADS * tq
    return pl.pallas_call(
        functools.partial(_attn_prompt_kernel, tq=tq),
        out_shape=jax.ShapeDtypeStruct((nb * t, MLA_HEADS * V_HEAD), BF16),
        grid=(nb, nq),
        in_specs=[pl.BlockSpec((MLA_HEADS, tq, KV_LORA), lambda b, i: (0, b * nq + i, 0)),
                  pl.BlockSpec((MLA_HEADS, tq, ROPE_DIM), lambda b, i: (0, b * nq + i, 0)),
                  pl.BlockSpec((t, KV_LORA), lambda b, i: (b, 0)),
                  pl.BlockSpec((t, ROPE_DIM), lambda b, i: (b, 0)),
                  pl.BlockSpec(wuv.shape, lambda b, i: (0, 0, 0))],
        out_specs=pl.BlockSpec((tq, MLA_HEADS * V_HEAD), lambda b, i: (b * nq + i, 0)),
        scratch_shapes=[pltpu.VMEM((R, LANES), F32), pltpu.VMEM((R, LANES), F32),
                        pltpu.VMEM((R, KV_LORA), F32)],
        compiler_params=_cparams(2),
        name="attn_prompt",
    )(qlat, qrope, cbf, krbf, wuv)


def _attn_sample_kernel(pt_ref, qlat_ref, qrope_ref, cnew_ref, krnew_ref, kv_hbm, kr_hbm, o_ref,
                        kvbuf, krbuf, kv_sem, kr_sem, m_scr, l_scr, acc_scr, *, n_groups, gp, t_new):
    b = pl.program_id(0)
    nb = pl.num_programs(0)
    R = qlat_ref.shape[1]
    page = kvbuf.shape[2]
    q = qlat_ref[0]
    qr = qrope_ref[0]

    def start_group(bb, g, slot):
        for p in range(gp):
            pid = pt_ref[bb, g * gp + p]
            pltpu.make_async_copy(kv_hbm.at[pid], kvbuf.at[slot, p], kv_sem.at[slot]).start()
            pltpu.make_async_copy(kr_hbm.at[pid], krbuf.at[slot, p], kr_sem.at[slot]).start()

    def wait_group(slot):
        pltpu.make_async_copy(kvbuf.at[slot], kvbuf.at[slot], kv_sem.at[slot]).wait()
        pltpu.make_async_copy(krbuf.at[slot], krbuf.at[slot], kr_sem.at[slot]).wait()

    @pl.when(b == 0)
    def _():
        start_group(0, 0, 0)

    kc = cnew_ref[0].astype(BF16)
    kr = krnew_ref[0].astype(BF16)
    s = _dot_nt(q, kc) + _dot(qr, kr)
    t = lax.broadcasted_iota(jnp.int32, (R, LANES), 0) & (t_new - 1)
    kcol = lax.broadcasted_iota(jnp.int32, (R, LANES), 1)
    s = jnp.where(kcol <= t, s, NEG)
    m0 = jnp.max(s, axis=1, keepdims=True)
    p0 = jnp.exp(s - m0)
    m_scr[...] = jnp.broadcast_to(m0, m_scr.shape)
    l_scr[...] = jnp.broadcast_to(jnp.sum(p0, axis=1, keepdims=True), l_scr.shape)
    acc_scr[...] = _dot(p0.astype(BF16), kc)

    for g in range(n_groups):
        slot = g % 2
        if g + 1 < n_groups:
            start_group(b, g + 1, 1 - slot)
        else:
            @pl.when(b + 1 < nb)
            def _():
                start_group(b + 1, 0, 1 - slot)
        wait_group(slot)
        kc = kvbuf[slot].reshape(gp * page, KV_LORA).astype(BF16)
        kr = jnp.concatenate([krbuf[slot, p].astype(BF16) for p in range(gp)], axis=1)
        s = _dot_nt(q, kc) + _dot(qr, kr)
        _softmax_step(s, kc, m_scr, l_scr, acc_scr)

    inv = 1.0 / l_scr[...]
    o_ref[0] = acc_scr[...] * jnp.concatenate([inv] * (KV_LORA // LANES), axis=1)


def _attn_sample(page_table, qlat, qrope, cnew, krnew, cache_kv, cache_kr, *, t_new, gp):
    nb, n_pages = page_table.shape
    n_groups = n_pages // gp
    R = qlat.shape[1]
    page = cache_kv.shape[1]
    per_b = lambda r, w: pl.BlockSpec((1, r, w), lambda b, pt: (b, 0, 0))
    grid_spec = pltpu.PrefetchScalarGridSpec(
        num_scalar_prefetch=1,
        grid=(nb,),
        in_specs=[per_b(R, KV_LORA), per_b(R, ROPE_DIM), per_b(LANES, KV_LORA), per_b(ROPE_DIM, LANES),
                  pl.BlockSpec(memory_space=pl.ANY), pl.BlockSpec(memory_space=pl.ANY)],
        out_specs=per_b(R, KV_LORA),
        scratch_shapes=[pltpu.VMEM((2, gp, page, KV_LORA), F32), pltpu.VMEM((2, gp, ROPE_DIM, page), F32),
                        pltpu.SemaphoreType.DMA((2,)), pltpu.SemaphoreType.DMA((2,)),
                        pltpu.VMEM((R, LANES), F32), pltpu.VMEM((R, LANES), F32),
                        pltpu.VMEM((R, KV_LORA), F32)],
    )
    return pl.pallas_call(
        functools.partial(_attn_sample_kernel, n_groups=n_groups, gp=gp, t_new=t_new),
        out_shape=jax.ShapeDtypeStruct((nb, R, KV_LORA), F32),
        grid_spec=grid_spec,
        compiler_params=_cparams(1),
        name="attn_sample",
    )(page_table, qlat, qrope, cnew, krnew, cache_kv, cache_kr)


def _head_proj_kernel(x_ref, w_ref, o_ref):
    o_ref[...] = _dot(x_ref[0].astype(BF16), w_ref[0]).astype(o_ref.dtype)


def _head_proj(x, w):
    h, m, k = x.shape
    n = w.shape[2]
    return pl.pallas_call(
        _head_proj_kernel,
        out_shape=jax.ShapeDtypeStruct((m, h * n), BF16),
        grid=(h,),
        in_specs=[pl.BlockSpec((1, m, k), lambda i: (i, 0, 0)),
                  pl.BlockSpec((1, k, n), lambda i: (i, 0, 0))],
        out_specs=pl.BlockSpec((m, n), lambda i: (0, i)),
        compiler_params=_cparams(1),
        name="head_proj",
    )(x, w)


def _merge_kernel(ya_p_ref, ya_s_ref, o_p_ref, o_s_ref, g_p_ref, g_s_ref, u_p_ref, u_s_ref, wa_ref, wb_ref, wo_ref,
                  lg_ref, lb_ref, rwh_ref, rwl_ref, rb_ref,
                  h_ref, h3_ref, ti_ref, tg_ref, rank_ref, cnt_ref, run_ref, *, n_prompt_tiles):
    i = pl.program_id(0)
    is_p = i < n_prompt_tiles
    yn = jnp.where(is_p, ya_p_ref[...], ya_s_ref[...])
    ob = jnp.where(is_p, o_p_ref[...], o_s_ref[...])
    g = jnp.where(is_p, g_p_ref[...], g_s_ref[...])
    u = jnp.where(is_p, u_p_ref[...], u_s_ref[...])
    ya = _dot(yn, wa_ref[...])
    yb = _dot(ob, wb_ref[...])
    merged = _sigmoid(g[:, 0:D_MODEL]) * ya + _sigmoid(g[:, D_MODEL:2 * D_MODEL]) * yb
    mix = _dot(merged.astype(BF16), wo_ref[...])
    h = _layer_norm(DN_ALPHA * u + mix, lg_ref[...], lb_ref[...])
    h_ref[...] = h
    tm = h.shape[0]
    for j in range(D_MODEL // LANES):
        h3_ref[pl.ds(j, tm, stride=SUBLANES), :] = h[:, j * LANES:(j + 1) * LANES]
    hh = h.astype(BF16)
    hl = (h - hh.astype(F32)).astype(BF16)
    logits = _dot(hh, rwh_ref[...]) + _dot(hl, rwh_ref[...]) + _dot(hh, rwl_ref[...]) + rb_ref[...]
    lane = lax.broadcasted_iota(jnp.int32, logits.shape, 1).astype(F32)
    vals = jnp.zeros_like(logits)
    idxs = jnp.zeros_like(logits)
    cur = logits
    picked = []
    for k in range(TOP_K):
        mx = jnp.max(cur, axis=1, keepdims=True)
        ix = jnp.min(jnp.where(cur == mx, lane, float(LANES)), axis=1, keepdims=True)
        vals = jnp.where(lane == k, mx, vals)
        idxs = jnp.where(lane == k, ix, idxs)
        picked.append(lane == ix)
        cur = jnp.where(picked[k], NEG * 2.0, cur)
    v0 = vals[:, 0:1]
    e = jnp.where(lane < TOP_K, jnp.exp(vals - v0), 0.0)
    tg_ref[...] = e / jnp.sum(e, axis=1, keepdims=True)
    ti_ref[...] = idxs.astype(jnp.int32)

    @pl.when(i == 0)
    def _():
        run_ref[...] = jnp.zeros_like(run_ref)

    cnt = jnp.zeros_like(logits)
    for k in range(TOP_K):
        cnt = cnt + jnp.where(picked[k], 1.0, 0.0)
    r_i = lax.broadcasted_iota(jnp.int32, (tm, tm), 0)
    c_i = lax.broadcasted_iota(jnp.int32, (tm, tm), 1)
    before = _dot(jnp.where(r_i > c_i, 1.0, 0.0).astype(BF16), cnt.astype(BF16)) + run_ref[0:1, :]
    ranks = jnp.zeros_like(logits)
    for k in range(TOP_K):
        rk = jnp.sum(jnp.where(picked[k], before, 0.0), axis=1, keepdims=True)
        ranks = jnp.where(lane == k, rk, ranks)
    rank_ref[...] = ranks.astype(jnp.int32)
    run_ref[0:1, :] = run_ref[0:1, :] + jnp.sum(cnt, axis=0, keepdims=True)
    cnt_ref[...] = run_ref[...].astype(jnp.int32)


def _merge(ya_p, ya_s, o_p, o_s, g_p, g_s, u_p, u_s, wa, wb, wo, lg, lb, rwh, rwl, rb, tm):
    m = u_p.shape[0] + u_s.shape[0]
    npt = ya_p.shape[0] // tm
    row = lambda w: pl.BlockSpec((tm, w), lambda i: (i, 0))
    prow = lambda w: pl.BlockSpec((tm, w), lambda i: (jnp.minimum(i, npt - 1), 0))
    srow = lambda w: pl.BlockSpec((tm, w), lambda i: (jnp.maximum(i - npt, 0), 0))
    par = lambda a: pl.BlockSpec(a.shape, lambda i: (0, 0))
    return pl.pallas_call(
        functools.partial(_merge_kernel, n_prompt_tiles=npt),
        out_shape=(jax.ShapeDtypeStruct((m, D_MODEL), F32),
                   jax.ShapeDtypeStruct((m * SUBLANES, LANES), F32),
                   jax.ShapeDtypeStruct((m, LANES), jnp.int32),
                   jax.ShapeDtypeStruct((m, LANES), F32),
                   jax.ShapeDtypeStruct((m, LANES), jnp.int32),
                   jax.ShapeDtypeStruct((SUBLANES, LANES), jnp.int32)),
        grid=(m // tm,),
        in_specs=[prow(D_INNER), srow(D_INNER), prow(D_MODEL), srow(D_MODEL), prow(2 * D_MODEL), srow(2 * D_MODEL),
                  prow(D_MODEL), srow(D_MODEL),
                  par(wa), par(wb), par(wo), par(lg), par(lb), par(rwh), par(rwl), par(rb)],
        out_specs=(row(D_MODEL), pl.BlockSpec((tm * SUBLANES, LANES), lambda i: (i, 0)), row(LANES), row(LANES),
                   row(LANES), pl.BlockSpec((SUBLANES, LANES), lambda i: (0, 0))),
        scratch_shapes=[pltpu.VMEM((SUBLANES, LANES), F32)],
        compiler_params=_cparams(1),
        name="merge",
    )(ya_p, ya_s, o_p, o_s, g_p, g_s, u_p, u_s, wa, wb, wo, lg, lb, rwh, rwl, rb)


DEINT_IN = 2 * LANES


def _deint_perm():
    perm = np.zeros((DEINT_IN, DEINT_IN), np.float32)
    perm[2 * np.arange(LANES), np.arange(LANES)] = 1.0
    perm[2 * np.arange(LANES) + 1, LANES + np.arange(LANES)] = 1.0
    return jnp.asarray(perm, BF16)


def _row_pipeline(i, n_steps, idx_hbm, src_hbm, idx_smem, idx_sem, buf, row_sem, rows, compute):
    def idx_copy(chunk, s):
        return pltpu.make_async_copy(idx_hbm.at[chunk], idx_smem.at[s], idx_sem.at[s])

    def issue_row(s, r):
        src0 = pl.multiple_of(idx_smem[s, r] * SUBLANES, SUBLANES)
        dst0 = pl.multiple_of(r * SUBLANES, SUBLANES)
        pltpu.make_async_copy(src_hbm.at[pl.ds(src0, SUBLANES), :],
                              buf.at[s, pl.ds(dst0, SUBLANES), :], row_sem.at[s]).start()

    @pl.when(i == 0)
    def _():
        idx_copy(0, 0).start()
        idx_copy(0, 0).wait()

        def body(r, carry):
            issue_row(0, r)
            return carry
        lax.fori_loop(0, rows, body, 0, unroll=8)
        if n_steps > 1:
            idx_copy(1, 1).start()

    def step(slot):
        has_next = i + 1 < n_steps

        @pl.when(has_next)
        def _():
            idx_copy(i + 1, 1 - slot).wait()

        @pl.when(i + 2 < n_steps)
        def _():
            idx_copy(i + 2, slot).start()

        pltpu.make_async_copy(src_hbm.at[pl.ds(0, rows * SUBLANES), :], buf.at[slot], row_sem.at[slot]).wait()

        @pl.when(has_next)
        def _():
            for r in range(rows):
                issue_row(1 - slot, r)
            compute(slot)

        @pl.when(i + 1 == n_steps)
        def _():
            compute(slot)

    step(lax.rem(i, 2))


def _fetch_scratch(rows):
    return [pltpu.SMEM((2, rows), jnp.int32), pltpu.SemaphoreType.DMA((2,)),
            pltpu.VMEM((2, rows * SUBLANES, LANES), F32), pltpu.SemaphoreType.DMA((2,))]


def _dispatch_kernel(pad_ref, dest_hbm, h3_ref, xr_hbm, idx_smem, idx_sem, stage, row_sem, zbuf, z_sem,
                     *, tm, tb, min_used):
    i = pl.program_id(0)
    n = pl.num_programs(0)
    rows = tm * TOP_K

    def idx_copy(chunk, s):
        return pltpu.make_async_copy(dest_hbm.at[chunk], idx_smem.at[s], idx_sem.at[s])

    def rows_wait(s):
        span = xr_hbm.at[pl.ds(0, rows * SUBLANES), :]
        pltpu.make_async_copy(span, span, row_sem.at[s]).wait()

    @pl.when(i == 0)
    def _():
        zbuf[...] = jnp.zeros_like(zbuf)
        zero_to = lambda start: pltpu.make_async_copy(zbuf, xr_hbm.at[pl.ds(start, tb * SUBLANES), :], z_sem)
        for e in range(N_EXPERTS):
            zero_to(pl.multiple_of(pad_ref[e] * SUBLANES, SUBLANES)).start()
        for e in range(N_EXPERTS):
            zero_to(0).wait()
        n_used = pad_ref[N_EXPERTS]
        tail = range(min_used, xr_hbm.shape[0] // (tb * SUBLANES))
        for blk in tail:
            pl.when(blk >= n_used)(zero_to(blk * tb * SUBLANES).start)
        for blk in tail:
            pl.when(blk >= n_used)(zero_to(0).wait)
        idx_copy(0, 0).start()

    def step(slot):
        idx_copy(i, slot).wait()

        @pl.when(i + 1 < n)
        def _():
            idx_copy(i + 1, 1 - slot).start()

        stage[slot] = h3_ref[...]
        for r in range(rows):
            dst0 = pl.multiple_of(idx_smem[slot, r] * SUBLANES, SUBLANES)
            pltpu.make_async_copy(stage.at[slot, pl.ds((r // TOP_K) * SUBLANES, SUBLANES), :],
                                  xr_hbm.at[pl.ds(dst0, SUBLANES), :], row_sem.at[slot]).start()

        @pl.when(i >= 1)
        def _():
            rows_wait(1 - slot)

        @pl.when(i == n - 1)
        def _():
            rows_wait(slot)

    step(lax.rem(i, 2))


def _dispatch(pad_start, dest, h3, cap_rows, tm, tb):
    n = h3.shape[0] // SUBLANES
    grid_spec = pltpu.PrefetchScalarGridSpec(
        num_scalar_prefetch=1,
        grid=(n // tm,),
        in_specs=[pl.BlockSpec(memory_space=pl.ANY),
                  pl.BlockSpec((tm * SUBLANES, LANES), lambda i, pad: (i, 0))],
        out_specs=pl.BlockSpec(memory_space=pl.ANY),
        scratch_shapes=[pltpu.SMEM((2, tm * TOP_K), jnp.int32), pltpu.SemaphoreType.DMA((2,)),
                        pltpu.VMEM((2, tm * SUBLANES, LANES), F32), pltpu.SemaphoreType.DMA((2,)),
                        pltpu.VMEM((tb * SUBLANES, LANES), F32), pltpu.SemaphoreType.DMA(())],
    )
    return pl.pallas_call(
        functools.partial(_dispatch_kernel, tm=tm, tb=tb, min_used=-(-(n * TOP_K) // tb)),
        out_shape=jax.ShapeDtypeStruct((cap_rows * SUBLANES, LANES), F32),
        grid_spec=grid_spec,
        compiler_params=_cparams(1),
        name="dispatch",
    )(pad_start, dest.reshape(n // tm, tm * TOP_K), h3)


def _ffn_kernel(be_ref, nu_ref, x_ref, wgu_ref, bgu_ref, wdn_ref, bdn_ref, perm_ref, o_ref,
                wg_s, wu_s, wd_s, *, tb):
    i = pl.program_id(0)
    nu = nu_ref[0]
    new_expert = jnp.logical_or(i == 0, be_ref[i] != be_ref[jnp.maximum(i - 1, 0)])

    @pl.when(jnp.logical_and(new_expert, i < nu))
    def _():
        for t in range(2 * D_FF // DEINT_IN):
            r = _dot(wgu_ref[:, t * DEINT_IN:(t + 1) * DEINT_IN].astype(BF16), perm_ref[...])
            wg_s[:, t * LANES:(t + 1) * LANES] = r[:, 0:LANES].astype(BF16)
            wu_s[:, t * LANES:(t + 1) * LANES] = r[:, LANES:DEINT_IN].astype(BF16)
        wd_s[...] = wdn_ref[...].astype(BF16)

    @pl.when(i < nu)
    def _():
        x = jnp.concatenate([x_ref[pl.ds(j, tb, stride=SUBLANES), :].astype(BF16)
                             for j in range(D_MODEL // LANES)], axis=1)
        bgu = bgu_ref[...]
        g = jnp.minimum(_dot(x, wg_s[...]) + bgu[:, 0:D_FF], SWIGLU_LIMIT)
        u = jnp.clip(_dot(x, wu_s[...]) + bgu[:, D_FF:2 * D_FF], -SWIGLU_LIMIT, SWIGLU_LIMIT)
        hh = (u + 1.0) * (g * _sigmoid(SWIGLU_ALPHA * g))
        out = _dot(hh.astype(BF16), wd_s[...]) + bdn_ref[...]
        for j in range(D_MODEL // LANES):
            o_ref[pl.ds(j, tb, stride=SUBLANES), :] = out[:, j * LANES:(j + 1) * LANES]

    @pl.when(i >= nu)
    def _():
        o_ref[...] = jnp.zeros_like(o_ref)


def _ffn(blk_e, n_used, xr3, nblk, wgu, bgu, wdn, bdn, tb):
    perm = _deint_perm()
    wspec = lambda a: pl.BlockSpec((None,) + a.shape[1:], lambda i, be, nu: (be[i], 0, 0))
    grid_spec = pltpu.PrefetchScalarGridSpec(
        num_scalar_prefetch=2,
        grid=(nblk,),
        in_specs=[pl.BlockSpec((tb * SUBLANES, LANES), lambda i, be, nu: (jnp.minimum(i, nu[0] - 1), 0)),
                  wspec(wgu), wspec(bgu), wspec(wdn), wspec(bdn),
                  pl.BlockSpec(perm.shape, lambda i, be, nu: (0, 0))],
        out_specs=pl.BlockSpec((tb * SUBLANES, LANES), lambda i, be, nu: (i, 0)),
        scratch_shapes=[pltpu.VMEM((D_MODEL, D_FF), BF16), pltpu.VMEM((D_MODEL, D_FF), BF16),
                        pltpu.VMEM((D_FF, D_MODEL), BF16)],
    )
    return pl.pallas_call(
        functools.partial(_ffn_kernel, tb=tb),
        out_shape=jax.ShapeDtypeStruct((nblk * tb * SUBLANES, LANES), F32),
        grid_spec=grid_spec,
        compiler_params=_cparams(1),
        name="moe_ffn",
    )(blk_e, n_used, xr3, wgu, bgu, wdn, bdn, perm)


def _combine_kernel(dest_hbm, yr3_hbm, tg_ref, h_ref, lg_ref, lb_ref, op_ref, os_ref,
                    idx_smem, idx_sem, ybuf, row_sem, *, tm, n_prompt_tiles, n_steps):
    i = pl.program_id(0)
    rows = tm * TOP_K

    def compute(slot):
        tg = tg_ref[...]
        stride = TOP_K * SUBLANES
        pieces = []
        for j in range(D_MODEL // LANES):
            acc = tg[:, 0:1] * ybuf[slot, pl.ds(j, tm, stride=stride), :]
            for k in range(1, TOP_K):
                acc = acc + tg[:, k:k + 1] * ybuf[slot, pl.ds(k * SUBLANES + j, tm, stride=stride), :]
            pieces.append(acc)
        y = jnp.concatenate(pieces, axis=1)
        out = _layer_norm(DN_ALPHA * h_ref[...] + y, lg_ref[...], lb_ref[...])

        @pl.when(i < n_prompt_tiles)
        def _():
            op_ref[...] = out

        @pl.when(i >= n_prompt_tiles)
        def _():
            os_ref[...] = out

    _row_pipeline(i, n_steps, dest_hbm, yr3_hbm, idx_smem, idx_sem, ybuf, row_sem, rows, compute)


def _combine(dest, yr3, tg, h, lg, lb, n_prompt, tm):
    m = h.shape[0]
    npt = n_prompt // tm
    row = lambda w: pl.BlockSpec((tm, w), lambda i: (i, 0))
    par = lambda a: pl.BlockSpec(a.shape, lambda i: (0, 0))
    return pl.pallas_call(
        functools.partial(_combine_kernel, tm=tm, n_prompt_tiles=npt, n_steps=m // tm),
        out_shape=(jax.ShapeDtypeStruct((n_prompt, D_MODEL), F32),
                   jax.ShapeDtypeStruct((m - n_prompt, D_MODEL), F32)),
        grid=(m // tm,),
        in_specs=[pl.BlockSpec(memory_space=pl.ANY), pl.BlockSpec(memory_space=pl.ANY),
                  row(LANES), row(D_MODEL), par(lg), par(lb)],
        out_specs=(pl.BlockSpec((tm, D_MODEL), lambda i: (jnp.minimum(i, npt - 1), 0)),
                   pl.BlockSpec((tm, D_MODEL), lambda i: (jnp.maximum(i - npt, 0), 0))),
        scratch_shapes=_fetch_scratch(tm * TOP_K),
        compiler_params=_cparams(1),
        name="combine",
    )(dest.reshape(m // tm, tm * TOP_K), yr3, tg, h, lg, lb)


def _rope_tables(pos):
    half = ROPE_DIM // 2
    inv = ROPE_THETA ** (-jnp.arange(half, dtype=F32) / half)
    ang = pos.astype(F32)[:, None] * inv[None, :]
    cos, sin = jnp.cos(ang), jnp.sin(ang)
    return jnp.concatenate([cos] * 4, axis=1), jnp.concatenate([-sin, sin] * 2, axis=1)


def _pad_cols(w, width):
    return jnp.pad(w, ((0, 0), (0, width - w.shape[1])))


def _routing(top_i, rank, counts, n_blocks, tb):
    flat_e = top_i.reshape(-1)
    nblk_e = (counts + tb - 1) // tb
    pend = jnp.cumsum(nblk_e)
    pstart = (pend - nblk_e) * tb
    onehot = flat_e[:, None] == jnp.arange(N_EXPERTS, dtype=jnp.int32)[None, :]
    dest = (jnp.sum(jnp.where(onehot, pstart[None, :], 0), axis=1) + rank.reshape(-1)).astype(jnp.int32)
    n_used = pend[-1].astype(jnp.int32)
    pad_start = jnp.concatenate([(pstart + counts).astype(jnp.int32), n_used.reshape(1)])
    bidx = jnp.minimum(jnp.arange(n_blocks, dtype=jnp.int32), n_used - 1)
    blk_e = jnp.minimum(jnp.sum((pend[None, :] <= bidx[:, None]).astype(jnp.int32), axis=1), N_EXPERTS - 1)
    return dest, pad_start, blk_e, n_used.reshape(1)


def kernel(x_prompt, x_sample, cache_kv_latent, cache_k_rope, state_ssm, state_conv, page_table, w_in, ssd_conv_w, ssd_conv_b, ssd_dt_bias, ssd_a_log, ssd_d, ssd_norm_g, mla_q_norm_g, w_q_b, mla_kv_norm_g, w_kv_b, w_branch_a, w_branch_b, w_out, ln1_g, ln1_b, router_w, router_b, exp_w_gu, exp_b_gu, exp_w_down, exp_b_down, ln2_g, ln2_b):
    b_p, t_p, _ = x_prompt.shape
    b_s, t_s, _ = x_sample.shape
    n_pages = page_table.shape[1]
    page = cache_kv_latent.shape[2]
    past_len = n_pages * page
    n_p = b_p * t_p
    n_s = b_s * t_s
    n = n_p + n_s
    assert DEPTH == 1 and w_in.shape[0] == 1
    assert n_p % PROJ_TILE == 0 and t_p % ROW_TILE == 0 and n_s % ROW_TILE == 0
    assert t_p % (2 * ATTN_TQ) == 0 and t_p % SSD_CHUNK == 0
    assert t_s == SUBLANES and n_pages % (2 * PAGE_GROUP) == 0 and page == LANES

    wi = w_in[0]
    seg = lambda k: wi[:, IN_OFFS[k]:IN_OFFS[k + 1]]
    half = ROPE_DIM // 2
    w_kr = seg(2)
    w_kr_rot = jnp.concatenate([w_kr[:, half:], w_kr[:, :half]], axis=1)
    w_mla = jnp.concatenate([seg(0), seg(1), _pad_cols(w_kr, LANES), _pad_cols(w_kr_rot, LANES),
                             _pad_cols(seg(5), LANES)], axis=1).astype(BF16)
    w_z = seg(3).astype(BF16)
    w_xbc = seg(4).astype(BF16)
    w_gates = jnp.concatenate([seg(6), seg(7)], axis=1).astype(BF16)
    wq3 = w_q_b[0].reshape(Q_LORA, MLA_HEADS, QK_NOPE + ROPE_DIM)
    wq_nope = wq3[:, :, :QK_NOPE].reshape(Q_LORA, MLA_HEADS * QK_NOPE)
    wq_rope = wq3[:, :, QK_NOPE:]
    wq_rope_rot = jnp.concatenate([wq_rope[..., half:], wq_rope[..., :half]], axis=-1)
    wq = jnp.concatenate([wq_nope, wq_rope.reshape(Q_LORA, -1), wq_rope_rot.reshape(Q_LORA, -1)],
                         axis=1).astype(BF16)
    wkv3 = w_kv_b[0].reshape(KV_LORA, MLA_HEADS, QK_NOPE + V_HEAD)
    wuk = jnp.transpose(wkv3[:, :, :QK_NOPE], (1, 2, 0)).astype(BF16)
    wuv = jnp.transpose(wkv3[:, :, QK_NOPE:], (1, 0, 2)).astype(BF16)
    qg = mla_q_norm_g[0].reshape(1, Q_LORA)
    kvg = mla_kv_norm_g[0].reshape(1, KV_LORA)
    cw = ssd_conv_w[0]
    cb = ssd_conv_b[0].reshape(1, CONV_DIM)
    dtb = _pad_cols(ssd_dt_bias[0].reshape(1, SSD_HEADS), LANES)
    a_neg = _pad_cols(-jnp.exp(ssd_a_log[0].astype(F32)).reshape(1, SSD_HEADS), LANES)
    d_x = jnp.repeat(ssd_d[0].astype(F32), SSD_HEAD_DIM).reshape(1, D_INNER)
    e_mat = jnp.asarray(np.arange(LANES)[:, None] == np.arange(D_INNER)[None, :] // SSD_HEAD_DIM, BF16)
    ng = ssd_norm_g[0].reshape(1, D_INNER)
    wa = w_branch_a[0].astype(BF16)
    wb = w_branch_b[0].astype(BF16)
    wo = w_out[0].astype(BF16)
    rw = _pad_cols(router_w[0], LANES)
    rwh = rw.astype(BF16)
    rwl = (rw - rwh.astype(F32)).astype(BF16)
    rb = jnp.concatenate([router_b[0].reshape(1, N_EXPERTS),
                          jnp.full((1, LANES - N_EXPERTS), NEG, F32)], axis=1)
    bgu = jnp.concatenate([exp_b_gu[0][:, 0::2], exp_b_gu[0][:, 1::2]], axis=1).reshape(N_EXPERTS, 1, 2 * D_FF)
    bdn = exp_b_down[0].reshape(N_EXPERTS, 1, D_MODEL)
    row2 = lambda v: v[0].reshape(1, D_MODEL)

    cos_p, sin_p = _rope_tables(jnp.arange(t_p, dtype=jnp.int32))
    cos_s, sin_s = _rope_tables(jnp.tile(past_len + jnp.arange(t_s, dtype=jnp.int32), b_s))

    xp = x_prompt.reshape(n_p, D_MODEL)
    xs = x_sample.reshape(n_s, D_MODEL)
    qlat_p, qrope_p, c_p, cbf_p, kr_p, krbf_p, dt_p = _mla_in(xp, w_mla, qg, kvg, wq, wuk, cos_p, sin_p, ROW_TILE)
    qlat_s, qrope_s, c_s, _, kr_s, _, dt_s = _mla_in(xs, w_mla, qg, kvg, wq, wuk, cos_s, sin_s, ROW_TILE)
    z_p, xbc_p, gates_p = (_proj(xp, w, PROJ_TILE) for w in (w_z, w_xbc, w_gates))
    z_s, xbc_s, gates_s = (_proj(xs, w, ROW_TILE) for w in (w_z, w_xbc, w_gates))

    conv0_p = jnp.zeros((b_p, SUBLANES, CONV_DIM), F32)
    conv0_s = jnp.pad(state_conv[0], ((0, 0), (SUBLANES - (D_CONV - 1), 0), (0, 0)))
    ya_p, ssm_p, ctail_p = _ssd(xbc_p, z_p, dt_p, conv0_p, None, cw, cb, dtb, a_neg, d_x, ng, e_mat,
                                nb=b_p, nc=t_p // SSD_CHUNK, t_in=SSD_CHUNK, chunk=SSD_CHUNK, y_dtype=BF16)
    ya_s, ssm_s, ctail_s = _ssd(xbc_s, z_s, dt_s, conv0_s, state_ssm[0].reshape(b_s, D_INNER, D_STATE),
                                cw, cb, dtb, a_neg, d_x, ng, e_mat,
                                nb=b_s, nc=1, t_in=t_s, chunk=SAMPLE_CHUNK, y_dtype=F32)
    conv_p = ctail_p[:, SUBLANES - (D_CONV - 1):, :]
    conv_s = ctail_s[:, SUBLANES - (D_CONV - 1):, :]

    o_p = _attn_prompt(qlat_p, qrope_p, cbf_p, krbf_p, wuv, nb=b_p, t=t_p, tq=ATTN_TQ)
    ql_s = jnp.transpose(qlat_s.reshape(MLA_HEADS, b_s, t_s, KV_LORA), (1, 0, 2, 3))
    qr_s = jnp.transpose(qrope_s.reshape(MLA_HEADS, b_s, t_s, ROPE_DIM), (1, 0, 2, 3))
    cnew_s = jnp.pad(c_s.reshape(b_s, t_s, KV_LORA), ((0, 0), (0, LANES - t_s), (0, 0)))
    krnew_s = jnp.swapaxes(jnp.pad(kr_s.reshape(b_s, t_s, ROPE_DIM), ((0, 0), (0, LANES - t_s), (0, 0))), 1, 2)
    olat_s = _attn_sample(page_table, ql_s.reshape(b_s, MLA_HEADS * t_s, KV_LORA),
                          qr_s.reshape(b_s, MLA_HEADS * t_s, ROPE_DIM), cnew_s, krnew_s,
                          cache_kv_latent[0], jnp.swapaxes(cache_k_rope[0], 1, 2),
                          t_new=t_s, gp=PAGE_GROUP)
    olat_s = jnp.transpose(olat_s.reshape(b_s, MLA_HEADS, t_s, KV_LORA), (1, 0, 2, 3)).reshape(MLA_HEADS, n_s, KV_LORA)
    o_s = _head_proj(olat_s, wuv)

    h, h3, top_i, top_g, rank, counts = _merge(ya_p, ya_s.astype(BF16), o_p, o_s, gates_p, gates_s, xp, xs, wa, wb, wo,
                                 row2(ln1_g), row2(ln1_b), rwh, rwl, rb, ROW_TILE)

    tb = MOE_TB
    n_blocks = -(-(n * TOP_K + N_EXPERTS * (tb - 1)) // tb)
    dest, pad_start, blk_e, n_used = _routing(top_i[:, :TOP_K], rank[:, :TOP_K], counts[0, :N_EXPERTS], n_blocks, tb)
    xr3 = _dispatch(pad_start, dest, h3, (n_blocks + 1) * tb, COMBINE_TILE, tb)
    yr3 = _ffn(blk_e, n_used, xr3, n_blocks, exp_w_gu[0], bgu, exp_w_down[0], bdn, tb)
    y_p, y_s = _combine(dest, yr3, top_g, h, row2(ln2_g), row2(ln2_b), n_p, COMBINE_TILE)

    return (y_p.reshape(b_p, t_p, D_MODEL), y_s.reshape(b_s, t_s, D_MODEL),
            c_p.reshape(1, b_p, t_p, KV_LORA), kr_p.reshape(1, b_p, t_p, ROPE_DIM),
            ssm_p.reshape(1, b_p, SSD_HEADS, SSD_HEAD_DIM, D_STATE), conv_p[None],
            c_s.reshape(1, b_s, t_s, KV_LORA), kr_s.reshape(1, b_s, t_s, ROPE_DIM),
            ssm_s.reshape(1, b_s, SSD_HEADS, SSD_HEAD_DIM, D_STATE), conv_s[None])
```

```python
import functools

import jax
import jax.numpy as jnp
import numpy as np
from jax import lax
from jax.experimental import pallas as pl
from jax.experimental.pallas import tpu as pltpu

F32 = jnp.float32
BF16 = jnp.bfloat16

D_MODEL = 1024
D_INNER = 2048
SSD_HEAD_DIM = 64
SSD_HEADS = 32
SSD_GROUPS = 4
HEADS_PER_GROUP = 8
D_STATE = 128
D_CONV = 4
CONV_DIM = D_INNER + 2 * SSD_GROUPS * D_STATE
SSD_CHUNK = 128
MLA_HEADS = 8
Q_LORA = 256
KV_LORA = 256
QK_NOPE = 128
ROPE_DIM = 64
V_HEAD = 128
ROPE_THETA = 10000.0
ATTN_SCALE = (QK_NOPE + ROPE_DIM) ** -0.5
N_EXPERTS = 32
TOP_K = 4
D_FF = 1024
SWIGLU_ALPHA = 1.702
SWIGLU_LIMIT = 7.0
DEPTH = 1
DN_ALPHA = (2 * DEPTH) ** 0.25
LN_EPS = 1e-5
RMS_EPS = 1e-6
IN_SIZES = (Q_LORA, KV_LORA, ROPE_DIM, D_INNER, CONV_DIM, SSD_HEADS, D_MODEL, D_MODEL)
IN_OFFS = tuple(int(v) for v in np.cumsum((0,) + IN_SIZES))

LANES = 128
SUBLANES = 8
SLAB_PITCH = SUBLANES + 1
VMEM_LIMIT = 56 * 1024 * 1024

NEG = -1e30

ROW_TILE = 256
PROJ_TILE = 512
ATTN_TQ = 256
PAGE_GROUP = 64
SAMPLE_CHUNK = 32
MOE_TB = 256
COMBINE_TILE = 128
MLA_A_COLS = 896


def _cparams(n_axes):
    return pltpu.CompilerParams(dimension_semantics=("arbitrary",) * n_axes,
                                vmem_limit_bytes=VMEM_LIMIT)


def _dot(a, b):
    return jnp.dot(a, b, preferred_element_type=F32)


def _dot_nt(a, b):
    return lax.dot_general(a, b, (((1,), (1,)), ((), ())), preferred_element_type=F32)


def _split3(x):
    hi = x.astype(BF16)
    r1 = x - hi.astype(F32)
    mid = r1.astype(BF16)
    lo = (r1 - mid.astype(F32)).astype(BF16)
    return hi, mid, lo


def _sigmoid(x):
    return 1.0 / (1.0 + jnp.exp(-x))


def _layer_norm(x, g, b):
    mu = jnp.mean(x, axis=-1, keepdims=True)
    xc = x - mu
    var = jnp.mean(xc * xc, axis=-1, keepdims=True)
    return xc * lax.rsqrt(var + LN_EPS) * g + b


def _rms_norm(x, g):
    return x * lax.rsqrt(jnp.mean(x * x, axis=-1, keepdims=True) + RMS_EPS) * g


def _proj_kernel(x_ref, w_ref, o_ref):
    o_ref[...] = _dot(x_ref[...].astype(BF16), w_ref[...])


def _proj(x, w, tm):
    m, k = x.shape
    n = w.shape[1]
    return pl.pallas_call(
        _proj_kernel,
        out_shape=jax.ShapeDtypeStruct((m, n), F32),
        grid=(m // tm,),
        in_specs=[pl.BlockSpec((tm, k), lambda i: (i, 0)),
                  pl.BlockSpec((k, n), lambda i: (0, 0))],
        out_specs=pl.BlockSpec((tm, n), lambda i: (i, 0)),
        compiler_params=_cparams(1),
        name="proj",
    )(x, w)


def _mla_in_kernel(x_ref, wa_ref, qg_ref, kvg_ref, wq_ref, wuk_ref, cos_ref, sin_ref,
                   qlat_ref, qrope_ref, c_ref, cbf_ref, kr_ref, krbf_ref, dt_ref):
    xb = x_ref[...].astype(BF16)
    pa = _dot(xb, wa_ref[...])
    q_a = pa[:, 0:256]
    kv_a = pa[:, 256:512]
    kr_raw = pa[:, 512:640]
    kr_rot = pa[:, 640:768]
    dt_ref[...] = pa[:, 768:896]
    cos = cos_ref[...]
    sin = sin_ref[...]
    c = _rms_norm(kv_a, kvg_ref[...])
    c_ref[...] = c
    cbf_ref[...] = c.astype(BF16)
    kr = kr_raw * cos + kr_rot * sin
    kr_ref[...] = kr[:, 0:ROPE_DIM]
    krbf_ref[...] = kr[:, 0:ROPE_DIM].astype(BF16)
    qn = _rms_norm(q_a, qg_ref[...]).astype(BF16)
    qq = _dot(qn, wq_ref[...])
    cos4 = jnp.concatenate([cos] * 4, axis=1)
    sin4 = jnp.concatenate([sin] * 4, axis=1)
    qr = (qq[:, 1024:1536] * cos4 + qq[:, 1536:2048] * sin4) * ATTN_SCALE
    for h in range(MLA_HEADS):
        qrope_ref[h] = qr[:, h * ROPE_DIM:(h + 1) * ROPE_DIM].astype(BF16)
        nope = qq[:, h * QK_NOPE:(h + 1) * QK_NOPE].astype(BF16)
        qlat_ref[h] = (_dot(nope, wuk_ref[h]) * ATTN_SCALE).astype(BF16)


def _mla_in(x, wa, qg, kvg, wq, wuk, cos_t, sin_t, tm):
    m = x.shape[0]
    tab_blocks = cos_t.shape[0] // tm
    row = lambda w: pl.BlockSpec((tm, w), lambda i: (i, 0))
    tab = pl.BlockSpec((tm, LANES), lambda i: (i % tab_blocks, 0))
    full2 = lambda a: pl.BlockSpec(a.shape, lambda i: (0, 0))
    full3 = lambda a: pl.BlockSpec(a.shape, lambda i: (0, 0, 0))
    return pl.pallas_call(
        _mla_in_kernel,
        out_shape=(jax.ShapeDtypeStruct((MLA_HEADS, m, KV_LORA), BF16),
                   jax.ShapeDtypeStruct((MLA_HEADS, m, ROPE_DIM), BF16),
                   jax.ShapeDtypeStruct((m, KV_LORA), F32),
                   jax.ShapeDtypeStruct((m, KV_LORA), BF16),
                   jax.ShapeDtypeStruct((m, ROPE_DIM), F32),
                   jax.ShapeDtypeStruct((m, ROPE_DIM), BF16),
                   jax.ShapeDtypeStruct((m, LANES), F32)),
        grid=(m // tm,),
        in_specs=[row(D_MODEL), full2(wa), full2(qg), full2(kvg), full2(wq), full3(wuk), tab, tab],
        out_specs=(pl.BlockSpec((MLA_HEADS, tm, KV_LORA), lambda i: (0, i, 0)),
                   pl.BlockSpec((MLA_HEADS, tm, ROPE_DIM), lambda i: (0, i, 0)),
                   row(KV_LORA), row(KV_LORA), row(ROPE_DIM), row(ROPE_DIM), row(LANES)),
        compiler_params=_cparams(1),
        name="mla_in",
    )(x, wa, qg, kvg, wq, wuk, cos_t, sin_t)


def _expand_heads(v, e_ref, terms):
    out = None
    rem = v
    for _ in range(terms):
        part = rem.astype(BF16)
        rem = rem - part.astype(F32)
        d = _dot(part, e_ref[...])
        out = d if out is None else out + d
    return out


def _conv_silu(ext, r0, rows, cw_ref, cb_ref, out, out_r0):
    for j in range(CONV_DIM // LANES):
        sl = slice(j * LANES, (j + 1) * LANES)
        acc = cb_ref[:, sl]
        for k in range(D_CONV):
            acc = acc + ext[pl.ds(r0 + SUBLANES - (D_CONV - 1) + k, rows), sl] * cw_ref[k:k + 1, sl]
        out[out_r0:out_r0 + rows, sl] = acc * _sigmoid(acc)


def _ssd_kernel(*refs, t_in, L, has_h0):
    n_pairs = SSD_HEADS // 2
    refs = list(refs)
    hT = refs[-n_pairs:]
    it = iter(refs[:-n_pairs])
    xbc_ref, z_ref, dt_ref, conv0_ref = next(it), next(it), next(it), next(it)
    h0_ref = next(it) if has_h0 else None
    (cw_ref, cb_ref, dtb_ref, a_ref, dx_ref, ng_ref, e_ref, y_ref, hfin_ref, ctail_ref,
     ext, xcv, ybuf, acs_s, acsT_s, dtT_s, eax_s, wtx_s) = it
    c = pl.program_id(1)
    nc = pl.num_programs(1)

    @pl.when(c == 0)
    def _():
        ext[0:SUBLANES, :] = conv0_ref[0]
        if t_in < L:
            ext[SUBLANES:SUBLANES + L, :] = jnp.zeros((L, CONV_DIM), F32)
        for j in range(n_pairs):
            if has_h0:
                hT[j][...] = h0_ref[0, j * LANES:(j + 1) * LANES, :].T
            else:
                hT[j][...] = jnp.zeros((D_STATE, LANES), F32)

    ext[SUBLANES:SUBLANES + t_in, :] = xbc_ref[...]

    @pl.when(c == nc - 1)
    def _():
        ctail_ref[0] = ext[t_in:t_in + SUBLANES, :]

    _conv_silu(ext, 0, L, cw_ref, cb_ref, xcv, 0)
    ext[0:SUBLANES, :] = ext[L:L + SUBLANES, :]

    row = lax.broadcasted_iota(jnp.int32, (L, L), 0)
    col = lax.broadcasted_iota(jnp.int32, (L, L), 1)
    tri = row >= col
    tri_b = jnp.where(tri, 1.0, 0.0).astype(BF16)
    if t_in < L:
        dt_raw = jnp.concatenate([dt_ref[...], jnp.zeros((L - t_in, LANES), F32)], axis=0)
    else:
        dt_raw = dt_ref[...]
    xs = dt_raw + dtb_ref[...]
    dtv = jnp.maximum(xs, 0.0) + jnp.log1p(jnp.exp(-jnp.abs(xs)))
    if t_in < L:
        dtv = jnp.where(lax.broadcasted_iota(jnp.int32, (L, LANES), 0) < t_in, dtv, 0.0)
    da = dtv * a_ref[...]
    hi, mid, lo = _split3(da)
    acs = _dot(tri_b, hi) + _dot(tri_b, mid) + _dot(tri_b, lo)
    acs_s[...] = acs
    acsT_s[...] = acs.T
    dtT_s[...] = dtv.T
    eax_s[...] = _expand_heads(jnp.exp(acs), e_ref, 3)
    wtx_s[...] = _expand_heads(jnp.exp(acs[L - 1:L, :] - acs) * dtv, e_ref, 2)

    low_half = lax.broadcasted_iota(jnp.int32, (L, LANES), 1) < SSD_HEAD_DIM
    pairs_per_group = HEADS_PER_GROUP // 2
    for g in range(SSD_GROUPS):
        bo = D_INNER + g * D_STATE
        co = D_INNER + SSD_GROUPS * D_STATE + g * D_STATE
        bm = xcv[:, bo:bo + D_STATE]
        cmb = xcv[:, co:co + D_STATE].astype(BF16)
        cb = _dot_nt(cmb, bm.astype(BF16))
        bmT = bm.T.astype(BF16)
        for k in range(pairs_per_group):
            kk = g * pairs_per_group + k
            tl = slice(kk * LANES, (kk + 1) * LANES)
            x2 = xcv[:, tl]
            ws = []
            for idx in (2 * kk, 2 * kk + 1):
                seg = acs_s[:, idx:idx + 1] - acsT_s[idx:idx + 1, :]
                w = cb * jnp.exp(jnp.where(tri, seg, NEG)) * dtT_s[idx:idx + 1, :]
                ws.append(w.astype(BF16))
            w2 = jnp.concatenate(ws, axis=1)
            xbd = jnp.concatenate([jnp.where(low_half, x2, 0.0).astype(BF16),
                                   jnp.where(low_half, 0.0, x2).astype(BF16)], axis=0)
            yd = _dot(w2, xbd)
            hp = hT[kk][...]
            yo = _dot(cmb, hp.astype(BF16))
            ybuf[:, tl] = yd + yo * eax_s[:, tl] + dx_ref[:, tl] * x2
            s = _dot(bmT, (x2 * wtx_s[:, tl]).astype(BF16))
            hT[kk][...] = eax_s[L - 1:L, tl] * hp + s

    gw = D_INNER // SSD_GROUPS
    for g in range(SSD_GROUPS):
        sl = slice(g * gw, (g + 1) * gw)
        zg = z_ref[:, sl]
        v = ybuf[0:t_in, sl] * (zg * _sigmoid(zg))
        ms = jnp.mean(v * v, axis=-1, keepdims=True)
        y_ref[:, sl] = (v * lax.rsqrt(ms + RMS_EPS) * ng_ref[:, sl]).astype(y_ref.dtype)

    @pl.when(c == nc - 1)
    def _():
        for j in range(n_pairs):
            hfin_ref[0, j * LANES:(j + 1) * LANES, :] = hT[j][...].T


def _ssd(xbc, z, dt, conv0, h0, cw, cb, dtb, a_neg, d_x, ng, e_mat, *, nb, nc, t_in, chunk, y_dtype):
    L = chunk
    has_h0 = h0 is not None
    rowspec = lambda w: pl.BlockSpec((t_in, w), lambda b, c: (b * nc + c, 0))
    par = lambda a: pl.BlockSpec(a.shape, lambda b, c: (0, 0))
    per_seq = lambda r, w: pl.BlockSpec((1, r, w), lambda b, c: (b, 0, 0))
    in_specs = [rowspec(CONV_DIM), rowspec(D_INNER), rowspec(LANES), per_seq(SUBLANES, CONV_DIM)]
    args = [xbc, z, dt, conv0]
    if has_h0:
        in_specs.append(per_seq(D_INNER, D_STATE))
        args.append(h0)
    params = [cw, cb, dtb, a_neg, d_x, ng, e_mat]
    in_specs += [par(p) for p in params]
    args += params
    return pl.pallas_call(
        functools.partial(_ssd_kernel, t_in=t_in, L=L, has_h0=has_h0),
        out_shape=(jax.ShapeDtypeStruct((nb * nc * t_in, D_INNER), y_dtype),
                   jax.ShapeDtypeStruct((nb, D_INNER, D_STATE), F32),
                   jax.ShapeDtypeStruct((nb, SUBLANES, CONV_DIM), F32)),
        grid=(nb, nc),
        in_specs=in_specs,
        out_specs=(rowspec(D_INNER), per_seq(D_INNER, D_STATE), per_seq(SUBLANES, CONV_DIM)),
        scratch_shapes=[pltpu.VMEM((L + 2 * SUBLANES, CONV_DIM), F32),
                        pltpu.VMEM((L, CONV_DIM), F32),
                        pltpu.VMEM((L, D_INNER), F32),
                        pltpu.VMEM((L, LANES), F32), pltpu.VMEM((LANES, L), F32), pltpu.VMEM((LANES, L), F32),
                        pltpu.VMEM((L, D_INNER), F32), pltpu.VMEM((L, D_INNER), F32)]
                       + [pltpu.VMEM((D_STATE, LANES), F32)] * (SSD_HEADS // 2),
        compiler_params=_cparams(2),
        name="ssd",
    )(*args)


def _softmax_step(s, kc, m_scr, l_scr, acc_scr):
    reps = s.shape[1] // LANES
    m_prev = m_scr[...]
    m_new = jnp.maximum(m_prev, jnp.max(s, axis=1, keepdims=True))
    alpha = jnp.exp(m_prev - m_new)
    p = jnp.exp(s - (jnp.concatenate([m_new] * reps, axis=1) if reps > 1 else m_new))
    l_scr[...] = alpha * l_scr[...] + jnp.sum(p, axis=1, keepdims=True)
    acc_scr[...] = acc_scr[...] * jnp.concatenate([alpha] * (KV_LORA // LANES), axis=1) + _dot(p.astype(BF16), kc)
    m_scr[...] = m_new


def _attn_prompt_kernel(qlat_ref, qrope_ref, c_ref, kr_ref, wuv_ref, o_ref, m_scr, l_scr, acc_scr, *, tq):
    qi = pl.program_id(1)
    R = MLA_HEADS * tq

    def scores(block, width):
        start = pl.multiple_of(block * width, width)
        kc = c_ref[pl.ds(start, width), :]
        kr = kr_ref[pl.ds(start, width), :]
        q = qlat_ref[...].reshape(R, KV_LORA)
        qr = qrope_ref[...].reshape(R, ROPE_DIM)
        return _dot_nt(q, kc) + _dot_nt(qr, kr), kc

    s, kc = scores(qi, tq)
    t = lax.broadcasted_iota(jnp.int32, (R, tq), 0) & (tq - 1)
    kcol = lax.broadcasted_iota(jnp.int32, (R, tq), 1)
    s = jnp.where(kcol <= t, s, NEG)
    m0 = jnp.max(s, axis=1, keepdims=True)
    p = jnp.exp(s - m0)
    m_scr[...] = jnp.broadcast_to(m0, m_scr.shape)
    l_scr[...] = jnp.broadcast_to(jnp.sum(p, axis=1, keepdims=True), l_scr.shape)
    acc_scr[...] = _dot(p.astype(BF16), kc)

    def body(j, carry):
        s2, kc2 = scores(j, 2 * tq)
        _softmax_step(s2, kc2, m_scr, l_scr, acc_scr)
        return carry

    lax.fori_loop(0, qi // 2, body, 0)

    @pl.when(qi % 2 == 1)
    def _():
        s1, kc1 = scores(qi - 1, tq)
        _softmax_step(s1, kc1, m_scr, l_scr, acc_scr)

    inv = 1.0 / l_scr[...]
    o_lat = acc_scr[...] * jnp.concatenate([inv] * (KV_LORA // LANES), axis=1)
    for h in range(MLA_HEADS):
        oh = _dot(o_lat[h * tq:(h + 1) * tq, :].astype(BF16), wuv_ref[h])
        o_ref[:, h * V_HEAD:(h + 1) * V_HEAD] = oh.astype(o_ref.dtype)


def _attn_prompt(qlat, qrope, cbf, krbf, wuv, *, nb, t, tq):
    nq = t // tq
    R = MLA_HEADS * tq
    return pl.pallas_call(
        functools.partial(_attn_prompt_kernel, tq=tq),
        out_shape=jax.ShapeDtypeStruct((nb * t, MLA_HEADS * V_HEAD), BF16),
        grid=(nb, nq),
        in_specs=[pl.BlockSpec((MLA_HEADS, tq, KV_LORA), lambda b, i: (0, b * nq + i, 0)),
                  pl.BlockSpec((MLA_HEADS, tq, ROPE_DIM), lambda b, i: (0, b * nq + i, 0)),
                  pl.BlockSpec((t, KV_LORA), lambda b, i: (b, 0)),
                  pl.BlockSpec((t, ROPE_DIM), lambda b, i: (b, 0)),
                  pl.BlockSpec(wuv.shape, lambda b, i: (0, 0, 0))],
        out_specs=pl.BlockSpec((tq, MLA_HEADS * V_HEAD), lambda b, i: (b * nq + i, 0)),
        scratch_shapes=[pltpu.VMEM((R, LANES), F32), pltpu.VMEM((R, LANES), F32),
                        pltpu.VMEM((R, KV_LORA), F32)],
        compiler_params=_cparams(2),
        name="attn_prompt",
    )(qlat, qrope, cbf, krbf, wuv)


def _attn_sample_kernel(pt_ref, qlat_ref, qrope_ref, cnew_ref, krnew_ref, kv_hbm, kr_hbm, o_ref,
                        kvbuf, krbuf, kv_sem, kr_sem, m_scr, l_scr, acc_scr, *, n_groups, gp, t_new):
    b = pl.program_id(0)
    nb = pl.num_programs(0)
    R = qlat_ref.shape[1]
    page = kvbuf.shape[2]
    q = qlat_ref[0]
    qr = qrope_ref[0]

    def start_group(bb, g, slot):
        for p in range(gp):
            pid = pt_ref[bb, g * gp + p]
            pltpu.make_async_copy(kv_hbm.at[pid], kvbuf.at[slot, p], kv_sem.at[slot]).start()
            pltpu.make_async_copy(kr_hbm.at[pid], krbuf.at[slot, p], kr_sem.at[slot]).start()

    def wait_group(slot):
        pltpu.make_async_copy(kvbuf.at[slot], kvbuf.at[slot], kv_sem.at[slot]).wait()
        pltpu.make_async_copy(krbuf.at[slot], krbuf.at[slot], kr_sem.at[slot]).wait()

    @pl.when(b == 0)
    def _():
        start_group(0, 0, 0)

    kc = cnew_ref[0].astype(BF16)
    kr = krnew_ref[0].astype(BF16)
    s = _dot_nt(q, kc) + _dot(qr, kr)
    t = lax.broadcasted_iota(jnp.int32, (R, LANES), 0) & (t_new - 1)
    kcol = lax.broadcasted_iota(jnp.int32, (R, LANES), 1)
    s = jnp.where(kcol <= t, s, NEG)
    m0 = jnp.max(s, axis=1, keepdims=True)
    p0 = jnp.exp(s - m0)
    m_scr[...] = jnp.broadcast_to(m0, m_scr.shape)
    l_scr[...] = jnp.broadcast_to(jnp.sum(p0, axis=1, keepdims=True), l_scr.shape)
    acc_scr[...] = _dot(p0.astype(BF16), kc)

    for g in range(n_groups):
        slot = g % 2
        if g + 1 < n_groups:
            start_group(b, g + 1, 1 - slot)
        else:
            @pl.when(b + 1 < nb)
            def _():
                start_group(b + 1, 0, 1 - slot)
        wait_group(slot)
        kc = kvbuf[slot].reshape(gp * page, KV_LORA).astype(BF16)
        kr = jnp.concatenate([krbuf[slot, p].astype(BF16) for p in range(gp)], axis=1)
        s = _dot_nt(q, kc) + _dot(qr, kr)
        _softmax_step(s, kc, m_scr, l_scr, acc_scr)

    inv = 1.0 / l_scr[...]
    o_ref[0] = acc_scr[...] * jnp.concatenate([inv] * (KV_LORA // LANES), axis=1)


def _attn_sample(page_table, qlat, qrope, cnew, krnew, cache_kv, cache_kr, *, t_new, gp):
    nb, n_pages = page_table.shape
    n_groups = n_pages // gp
    R = qlat.shape[1]
    page = cache_kv.shape[1]
    per_b = lambda r, w: pl.BlockSpec((1, r, w), lambda b, pt: (b, 0, 0))
    grid_spec = pltpu.PrefetchScalarGridSpec(
        num_scalar_prefetch=1,
        grid=(nb,),
        in_specs=[per_b(R, KV_LORA), per_b(R, ROPE_DIM), per_b(LANES, KV_LORA), per_b(ROPE_DIM, LANES),
                  pl.BlockSpec(memory_space=pl.ANY), pl.BlockSpec(memory_space=pl.ANY)],
        out_specs=per_b(R, KV_LORA),
        scratch_shapes=[pltpu.VMEM((2, gp, page, KV_LORA), F32), pltpu.VMEM((2, gp, ROPE_DIM, page), F32),
                        pltpu.SemaphoreType.DMA((2,)), pltpu.SemaphoreType.DMA((2,)),
                        pltpu.VMEM((R, LANES), F32), pltpu.VMEM((R, LANES), F32),
                        pltpu.VMEM((R, KV_LORA), F32)],
    )
    return pl.pallas_call(
        functools.partial(_attn_sample_kernel, n_groups=n_groups, gp=gp, t_new=t_new),
        out_shape=jax.ShapeDtypeStruct((nb, R, KV_LORA), F32),
        grid_spec=grid_spec,
        compiler_params=_cparams(1),
        name="attn_sample",
    )(page_table, qlat, qrope, cnew, krnew, cache_kv, cache_kr)


def _head_proj_kernel(x_ref, w_ref, o_ref):
    o_ref[...] = _dot(x_ref[0].astype(BF16), w_ref[0]).astype(o_ref.dtype)


def _head_proj(x, w):
    h, m, k = x.shape
    n = w.shape[2]
    return pl.pallas_call(
        _head_proj_kernel,
        out_shape=jax.ShapeDtypeStruct((m, h * n), BF16),
        grid=(h,),
        in_specs=[pl.BlockSpec((1, m, k), lambda i: (i, 0, 0)),
                  pl.BlockSpec((1, k, n), lambda i: (i, 0, 0))],
        out_specs=pl.BlockSpec((m, n), lambda i: (0, i)),
        compiler_params=_cparams(1),
        name="head_proj",
    )(x, w)


def _merge_kernel(ya_p_ref, ya_s_ref, o_p_ref, o_s_ref, g_p_ref, g_s_ref, u_p_ref, u_s_ref, wa_ref, wb_ref, wo_ref,
                  lg_ref, lb_ref, rwh_ref, rwl_ref, rb_ref,
                  h_ref, h3_ref, ti_ref, tg_ref, rank_ref, cnt_ref, run_ref, *, n_prompt_tiles):
    i = pl.program_id(0)
    is_p = i < n_prompt_tiles
    yn = jnp.where(is_p, ya_p_ref[...], ya_s_ref[...])
    ob = jnp.where(is_p, o_p_ref[...], o_s_ref[...])
    g = jnp.where(is_p, g_p_ref[...], g_s_ref[...])
    u = jnp.where(is_p, u_p_ref[...], u_s_ref[...])
    ya = _dot(yn, wa_ref[...])
    yb = _dot(ob, wb_ref[...])
    merged = _sigmoid(g[:, 0:D_MODEL]) * ya + _sigmoid(g[:, D_MODEL:2 * D_MODEL]) * yb
    mix = _dot(merged.astype(BF16), wo_ref[...])
    h = _layer_norm(DN_ALPHA * u + mix, lg_ref[...], lb_ref[...])
    h_ref[...] = h
    tm = h.shape[0]
    for j in range(D_MODEL // LANES):
        h3_ref[pl.ds(j, tm, stride=SLAB_PITCH), :] = h[:, j * LANES:(j + 1) * LANES]
    h3_ref[pl.ds(SUBLANES, tm, stride=SLAB_PITCH), :] = jnp.zeros((tm, LANES), F32)
    hh = h.astype(BF16)
    hl = (h - hh.astype(F32)).astype(BF16)
    logits = _dot(hh, rwh_ref[...]) + _dot(hl, rwh_ref[...]) + _dot(hh, rwl_ref[...]) + rb_ref[...]
    lane = lax.broadcasted_iota(jnp.int32, logits.shape, 1).astype(F32)
    vals = jnp.zeros_like(logits)
    idxs = jnp.zeros_like(logits)
    cur = logits
    picked = []
    for k in range(TOP_K):
        mx = jnp.max(cur, axis=1, keepdims=True)
        ix = jnp.min(jnp.where(cur == mx, lane, float(LANES)), axis=1, keepdims=True)
        vals = jnp.where(lane == k, mx, vals)
        idxs = jnp.where(lane == k, ix, idxs)
        picked.append(lane == ix)
        cur = jnp.where(picked[k], NEG * 2.0, cur)
    v0 = vals[:, 0:1]
    e = jnp.where(lane < TOP_K, jnp.exp(vals - v0), 0.0)
    tg_ref[...] = e / jnp.sum(e, axis=1, keepdims=True)
    ti_ref[...] = idxs.astype(jnp.int32)

    @pl.when(i == 0)
    def _():
        run_ref[...] = jnp.zeros_like(run_ref)

    cnt = jnp.zeros_like(logits)
    for k in range(TOP_K):
        cnt = cnt + jnp.where(picked[k], 1.0, 0.0)
    r_i = lax.broadcasted_iota(jnp.int32, (tm, tm), 0)
    c_i = lax.broadcasted_iota(jnp.int32, (tm, tm), 1)
    before = _dot(jnp.where(r_i > c_i, 1.0, 0.0).astype(BF16), cnt.astype(BF16)) + run_ref[0:1, :]
    ranks = jnp.zeros_like(logits)
    for k in range(TOP_K):
        rk = jnp.sum(jnp.where(picked[k], before, 0.0), axis=1, keepdims=True)
        ranks = jnp.where(lane == k, rk, ranks)
    rank_ref[...] = ranks.astype(jnp.int32)
    run_ref[0:1, :] = run_ref[0:1, :] + jnp.sum(cnt, axis=0, keepdims=True)
    cnt_ref[...] = run_ref[...].astype(jnp.int32)


def _merge(ya_p, ya_s, o_p, o_s, g_p, g_s, u_p, u_s, wa, wb, wo, lg, lb, rwh, rwl, rb, tm):
    m = u_p.shape[0] + u_s.shape[0]
    npt = ya_p.shape[0] // tm
    row = lambda w: pl.BlockSpec((tm, w), lambda i: (i, 0))
    prow = lambda w: pl.BlockSpec((tm, w), lambda i: (jnp.minimum(i, npt - 1), 0))
    srow = lambda w: pl.BlockSpec((tm, w), lambda i: (jnp.maximum(i - npt, 0), 0))
    par = lambda a: pl.BlockSpec(a.shape, lambda i: (0, 0))
    return pl.pallas_call(
        functools.partial(_merge_kernel, n_prompt_tiles=npt),
        out_shape=(jax.ShapeDtypeStruct((m, D_MODEL), F32),
                   jax.ShapeDtypeStruct((m * SLAB_PITCH, LANES), F32),
                   jax.ShapeDtypeStruct((m, LANES), jnp.int32),
                   jax.ShapeDtypeStruct((m, LANES), F32),
                   jax.ShapeDtypeStruct((m, LANES), jnp.int32),
                   jax.ShapeDtypeStruct((SUBLANES, LANES), jnp.int32)),
        grid=(m // tm,),
        in_specs=[prow(D_INNER), srow(D_INNER), prow(D_MODEL), srow(D_MODEL), prow(2 * D_MODEL), srow(2 * D_MODEL),
                  prow(D_MODEL), srow(D_MODEL),
                  par(wa), par(wb), par(wo), par(lg), par(lb), par(rwh), par(rwl), par(rb)],
        out_specs=(row(D_MODEL), pl.BlockSpec((tm * SLAB_PITCH, LANES), lambda i: (i, 0)), row(LANES), row(LANES),
                   row(LANES), pl.BlockSpec((SUBLANES, LANES), lambda i: (0, 0))),
        scratch_shapes=[pltpu.VMEM((SUBLANES, LANES), F32)],
        compiler_params=_cparams(1),
        name="merge",
    )(ya_p, ya_s, o_p, o_s, g_p, g_s, u_p, u_s, wa, wb, wo, lg, lb, rwh, rwl, rb)


DEINT_IN = 2 * LANES


def _deint_perm():
    perm = np.zeros((DEINT_IN, DEINT_IN), np.float32)
    perm[2 * np.arange(LANES), np.arange(LANES)] = 1.0
    perm[2 * np.arange(LANES) + 1, LANES + np.arange(LANES)] = 1.0
    return jnp.asarray(perm, BF16)


def _row_pipeline(i, n_steps, idx_hbm, src_hbm, idx_smem, idx_sem, buf, row_sem, rows, compute):
    def idx_copy(chunk, s):
        return pltpu.make_async_copy(idx_hbm.at[chunk], idx_smem.at[s], idx_sem.at[s])

    def issue_row(s, r):
        pltpu.make_async_copy(src_hbm.at[pl.ds(idx_smem[s, r] * SLAB_PITCH, SLAB_PITCH), :],
                              buf.at[s, pl.ds(r * SLAB_PITCH, SLAB_PITCH), :], row_sem.at[s]).start()

    @pl.when(i == 0)
    def _():
        idx_copy(0, 0).start()
        idx_copy(0, 0).wait()

        def body(r, carry):
            issue_row(0, r)
            return carry
        lax.fori_loop(0, rows, body, 0, unroll=8)
        if n_steps > 1:
            idx_copy(1, 1).start()

    def step(slot):
        has_next = i + 1 < n_steps

        @pl.when(has_next)
        def _():
            idx_copy(i + 1, 1 - slot).wait()

        @pl.when(i + 2 < n_steps)
        def _():
            idx_copy(i + 2, slot).start()

        pltpu.make_async_copy(src_hbm.at[pl.ds(0, rows * SLAB_PITCH), :], buf.at[slot], row_sem.at[slot]).wait()

        @pl.when(has_next)
        def _():
            for r in range(rows):
                issue_row(1 - slot, r)
            compute(slot)

        @pl.when(i + 1 == n_steps)
        def _():
            compute(slot)

    step(lax.rem(i, 2))


def _fetch_scratch(rows):
    return [pltpu.SMEM((2, rows), jnp.int32), pltpu.SemaphoreType.DMA((2,)),
            pltpu.VMEM((2, rows * SLAB_PITCH, LANES), F32), pltpu.SemaphoreType.DMA((2,))]


def _dispatch_kernel(pad_ref, dest_hbm, h3_ref, xr_hbm, idx_smem, idx_sem, stage, row_sem, zbuf, z_sem,
                     *, tm, tb, min_used):
    i = pl.program_id(0)
    n = pl.num_programs(0)
    rows = tm * TOP_K

    def idx_copy(chunk, s):
        return pltpu.make_async_copy(dest_hbm.at[chunk], idx_smem.at[s], idx_sem.at[s])

    def rows_wait(s):
        span = xr_hbm.at[pl.ds(0, rows * SLAB_PITCH), :]
        pltpu.make_async_copy(span, span, row_sem.at[s]).wait()

    @pl.when(i == 0)
    def _():
        zbuf[...] = jnp.zeros_like(zbuf)
        zero_to = lambda start: pltpu.make_async_copy(zbuf, xr_hbm.at[pl.ds(start, tb * SLAB_PITCH), :], z_sem)
        for e in range(N_EXPERTS):
            zero_to(pad_ref[e] * SLAB_PITCH).start()
        for e in range(N_EXPERTS):
            zero_to(0).wait()
        n_used = pad_ref[N_EXPERTS]
        tail = range(min_used, xr_hbm.shape[0] // (tb * SLAB_PITCH))
        for blk in tail:
            pl.when(blk >= n_used)(zero_to(blk * tb * SLAB_PITCH).start)
        for blk in tail:
            pl.when(blk >= n_used)(zero_to(0).wait)
        idx_copy(0, 0).start()

    def step(slot):
        idx_copy(i, slot).wait()

        @pl.when(i + 1 < n)
        def _():
            idx_copy(i + 1, 1 - slot).start()

        stage[slot] = h3_ref[...]
        for r in range(rows):
            pltpu.make_async_copy(stage.at[slot, pl.ds((r // TOP_K) * SLAB_PITCH, SLAB_PITCH), :],
                                  xr_hbm.at[pl.ds(idx_smem[slot, r] * SLAB_PITCH, SLAB_PITCH), :],
                                  row_sem.at[slot]).start()

        @pl.when(i >= 1)
        def _():
            rows_wait(1 - slot)

        @pl.when(i == n - 1)
        def _():
            rows_wait(slot)

    step(lax.rem(i, 2))


def _dispatch(pad_start, dest, h3, cap_rows, tm, tb):
    n = h3.shape[0] // SLAB_PITCH
    grid_spec = pltpu.PrefetchScalarGridSpec(
        num_scalar_prefetch=1,
        grid=(n // tm,),
        in_specs=[pl.BlockSpec(memory_space=pl.ANY),
                  pl.BlockSpec((tm * SLAB_PITCH, LANES), lambda i, pad: (i, 0))],
        out_specs=pl.BlockSpec(memory_space=pl.ANY),
        scratch_shapes=[pltpu.SMEM((2, tm * TOP_K), jnp.int32), pltpu.SemaphoreType.DMA((2,)),
                        pltpu.VMEM((2, tm * SLAB_PITCH, LANES), F32), pltpu.SemaphoreType.DMA((2,)),
                        pltpu.VMEM((tb * SLAB_PITCH, LANES), F32), pltpu.SemaphoreType.DMA(())],
    )
    return pl.pallas_call(
        functools.partial(_dispatch_kernel, tm=tm, tb=tb, min_used=-(-(n * TOP_K) // tb)),
        out_shape=jax.ShapeDtypeStruct((cap_rows * SLAB_PITCH, LANES), F32),
        grid_spec=grid_spec,
        compiler_params=_cparams(1),
        name="dispatch",
    )(pad_start, dest.reshape(n // tm, tm * TOP_K), h3)


def _ffn_kernel(be_ref, nu_ref, x_ref, wgu_ref, bgu_ref, wdn_ref, bdn_ref, perm_ref, o_ref,
                wg_s, wu_s, wd_s, *, tb):
    i = pl.program_id(0)
    nu = nu_ref[0]
    new_expert = jnp.logical_or(i == 0, be_ref[i] != be_ref[jnp.maximum(i - 1, 0)])

    @pl.when(jnp.logical_and(new_expert, i < nu))
    def _():
        for t in range(2 * D_FF // DEINT_IN):
            r = _dot(wgu_ref[:, t * DEINT_IN:(t + 1) * DEINT_IN].astype(BF16), perm_ref[...])
            wg_s[:, t * LANES:(t + 1) * LANES] = r[:, 0:LANES].astype(BF16)
            wu_s[:, t * LANES:(t + 1) * LANES] = r[:, LANES:DEINT_IN].astype(BF16)
        wd_s[...] = wdn_ref[...].astype(BF16)

    @pl.when(i < nu)
    def _():
        x = jnp.concatenate([x_ref[pl.ds(j, tb, stride=SLAB_PITCH), :].astype(BF16)
                             for j in range(D_MODEL // LANES)], axis=1)
        bgu = bgu_ref[...]
        g = jnp.minimum(_dot(x, wg_s[...]) + bgu[:, 0:D_FF], SWIGLU_LIMIT)
        u = jnp.clip(_dot(x, wu_s[...]) + bgu[:, D_FF:2 * D_FF], -SWIGLU_LIMIT, SWIGLU_LIMIT)
        hh = (u + 1.0) * (g * _sigmoid(SWIGLU_ALPHA * g))
        out = _dot(hh.astype(BF16), wd_s[...]) + bdn_ref[...]
        for j in range(D_MODEL // LANES):
            o_ref[pl.ds(j, tb, stride=SLAB_PITCH), :] = out[:, j * LANES:(j + 1) * LANES]
        o_ref[pl.ds(SUBLANES, tb, stride=SLAB_PITCH), :] = jnp.zeros((tb, LANES), F32)

    @pl.when(i >= nu)
    def _():
        o_ref[...] = jnp.zeros_like(o_ref)


def _ffn(blk_e, n_used, xr3, nblk, wgu, bgu, wdn, bdn, tb):
    perm = _deint_perm()
    wspec = lambda a: pl.BlockSpec((None,) + a.shape[1:], lambda i, be, nu: (be[i], 0, 0))
    grid_spec = pltpu.PrefetchScalarGridSpec(
        num_scalar_prefetch=2,
        grid=(nblk,),
        in_specs=[pl.BlockSpec((tb * SLAB_PITCH, LANES), lambda i, be, nu: (jnp.minimum(i, nu[0] - 1), 0)),
                  wspec(wgu), wspec(bgu), wspec(wdn), wspec(bdn),
                  pl.BlockSpec(perm.shape, lambda i, be, nu: (0, 0))],
        out_specs=pl.BlockSpec((tb * SLAB_PITCH, LANES), lambda i, be, nu: (i, 0)),
        scratch_shapes=[pltpu.VMEM((D_MODEL, D_FF), BF16), pltpu.VMEM((D_MODEL, D_FF), BF16),
                        pltpu.VMEM((D_FF, D_MODEL), BF16)],
    )
    return pl.pallas_call(
        functools.partial(_ffn_kernel, tb=tb),
        out_shape=jax.ShapeDtypeStruct((nblk * tb * SLAB_PITCH, LANES), F32),
        grid_spec=grid_spec,
        compiler_params=_cparams(1),
        name="moe_ffn",
    )(blk_e, n_used, xr3, wgu, bgu, wdn, bdn, perm)


def _combine_kernel(dest_hbm, yr3_hbm, tg_ref, h_ref, lg_ref, lb_ref, op_ref, os_ref,
                    idx_smem, idx_sem, ybuf, row_sem, *, tm, n_prompt_tiles, n_steps):
    i = pl.program_id(0)
    rows = tm * TOP_K

    def compute(slot):
        tg = tg_ref[...]
        stride = TOP_K * SLAB_PITCH
        pieces = []
        for j in range(D_MODEL // LANES):
            acc = tg[:, 0:1] * ybuf[slot, pl.ds(j, tm, stride=stride), :]
            for k in range(1, TOP_K):
                acc = acc + tg[:, k:k + 1] * ybuf[slot, pl.ds(k * SLAB_PITCH + j, tm, stride=stride), :]
            pieces.append(acc)
        y = jnp.concatenate(pieces, axis=1)
        out = _layer_norm(DN_ALPHA * h_ref[...] + y, lg_ref[...], lb_ref[...])

        @pl.when(i < n_prompt_tiles)
        def _():
            op_ref[...] = out

        @pl.when(i >= n_prompt_tiles)
        def _():
            os_ref[...] = out

    _row_pipeline(i, n_steps, dest_hbm, yr3_hbm, idx_smem, idx_sem, ybuf, row_sem, rows, compute)


def _combine(dest, yr3, tg, h, lg, lb, n_prompt, tm):
    m = h.shape[0]
    npt = n_prompt // tm
    row = lambda w: pl.BlockSpec((tm, w), lambda i: (i, 0))
    par = lambda a: pl.BlockSpec(a.shape, lambda i: (0, 0))
    return pl.pallas_call(
        functools.partial(_combine_kernel, tm=tm, n_prompt_tiles=npt, n_steps=m // tm),
        out_shape=(jax.ShapeDtypeStruct((n_prompt, D_MODEL), F32),
                   jax.ShapeDtypeStruct((m - n_prompt, D_MODEL), F32)),
        grid=(m // tm,),
        in_specs=[pl.BlockSpec(memory_space=pl.ANY), pl.BlockSpec(memory_space=pl.ANY),
                  row(LANES), row(D_MODEL), par(lg), par(lb)],
        out_specs=(pl.BlockSpec((tm, D_MODEL), lambda i: (jnp.minimum(i, npt - 1), 0)),
                   pl.BlockSpec((tm, D_MODEL), lambda i: (jnp.maximum(i - npt, 0), 0))),
        scratch_shapes=_fetch_scratch(tm * TOP_K),
        compiler_params=_cparams(1),
        name="combine",
    )(dest.reshape(m // tm, tm * TOP_K), yr3, tg, h, lg, lb)


def _rope_tables(pos):
    half = ROPE_DIM // 2
    inv = ROPE_THETA ** (-jnp.arange(half, dtype=F32) / half)
    ang = pos.astype(F32)[:, None] * inv[None, :]
    cos, sin = jnp.cos(ang), jnp.sin(ang)
    return jnp.concatenate([cos] * 4, axis=1), jnp.concatenate([-sin, sin] * 2, axis=1)


def _pad_cols(w, width):
    return jnp.pad(w, ((0, 0), (0, width - w.shape[1])))


def _routing(top_i, rank, counts, n_blocks, tb):
    flat_e = top_i.reshape(-1)
    nblk_e = (counts + tb - 1) // tb
    pend = jnp.cumsum(nblk_e)
    pstart = (pend - nblk_e) * tb
    onehot = flat_e[:, None] == jnp.arange(N_EXPERTS, dtype=jnp.int32)[None, :]
    dest = (jnp.sum(jnp.where(onehot, pstart[None, :], 0), axis=1) + rank.reshape(-1)).astype(jnp.int32)
    n_used = pend[-1].astype(jnp.int32)
    pad_start = jnp.concatenate([(pstart + counts).astype(jnp.int32), n_used.reshape(1)])
    bidx = jnp.minimum(jnp.arange(n_blocks, dtype=jnp.int32), n_used - 1)
    blk_e = jnp.minimum(jnp.sum((pend[None, :] <= bidx[:, None]).astype(jnp.int32), axis=1), N_EXPERTS - 1)
    return dest, pad_start, blk_e, n_used.reshape(1)


def kernel(x_prompt, x_sample, cache_kv_latent, cache_k_rope, state_ssm, state_conv, page_table, w_in, ssd_conv_w, ssd_conv_b, ssd_dt_bias, ssd_a_log, ssd_d, ssd_norm_g, mla_q_norm_g, w_q_b, mla_kv_norm_g, w_kv_b, w_branch_a, w_branch_b, w_out, ln1_g, ln1_b, router_w, router_b, exp_w_gu, exp_b_gu, exp_w_down, exp_b_down, ln2_g, ln2_b):
    b_p, t_p, _ = x_prompt.shape
    b_s, t_s, _ = x_sample.shape
    n_pages = page_table.shape[1]
    page = cache_kv_latent.shape[2]
    past_len = n_pages * page
    n_p = b_p * t_p
    n_s = b_s * t_s
    n = n_p + n_s
    assert DEPTH == 1 and w_in.shape[0] == 1
    assert n_p % PROJ_TILE == 0 and t_p % ROW_TILE == 0 and n_s % ROW_TILE == 0
    assert t_p % (2 * ATTN_TQ) == 0 and t_p % SSD_CHUNK == 0
    assert t_s == SUBLANES and n_pages % (2 * PAGE_GROUP) == 0 and page == LANES

    wi = w_in[0]
    seg = lambda k: wi[:, IN_OFFS[k]:IN_OFFS[k + 1]]
    half = ROPE_DIM // 2
    w_kr = seg(2)
    w_kr_rot = jnp.concatenate([w_kr[:, half:], w_kr[:, :half]], axis=1)
    w_mla = jnp.concatenate([seg(0), seg(1), _pad_cols(w_kr, LANES), _pad_cols(w_kr_rot, LANES),
                             _pad_cols(seg(5), LANES)], axis=1).astype(BF16)
    w_z = seg(3).astype(BF16)
    w_xbc = seg(4).astype(BF16)
    w_gates = jnp.concatenate([seg(6), seg(7)], axis=1).astype(BF16)
    wq3 = w_q_b[0].reshape(Q_LORA, MLA_HEADS, QK_NOPE + ROPE_DIM)
    wq_nope = wq3[:, :, :QK_NOPE].reshape(Q_LORA, MLA_HEADS * QK_NOPE)
    wq_rope = wq3[:, :, QK_NOPE:]
    wq_rope_rot = jnp.concatenate([wq_rope[..., half:], wq_rope[..., :half]], axis=-1)
    wq = jnp.concatenate([wq_nope, wq_rope.reshape(Q_LORA, -1), wq_rope_rot.reshape(Q_LORA, -1)],
                         axis=1).astype(BF16)
    wkv3 = w_kv_b[0].reshape(KV_LORA, MLA_HEADS, QK_NOPE + V_HEAD)
    wuk = jnp.transpose(wkv3[:, :, :QK_NOPE], (1, 2, 0)).astype(BF16)
    wuv = jnp.transpose(wkv3[:, :, QK_NOPE:], (1, 0, 2)).astype(BF16)
    qg = mla_q_norm_g[0].reshape(1, Q_LORA)
    kvg = mla_kv_norm_g[0].reshape(1, KV_LORA)
    cw = ssd_conv_w[0]
    cb = ssd_conv_b[0].reshape(1, CONV_DIM)
    dtb = _pad_cols(ssd_dt_bias[0].reshape(1, SSD_HEADS), LANES)
    a_neg = _pad_cols(-jnp.exp(ssd_a_log[0].astype(F32)).reshape(1, SSD_HEADS), LANES)
    d_x = jnp.repeat(ssd_d[0].astype(F32), SSD_HEAD_DIM).reshape(1, D_INNER)
    e_mat = jnp.asarray(np.arange(LANES)[:, None] == np.arange(D_INNER)[None, :] // SSD_HEAD_DIM, BF16)
    ng = ssd_norm_g[0].reshape(1, D_INNER)
    wa = w_branch_a[0].astype(BF16)
    wb = w_branch_b[0].astype(BF16)
    wo = w_out[0].astype(BF16)
    rw = _pad_cols(router_w[0], LANES)
    rwh = rw.astype(BF16)
    rwl = (rw - rwh.astype(F32)).astype(BF16)
    rb = jnp.concatenate([router_b[0].reshape(1, N_EXPERTS),
                          jnp.full((1, LANES - N_EXPERTS), NEG, F32)], axis=1)
    bgu = jnp.concatenate([exp_b_gu[0][:, 0::2], exp_b_gu[0][:, 1::2]], axis=1).reshape(N_EXPERTS, 1, 2 * D_FF)
    bdn = exp_b_down[0].reshape(N_EXPERTS, 1, D_MODEL)
    row2 = lambda v: v[0].reshape(1, D_MODEL)

    cos_p, sin_p = _rope_tables(jnp.arange(t_p, dtype=jnp.int32))
    cos_s, sin_s = _rope_tables(jnp.tile(past_len + jnp.arange(t_s, dtype=jnp.int32), b_s))

    xp = x_prompt.reshape(n_p, D_MODEL)
    xs = x_sample.reshape(n_s, D_MODEL)
    qlat_p, qrope_p, c_p, cbf_p, kr_p, krbf_p, dt_p = _mla_in(xp, w_mla, qg, kvg, wq, wuk, cos_p, sin_p, ROW_TILE)
    qlat_s, qrope_s, c_s, _, kr_s, _, dt_s = _mla_in(xs, w_mla, qg, kvg, wq, wuk, cos_s, sin_s, ROW_TILE)
    z_p, xbc_p, gates_p = (_proj(xp, w, PROJ_TILE) for w in (w_z, w_xbc, w_gates))
    z_s, xbc_s, gates_s = (_proj(xs, w, ROW_TILE) for w in (w_z, w_xbc, w_gates))

    conv0_p = jnp.zeros((b_p, SUBLANES, CONV_DIM), F32)
    conv0_s = jnp.pad(state_conv[0], ((0, 0), (SUBLANES - (D_CONV - 1), 0), (0, 0)))
    ya_p, ssm_p, ctail_p = _ssd(xbc_p, z_p, dt_p, conv0_p, None, cw, cb, dtb, a_neg, d_x, ng, e_mat,
                                nb=b_p, nc=t_p // SSD_CHUNK, t_in=SSD_CHUNK, chunk=SSD_CHUNK, y_dtype=BF16)
    ya_s, ssm_s, ctail_s = _ssd(xbc_s, z_s, dt_s, conv0_s, state_ssm[0].reshape(b_s, D_INNER, D_STATE),
                                cw, cb, dtb, a_neg, d_x, ng, e_mat,
                                nb=b_s, nc=1, t_in=t_s, chunk=SAMPLE_CHUNK, y_dtype=F32)
    conv_p = ctail_p[:, SUBLANES - (D_CONV - 1):, :]
    conv_s = ctail_s[:, SUBLANES - (D_CONV - 1):, :]

    o_p = _attn_prompt(qlat_p, qrope_p, cbf_p, krbf_p, wuv, nb=b_p, t=t_p, tq=ATTN_TQ)
    ql_s = jnp.transpose(qlat_s.reshape(MLA_HEADS, b_s, t_s, KV_LORA), (1, 0, 2, 3))
    qr_s = jnp.transpose(qrope_s.reshape(MLA_HEADS, b_s, t_s, ROPE_DIM), (1, 0, 2, 3))
    cnew_s = jnp.pad(c_s.reshape(b_s, t_s, KV_LORA), ((0, 0), (0, LANES - t_s), (0, 0)))
    krnew_s = jnp.swapaxes(jnp.pad(kr_s.reshape(b_s, t_s, ROPE_DIM), ((0, 0), (0, LANES - t_s), (0, 0))), 1, 2)
    olat_s = _attn_sample(page_table, ql_s.reshape(b_s, MLA_HEADS * t_s, KV_LORA),
                          qr_s.reshape(b_s, MLA_HEADS * t_s, ROPE_DIM), cnew_s, krnew_s,
                          cache_kv_latent[0], jnp.swapaxes(cache_k_rope[0], 1, 2),
                          t_new=t_s, gp=PAGE_GROUP)
    olat_s = jnp.transpose(olat_s.reshape(b_s, MLA_HEADS, t_s, KV_LORA), (1, 0, 2, 3)).reshape(MLA_HEADS, n_s, KV_LORA)
    o_s = _head_proj(olat_s, wuv)

    h, h3, top_i, top_g, rank, counts = _merge(ya_p, ya_s.astype(BF16), o_p, o_s, gates_p, gates_s, xp, xs, wa, wb, wo,
                                 row2(ln1_g), row2(ln1_b), rwh, rwl, rb, ROW_TILE)

    tb = MOE_TB
    n_blocks = -(-(n * TOP_K + N_EXPERTS * (tb - 1)) // tb)
    dest, pad_start, blk_e, n_used = _routing(top_i[:, :TOP_K], rank[:, :TOP_K], counts[0, :N_EXPERTS], n_blocks, tb)
    xr3 = _dispatch(pad_start, dest, h3, (n_blocks + 1) * tb, COMBINE_TILE, tb)
    yr3 = _ffn(blk_e, n_used, xr3, n_blocks, exp_w_gu[0], bgu, exp_w_down[0], bdn, tb)
    y_p, y_s = _combine(dest, yr3, top_g, h, row2(ln2_g), row2(ln2_b), n_p, COMBINE_TILE)

    return (y_p.reshape(b_p, t_p, D_MODEL), y_s.reshape(b_s, t_s, D_MODEL),
            c_p.reshape(1, b_p, t_p, KV_LORA), kr_p.reshape(1, b_p, t_p, ROPE_DIM),
            ssm_p.reshape(1, b_p, SSD_HEADS, SSD_HEAD_DIM, D_STATE), conv_p[None],
            c_s.reshape(1, b_s, t_s, KV_LORA), kr_s.reshape(1, b_s, t_s, ROPE_DIM),
            ssm_s.reshape(1, b_s, SSD_HEADS, SSD_HEAD_DIM, D_STATE), conv_s[None])
```

```python
import functools

import jax
import jax.numpy as jnp
import numpy as np
from jax import lax
from jax.experimental import pallas as pl
from jax.experimental.pallas import tpu as pltpu

F32 = jnp.float32
BF16 = jnp.bfloat16

D_MODEL = 1024
D_INNER = 2048
SSD_HEAD_DIM = 64
SSD_HEADS = 32
SSD_GROUPS = 4
HEADS_PER_GROUP = 8
D_STATE = 128
D_CONV = 4
CONV_DIM = D_INNER + 2 * SSD_GROUPS * D_STATE
SSD_CHUNK = 128
MLA_HEADS = 8
Q_LORA = 256
KV_LORA = 256
QK_NOPE = 128
ROPE_DIM = 64
V_HEAD = 128
ROPE_THETA = 10000.0
ATTN_SCALE = (QK_NOPE + ROPE_DIM) ** -0.5
N_EXPERTS = 32
TOP_K = 4
D_FF = 1024
SWIGLU_ALPHA = 1.702
SWIGLU_LIMIT = 7.0
DEPTH = 1
DN_ALPHA = (2 * DEPTH) ** 0.25
LN_EPS = 1e-5
RMS_EPS = 1e-6
IN_SIZES = (Q_LORA, KV_LORA, ROPE_DIM, D_INNER, CONV_DIM, SSD_HEADS, D_MODEL, D_MODEL)
IN_OFFS = tuple(int(v) for v in np.cumsum((0,) + IN_SIZES))

LANES = 128
SUBLANES = 8
VMEM_LIMIT = 56 * 1024 * 1024

NEG = -1e30

ROW_TILE = 256
PROJ_TILE = 512
ATTN_TQ = 256
ATTN_HEAD_GROUPS = 2
PAGE_GROUP = 64
SAMPLE_CHUNK = 32
MOE_TB = 256
COMBINE_TILE = 128
MLA_A_COLS = 896


def _cparams(n_axes):
    return pltpu.CompilerParams(dimension_semantics=("arbitrary",) * n_axes,
                                vmem_limit_bytes=VMEM_LIMIT)


def _dot(a, b):
    return jnp.dot(a, b, preferred_element_type=F32)


def _dot_nt(a, b):
    return lax.dot_general(a, b, (((1,), (1,)), ((), ())), preferred_element_type=F32)


def _split3(x):
    hi = x.astype(BF16)
    r1 = x - hi.astype(F32)
    mid = r1.astype(BF16)
    lo = (r1 - mid.astype(F32)).astype(BF16)
    return hi, mid, lo


def _sigmoid(x):
    return 1.0 / (1.0 + jnp.exp(-x))


def _layer_norm(x, g, b):
    mu = jnp.mean(x, axis=-1, keepdims=True)
    xc = x - mu
    var = jnp.mean(xc * xc, axis=-1, keepdims=True)
    return xc * lax.rsqrt(var + LN_EPS) * g + b


def _rms_norm(x, g):
    return x * lax.rsqrt(jnp.mean(x * x, axis=-1, keepdims=True) + RMS_EPS) * g


def _proj_kernel(x_ref, w_ref, o_ref):
    o_ref[...] = _dot(x_ref[...].astype(BF16), w_ref[...])


def _proj(x, w, tm):
    m, k = x.shape
    n = w.shape[1]
    return pl.pallas_call(
        _proj_kernel,
        out_shape=jax.ShapeDtypeStruct((m, n), F32),
        grid=(m // tm,),
        in_specs=[pl.BlockSpec((tm, k), lambda i: (i, 0)),
                  pl.BlockSpec((k, n), lambda i: (0, 0))],
        out_specs=pl.BlockSpec((tm, n), lambda i: (i, 0)),
        compiler_params=_cparams(1),
        name="proj",
    )(x, w)


def _mla_in_kernel(x_ref, wa_ref, qg_ref, kvg_ref, wq_ref, wuk_ref, cos_ref, sin_ref,
                   qlat_ref, qrope_ref, c_ref, cbf_ref, kr_ref, krbf_ref, dt_ref):
    xb = x_ref[...].astype(BF16)
    pa = _dot(xb, wa_ref[...])
    q_a = pa[:, 0:256]
    kv_a = pa[:, 256:512]
    kr_raw = pa[:, 512:640]
    kr_rot = pa[:, 640:768]
    dt_ref[...] = pa[:, 768:896]
    cos = cos_ref[...]
    sin = sin_ref[...]
    c = _rms_norm(kv_a, kvg_ref[...])
    c_ref[...] = c
    cbf_ref[...] = c.astype(BF16)
    kr = kr_raw * cos + kr_rot * sin
    kr_ref[...] = kr[:, 0:ROPE_DIM]
    krbf_ref[...] = kr[:, 0:ROPE_DIM].astype(BF16)
    qn = _rms_norm(q_a, qg_ref[...]).astype(BF16)
    qq = _dot(qn, wq_ref[...])
    cos4 = jnp.concatenate([cos] * 4, axis=1)
    sin4 = jnp.concatenate([sin] * 4, axis=1)
    qr = (qq[:, 1024:1536] * cos4 + qq[:, 1536:2048] * sin4) * ATTN_SCALE
    for h in range(MLA_HEADS):
        qrope_ref[h] = qr[:, h * ROPE_DIM:(h + 1) * ROPE_DIM].astype(BF16)
        nope = qq[:, h * QK_NOPE:(h + 1) * QK_NOPE].astype(BF16)
        qlat_ref[h] = (_dot(nope, wuk_ref[h]) * ATTN_SCALE).astype(BF16)


def _mla_in(x, wa, qg, kvg, wq, wuk, cos_t, sin_t, tm):
    m = x.shape[0]
    tab_blocks = cos_t.shape[0] // tm
    row = lambda w: pl.BlockSpec((tm, w), lambda i: (i, 0))
    tab = pl.BlockSpec((tm, LANES), lambda i: (i % tab_blocks, 0))
    full2 = lambda a: pl.BlockSpec(a.shape, lambda i: (0, 0))
    full3 = lambda a: pl.BlockSpec(a.shape, lambda i: (0, 0, 0))
    return pl.pallas_call(
        _mla_in_kernel,
        out_shape=(jax.ShapeDtypeStruct((MLA_HEADS, m, KV_LORA), BF16),
                   jax.ShapeDtypeStruct((MLA_HEADS, m, ROPE_DIM), BF16),
                   jax.ShapeDtypeStruct((m, KV_LORA), F32),
                   jax.ShapeDtypeStruct((m, KV_LORA), BF16),
                   jax.ShapeDtypeStruct((m, ROPE_DIM), F32),
                   jax.ShapeDtypeStruct((m, ROPE_DIM), BF16),
                   jax.ShapeDtypeStruct((m, LANES), F32)),
        grid=(m // tm,),
        in_specs=[row(D_MODEL), full2(wa), full2(qg), full2(kvg), full2(wq), full3(wuk), tab, tab],
        out_specs=(pl.BlockSpec((MLA_HEADS, tm, KV_LORA), lambda i: (0, i, 0)),
                   pl.BlockSpec((MLA_HEADS, tm, ROPE_DIM), lambda i: (0, i, 0)),
                   row(KV_LORA), row(KV_LORA), row(ROPE_DIM), row(ROPE_DIM), row(LANES)),
        compiler_params=_cparams(1),
        name="mla_in",
    )(x, wa, qg, kvg, wq, wuk, cos_t, sin_t)


def _expand_heads(v, e_ref, terms):
    out = None
    rem = v
    for _ in range(terms):
        part = rem.astype(BF16)
        rem = rem - part.astype(F32)
        d = _dot(part, e_ref[...])
        out = d if out is None else out + d
    return out


def _conv_silu(ext, r0, rows, cw_ref, cb_ref, out, out_r0):
    for j in range(CONV_DIM // LANES):
        sl = slice(j * LANES, (j + 1) * LANES)
        acc = cb_ref[:, sl]
        for k in range(D_CONV):
            acc = acc + ext[pl.ds(r0 + SUBLANES - (D_CONV - 1) + k, rows), sl] * cw_ref[k:k + 1, sl]
        out[out_r0:out_r0 + rows, sl] = acc * _sigmoid(acc)


def _ssd_kernel(*refs, t_in, L, has_h0):
    n_pairs = SSD_HEADS // 2
    refs = list(refs)
    hT = refs[-n_pairs:]
    it = iter(refs[:-n_pairs])
    xbc_ref, z_ref, dt_ref, conv0_ref = next(it), next(it), next(it), next(it)
    h0_ref = next(it) if has_h0 else None
    (cw_ref, cb_ref, dtb_ref, a_ref, dx_ref, ng_ref, e_ref, y_ref, hfin_ref, ctail_ref,
     ext, xcv, ybuf, acs_s, acsT_s, dtT_s, eax_s, wtx_s) = it
    c = pl.program_id(1)
    nc = pl.num_programs(1)

    @pl.when(c == 0)
    def _():
        ext[0:SUBLANES, :] = conv0_ref[0]
        if t_in < L:
            ext[SUBLANES:SUBLANES + L, :] = jnp.zeros((L, CONV_DIM), F32)
        for j in range(n_pairs):
            if has_h0:
                hT[j][...] = h0_ref[0, j * LANES:(j + 1) * LANES, :].T
            else:
                hT[j][...] = jnp.zeros((D_STATE, LANES), F32)

    ext[SUBLANES:SUBLANES + t_in, :] = xbc_ref[...]

    @pl.when(c == nc - 1)
    def _():
        ctail_ref[0] = ext[t_in:t_in + SUBLANES, :]

    _conv_silu(ext, 0, L, cw_ref, cb_ref, xcv, 0)
    ext[0:SUBLANES, :] = ext[L:L + SUBLANES, :]

    row = lax.broadcasted_iota(jnp.int32, (L, L), 0)
    col = lax.broadcasted_iota(jnp.int32, (L, L), 1)
    tri = row >= col
    tri_b = jnp.where(tri, 1.0, 0.0).astype(BF16)
    if t_in < L:
        dt_raw = jnp.concatenate([dt_ref[...], jnp.zeros((L - t_in, LANES), F32)], axis=0)
    else:
        dt_raw = dt_ref[...]
    xs = dt_raw + dtb_ref[...]
    dtv = jnp.maximum(xs, 0.0) + jnp.log1p(jnp.exp(-jnp.abs(xs)))
    if t_in < L:
        dtv = jnp.where(lax.broadcasted_iota(jnp.int32, (L, LANES), 0) < t_in, dtv, 0.0)
    da = dtv * a_ref[...]
    hi, mid, lo = _split3(da)
    acs = _dot(tri_b, hi) + _dot(tri_b, mid) + _dot(tri_b, lo)
    acs_s[...] = acs
    acsT_s[...] = acs.T
    dtT_s[...] = dtv.T
    eax_s[...] = _expand_heads(jnp.exp(acs), e_ref, 3)
    wtx_s[...] = _expand_heads(jnp.exp(acs[L - 1:L, :] - acs) * dtv, e_ref, 2)

    low_half = lax.broadcasted_iota(jnp.int32, (L, LANES), 1) < SSD_HEAD_DIM
    pairs_per_group = HEADS_PER_GROUP // 2
    for g in range(SSD_GROUPS):
        bo = D_INNER + g * D_STATE
        co = D_INNER + SSD_GROUPS * D_STATE + g * D_STATE
        bm = xcv[:, bo:bo + D_STATE]
        cmb = xcv[:, co:co + D_STATE].astype(BF16)
        cb = _dot_nt(cmb, bm.astype(BF16))
        bmT = bm.T.astype(BF16)
        for k in range(pairs_per_group):
            kk = g * pairs_per_group + k
            tl = slice(kk * LANES, (kk + 1) * LANES)
            x2 = xcv[:, tl]
            ws = []
            for idx in (2 * kk, 2 * kk + 1):
                seg = acs_s[:, idx:idx + 1] - acsT_s[idx:idx + 1, :]
                w = cb * jnp.exp(jnp.where(tri, seg, NEG)) * dtT_s[idx:idx + 1, :]
                ws.append(w.astype(BF16))
            w2 = jnp.concatenate(ws, axis=1)
            xbd = jnp.concatenate([jnp.where(low_half, x2, 0.0).astype(BF16),
                                   jnp.where(low_half, 0.0, x2).astype(BF16)], axis=0)
            yd = _dot(w2, xbd)
            hp = hT[kk][...]
            yo = _dot(cmb, hp.astype(BF16))
            ybuf[:, tl] = yd + yo * eax_s[:, tl] + dx_ref[:, tl] * x2
            s = _dot(bmT, (x2 * wtx_s[:, tl]).astype(BF16))
            hT[kk][...] = eax_s[L - 1:L, tl] * hp + s

    gw = D_INNER // SSD_GROUPS
    for g in range(SSD_GROUPS):
        sl = slice(g * gw, (g + 1) * gw)
        zg = z_ref[:, sl]
        v = ybuf[0:t_in, sl] * (zg * _sigmoid(zg))
        ms = jnp.mean(v * v, axis=-1, keepdims=True)
        y_ref[:, sl] = (v * lax.rsqrt(ms + RMS_EPS) * ng_ref[:, sl]).astype(y_ref.dtype)

    @pl.when(c == nc - 1)
    def _():
        for j in range(n_pairs):
            hfin_ref[0, j * LANES:(j + 1) * LANES, :] = hT[j][...].T


def _ssd(xbc, z, dt, conv0, h0, cw, cb, dtb, a_neg, d_x, ng, e_mat, *, nb, nc, t_in, chunk, y_dtype):
    L = chunk
    has_h0 = h0 is not None
    rowspec = lambda w: pl.BlockSpec((t_in, w), lambda b, c: (b * nc + c, 0))
    par = lambda a: pl.BlockSpec(a.shape, lambda b, c: (0, 0))
    per_seq = lambda r, w: pl.BlockSpec((1, r, w), lambda b, c: (b, 0, 0))
    in_specs = [rowspec(CONV_DIM), rowspec(D_INNER), rowspec(LANES), per_seq(SUBLANES, CONV_DIM)]
    args = [xbc, z, dt, conv0]
    if has_h0:
        in_specs.append(per_seq(D_INNER, D_STATE))
        args.append(h0)
    params = [cw, cb, dtb, a_neg, d_x, ng, e_mat]
    in_specs += [par(p) for p in params]
    args += params
    return pl.pallas_call(
        functools.partial(_ssd_kernel, t_in=t_in, L=L, has_h0=has_h0),
        out_shape=(jax.ShapeDtypeStruct((nb * nc * t_in, D_INNER), y_dtype),
                   jax.ShapeDtypeStruct((nb, D_INNER, D_STATE), F32),
                   jax.ShapeDtypeStruct((nb, SUBLANES, CONV_DIM), F32)),
        grid=(nb, nc),
        in_specs=in_specs,
        out_specs=(rowspec(D_INNER), per_seq(D_INNER, D_STATE), per_seq(SUBLANES, CONV_DIM)),
        scratch_shapes=[pltpu.VMEM((L + 2 * SUBLANES, CONV_DIM), F32),
                        pltpu.VMEM((L, CONV_DIM), F32),
                        pltpu.VMEM((L, D_INNER), F32),
                        pltpu.VMEM((L, LANES), F32), pltpu.VMEM((LANES, L), F32), pltpu.VMEM((LANES, L), F32),
                        pltpu.VMEM((L, D_INNER), F32), pltpu.VMEM((L, D_INNER), F32)]
                       + [pltpu.VMEM((D_STATE, LANES), F32)] * (SSD_HEADS // 2),
        compiler_params=_cparams(2),
        name="ssd",
    )(*args)


def _softmax_step(s, kc, m_scr, l_scr, acc_scr):
    reps = s.shape[1] // LANES
    m_prev = m_scr[...]
    m_new = jnp.maximum(m_prev, jnp.max(s, axis=1, keepdims=True))
    alpha = jnp.exp(m_prev - m_new)
    p = jnp.exp(s - (jnp.concatenate([m_new] * reps, axis=1) if reps > 1 else m_new))
    l_scr[...] = alpha * l_scr[...] + jnp.sum(p, axis=1, keepdims=True)
    acc_scr[...] = acc_scr[...] * jnp.concatenate([alpha] * (KV_LORA // LANES), axis=1) + _dot(p.astype(BF16), kc)
    m_scr[...] = m_new


def _attn_prompt_kernel(qlat_ref, qrope_ref, c_ref, kr_ref, wuv_ref, o_ref, m_scr, l_scr, acc_scr, *, tq):
    qi = pl.program_id(1)
    R = MLA_HEADS * tq

    def scores(block, width, g=0, groups=1):
        start = pl.multiple_of(block * width, width)
        kc = c_ref[pl.ds(start, width), :]
        kr = kr_ref[pl.ds(start, width), :]
        hpg = MLA_HEADS // groups
        q = qlat_ref[g * hpg:(g + 1) * hpg].reshape(hpg * tq, KV_LORA)
        qr = qrope_ref[g * hpg:(g + 1) * hpg].reshape(hpg * tq, ROPE_DIM)
        return _dot_nt(q, kc) + _dot_nt(qr, kr), kc

    s, kc = scores(qi, tq)
    t = lax.broadcasted_iota(jnp.int32, (R, tq), 0) & (tq - 1)
    kcol = lax.broadcasted_iota(jnp.int32, (R, tq), 1)
    s = jnp.where(kcol <= t, s, NEG)
    m0 = jnp.max(s, axis=1, keepdims=True)
    p = jnp.exp(s - m0)
    m_scr[...] = jnp.broadcast_to(m0, m_scr.shape)
    l_scr[...] = jnp.broadcast_to(jnp.sum(p, axis=1, keepdims=True), l_scr.shape)
    acc_scr[...] = _dot(p.astype(BF16), kc)

    def body(j, carry):
        rg = R // ATTN_HEAD_GROUPS
        for g in range(ATTN_HEAD_GROUPS):
            rows = pl.ds(g * rg, rg)
            s2, kc2 = scores(j, 2 * tq, g, ATTN_HEAD_GROUPS)
            _softmax_step(s2, kc2, m_scr.at[rows], l_scr.at[rows], acc_scr.at[rows])
        return carry

    lax.fori_loop(0, qi // 2, body, 0)

    @pl.when(qi % 2 == 1)
    def _():
        s1, kc1 = scores(qi - 1, tq)
        _softmax_step(s1, kc1, m_scr, l_scr, acc_scr)

    inv = 1.0 / l_scr[...]
    o_lat = acc_scr[...] * jnp.concatenate([inv] * (KV_LORA // LANES), axis=1)
    for h in range(MLA_HEADS):
        oh = _dot(o_lat[h * tq:(h + 1) * tq, :].astype(BF16), wuv_ref[h])
        o_ref[:, h * V_HEAD:(h + 1) * V_HEAD] = oh.astype(o_ref.dtype)


def _attn_prompt(qlat, qrope, cbf, krbf, wuv, *, nb, t, tq):
    nq = t // tq
    R = MLA_HEADS * tq
    return pl.pallas_call(
        functools.partial(_attn_prompt_kernel, tq=tq),
        out_shape=jax.ShapeDtypeStruct((nb * t, MLA_HEADS * V_HEAD), BF16),
        grid=(nb, nq),
        in_specs=[pl.BlockSpec((MLA_HEADS, tq, KV_LORA), lambda b, i: (0, b * nq + i, 0)),
                  pl.BlockSpec((MLA_HEADS, tq, ROPE_DIM), lambda b, i: (0, b * nq + i, 0)),
                  pl.BlockSpec((t, KV_LORA), lambda b, i: (b, 0)),
                  pl.BlockSpec((t, ROPE_DIM), lambda b, i: (b, 0)),
                  pl.BlockSpec(wuv.shape, lambda b, i: (0, 0, 0))],
        out_specs=pl.BlockSpec((tq, MLA_HEADS * V_HEAD), lambda b, i: (b * nq + i, 0)),
        scratch_shapes=[pltpu.VMEM((R, LANES), F32), pltpu.VMEM((R, LANES), F32),
                        pltpu.VMEM((R, KV_LORA), F32)],
        compiler_params=_cparams(2),
        name="attn_prompt",
    )(qlat, qrope, cbf, krbf, wuv)


def _attn_sample_kernel(pt_ref, qlat_ref, qrope_ref, cnew_ref, krnew_ref, kv_hbm, kr_hbm, o_ref,
                        kvbuf, krbuf, kv_sem, kr_sem, m_scr, l_scr, acc_scr, *, n_groups, gp, t_new):
    b = pl.program_id(0)
    nb = pl.num_programs(0)
    R = qlat_ref.shape[1]
    page = kvbuf.shape[2]
    q = qlat_ref[0]
    qr = qrope_ref[0]

    def start_group(bb, g, slot):
        for p in range(gp):
            pid = pt_ref[bb, g * gp + p]
            pltpu.make_async_copy(kv_hbm.at[pid], kvbuf.at[slot, p], kv_sem.at[slot]).start()
            pltpu.make_async_copy(kr_hbm.at[pid], krbuf.at[slot, p], kr_sem.at[slot]).start()

    def wait_group(slot):
        pltpu.make_async_copy(kvbuf.at[slot], kvbuf.at[slot], kv_sem.at[slot]).wait()
        pltpu.make_async_copy(krbuf.at[slot], krbuf.at[slot], kr_sem.at[slot]).wait()

    @pl.when(b == 0)
    def _():
        start_group(0, 0, 0)

    kc = cnew_ref[0].astype(BF16)
    kr = krnew_ref[0].astype(BF16)
    s = _dot_nt(q, kc) + _dot(qr, kr)
    t = lax.broadcasted_iota(jnp.int32, (R, LANES), 0) & (t_new - 1)
    kcol = lax.broadcasted_iota(jnp.int32, (R, LANES), 1)
    s = jnp.where(kcol <= t, s, NEG)
    m0 = jnp.max(s, axis=1, keepdims=True)
    p0 = jnp.exp(s - m0)
    m_scr[...] = jnp.broadcast_to(m0, m_scr.shape)
    l_scr[...] = jnp.broadcast_to(jnp.sum(p0, axis=1, keepdims=True), l_scr.shape)
    acc_scr[...] = _dot(p0.astype(BF16), kc)

    for g in range(n_groups):
        slot = g % 2
        if g + 1 < n_groups:
            start_group(b, g + 1, 1 - slot)
        else:
            @pl.when(b + 1 < nb)
            def _():
                start_group(b + 1, 0, 1 - slot)
        wait_group(slot)
        kc = kvbuf[slot].reshape(gp * page, KV_LORA).astype(BF16)
        kr = jnp.concatenate([krbuf[slot, p].astype(BF16) for p in range(gp)], axis=1)
        s = _dot_nt(q, kc) + _dot(qr, kr)
        _softmax_step(s, kc, m_scr, l_scr, acc_scr)

    inv = 1.0 / l_scr[...]
    o_ref[0] = acc_scr[...] * jnp.concatenate([inv] * (KV_LORA // LANES), axis=1)


def _attn_sample(page_table, qlat, qrope, cnew, krnew, cache_kv, cache_kr, *, t_new, gp):
    nb, n_pages = page_table.shape
    n_groups = n_pages // gp
    R = qlat.shape[1]
    page = cache_kv.shape[1]
    per_b = lambda r, w: pl.BlockSpec((1, r, w), lambda b, pt: (b, 0, 0))
    grid_spec = pltpu.PrefetchScalarGridSpec(
        num_scalar_prefetch=1,
        grid=(nb,),
        in_specs=[per_b(R, KV_LORA), per_b(R, ROPE_DIM), per_b(LANES, KV_LORA), per_b(ROPE_DIM, LANES),
                  pl.BlockSpec(memory_space=pl.ANY), pl.BlockSpec(memory_space=pl.ANY)],
        out_specs=per_b(R, KV_LORA),
        scratch_shapes=[pltpu.VMEM((2, gp, page, KV_LORA), F32), pltpu.VMEM((2, gp, ROPE_DIM, page), F32),
                        pltpu.SemaphoreType.DMA((2,)), pltpu.SemaphoreType.DMA((2,)),
                        pltpu.VMEM((R, LANES), F32), pltpu.VMEM((R, LANES), F32),
                        pltpu.VMEM((R, KV_LORA), F32)],
    )
    return pl.pallas_call(
        functools.partial(_attn_sample_kernel, n_groups=n_groups, gp=gp, t_new=t_new),
        out_shape=jax.ShapeDtypeStruct((nb, R, KV_LORA), F32),
        grid_spec=grid_spec,
        compiler_params=_cparams(1),
        name="attn_sample",
    )(page_table, qlat, qrope, cnew, krnew, cache_kv, cache_kr)


def _head_proj_kernel(x_ref, w_ref, o_ref):
    o_ref[...] = _dot(x_ref[0].astype(BF16), w_ref[0]).astype(o_ref.dtype)


def _head_proj(x, w):
    h, m, k = x.shape
    n = w.shape[2]
    return pl.pallas_call(
        _head_proj_kernel,
        out_shape=jax.ShapeDtypeStruct((m, h * n), BF16),
        grid=(h,),
        in_specs=[pl.BlockSpec((1, m, k), lambda i: (i, 0, 0)),
                  pl.BlockSpec((1, k, n), lambda i: (i, 0, 0))],
        out_specs=pl.BlockSpec((m, n), lambda i: (0, i)),
        compiler_params=_cparams(1),
        name="head_proj",
    )(x, w)


def _merge_kernel(ya_p_ref, ya_s_ref, o_p_ref, o_s_ref, g_p_ref, g_s_ref, u_p_ref, u_s_ref, wa_ref, wb_ref, wo_ref,
                  lg_ref, lb_ref, rwh_ref, rwl_ref, rb_ref,
                  h_ref, h3_ref, ti_ref, tg_ref, rank_ref, cnt_ref, run_ref, *, n_prompt_tiles):
    i = pl.program_id(0)
    is_p = i < n_prompt_tiles
    yn = jnp.where(is_p, ya_p_ref[...], ya_s_ref[...])
    ob = jnp.where(is_p, o_p_ref[...], o_s_ref[...])
    g = jnp.where(is_p, g_p_ref[...], g_s_ref[...])
    u = jnp.where(is_p, u_p_ref[...], u_s_ref[...])
    ya = _dot(yn, wa_ref[...])
    yb = _dot(ob, wb_ref[...])
    merged = _sigmoid(g[:, 0:D_MODEL]) * ya + _sigmoid(g[:, D_MODEL:2 * D_MODEL]) * yb
    mix = _dot(merged.astype(BF16), wo_ref[...])
    h = _layer_norm(DN_ALPHA * u + mix, lg_ref[...], lb_ref[...])
    h_ref[...] = h
    tm = h.shape[0]
    for j in range(D_MODEL // LANES):
        h3_ref[pl.ds(j, tm, stride=SUBLANES), :] = h[:, j * LANES:(j + 1) * LANES]
    hh = h.astype(BF16)
    hl = (h - hh.astype(F32)).astype(BF16)
    logits = _dot(hh, rwh_ref[...]) + _dot(hl, rwh_ref[...]) + _dot(hh, rwl_ref[...]) + rb_ref[...]
    lane = lax.broadcasted_iota(jnp.int32, logits.shape, 1).astype(F32)
    vals = jnp.zeros_like(logits)
    idxs = jnp.zeros_like(logits)
    cur = logits
    picked = []
    for k in range(TOP_K):
        mx = jnp.max(cur, axis=1, keepdims=True)
        ix = jnp.min(jnp.where(cur == mx, lane, float(LANES)), axis=1, keepdims=True)
        vals = jnp.where(lane == k, mx, vals)
        idxs = jnp.where(lane == k, ix, idxs)
        picked.append(lane == ix)
        cur = jnp.where(picked[k], NEG * 2.0, cur)
    v0 = vals[:, 0:1]
    e = jnp.where(lane < TOP_K, jnp.exp(vals - v0), 0.0)
    tg_ref[...] = e / jnp.sum(e, axis=1, keepdims=True)
    ti_ref[...] = idxs.astype(jnp.int32)

    @pl.when(i == 0)
    def _():
        run_ref[...] = jnp.zeros_like(run_ref)

    cnt = jnp.zeros_like(logits)
    for k in range(TOP_K):
        cnt = cnt + jnp.where(picked[k], 1.0, 0.0)
    r_i = lax.broadcasted_iota(jnp.int32, (tm, tm), 0)
    c_i = lax.broadcasted_iota(jnp.int32, (tm, tm), 1)
    before = _dot(jnp.where(r_i > c_i, 1.0, 0.0).astype(BF16), cnt.astype(BF16)) + run_ref[0:1, :]
    ranks = jnp.zeros_like(logits)
    for k in range(TOP_K):
        rk = jnp.sum(jnp.where(picked[k], before, 0.0), axis=1, keepdims=True)
        ranks = jnp.where(lane == k, rk, ranks)
    rank_ref[...] = ranks.astype(jnp.int32)
    run_ref[0:1, :] = run_ref[0:1, :] + jnp.sum(cnt, axis=0, keepdims=True)
    cnt_ref[...] = run_ref[...].astype(jnp.int32)


def _merge(ya_p, ya_s, o_p, o_s, g_p, g_s, u_p, u_s, wa, wb, wo, lg, lb, rwh, rwl, rb, tm):
    m = u_p.shape[0] + u_s.shape[0]
    npt = ya_p.shape[0] // tm
    row = lambda w: pl.BlockSpec((tm, w), lambda i: (i, 0))
    prow = lambda w: pl.BlockSpec((tm, w), lambda i: (jnp.minimum(i, npt - 1), 0))
    srow = lambda w: pl.BlockSpec((tm, w), lambda i: (jnp.maximum(i - npt, 0), 0))
    par = lambda a: pl.BlockSpec(a.shape, lambda i: (0, 0))
    return pl.pallas_call(
        functools.partial(_merge_kernel, n_prompt_tiles=npt),
        out_shape=(jax.ShapeDtypeStruct((m, D_MODEL), F32),
                   jax.ShapeDtypeStruct((m * SUBLANES, LANES), F32),
                   jax.ShapeDtypeStruct((m, LANES), jnp.int32),
                   jax.ShapeDtypeStruct((m, LANES), F32),
                   jax.ShapeDtypeStruct((m, LANES), jnp.int32),
                   jax.ShapeDtypeStruct((SUBLANES, LANES), jnp.int32)),
        grid=(m // tm,),
        in_specs=[prow(D_INNER), srow(D_INNER), prow(D_MODEL), srow(D_MODEL), prow(2 * D_MODEL), srow(2 * D_MODEL),
                  prow(D_MODEL), srow(D_MODEL),
                  par(wa), par(wb), par(wo), par(lg), par(lb), par(rwh), par(rwl), par(rb)],
        out_specs=(row(D_MODEL), pl.BlockSpec((tm * SUBLANES, LANES), lambda i: (i, 0)), row(LANES), row(LANES),
                   row(LANES), pl.BlockSpec((SUBLANES, LANES), lambda i: (0, 0))),
        scratch_shapes=[pltpu.VMEM((SUBLANES, LANES), F32)],
        compiler_params=_cparams(1),
        name="merge",
    )(ya_p, ya_s, o_p, o_s, g_p, g_s, u_p, u_s, wa, wb, wo, lg, lb, rwh, rwl, rb)


DEINT_IN = 2 * LANES


def _deint_perm():
    perm = np.zeros((DEINT_IN, DEINT_IN), np.float32)
    perm[2 * np.arange(LANES), np.arange(LANES)] = 1.0
    perm[2 * np.arange(LANES) + 1, LANES + np.arange(LANES)] = 1.0
    return jnp.asarray(perm, BF16)


def _row_pipeline(i, n_steps, idx_hbm, src_hbm, idx_smem, idx_sem, buf, row_sem, rows, compute):
    def idx_copy(chunk, s):
        return pltpu.make_async_copy(idx_hbm.at[chunk], idx_smem.at[s], idx_sem.at[s])

    def issue_row(s, r):
        src0 = pl.multiple_of(idx_smem[s, r] * SUBLANES, SUBLANES)
        dst0 = pl.multiple_of(r * SUBLANES, SUBLANES)
        pltpu.make_async_copy(src_hbm.at[pl.ds(src0, SUBLANES), :],
                              buf.at[s, pl.ds(dst0, SUBLANES), :], row_sem.at[s]).start()

    @pl.when(i == 0)
    def _():
        idx_copy(0, 0).start()
        idx_copy(0, 0).wait()

        def body(r, carry):
            issue_row(0, r)
            return carry
        lax.fori_loop(0, rows, body, 0, unroll=8)
        if n_steps > 1:
            idx_copy(1, 1).start()

    def step(slot):
        has_next = i + 1 < n_steps

        @pl.when(has_next)
        def _():
            idx_copy(i + 1, 1 - slot).wait()

        @pl.when(i + 2 < n_steps)
        def _():
            idx_copy(i + 2, slot).start()

        pltpu.make_async_copy(src_hbm.at[pl.ds(0, rows * SUBLANES), :], buf.at[slot], row_sem.at[slot]).wait()

        @pl.when(has_next)
        def _():
            for r in range(rows):
                issue_row(1 - slot, r)
            compute(slot)

        @pl.when(i + 1 == n_steps)
        def _():
            compute(slot)

    step(lax.rem(i, 2))


def _fetch_scratch(rows):
    return [pltpu.SMEM((2, rows), jnp.int32), pltpu.SemaphoreType.DMA((2,)),
            pltpu.VMEM((2, rows * SUBLANES, LANES), F32), pltpu.SemaphoreType.DMA((2,))]


def _dispatch_kernel(pad_ref, dest_hbm, h3_ref, xr_hbm, idx_smem, idx_sem, stage, row_sem, zbuf, z_sem,
                     *, tm, tb, min_used):
    i = pl.program_id(0)
    n = pl.num_programs(0)
    rows = tm * TOP_K

    def idx_copy(chunk, s):
        return pltpu.make_async_copy(dest_hbm.at[chunk], idx_smem.at[s], idx_sem.at[s])

    def rows_wait(s):
        span = xr_hbm.at[pl.ds(0, rows * SUBLANES), :]
        pltpu.make_async_copy(span, span, row_sem.at[s]).wait()

    @pl.when(i == 0)
    def _():
        zbuf[...] = jnp.zeros_like(zbuf)
        zero_to = lambda start: pltpu.make_async_copy(zbuf, xr_hbm.at[pl.ds(start, tb * SUBLANES), :], z_sem)
        for e in range(N_EXPERTS):
            zero_to(pl.multiple_of(pad_ref[e] * SUBLANES, SUBLANES)).start()
        for e in range(N_EXPERTS):
            zero_to(0).wait()
        n_used = pad_ref[N_EXPERTS]
        tail = range(min_used, xr_hbm.shape[0] // (tb * SUBLANES))
        for blk in tail:
            pl.when(blk >= n_used)(zero_to(blk * tb * SUBLANES).start)
        for blk in tail:
            pl.when(blk >= n_used)(zero_to(0).wait)
        idx_copy(0, 0).start()

    def step(slot):
        idx_copy(i, slot).wait()

        @pl.when(i + 1 < n)
        def _():
            idx_copy(i + 1, 1 - slot).start()

        stage[slot] = h3_ref[...]
        for r in range(rows):
            dst0 = pl.multiple_of(idx_smem[slot, r] * SUBLANES, SUBLANES)
            pltpu.make_async_copy(stage.at[slot, pl.ds((r // TOP_K) * SUBLANES, SUBLANES), :],
                                  xr_hbm.at[pl.ds(dst0, SUBLANES), :], row_sem.at[slot]).start()

        @pl.when(i >= 1)
        def _():
            rows_wait(1 - slot)

        @pl.when(i == n - 1)
        def _():
            rows_wait(slot)

    step(lax.rem(i, 2))


def _dispatch(pad_start, dest, h3, cap_rows, tm, tb):
    n = h3.shape[0] // SUBLANES
    grid_spec = pltpu.PrefetchScalarGridSpec(
        num_scalar_prefetch=1,
        grid=(n // tm,),
        in_specs=[pl.BlockSpec(memory_space=pl.ANY),
                  pl.BlockSpec((tm * SUBLANES, LANES), lambda i, pad: (i, 0))],
        out_specs=pl.BlockSpec(memory_space=pl.ANY),
        scratch_shapes=[pltpu.SMEM((2, tm * TOP_K), jnp.int32), pltpu.SemaphoreType.DMA((2,)),
                        pltpu.VMEM((2, tm * SUBLANES, LANES), F32), pltpu.SemaphoreType.DMA((2,)),
                        pltpu.VMEM((tb * SUBLANES, LANES), F32), pltpu.SemaphoreType.DMA(())],
    )
    return pl.pallas_call(
        functools.partial(_dispatch_kernel, tm=tm, tb=tb, min_used=-(-(n * TOP_K) // tb)),
        out_shape=jax.ShapeDtypeStruct((cap_rows * SUBLANES, LANES), F32),
        grid_spec=grid_spec,
        compiler_params=_cparams(1),
        name="dispatch",
    )(pad_start, dest.reshape(n // tm, tm * TOP_K), h3)


def _ffn_kernel(be_ref, nu_ref, x_ref, wgu_ref, bgu_ref, wdn_ref, bdn_ref, perm_ref, o_ref,
                wg_s, wu_s, wd_s, *, tb):
    i = pl.program_id(0)
    nu = nu_ref[0]
    new_expert = jnp.logical_or(i == 0, be_ref[i] != be_ref[jnp.maximum(i - 1, 0)])

    @pl.when(jnp.logical_and(new_expert, i < nu))
    def _():
        for t in range(2 * D_FF // DEINT_IN):
            r = _dot(wgu_ref[:, t * DEINT_IN:(t + 1) * DEINT_IN].astype(BF16), perm_ref[...])
            wg_s[:, t * LANES:(t + 1) * LANES] = r[:, 0:LANES].astype(BF16)
            wu_s[:, t * LANES:(t + 1) * LANES] = r[:, LANES:DEINT_IN].astype(BF16)
        wd_s[...] = wdn_ref[...].astype(BF16)

    @pl.when(i < nu)
    def _():
        x = jnp.concatenate([x_ref[pl.ds(j, tb, stride=SUBLANES), :].astype(BF16)
                             for j in range(D_MODEL // LANES)], axis=1)
        bgu = bgu_ref[...]
        g = jnp.minimum(_dot(x, wg_s[...]) + bgu[:, 0:D_FF], SWIGLU_LIMIT)
        u = jnp.clip(_dot(x, wu_s[...]) + bgu[:, D_FF:2 * D_FF], -SWIGLU_LIMIT, SWIGLU_LIMIT)
        hh = (u + 1.0) * (g * _sigmoid(SWIGLU_ALPHA * g))
        out = _dot(hh.astype(BF16), wd_s[...]) + bdn_ref[...]
        for j in range(D_MODEL // LANES):
            o_ref[pl.ds(j, tb, stride=SUBLANES), :] = out[:, j * LANES:(j + 1) * LANES]

    @pl.when(i >= nu)
    def _():
        o_ref[...] = jnp.zeros_like(o_ref)


def _ffn(blk_e, n_used, xr3, nblk, wgu, bgu, wdn, bdn, tb):
    perm = _deint_perm()
    wspec = lambda a: pl.BlockSpec((None,) + a.shape[1:], lambda i, be, nu: (be[i], 0, 0))
    grid_spec = pltpu.PrefetchScalarGridSpec(
        num_scalar_prefetch=2,
        grid=(nblk,),
        in_specs=[pl.BlockSpec((tb * SUBLANES, LANES), lambda i, be, nu: (jnp.minimum(i, nu[0] - 1), 0)),
                  wspec(wgu), wspec(bgu), wspec(wdn), wspec(bdn),
                  pl.BlockSpec(perm.shape, lambda i, be, nu: (0, 0))],
        out_specs=pl.BlockSpec((tb * SUBLANES, LANES), lambda i, be, nu: (i, 0)),
        scratch_shapes=[pltpu.VMEM((D_MODEL, D_FF), BF16), pltpu.VMEM((D_MODEL, D_FF), BF16),
                        pltpu.VMEM((D_FF, D_MODEL), BF16)],
    )
    return pl.pallas_call(
        functools.partial(_ffn_kernel, tb=tb),
        out_shape=jax.ShapeDtypeStruct((nblk * tb * SUBLANES, LANES), F32),
        grid_spec=grid_spec,
        compiler_params=_cparams(1),
        name="moe_ffn",
    )(blk_e, n_used, xr3, wgu, bgu, wdn, bdn, perm)


def _combine_kernel(dest_hbm, yr3_hbm, tg_ref, h_ref, lg_ref, lb_ref, op_ref, os_ref,
                    idx_smem, idx_sem, ybuf, row_sem, *, tm, n_prompt_tiles, n_steps):
    i = pl.program_id(0)
    rows = tm * TOP_K

    def compute(slot):
        tg = tg_ref[...]
        stride = TOP_K * SUBLANES
        pieces = []
        for j in range(D_MODEL // LANES):
            acc = tg[:, 0:1] * ybuf[slot, pl.ds(j, tm, stride=stride), :]
            for k in range(1, TOP_K):
                acc = acc + tg[:, k:k + 1] * ybuf[slot, pl.ds(k * SUBLANES + j, tm, stride=stride), :]
            pieces.append(acc)
        y = jnp.concatenate(pieces, axis=1)
        out = _layer_norm(DN_ALPHA * h_ref[...] + y, lg_ref[...], lb_ref[...])

        @pl.when(i < n_prompt_tiles)
        def _():
            op_ref[...] = out

        @pl.when(i >= n_prompt_tiles)
        def _():
            os_ref[...] = out

    _row_pipeline(i, n_steps, dest_hbm, yr3_hbm, idx_smem, idx_sem, ybuf, row_sem, rows, compute)


def _combine(dest, yr3, tg, h, lg, lb, n_prompt, tm):
    m = h.shape[0]
    npt = n_prompt // tm
    row = lambda w: pl.BlockSpec((tm, w), lambda i: (i, 0))
    par = lambda a: pl.BlockSpec(a.shape, lambda i: (0, 0))
    return pl.pallas_call(
        functools.partial(_combine_kernel, tm=tm, n_prompt_tiles=npt, n_steps=m // tm),
        out_shape=(jax.ShapeDtypeStruct((n_prompt, D_MODEL), F32),
                   jax.ShapeDtypeStruct((m - n_prompt, D_MODEL), F32)),
        grid=(m // tm,),
        in_specs=[pl.BlockSpec(memory_space=pl.ANY), pl.BlockSpec(memory_space=pl.ANY),
                  row(LANES), row(D_MODEL), par(lg), par(lb)],
        out_specs=(pl.BlockSpec((tm, D_MODEL), lambda i: (jnp.minimum(i, npt - 1), 0)),
                   pl.BlockSpec((tm, D_MODEL), lambda i: (jnp.maximum(i - npt, 0), 0))),
        scratch_shapes=_fetch_scratch(tm * TOP_K),
        compiler_params=_cparams(1),
        name="combine",
    )(dest.reshape(m // tm, tm * TOP_K), yr3, tg, h, lg, lb)


def _rope_tables(pos):
    half = ROPE_DIM // 2
    inv = ROPE_THETA ** (-jnp.arange(half, dtype=F32) / half)
    ang = pos.astype(F32)[:, None] * inv[None, :]
    cos, sin = jnp.cos(ang), jnp.sin(ang)
    return jnp.concatenate([cos] * 4, axis=1), jnp.concatenate([-sin, sin] * 2, axis=1)


def _pad_cols(w, width):
    return jnp.pad(w, ((0, 0), (0, width - w.shape[1])))


def _routing(top_i, rank, counts, n_blocks, tb):
    flat_e = top_i.reshape(-1)
    nblk_e = (counts + tb - 1) // tb
    pend = jnp.cumsum(nblk_e)
    pstart = (pend - nblk_e) * tb
    onehot = flat_e[:, None] == jnp.arange(N_EXPERTS, dtype=jnp.int32)[None, :]
    dest = (jnp.sum(jnp.where(onehot, pstart[None, :], 0), axis=1) + rank.reshape(-1)).astype(jnp.int32)
    n_used = pend[-1].astype(jnp.int32)
    pad_start = jnp.concatenate([(pstart + counts).astype(jnp.int32), n_used.reshape(1)])
    bidx = jnp.minimum(jnp.arange(n_blocks, dtype=jnp.int32), n_used - 1)
    blk_e = jnp.minimum(jnp.sum((pend[None, :] <= bidx[:, None]).astype(jnp.int32), axis=1), N_EXPERTS - 1)
    return dest, pad_start, blk_e, n_used.reshape(1)


def kernel(x_prompt, x_sample, cache_kv_latent, cache_k_rope, state_ssm, state_conv, page_table, w_in, ssd_conv_w, ssd_conv_b, ssd_dt_bias, ssd_a_log, ssd_d, ssd_norm_g, mla_q_norm_g, w_q_b, mla_kv_norm_g, w_kv_b, w_branch_a, w_branch_b, w_out, ln1_g, ln1_b, router_w, router_b, exp_w_gu, exp_b_gu, exp_w_down, exp_b_down, ln2_g, ln2_b):
    b_p, t_p, _ = x_prompt.shape
    b_s, t_s, _ = x_sample.shape
    n_pages = page_table.shape[1]
    page = cache_kv_latent.shape[2]
    past_len = n_pages * page
    n_p = b_p * t_p
    n_s = b_s * t_s
    n = n_p + n_s
    assert DEPTH == 1 and w_in.shape[0] == 1
    assert n_p % PROJ_TILE == 0 and t_p % ROW_TILE == 0 and n_s % ROW_TILE == 0
    assert t_p % (2 * ATTN_TQ) == 0 and t_p % SSD_CHUNK == 0
    assert t_s == SUBLANES and n_pages % (2 * PAGE_GROUP) == 0 and page == LANES

    wi = w_in[0]
    seg = lambda k: wi[:, IN_OFFS[k]:IN_OFFS[k + 1]]
    half = ROPE_DIM // 2
    w_kr = seg(2)
    w_kr_rot = jnp.concatenate([w_kr[:, half:], w_kr[:, :half]], axis=1)
    w_mla = jnp.concatenate([seg(0), seg(1), _pad_cols(w_kr, LANES), _pad_cols(w_kr_rot, LANES),
                             _pad_cols(seg(5), LANES)], axis=1).astype(BF16)
    w_z = seg(3).astype(BF16)
    w_xbc = seg(4).astype(BF16)
    w_gates = jnp.concatenate([seg(6), seg(7)], axis=1).astype(BF16)
    wq3 = w_q_b[0].reshape(Q_LORA, MLA_HEADS, QK_NOPE + ROPE_DIM)
    wq_nope = wq3[:, :, :QK_NOPE].reshape(Q_LORA, MLA_HEADS * QK_NOPE)
    wq_rope = wq3[:, :, QK_NOPE:]
    wq_rope_rot = jnp.concatenate([wq_rope[..., half:], wq_rope[..., :half]], axis=-1)
    wq = jnp.concatenate([wq_nope, wq_rope.reshape(Q_LORA, -1), wq_rope_rot.reshape(Q_LORA, -1)],
                         axis=1).astype(BF16)
    wkv3 = w_kv_b[0].reshape(KV_LORA, MLA_HEADS, QK_NOPE + V_HEAD)
    wuk = jnp.transpose(wkv3[:, :, :QK_NOPE], (1, 2, 0)).astype(BF16)
    wuv = jnp.transpose(wkv3[:, :, QK_NOPE:], (1, 0, 2)).astype(BF16)
    qg = mla_q_norm_g[0].reshape(1, Q_LORA)
    kvg = mla_kv_norm_g[0].reshape(1, KV_LORA)
    cw = ssd_conv_w[0]
    cb = ssd_conv_b[0].reshape(1, CONV_DIM)
    dtb = _pad_cols(ssd_dt_bias[0].reshape(1, SSD_HEADS), LANES)
    a_neg = _pad_cols(-jnp.exp(ssd_a_log[0].astype(F32)).reshape(1, SSD_HEADS), LANES)
    d_x = jnp.repeat(ssd_d[0].astype(F32), SSD_HEAD_DIM).reshape(1, D_INNER)
    e_mat = jnp.asarray(np.arange(LANES)[:, None] == np.arange(D_INNER)[None, :] // SSD_HEAD_DIM, BF16)
    ng = ssd_norm_g[0].reshape(1, D_INNER)
    wa = w_branch_a[0].astype(BF16)
    wb = w_branch_b[0].astype(BF16)
    wo = w_out[0].astype(BF16)
    rw = _pad_cols(router_w[0], LANES)
    rwh = rw.astype(BF16)
    rwl = (rw - rwh.astype(F32)).astype(BF16)
    rb = jnp.concatenate([router_b[0].reshape(1, N_EXPERTS),
                          jnp.full((1, LANES - N_EXPERTS), NEG, F32)], axis=1)
    bgu = jnp.concatenate([exp_b_gu[0][:, 0::2], exp_b_gu[0][:, 1::2]], axis=1).reshape(N_EXPERTS, 1, 2 * D_FF)
    bdn = exp_b_down[0].reshape(N_EXPERTS, 1, D_MODEL)
    row2 = lambda v: v[0].reshape(1, D_MODEL)

    cos_p, sin_p = _rope_tables(jnp.arange(t_p, dtype=jnp.int32))
    cos_s, sin_s = _rope_tables(jnp.tile(past_len + jnp.arange(t_s, dtype=jnp.int32), b_s))

    xp = x_prompt.reshape(n_p, D_MODEL)
    xs = x_sample.reshape(n_s, D_MODEL)
    qlat_p, qrope_p, c_p, cbf_p, kr_p, krbf_p, dt_p = _mla_in(xp, w_mla, qg, kvg, wq, wuk, cos_p, sin_p, ROW_TILE)
    qlat_s, qrope_s, c_s, _, kr_s, _, dt_s = _mla_in(xs, w_mla, qg, kvg, wq, wuk, cos_s, sin_s, ROW_TILE)
    z_p, xbc_p, gates_p = (_proj(xp, w, PROJ_TILE) for w in (w_z, w_xbc, w_gates))
    z_s, xbc_s, gates_s = (_proj(xs, w, ROW_TILE) for w in (w_z, w_xbc, w_gates))

    conv0_p = jnp.zeros((b_p, SUBLANES, CONV_DIM), F32)
    conv0_s = jnp.pad(state_conv[0], ((0, 0), (SUBLANES - (D_CONV - 1), 0), (0, 0)))
    ya_p, ssm_p, ctail_p = _ssd(xbc_p, z_p, dt_p, conv0_p, None, cw, cb, dtb, a_neg, d_x, ng, e_mat,
                                nb=b_p, nc=t_p // SSD_CHUNK, t_in=SSD_CHUNK, chunk=SSD_CHUNK, y_dtype=BF16)
    ya_s, ssm_s, ctail_s = _ssd(xbc_s, z_s, dt_s, conv0_s, state_ssm[0].reshape(b_s, D_INNER, D_STATE),
                                cw, cb, dtb, a_neg, d_x, ng, e_mat,
                                nb=b_s, nc=1, t_in=t_s, chunk=SAMPLE_CHUNK, y_dtype=F32)
    conv_p = ctail_p[:, SUBLANES - (D_CONV - 1):, :]
    conv_s = ctail_s[:, SUBLANES - (D_CONV - 1):, :]

    o_p = _attn_prompt(qlat_p, qrope_p, cbf_p, krbf_p, wuv, nb=b_p, t=t_p, tq=ATTN_TQ)
    ql_s = jnp.transpose(qlat_s.reshape(MLA_HEADS, b_s, t_s, KV_LORA), (1, 0, 2, 3))
    qr_s = jnp.transpose(qrope_s.reshape(MLA_HEADS, b_s, t_s, ROPE_DIM), (1, 0, 2, 3))
    cnew_s = jnp.pad(c_s.reshape(b_s, t_s, KV_LORA), ((0, 0), (0, LANES - t_s), (0, 0)))
    krnew_s = jnp.swapaxes(jnp.pad(kr_s.reshape(b_s, t_s, ROPE_DIM), ((0, 0), (0, LANES - t_s), (0, 0))), 1, 2)
    olat_s = _attn_sample(page_table, ql_s.reshape(b_s, MLA_HEADS * t_s, KV_LORA),
                          qr_s.reshape(b_s, MLA_HEADS * t_s, ROPE_DIM), cnew_s, krnew_s,
                          cache_kv_latent[0], jnp.swapaxes(cache_k_rope[0], 1, 2),
                          t_new=t_s, gp=PAGE_GROUP)
    olat_s = jnp.transpose(olat_s.reshape(b_s, MLA_HEADS, t_s, KV_LORA), (1, 0, 2, 3)).reshape(MLA_HEADS, n_s, KV_LORA)
    o_s = _head_proj(olat_s, wuv)

    h, h3, top_i, top_g, rank, counts = _merge(ya_p, ya_s.astype(BF16), o_p, o_s, gates_p, gates_s, xp, xs, wa, wb, wo,
                                 row2(ln1_g), row2(ln1_b), rwh, rwl, rb, ROW_TILE)

    tb = MOE_TB
    n_blocks = -(-(n * TOP_K + N_EXPERTS * (tb - 1)) // tb)
    dest, pad_start, blk_e, n_used = _routing(top_i[:, :TOP_K], rank[:, :TOP_K], counts[0, :N_EXPERTS], n_blocks, tb)
    xr3 = _dispatch(pad_start, dest, h3, (n_blocks + 1) * tb, COMBINE_TILE, tb)
    yr3 = _ffn(blk_e, n_used, xr3, n_blocks, exp_w_gu[0], bgu, exp_w_down[0], bdn, tb)
    y_p, y_s = _combine(dest, yr3, top_g, h, row2(ln2_g), row2(ln2_b), n_p, COMBINE_TILE)

    return (y_p.reshape(b_p, t_p, D_MODEL), y_s.reshape(b_s, t_s, D_MODEL),
            c_p.reshape(1, b_p, t_p, KV_LORA), kr_p.reshape(1, b_p, t_p, ROPE_DIM),
            ssm_p.reshape(1, b_p, SSD_HEADS, SSD_HEAD_DIM, D_STATE), conv_p[None],
            c_s.reshape(1, b_s, t_s, KV_LORA), kr_s.reshape(1, b_s, t_s, ROPE_DIM),
            ssm_s.reshape(1, b_s, SSD_HEADS, SSD_HEAD_DIM, D_STATE), conv_s[None])
```

```python
import functools

import jax
import jax.numpy as jnp
import numpy as np
from jax import lax
from jax.experimental import pallas as pl
from jax.experimental.pallas import tpu as pltpu

F32 = jnp.float32
BF16 = jnp.bfloat16

D_MODEL = 1024
D_INNER = 2048
SSD_HEAD_DIM = 64
SSD_HEADS = 32
SSD_GROUPS = 4
HEADS_PER_GROUP = 8
D_STATE = 128
D_CONV = 4
CONV_DIM = D_INNER + 2 * SSD_GROUPS * D_STATE
SSD_CHUNK = 128
MLA_HEADS = 8
Q_LORA = 256
KV_LORA = 256
QK_NOPE = 128
ROPE_DIM = 64
V_HEAD = 128
ROPE_THETA = 10000.0
ATTN_SCALE = (QK_NOPE + ROPE_DIM) ** -0.5
N_EXPERTS = 32
TOP_K = 4
D_FF = 1024
SWIGLU_ALPHA = 1.702
SWIGLU_LIMIT = 7.0
DEPTH = 1
DN_ALPHA = (2 * DEPTH) ** 0.25
LN_EPS = 1e-5
RMS_EPS = 1e-6
IN_SIZES = (Q_LORA, KV_LORA, ROPE_DIM, D_INNER, CONV_DIM, SSD_HEADS, D_MODEL, D_MODEL)
IN_OFFS = tuple(int(v) for v in np.cumsum((0,) + IN_SIZES))

LANES = 128
SUBLANES = 8
VMEM_LIMIT = 56 * 1024 * 1024

NEG = -1e30

ROW_TILE = 256
PROJ_TILE = 512
ATTN_TQ = 256
ATTN_HEAD_GROUPS = 2
PAGE_GROUP = 64
SAMPLE_CHUNK = 32
MOE_TB = 256
COMBINE_TILE = 256
MLA_A_COLS = 896


def _cparams(n_axes):
    return pltpu.CompilerParams(dimension_semantics=("arbitrary",) * n_axes,
                                vmem_limit_bytes=VMEM_LIMIT)


def _dot(a, b):
    return jnp.dot(a, b, preferred_element_type=F32)


def _dot_nt(a, b):
    return lax.dot_general(a, b, (((1,), (1,)), ((), ())), preferred_element_type=F32)


def _split3(x):
    hi = x.astype(BF16)
    r1 = x - hi.astype(F32)
    mid = r1.astype(BF16)
    lo = (r1 - mid.astype(F32)).astype(BF16)
    return hi, mid, lo


def _sigmoid(x):
    return 1.0 / (1.0 + jnp.exp(-x))


def _layer_norm(x, g, b):
    mu = jnp.mean(x, axis=-1, keepdims=True)
    xc = x - mu
    var = jnp.mean(xc * xc, axis=-1, keepdims=True)
    return xc * lax.rsqrt(var + LN_EPS) * g + b


def _rms_norm(x, g):
    return x * lax.rsqrt(jnp.mean(x * x, axis=-1, keepdims=True) + RMS_EPS) * g


def _proj_kernel(x_ref, w_ref, o_ref):
    o_ref[...] = _dot(x_ref[...].astype(BF16), w_ref[...])


def _proj(x, w, tm):
    m, k = x.shape
    n = w.shape[1]
    return pl.pallas_call(
        _proj_kernel,
        out_shape=jax.ShapeDtypeStruct((m, n), F32),
        grid=(m // tm,),
        in_specs=[pl.BlockSpec((tm, k), lambda i: (i, 0)),
                  pl.BlockSpec((k, n), lambda i: (0, 0))],
        out_specs=pl.BlockSpec((tm, n), lambda i: (i, 0)),
        compiler_params=_cparams(1),
        name="proj",
    )(x, w)


def _mla_in_kernel(x_ref, wa_ref, qg_ref, kvg_ref, wq_ref, wuk_ref, cos_ref, sin_ref,
                   qlat_ref, qrope_ref, c_ref, cbf_ref, kr_ref, krbf_ref, dt_ref):
    xb = x_ref[...].astype(BF16)
    pa = _dot(xb, wa_ref[...])
    q_a = pa[:, 0:256]
    kv_a = pa[:, 256:512]
    kr_raw = pa[:, 512:640]
    kr_rot = pa[:, 640:768]
    dt_ref[...] = pa[:, 768:896]
    cos = cos_ref[...]
    sin = sin_ref[...]
    c = _rms_norm(kv_a, kvg_ref[...])
    c_ref[...] = c
    cbf_ref[...] = c.astype(BF16)
    kr = kr_raw * cos + kr_rot * sin
    kr_ref[...] = kr[:, 0:ROPE_DIM]
    krbf_ref[...] = kr[:, 0:ROPE_DIM].astype(BF16)
    qn = _rms_norm(q_a, qg_ref[...]).astype(BF16)
    qq = _dot(qn, wq_ref[...])
    cos4 = jnp.concatenate([cos] * 4, axis=1)
    sin4 = jnp.concatenate([sin] * 4, axis=1)
    qr = (qq[:, 1024:1536] * cos4 + qq[:, 1536:2048] * sin4) * ATTN_SCALE
    for h in range(MLA_HEADS):
        qrope_ref[h] = qr[:, h * ROPE_DIM:(h + 1) * ROPE_DIM].astype(BF16)
        nope = qq[:, h * QK_NOPE:(h + 1) * QK_NOPE].astype(BF16)
        qlat_ref[h] = (_dot(nope, wuk_ref[h]) * ATTN_SCALE).astype(BF16)


def _mla_in(x, wa, qg, kvg, wq, wuk, cos_t, sin_t, tm):
    m = x.shape[0]
    tab_blocks = cos_t.shape[0] // tm
    row = lambda w: pl.BlockSpec((tm, w), lambda i: (i, 0))
    tab = pl.BlockSpec((tm, LANES), lambda i: (i % tab_blocks, 0))
    full2 = lambda a: pl.BlockSpec(a.shape, lambda i: (0, 0))
    full3 = lambda a: pl.BlockSpec(a.shape, lambda i: (0, 0, 0))
    return pl.pallas_call(
        _mla_in_kernel,
        out_shape=(jax.ShapeDtypeStruct((MLA_HEADS, m, KV_LORA), BF16),
                   jax.ShapeDtypeStruct((MLA_HEADS, m, ROPE_DIM), BF16),
                   jax.ShapeDtypeStruct((m, KV_LORA), F32),
                   jax.ShapeDtypeStruct((m, KV_LORA), BF16),
                   jax.ShapeDtypeStruct((m, ROPE_DIM), F32),
                   jax.ShapeDtypeStruct((m, ROPE_DIM), BF16),
                   jax.ShapeDtypeStruct((m, LANES), F32)),
        grid=(m // tm,),
        in_specs=[row(D_MODEL), full2(wa), full2(qg), full2(kvg), full2(wq), full3(wuk), tab, tab],
        out_specs=(pl.BlockSpec((MLA_HEADS, tm, KV_LORA), lambda i: (0, i, 0)),
                   pl.BlockSpec((MLA_HEADS, tm, ROPE_DIM), lambda i: (0, i, 0)),
                   row(KV_LORA), row(KV_LORA), row(ROPE_DIM), row(ROPE_DIM), row(LANES)),
        compiler_params=_cparams(1),
        name="mla_in",
    )(x, wa, qg, kvg, wq, wuk, cos_t, sin_t)


def _expand_heads(v, e_ref, terms):
    out = None
    rem = v
    for _ in range(terms):
        part = rem.astype(BF16)
        rem = rem - part.astype(F32)
        d = _dot(part, e_ref[...])
        out = d if out is None else out + d
    return out


def _conv_silu(ext, r0, rows, cw_ref, cb_ref, out, out_r0):
    for j in range(CONV_DIM // LANES):
        sl = slice(j * LANES, (j + 1) * LANES)
        acc = cb_ref[:, sl]
        for k in range(D_CONV):
            acc = acc + ext[pl.ds(r0 + SUBLANES - (D_CONV - 1) + k, rows), sl] * cw_ref[k:k + 1, sl]
        out[out_r0:out_r0 + rows, sl] = acc * _sigmoid(acc)


def _ssd_kernel(*refs, t_in, L, has_h0):
    n_pairs = SSD_HEADS // 2
    refs = list(refs)
    hT = refs[-n_pairs:]
    it = iter(refs[:-n_pairs])
    xbc_ref, z_ref, dt_ref, conv0_ref = next(it), next(it), next(it), next(it)
    h0_ref = next(it) if has_h0 else None
    (cw_ref, cb_ref, dtb_ref, a_ref, dx_ref, ng_ref, e_ref, y_ref, hfin_ref, ctail_ref,
     ext, xcv, ybuf, acs_s, acsT_s, dtT_s, eax_s, wtx_s) = it
    c = pl.program_id(1)
    nc = pl.num_programs(1)

    @pl.when(c == 0)
    def _():
        ext[0:SUBLANES, :] = conv0_ref[0]
        if t_in < L:
            ext[SUBLANES:SUBLANES + L, :] = jnp.zeros((L, CONV_DIM), F32)
        for j in range(n_pairs):
            if has_h0:
                hT[j][...] = h0_ref[0, j * LANES:(j + 1) * LANES, :].T
            else:
                hT[j][...] = jnp.zeros((D_STATE, LANES), F32)

    ext[SUBLANES:SUBLANES + t_in, :] = xbc_ref[...]

    @pl.when(c == nc - 1)
    def _():
        ctail_ref[0] = ext[t_in:t_in + SUBLANES, :]

    _conv_silu(ext, 0, L, cw_ref, cb_ref, xcv, 0)
    ext[0:SUBLANES, :] = ext[L:L + SUBLANES, :]

    row = lax.broadcasted_iota(jnp.int32, (L, L), 0)
    col = lax.broadcasted_iota(jnp.int32, (L, L), 1)
    tri = row >= col
    tri_b = jnp.where(tri, 1.0, 0.0).astype(BF16)
    if t_in < L:
        dt_raw = jnp.concatenate([dt_ref[...], jnp.zeros((L - t_in, LANES), F32)], axis=0)
    else:
        dt_raw = dt_ref[...]
    xs = dt_raw + dtb_ref[...]
    dtv = jnp.maximum(xs, 0.0) + jnp.log1p(jnp.exp(-jnp.abs(xs)))
    if t_in < L:
        dtv = jnp.where(lax.broadcasted_iota(jnp.int32, (L, LANES), 0) < t_in, dtv, 0.0)
    da = dtv * a_ref[...]
    hi, mid, lo = _split3(da)
    acs = _dot(tri_b, hi) + _dot(tri_b, mid) + _dot(tri_b, lo)
    acs_s[...] = acs
    acsT_s[...] = acs.T
    dtT_s[...] = dtv.T
    eax_s[...] = _expand_heads(jnp.exp(acs), e_ref, 3)
    wtx_s[...] = _expand_heads(jnp.exp(acs[L - 1:L, :] - acs) * dtv, e_ref, 2)

    low_half = lax.broadcasted_iota(jnp.int32, (L, LANES), 1) < SSD_HEAD_DIM
    pairs_per_group = HEADS_PER_GROUP // 2
    for g in range(SSD_GROUPS):
        bo = D_INNER + g * D_STATE
        co = D_INNER + SSD_GROUPS * D_STATE + g * D_STATE
        bm = xcv[:, bo:bo + D_STATE]
        cmb = xcv[:, co:co + D_STATE].astype(BF16)
        cb = _dot_nt(cmb, bm.astype(BF16))
        bmT = bm.T.astype(BF16)
        for k in range(pairs_per_group):
            kk = g * pairs_per_group + k
            tl = slice(kk * LANES, (kk + 1) * LANES)
            x2 = xcv[:, tl]
            ws = []
            for idx in (2 * kk, 2 * kk + 1):
                seg = acs_s[:, idx:idx + 1] - acsT_s[idx:idx + 1, :]
                w = cb * jnp.exp(jnp.where(tri, seg, NEG)) * dtT_s[idx:idx + 1, :]
                ws.append(w.astype(BF16))
            w2 = jnp.concatenate(ws, axis=1)
            xbd = jnp.concatenate([jnp.where(low_half, x2, 0.0).astype(BF16),
                                   jnp.where(low_half, 0.0, x2).astype(BF16)], axis=0)
            yd = _dot(w2, xbd)
            hp = hT[kk][...]
            yo = _dot(cmb, hp.astype(BF16))
            ybuf[:, tl] = yd + yo * eax_s[:, tl] + dx_ref[:, tl] * x2
            s = _dot(bmT, (x2 * wtx_s[:, tl]).astype(BF16))
            hT[kk][...] = eax_s[L - 1:L, tl] * hp + s

    gw = D_INNER // SSD_GROUPS
    for g in range(SSD_GROUPS):
        sl = slice(g * gw, (g + 1) * gw)
        zg = z_ref[:, sl]
        v = ybuf[0:t_in, sl] * (zg * _sigmoid(zg))
        ms = jnp.mean(v * v, axis=-1, keepdims=True)
        y_ref[:, sl] = (v * lax.rsqrt(ms + RMS_EPS) * ng_ref[:, sl]).astype(y_ref.dtype)

    @pl.when(c == nc - 1)
    def _():
        for j in range(n_pairs):
            hfin_ref[0, j * LANES:(j + 1) * LANES, :] = hT[j][...].T


def _ssd(xbc, z, dt, conv0, h0, cw, cb, dtb, a_neg, d_x, ng, e_mat, *, nb, nc, t_in, chunk, y_dtype):
    L = chunk
    has_h0 = h0 is not None
    rowspec = lambda w: pl.BlockSpec((t_in, w), lambda b, c: (b * nc + c, 0))
    par = lambda a: pl.BlockSpec(a.shape, lambda b, c: (0, 0))
    per_seq = lambda r, w: pl.BlockSpec((1, r, w), lambda b, c: (b, 0, 0))
    in_specs = [rowspec(CONV_DIM), rowspec(D_INNER), rowspec(LANES), per_seq(SUBLANES, CONV_DIM)]
    args = [xbc, z, dt, conv0]
    if has_h0:
        in_specs.append(per_seq(D_INNER, D_STATE))
        args.append(h0)
    params = [cw, cb, dtb, a_neg, d_x, ng, e_mat]
    in_specs += [par(p) for p in params]
    args += params
    return pl.pallas_call(
        functools.partial(_ssd_kernel, t_in=t_in, L=L, has_h0=has_h0),
        out_shape=(jax.ShapeDtypeStruct((nb * nc * t_in, D_INNER), y_dtype),
                   jax.ShapeDtypeStruct((nb, D_INNER, D_STATE), F32),
                   jax.ShapeDtypeStruct((nb, SUBLANES, CONV_DIM), F32)),
        grid=(nb, nc),
        in_specs=in_specs,
        out_specs=(rowspec(D_INNER), per_seq(D_INNER, D_STATE), per_seq(SUBLANES, CONV_DIM)),
        scratch_shapes=[pltpu.VMEM((L + 2 * SUBLANES, CONV_DIM), F32),
                        pltpu.VMEM((L, CONV_DIM), F32),
                        pltpu.VMEM((L, D_INNER), F32),
                        pltpu.VMEM((L, LANES), F32), pltpu.VMEM((LANES, L), F32), pltpu.VMEM((LANES, L), F32),
                        pltpu.VMEM((L, D_INNER), F32), pltpu.VMEM((L, D_INNER), F32)]
                       + [pltpu.VMEM((D_STATE, LANES), F32)] * (SSD_HEADS // 2),
        compiler_params=_cparams(2),
        name="ssd",
    )(*args)


def _softmax_step(s, kc, m_scr, l_scr, acc_scr):
    reps = s.shape[1] // LANES
    m_prev = m_scr[...]
    m_new = jnp.maximum(m_prev, jnp.max(s, axis=1, keepdims=True))
    alpha = jnp.exp(m_prev - m_new)
    p = jnp.exp(s - (jnp.concatenate([m_new] * reps, axis=1) if reps > 1 else m_new))
    l_scr[...] = alpha * l_scr[...] + jnp.sum(p, axis=1, keepdims=True)
    acc_scr[...] = acc_scr[...] * jnp.concatenate([alpha] * (KV_LORA // LANES), axis=1) + _dot(p.astype(BF16), kc)
    m_scr[...] = m_new


def _attn_prompt_kernel(qlat_ref, qrope_ref, c_ref, kr_ref, wuv_ref, o_ref, m_scr, l_scr, acc_scr, *, tq):
    qi = pl.program_id(1)
    R = MLA_HEADS * tq

    def scores(block, width, g=0, groups=1):
        start = pl.multiple_of(block * width, width)
        kc = c_ref[pl.ds(start, width), :]
        kr = kr_ref[pl.ds(start, width), :]
        hpg = MLA_HEADS // groups
        q = qlat_ref[g * hpg:(g + 1) * hpg].reshape(hpg * tq, KV_LORA)
        qr = qrope_ref[g * hpg:(g + 1) * hpg].reshape(hpg * tq, ROPE_DIM)
        return _dot_nt(q, kc) + _dot_nt(qr, kr), kc

    s, kc = scores(qi, tq)
    t = lax.broadcasted_iota(jnp.int32, (R, tq), 0) & (tq - 1)
    kcol = lax.broadcasted_iota(jnp.int32, (R, tq), 1)
    s = jnp.where(kcol <= t, s, NEG)
    m0 = jnp.max(s, axis=1, keepdims=True)
    p = jnp.exp(s - m0)
    m_scr[...] = jnp.broadcast_to(m0, m_scr.shape)
    l_scr[...] = jnp.broadcast_to(jnp.sum(p, axis=1, keepdims=True), l_scr.shape)
    acc_scr[...] = _dot(p.astype(BF16), kc)

    def body(j, carry):
        rg = R // ATTN_HEAD_GROUPS
        for g in range(ATTN_HEAD_GROUPS):
            rows = pl.ds(g * rg, rg)
            s2, kc2 = scores(j, 2 * tq, g, ATTN_HEAD_GROUPS)
            _softmax_step(s2, kc2, m_scr.at[rows], l_scr.at[rows], acc_scr.at[rows])
        return carry

    lax.fori_loop(0, qi // 2, body, 0)

    @pl.when(qi % 2 == 1)
    def _():
        s1, kc1 = scores(qi - 1, tq)
        _softmax_step(s1, kc1, m_scr, l_scr, acc_scr)

    inv = 1.0 / l_scr[...]
    o_lat = acc_scr[...] * jnp.concatenate([inv] * (KV_LORA // LANES), axis=1)
    for h in range(MLA_HEADS):
        oh = _dot(o_lat[h * tq:(h + 1) * tq, :].astype(BF16), wuv_ref[h])
        o_ref[:, h * V_HEAD:(h + 1) * V_HEAD] = oh.astype(o_ref.dtype)


def _attn_prompt(qlat, qrope, cbf, krbf, wuv, *, nb, t, tq):
    nq = t // tq
    R = MLA_HEADS * tq
    return pl.pallas_call(
        functools.partial(_attn_prompt_kernel, tq=tq),
        out_shape=jax.ShapeDtypeStruct((nb * t, MLA_HEADS * V_HEAD), BF16),
        grid=(nb, nq),
        in_specs=[pl.BlockSpec((MLA_HEADS, tq, KV_LORA), lambda b, i: (0, b * nq + i, 0)),
                  pl.BlockSpec((MLA_HEADS, tq, ROPE_DIM), lambda b, i: (0, b * nq + i, 0)),
                  pl.BlockSpec((t, KV_LORA), lambda b, i: (b, 0)),
                  pl.BlockSpec((t, ROPE_DIM), lambda b, i: (b, 0)),
                  pl.BlockSpec(wuv.shape, lambda b, i: (0, 0, 0))],
        out_specs=pl.BlockSpec((tq, MLA_HEADS * V_HEAD), lambda b, i: (b * nq + i, 0)),
        scratch_shapes=[pltpu.VMEM((R, LANES), F32), pltpu.VMEM((R, LANES), F32),
                        pltpu.VMEM((R, KV_LORA), F32)],
        compiler_params=_cparams(2),
        name="attn_prompt",
    )(qlat, qrope, cbf, krbf, wuv)


def _attn_sample_kernel(pt_ref, qlat_ref, qrope_ref, cnew_ref, krnew_ref, kv_hbm, kr_hbm, o_ref,
                        kvbuf, krbuf, kv_sem, kr_sem, m_scr, l_scr, acc_scr, *, n_groups, gp, t_new):
    b = pl.program_id(0)
    nb = pl.num_programs(0)
    R = qlat_ref.shape[1]
    page = kvbuf.shape[2]
    q = qlat_ref[0]
    qr = qrope_ref[0]

    def start_group(bb, g, slot):
        for p in range(gp):
            pid = pt_ref[bb, g * gp + p]
            pltpu.make_async_copy(kv_hbm.at[pid], kvbuf.at[slot, p], kv_sem.at[slot]).start()
            pltpu.make_async_copy(kr_hbm.at[pid], krbuf.at[slot, p], kr_sem.at[slot]).start()

    def wait_group(slot):
        pltpu.make_async_copy(kvbuf.at[slot], kvbuf.at[slot], kv_sem.at[slot]).wait()
        pltpu.make_async_copy(krbuf.at[slot], krbuf.at[slot], kr_sem.at[slot]).wait()

    @pl.when(b == 0)
    def _():
        start_group(0, 0, 0)

    kc = cnew_ref[0].astype(BF16)
    kr = krnew_ref[0].astype(BF16)
    s = _dot_nt(q, kc) + _dot(qr, kr)
    t = lax.broadcasted_iota(jnp.int32, (R, LANES), 0) & (t_new - 1)
    kcol = lax.broadcasted_iota(jnp.int32, (R, LANES), 1)
    s = jnp.where(kcol <= t, s, NEG)
    m0 = jnp.max(s, axis=1, keepdims=True)
    p0 = jnp.exp(s - m0)
    m_scr[...] = jnp.broadcast_to(m0, m_scr.shape)
    l_scr[...] = jnp.broadcast_to(jnp.sum(p0, axis=1, keepdims=True), l_scr.shape)
    acc_scr[...] = _dot(p0.astype(BF16), kc)

    for g in range(n_groups):
        slot = g % 2
        if g + 1 < n_groups:
            start_group(b, g + 1, 1 - slot)
        else:
            @pl.when(b + 1 < nb)
            def _():
                start_group(b + 1, 0, 1 - slot)
        wait_group(slot)
        kc = kvbuf[slot].reshape(gp * page, KV_LORA).astype(BF16)
        kr = jnp.concatenate([krbuf[slot, p].astype(BF16) for p in range(gp)], axis=1)
        s = _dot_nt(q, kc) + _dot(qr, kr)
        _softmax_step(s, kc, m_scr, l_scr, acc_scr)

    inv = 1.0 / l_scr[...]
    o_ref[0] = acc_scr[...] * jnp.concatenate([inv] * (KV_LORA // LANES), axis=1)


def _attn_sample(page_table, qlat, qrope, cnew, krnew, cache_kv, cache_kr, *, t_new, gp):
    nb, n_pages = page_table.shape
    n_groups = n_pages // gp
    R = qlat.shape[1]
    page = cache_kv.shape[1]
    per_b = lambda r, w: pl.BlockSpec((1, r, w), lambda b, pt: (b, 0, 0))
    grid_spec = pltpu.PrefetchScalarGridSpec(
        num_scalar_prefetch=1,
        grid=(nb,),
        in_specs=[per_b(R, KV_LORA), per_b(R, ROPE_DIM), per_b(LANES, KV_LORA), per_b(ROPE_DIM, LANES),
                  pl.BlockSpec(memory_space=pl.ANY), pl.BlockSpec(memory_space=pl.ANY)],
        out_specs=per_b(R, KV_LORA),
        scratch_shapes=[pltpu.VMEM((2, gp, page, KV_LORA), F32), pltpu.VMEM((2, gp, ROPE_DIM, page), F32),
                        pltpu.SemaphoreType.DMA((2,)), pltpu.SemaphoreType.DMA((2,)),
                        pltpu.VMEM((R, LANES), F32), pltpu.VMEM((R, LANES), F32),
                        pltpu.VMEM((R, KV_LORA), F32)],
    )
    return pl.pallas_call(
        functools.partial(_attn_sample_kernel, n_groups=n_groups, gp=gp, t_new=t_new),
        out_shape=jax.ShapeDtypeStruct((nb, R, KV_LORA), F32),
        grid_spec=grid_spec,
        compiler_params=_cparams(1),
        name="attn_sample",
    )(page_table, qlat, qrope, cnew, krnew, cache_kv, cache_kr)


def _head_proj_kernel(x_ref, w_ref, o_ref):
    o_ref[...] = _dot(x_ref[0].astype(BF16), w_ref[0]).astype(o_ref.dtype)


def _head_proj(x, w):
    h, m, k = x.shape
    n = w.shape[2]
    return pl.pallas_call(
        _head_proj_kernel,
        out_shape=jax.ShapeDtypeStruct((m, h * n), BF16),
        grid=(h,),
        in_specs=[pl.BlockSpec((1, m, k), lambda i: (i, 0, 0)),
                  pl.BlockSpec((1, k, n), lambda i: (i, 0, 0))],
        out_specs=pl.BlockSpec((m, n), lambda i: (0, i)),
        compiler_params=_cparams(1),
        name="head_proj",
    )(x, w)


def _merge_kernel(ya_p_ref, ya_s_ref, o_p_ref, o_s_ref, g_p_ref, g_s_ref, u_p_ref, u_s_ref, wa_ref, wb_ref, wo_ref,
                  lg_ref, lb_ref, rwh_ref, rwl_ref, rb_ref,
                  h_ref, h3_ref, ti_ref, tg_ref, rank_ref, cnt_ref, run_ref, *, n_prompt_tiles):
    i = pl.program_id(0)
    is_p = i < n_prompt_tiles
    yn = jnp.where(is_p, ya_p_ref[...], ya_s_ref[...])
    ob = jnp.where(is_p, o_p_ref[...], o_s_ref[...])
    g = jnp.where(is_p, g_p_ref[...], g_s_ref[...])
    u = jnp.where(is_p, u_p_ref[...], u_s_ref[...])
    ya = _dot(yn, wa_ref[...])
    yb = _dot(ob, wb_ref[...])
    merged = _sigmoid(g[:, 0:D_MODEL]) * ya + _sigmoid(g[:, D_MODEL:2 * D_MODEL]) * yb
    mix = _dot(merged.astype(BF16), wo_ref[...])
    h = _layer_norm(DN_ALPHA * u + mix, lg_ref[...], lb_ref[...])
    h_ref[...] = h
    tm = h.shape[0]
    for j in range(D_MODEL // LANES):
        h3_ref[pl.ds(j, tm, stride=SUBLANES), :] = h[:, j * LANES:(j + 1) * LANES]
    hh = h.astype(BF16)
    hl = (h - hh.astype(F32)).astype(BF16)
    logits = _dot(hh, rwh_ref[...]) + _dot(hl, rwh_ref[...]) + _dot(hh, rwl_ref[...]) + rb_ref[...]
    lane = lax.broadcasted_iota(jnp.int32, logits.shape, 1).astype(F32)
    vals = jnp.zeros_like(logits)
    idxs = jnp.zeros_like(logits)
    cur = logits
    picked = []
    for k in range(TOP_K):
        mx = jnp.max(cur, axis=1, keepdims=True)
        ix = jnp.min(jnp.where(cur == mx, lane, float(LANES)), axis=1, keepdims=True)
        vals = jnp.where(lane == k, mx, vals)
        idxs = jnp.where(lane == k, ix, idxs)
        picked.append(lane == ix)
        cur = jnp.where(picked[k], NEG * 2.0, cur)
    v0 = vals[:, 0:1]
    e = jnp.where(lane < TOP_K, jnp.exp(vals - v0), 0.0)
    tg_ref[...] = e / jnp.sum(e, axis=1, keepdims=True)
    ti_ref[...] = idxs.astype(jnp.int32)

    @pl.when(i == 0)
    def _():
        run_ref[...] = jnp.zeros_like(run_ref)

    cnt = jnp.zeros_like(logits)
    for k in range(TOP_K):
        cnt = cnt + jnp.where(picked[k], 1.0, 0.0)
    r_i = lax.broadcasted_iota(jnp.int32, (tm, tm), 0)
    c_i = lax.broadcasted_iota(jnp.int32, (tm, tm), 1)
    before = _dot(jnp.where(r_i > c_i, 1.0, 0.0).astype(BF16), cnt.astype(BF16)) + run_ref[0:1, :]
    ranks = jnp.zeros_like(logits)
    for k in range(TOP_K):
        rk = jnp.sum(jnp.where(picked[k], before, 0.0), axis=1, keepdims=True)
        ranks = jnp.where(lane == k, rk, ranks)
    rank_ref[...] = ranks.astype(jnp.int32)
    run_ref[0:1, :] = run_ref[0:1, :] + jnp.sum(cnt, axis=0, keepdims=True)
    cnt_ref[...] = run_ref[...].astype(jnp.int32)


def _merge(ya_p, ya_s, o_p, o_s, g_p, g_s, u_p, u_s, wa, wb, wo, lg, lb, rwh, rwl, rb, tm):
    m = u_p.shape[0] + u_s.shape[0]
    npt = ya_p.shape[0] // tm
    row = lambda w: pl.BlockSpec((tm, w), lambda i: (i, 0))
    prow = lambda w: pl.BlockSpec((tm, w), lambda i: (jnp.minimum(i, npt - 1), 0))
    srow = lambda w: pl.BlockSpec((tm, w), lambda i: (jnp.maximum(i - npt, 0), 0))
    par = lambda a: pl.BlockSpec(a.shape, lambda i: (0, 0))
    return pl.pallas_call(
        functools.partial(_merge_kernel, n_prompt_tiles=npt),
        out_shape=(jax.ShapeDtypeStruct((m, D_MODEL), F32),
                   jax.ShapeDtypeStruct((m * SUBLANES, LANES), F32),
                   jax.ShapeDtypeStruct((m, LANES), jnp.int32),
                   jax.ShapeDtypeStruct((m, LANES), F32),
                   jax.ShapeDtypeStruct((m, LANES), jnp.int32),
                   jax.ShapeDtypeStruct((SUBLANES, LANES), jnp.int32)),
        grid=(m // tm,),
        in_specs=[prow(D_INNER), srow(D_INNER), prow(D_MODEL), srow(D_MODEL), prow(2 * D_MODEL), srow(2 * D_MODEL),
                  prow(D_MODEL), srow(D_MODEL),
                  par(wa), par(wb), par(wo), par(lg), par(lb), par(rwh), par(rwl), par(rb)],
        out_specs=(row(D_MODEL), pl.BlockSpec((tm * SUBLANES, LANES), lambda i: (i, 0)), row(LANES), row(LANES),
                   row(LANES), pl.BlockSpec((SUBLANES, LANES), lambda i: (0, 0))),
        scratch_shapes=[pltpu.VMEM((SUBLANES, LANES), F32)],
        compiler_params=_cparams(1),
        name="merge",
    )(ya_p, ya_s, o_p, o_s, g_p, g_s, u_p, u_s, wa, wb, wo, lg, lb, rwh, rwl, rb)


DEINT_IN = 2 * LANES


def _deint_perm():
    perm = np.zeros((DEINT_IN, DEINT_IN), np.float32)
    perm[2 * np.arange(LANES), np.arange(LANES)] = 1.0
    perm[2 * np.arange(LANES) + 1, LANES + np.arange(LANES)] = 1.0
    return jnp.asarray(perm, BF16)


def _row_pipeline(i, n_steps, idx_hbm, src_hbm, idx_smem, idx_sem, buf, row_sem, rows, compute):
    def idx_copy(chunk, s):
        return pltpu.make_async_copy(idx_hbm.at[chunk], idx_smem.at[s], idx_sem.at[s])

    def issue_row(s, r):
        src0 = pl.multiple_of(idx_smem[s, r] * SUBLANES, SUBLANES)
        dst0 = pl.multiple_of(r * SUBLANES, SUBLANES)
        pltpu.make_async_copy(src_hbm.at[pl.ds(src0, SUBLANES), :],
                              buf.at[s, pl.ds(dst0, SUBLANES), :], row_sem.at[s]).start()

    @pl.when(i == 0)
    def _():
        idx_copy(0, 0).start()
        idx_copy(0, 0).wait()

        def body(r, carry):
            issue_row(0, r)
            return carry
        lax.fori_loop(0, rows, body, 0, unroll=8)
        if n_steps > 1:
            idx_copy(1, 1).start()

    def step(slot):
        has_next = i + 1 < n_steps

        @pl.when(has_next)
        def _():
            idx_copy(i + 1, 1 - slot).wait()

        @pl.when(i + 2 < n_steps)
        def _():
            idx_copy(i + 2, slot).start()

        pltpu.make_async_copy(src_hbm.at[pl.ds(0, rows * SUBLANES), :], buf.at[slot], row_sem.at[slot]).wait()

        @pl.when(has_next)
        def _():
            for r in range(rows):
                issue_row(1 - slot, r)
            compute(slot)

        @pl.when(i + 1 == n_steps)
        def _():
            compute(slot)

    step(lax.rem(i, 2))


def _fetch_scratch(rows):
    return [pltpu.SMEM((2, rows), jnp.int32), pltpu.SemaphoreType.DMA((2,)),
            pltpu.VMEM((2, rows * SUBLANES, LANES), F32), pltpu.SemaphoreType.DMA((2,))]


def _dispatch_kernel(pad_ref, dest_hbm, h3_ref, xr_hbm, idx_smem, idx_sem, stage, row_sem, zbuf, z_sem,
                     *, tm, tb, min_used):
    i = pl.program_id(0)
    n = pl.num_programs(0)
    rows = tm * TOP_K

    def idx_copy(chunk, s):
        return pltpu.make_async_copy(dest_hbm.at[chunk], idx_smem.at[s], idx_sem.at[s])

    def rows_wait(s):
        span = xr_hbm.at[pl.ds(0, rows * SUBLANES), :]
        pltpu.make_async_copy(span, span, row_sem.at[s]).wait()

    @pl.when(i == 0)
    def _():
        zbuf[...] = jnp.zeros_like(zbuf)
        zero_to = lambda start: pltpu.make_async_copy(zbuf, xr_hbm.at[pl.ds(start, tb * SUBLANES), :], z_sem)
        for e in range(N_EXPERTS):
            zero_to(pl.multiple_of(pad_ref[e] * SUBLANES, SUBLANES)).start()
        for e in range(N_EXPERTS):
            zero_to(0).wait()
        n_used = pad_ref[N_EXPERTS]
        tail = range(min_used, xr_hbm.shape[0] // (tb * SUBLANES))
        for blk in tail:
            pl.when(blk >= n_used)(zero_to(blk * tb * SUBLANES).start)
        for blk in tail:
            pl.when(blk >= n_used)(zero_to(0).wait)
        idx_copy(0, 0).start()

    def step(slot):
        idx_copy(i, slot).wait()

        @pl.when(i + 1 < n)
        def _():
            idx_copy(i + 1, 1 - slot).start()

        stage[slot] = h3_ref[...]
        for r in range(rows):
            dst0 = pl.multiple_of(idx_smem[slot, r] * SUBLANES, SUBLANES)
            pltpu.make_async_copy(stage.at[slot, pl.ds((r // TOP_K) * SUBLANES, SUBLANES), :],
                                  xr_hbm.at[pl.ds(dst0, SUBLANES), :], row_sem.at[slot]).start()

        @pl.when(i >= 1)
        def _():
            rows_wait(1 - slot)

        @pl.when(i == n - 1)
        def _():
            rows_wait(slot)

    step(lax.rem(i, 2))


def _dispatch(pad_start, dest, h3, cap_rows, tm, tb):
    n = h3.shape[0] // SUBLANES
    grid_spec = pltpu.PrefetchScalarGridSpec(
        num_scalar_prefetch=1,
        grid=(n // tm,),
        in_specs=[pl.BlockSpec(memory_space=pl.ANY),
                  pl.BlockSpec((tm * SUBLANES, LANES), lambda i, pad: (i, 0))],
        out_specs=pl.BlockSpec(memory_space=pl.ANY),
        scratch_shapes=[pltpu.SMEM((2, tm * TOP_K), jnp.int32), pltpu.SemaphoreType.DMA((2,)),
                        pltpu.VMEM((2, tm * SUBLANES, LANES), F32), pltpu.SemaphoreType.DMA((2,)),
                        pltpu.VMEM((tb * SUBLANES, LANES), F32), pltpu.SemaphoreType.DMA(())],
    )
    return pl.pallas_call(
        functools.partial(_dispatch_kernel, tm=tm, tb=tb, min_used=-(-(n * TOP_K) // tb)),
        out_shape=jax.ShapeDtypeStruct((cap_rows * SUBLANES, LANES), F32),
        grid_spec=grid_spec,
        compiler_params=_cparams(1),
        name="dispatch",
    )(pad_start, dest.reshape(n // tm, tm * TOP_K), h3)


def _ffn_kernel(be_ref, nu_ref, x_ref, wgu_ref, bgu_ref, wdn_ref, bdn_ref, perm_ref, o_ref,
                wg_s, wu_s, wd_s, *, tb):
    i = pl.program_id(0)
    nu = nu_ref[0]
    new_expert = jnp.logical_or(i == 0, be_ref[i] != be_ref[jnp.maximum(i - 1, 0)])

    @pl.when(jnp.logical_and(new_expert, i < nu))
    def _():
        for t in range(2 * D_FF // DEINT_IN):
            r = _dot(wgu_ref[:, t * DEINT_IN:(t + 1) * DEINT_IN].astype(BF16), perm_ref[...])
            wg_s[:, t * LANES:(t + 1) * LANES] = r[:, 0:LANES].astype(BF16)
            wu_s[:, t * LANES:(t + 1) * LANES] = r[:, LANES:DEINT_IN].astype(BF16)
        wd_s[...] = wdn_ref[...].astype(BF16)

    @pl.when(i < nu)
    def _():
        x = jnp.concatenate([x_ref[pl.ds(j, tb, stride=SUBLANES), :].astype(BF16)
                             for j in range(D_MODEL // LANES)], axis=1)
        bgu = bgu_ref[...]
        g = jnp.minimum(_dot(x, wg_s[...]) + bgu[:, 0:D_FF], SWIGLU_LIMIT)
        u = jnp.clip(_dot(x, wu_s[...]) + bgu[:, D_FF:2 * D_FF], -SWIGLU_LIMIT, SWIGLU_LIMIT)
        hh = (u + 1.0) * (g * _sigmoid(SWIGLU_ALPHA * g))
        out = _dot(hh.astype(BF16), wd_s[...]) + bdn_ref[...]
        for j in range(D_MODEL // LANES):
            o_ref[pl.ds(j, tb, stride=SUBLANES), :] = out[:, j * LANES:(j + 1) * LANES]

    @pl.when(i >= nu)
    def _():
        o_ref[...] = jnp.zeros_like(o_ref)


def _ffn(blk_e, n_used, xr3, nblk, wgu, bgu, wdn, bdn, tb):
    perm = _deint_perm()
    wspec = lambda a: pl.BlockSpec((None,) + a.shape[1:], lambda i, be, nu: (be[i], 0, 0))
    grid_spec = pltpu.PrefetchScalarGridSpec(
        num_scalar_prefetch=2,
        grid=(nblk,),
        in_specs=[pl.BlockSpec((tb * SUBLANES, LANES), lambda i, be, nu: (jnp.minimum(i, nu[0] - 1), 0)),
                  wspec(wgu), wspec(bgu), wspec(wdn), wspec(bdn),
                  pl.BlockSpec(perm.shape, lambda i, be, nu: (0, 0))],
        out_specs=pl.BlockSpec((tb * SUBLANES, LANES), lambda i, be, nu: (i, 0)),
        scratch_shapes=[pltpu.VMEM((D_MODEL, D_FF), BF16), pltpu.VMEM((D_MODEL, D_FF), BF16),
                        pltpu.VMEM((D_FF, D_MODEL), BF16)],
    )
    return pl.pallas_call(
        functools.partial(_ffn_kernel, tb=tb),
        out_shape=jax.ShapeDtypeStruct((nblk * tb * SUBLANES, LANES), F32),
        grid_spec=grid_spec,
        compiler_params=_cparams(1),
        name="moe_ffn",
    )(blk_e, n_used, xr3, wgu, bgu, wdn, bdn, perm)


def _combine_kernel(dest_hbm, yr3_hbm, tg_ref, h_ref, lg_ref, lb_ref, op_ref, os_ref,
                    idx_smem, idx_sem, ybuf, row_sem, *, tm, n_prompt_tiles, n_steps):
    i = pl.program_id(0)
    rows = tm * TOP_K

    def compute(slot):
        tg = tg_ref[...]
        stride = TOP_K * SUBLANES
        pieces = []
        for j in range(D_MODEL // LANES):
            acc = tg[:, 0:1] * ybuf[slot, pl.ds(j, tm, stride=stride), :]
            for k in range(1, TOP_K):
                acc = acc + tg[:, k:k + 1] * ybuf[slot, pl.ds(k * SUBLANES + j, tm, stride=stride), :]
            pieces.append(acc)
        y = jnp.concatenate(pieces, axis=1)
        out = _layer_norm(DN_ALPHA * h_ref[...] + y, lg_ref[...], lb_ref[...])

        @pl.when(i < n_prompt_tiles)
        def _():
            op_ref[...] = out

        @pl.when(i >= n_prompt_tiles)
        def _():
            os_ref[...] = out

    _row_pipeline(i, n_steps, dest_hbm, yr3_hbm, idx_smem, idx_sem, ybuf, row_sem, rows, compute)


def _combine(dest, yr3, tg, h, lg, lb, n_prompt, tm):
    m = h.shape[0]
    npt = n_prompt // tm
    row = lambda w: pl.BlockSpec((tm, w), lambda i: (i, 0))
    par = lambda a: pl.BlockSpec(a.shape, lambda i: (0, 0))
    return pl.pallas_call(
        functools.partial(_combine_kernel, tm=tm, n_prompt_tiles=npt, n_steps=m // tm),
        out_shape=(jax.ShapeDtypeStruct((n_prompt, D_MODEL), F32),
                   jax.ShapeDtypeStruct((m - n_prompt, D_MODEL), F32)),
        grid=(m // tm,),
        in_specs=[pl.BlockSpec(memory_space=pl.ANY), pl.BlockSpec(memory_space=pl.ANY),
                  row(LANES), row(D_MODEL), par(lg), par(lb)],
        out_specs=(pl.BlockSpec((tm, D_MODEL), lambda i: (jnp.minimum(i, npt - 1), 0)),
                   pl.BlockSpec((tm, D_MODEL), lambda i: (jnp.maximum(i - npt, 0), 0))),
        scratch_shapes=_fetch_scratch(tm * TOP_K),
        compiler_params=_cparams(1),
        name="combine",
    )(dest.reshape(m // tm, tm * TOP_K), yr3, tg, h, lg, lb)


def _rope_tables(pos):
    half = ROPE_DIM // 2
    inv = ROPE_THETA ** (-jnp.arange(half, dtype=F32) / half)
    ang = pos.astype(F32)[:, None] * inv[None, :]
    cos, sin = jnp.cos(ang), jnp.sin(ang)
    return jnp.concatenate([cos] * 4, axis=1), jnp.concatenate([-sin, sin] * 2, axis=1)


def _pad_cols(w, width):
    return jnp.pad(w, ((0, 0), (0, width - w.shape[1])))


def _routing(top_i, rank, counts, n_blocks, tb):
    flat_e = top_i.reshape(-1)
    nblk_e = (counts + tb - 1) // tb
    pend = jnp.cumsum(nblk_e)
    pstart = (pend - nblk_e) * tb
    onehot = flat_e[:, None] == jnp.arange(N_EXPERTS, dtype=jnp.int32)[None, :]
    dest = (jnp.sum(jnp.where(onehot, pstart[None, :], 0), axis=1) + rank.reshape(-1)).astype(jnp.int32)
    n_used = pend[-1].astype(jnp.int32)
    pad_start = jnp.concatenate([(pstart + counts).astype(jnp.int32), n_used.reshape(1)])
    bidx = jnp.minimum(jnp.arange(n_blocks, dtype=jnp.int32), n_used - 1)
    blk_e = jnp.minimum(jnp.sum((pend[None, :] <= bidx[:, None]).astype(jnp.int32), axis=1), N_EXPERTS - 1)
    return dest, pad_start, blk_e, n_used.reshape(1)


def kernel(x_prompt, x_sample, cache_kv_latent, cache_k_rope, state_ssm, state_conv, page_table, w_in, ssd_conv_w, ssd_conv_b, ssd_dt_bias, ssd_a_log, ssd_d, ssd_norm_g, mla_q_norm_g, w_q_b, mla_kv_norm_g, w_kv_b, w_branch_a, w_branch_b, w_out, ln1_g, ln1_b, router_w, router_b, exp_w_gu, exp_b_gu, exp_w_down, exp_b_down, ln2_g, ln2_b):
    b_p, t_p, _ = x_prompt.shape
    b_s, t_s, _ = x_sample.shape
    n_pages = page_table.shape[1]
    page = cache_kv_latent.shape[2]
    past_len = n_pages * page
    n_p = b_p * t_p
    n_s = b_s * t_s
    n = n_p + n_s
    assert DEPTH == 1 and w_in.shape[0] == 1
    assert n_p % PROJ_TILE == 0 and t_p % ROW_TILE == 0 and n_s % ROW_TILE == 0
    assert t_p % (2 * ATTN_TQ) == 0 and t_p % SSD_CHUNK == 0
    assert t_s == SUBLANES and n_pages % (2 * PAGE_GROUP) == 0 and page == LANES

    wi = w_in[0]
    seg = lambda k: wi[:, IN_OFFS[k]:IN_OFFS[k + 1]]
    half = ROPE_DIM // 2
    w_kr = seg(2)
    w_kr_rot = jnp.concatenate([w_kr[:, half:], w_kr[:, :half]], axis=1)
    w_mla = jnp.concatenate([seg(0), seg(1), _pad_cols(w_kr, LANES), _pad_cols(w_kr_rot, LANES),
                             _pad_cols(seg(5), LANES)], axis=1).astype(BF16)
    w_z = seg(3).astype(BF16)
    w_xbc = seg(4).astype(BF16)
    w_gates = jnp.concatenate([seg(6), seg(7)], axis=1).astype(BF16)
    wq3 = w_q_b[0].reshape(Q_LORA, MLA_HEADS, QK_NOPE + ROPE_DIM)
    wq_nope = wq3[:, :, :QK_NOPE].reshape(Q_LORA, MLA_HEADS * QK_NOPE)
    wq_rope = wq3[:, :, QK_NOPE:]
    wq_rope_rot = jnp.concatenate([wq_rope[..., half:], wq_rope[..., :half]], axis=-1)
    wq = jnp.concatenate([wq_nope, wq_rope.reshape(Q_LORA, -1), wq_rope_rot.reshape(Q_LORA, -1)],
                         axis=1).astype(BF16)
    wkv3 = w_kv_b[0].reshape(KV_LORA, MLA_HEADS, QK_NOPE + V_HEAD)
    wuk = jnp.transpose(wkv3[:, :, :QK_NOPE], (1, 2, 0)).astype(BF16)
    wuv = jnp.transpose(wkv3[:, :, QK_NOPE:], (1, 0, 2)).astype(BF16)
    qg = mla_q_norm_g[0].reshape(1, Q_LORA)
    kvg = mla_kv_norm_g[0].reshape(1, KV_LORA)
    cw = ssd_conv_w[0]
    cb = ssd_conv_b[0].reshape(1, CONV_DIM)
    dtb = _pad_cols(ssd_dt_bias[0].reshape(1, SSD_HEADS), LANES)
    a_neg = _pad_cols(-jnp.exp(ssd_a_log[0].astype(F32)).reshape(1, SSD_HEADS), LANES)
    d_x = jnp.repeat(ssd_d[0].astype(F32), SSD_HEAD_DIM).reshape(1, D_INNER)
    e_mat = jnp.asarray(np.arange(LANES)[:, None] == np.arange(D_INNER)[None, :] // SSD_HEAD_DIM, BF16)
    ng = ssd_norm_g[0].reshape(1, D_INNER)
    wa = w_branch_a[0].astype(BF16)
    wb = w_branch_b[0].astype(BF16)
    wo = w_out[0].astype(BF16)
    rw = _pad_cols(router_w[0], LANES)
    rwh = rw.astype(BF16)
    rwl = (rw - rwh.astype(F32)).astype(BF16)
    rb = jnp.concatenate([router_b[0].reshape(1, N_EXPERTS),
                          jnp.full((1, LANES - N_EXPERTS), NEG, F32)], axis=1)
    bgu = jnp.concatenate([exp_b_gu[0][:, 0::2], exp_b_gu[0][:, 1::2]], axis=1).reshape(N_EXPERTS, 1, 2 * D_FF)
    bdn = exp_b_down[0].reshape(N_EXPERTS, 1, D_MODEL)
    row2 = lambda v: v[0].reshape(1, D_MODEL)

    cos_p, sin_p = _rope_tables(jnp.arange(t_p, dtype=jnp.int32))
    cos_s, sin_s = _rope_tables(jnp.tile(past_len + jnp.arange(t_s, dtype=jnp.int32), b_s))

    xp = x_prompt.reshape(n_p, D_MODEL)
    xs = x_sample.reshape(n_s, D_MODEL)
    qlat_p, qrope_p, c_p, cbf_p, kr_p, krbf_p, dt_p = _mla_in(xp, w_mla, qg, kvg, wq, wuk, cos_p, sin_p, ROW_TILE)
    qlat_s, qrope_s, c_s, _, kr_s, _, dt_s = _mla_in(xs, w_mla, qg, kvg, wq, wuk, cos_s, sin_s, ROW_TILE)
    z_p, xbc_p, gates_p = (_proj(xp, w, PROJ_TILE) for w in (w_z, w_xbc, w_gates))
    z_s, xbc_s, gates_s = (_proj(xs, w, ROW_TILE) for w in (w_z, w_xbc, w_gates))

    conv0_p = jnp.zeros((b_p, SUBLANES, CONV_DIM), F32)
    conv0_s = jnp.pad(state_conv[0], ((0, 0), (SUBLANES - (D_CONV - 1), 0), (0, 0)))
    ya_p, ssm_p, ctail_p = _ssd(xbc_p, z_p, dt_p, conv0_p, None, cw, cb, dtb, a_neg, d_x, ng, e_mat,
                                nb=b_p, nc=t_p // SSD_CHUNK, t_in=SSD_CHUNK, chunk=SSD_CHUNK, y_dtype=BF16)
    ya_s, ssm_s, ctail_s = _ssd(xbc_s, z_s, dt_s, conv0_s, state_ssm[0].reshape(b_s, D_INNER, D_STATE),
                                cw, cb, dtb, a_neg, d_x, ng, e_mat,
                                nb=b_s, nc=1, t_in=t_s, chunk=SAMPLE_CHUNK, y_dtype=F32)
    conv_p = ctail_p[:, SUBLANES - (D_CONV - 1):, :]
    conv_s = ctail_s[:, SUBLANES - (D_CONV - 1):, :]

    o_p = _attn_prompt(qlat_p, qrope_p, cbf_p, krbf_p, wuv, nb=b_p, t=t_p, tq=ATTN_TQ)
    ql_s = jnp.transpose(qlat_s.reshape(MLA_HEADS, b_s, t_s, KV_LORA), (1, 0, 2, 3))
    qr_s = jnp.transpose(qrope_s.reshape(MLA_HEADS, b_s, t_s, ROPE_DIM), (1, 0, 2, 3))
    cnew_s = jnp.pad(c_s.reshape(b_s, t_s, KV_LORA), ((0, 0), (0, LANES - t_s), (0, 0)))
    krnew_s = jnp.swapaxes(jnp.pad(kr_s.reshape(b_s, t_s, ROPE_DIM), ((0, 0), (0, LANES - t_s), (0, 0))), 1, 2)
    olat_s = _attn_sample(page_table, ql_s.reshape(b_s, MLA_HEADS * t_s, KV_LORA),
                          qr_s.reshape(b_s, MLA_HEADS * t_s, ROPE_DIM), cnew_s, krnew_s,
                          cache_kv_latent[0], jnp.swapaxes(cache_k_rope[0], 1, 2),
                          t_new=t_s, gp=PAGE_GROUP)
    olat_s = jnp.transpose(olat_s.reshape(b_s, MLA_HEADS, t_s, KV_LORA), (1, 0, 2, 3)).reshape(MLA_HEADS, n_s, KV_LORA)
    o_s = _head_proj(olat_s, wuv)

    h, h3, top_i, top_g, rank, counts = _merge(ya_p, ya_s.astype(BF16), o_p, o_s, gates_p, gates_s, xp, xs, wa, wb, wo,
                                 row2(ln1_g), row2(ln1_b), rwh, rwl, rb, ROW_TILE)

    tb = MOE_TB
    n_blocks = -(-(n * TOP_K + N_EXPERTS * (tb - 1)) // tb)
    dest, pad_start, blk_e, n_used = _routing(top_i[:, :TOP_K], rank[:, :TOP_K], counts[0, :N_EXPERTS], n_blocks, tb)
    xr3 = _dispatch(pad_start, dest, h3, (n_blocks + 1) * tb, COMBINE_TILE, tb)
    yr3 = _ffn(blk_e, n_used, xr3, n_blocks, exp_w_gu[0], bgu, exp_w_down[0], bdn, tb)
    y_p, y_s = _combine(dest, yr3, top_g, h, row2(ln2_g), row2(ln2_b), n_p, COMBINE_TILE)

    return (y_p.reshape(b_p, t_p, D_MODEL), y_s.reshape(b_s, t_s, D_MODEL),
            c_p.reshape(1, b_p, t_p, KV_LORA), kr_p.reshape(1, b_p, t_p, ROPE_DIM),
            ssm_p.reshape(1, b_p, SSD_HEADS, SSD_HEAD_DIM, D_STATE), conv_p[None],
            c_s.reshape(1, b_s, t_s, KV_LORA), kr_s.reshape(1, b_s, t_s, ROPE_DIM),
            ssm_s.reshape(1, b_s, SSD_HEADS, SSD_HEAD_DIM, D_STATE), conv_s[None])
```

```python
import functools

import jax
import jax.numpy as jnp
import numpy as np
from jax import lax
from jax.experimental import pallas as pl
from jax.experimental.pallas import tpu as pltpu

F32 = jnp.float32
BF16 = jnp.bfloat16

D_MODEL = 1024
D_INNER = 2048
SSD_HEAD_DIM = 64
SSD_HEADS = 32
SSD_GROUPS = 4
HEADS_PER_GROUP = 8
D_STATE = 128
D_CONV = 4
CONV_DIM = D_INNER + 2 * SSD_GROUPS * D_STATE
SSD_CHUNK = 128
MLA_HEADS = 8
Q_LORA = 256
KV_LORA = 256
QK_NOPE = 128
ROPE_DIM = 64
V_HEAD = 128
ROPE_THETA = 10000.0
ATTN_SCALE = (QK_NOPE + ROPE_DIM) ** -0.5
N_EXPERTS = 32
TOP_K = 4
D_FF = 1024
SWIGLU_ALPHA = 1.702
SWIGLU_LIMIT = 7.0
DEPTH = 1
DN_ALPHA = (2 * DEPTH) ** 0.25
LN_EPS = 1e-5
RMS_EPS = 1e-6
IN_SIZES = (Q_LORA, KV_LORA, ROPE_DIM, D_INNER, CONV_DIM, SSD_HEADS, D_MODEL, D_MODEL)
IN_OFFS = tuple(int(v) for v in np.cumsum((0,) + IN_SIZES))

LANES = 128
SUBLANES = 8
VMEM_LIMIT = 56 * 1024 * 1024

NEG = -1e30

ROW_TILE = 256
PROJ_TILE = 512
ATTN_TQ = 256
ATTN_HEAD_GROUPS = 2
PAGE_GROUP = 64
SAMPLE_CHUNK = 32
MOE_TB = 256
COMBINE_TILE = 256
MLA_A_COLS = 896


def _cparams(n_axes):
    return pltpu.CompilerParams(dimension_semantics=("arbitrary",) * n_axes,
                                vmem_limit_bytes=VMEM_LIMIT)


def _dot(a, b):
    return jnp.dot(a, b, preferred_element_type=F32)


def _dot_nt(a, b):
    return lax.dot_general(a, b, (((1,), (1,)), ((), ())), preferred_element_type=F32)


def _split3(x):
    hi = x.astype(BF16)
    r1 = x - hi.astype(F32)
    mid = r1.astype(BF16)
    lo = (r1 - mid.astype(F32)).astype(BF16)
    return hi, mid, lo


def _sigmoid(x):
    return 1.0 / (1.0 + jnp.exp(-x))


def _layer_norm(x, g, b):
    mu = jnp.mean(x, axis=-1, keepdims=True)
    xc = x - mu
    var = jnp.mean(xc * xc, axis=-1, keepdims=True)
    return xc * lax.rsqrt(var + LN_EPS) * g + b


def _rms_norm(x, g):
    return x * lax.rsqrt(jnp.mean(x * x, axis=-1, keepdims=True) + RMS_EPS) * g


def _proj_kernel(x_ref, w_ref, o_ref):
    o_ref[...] = _dot(x_ref[...].astype(BF16), w_ref[...])


def _proj(x, w, tm):
    m, k = x.shape
    n = w.shape[1]
    return pl.pallas_call(
        _proj_kernel,
        out_shape=jax.ShapeDtypeStruct((m, n), F32),
        grid=(m // tm,),
        in_specs=[pl.BlockSpec((tm, k), lambda i: (i, 0)),
                  pl.BlockSpec((k, n), lambda i: (0, 0))],
        out_specs=pl.BlockSpec((tm, n), lambda i: (i, 0)),
        compiler_params=_cparams(1),
        name="proj",
    )(x, w)


def _mla_in_kernel(x_ref, wa_ref, qg_ref, kvg_ref, wq_ref, wuk_ref, cos_ref, sin_ref,
                   qlat_ref, qrope_ref, c_ref, cbf_ref, kr_ref, krbf_ref, dt_ref):
    xb = x_ref[...].astype(BF16)
    pa = _dot(xb, wa_ref[...])
    q_a = pa[:, 0:256]
    kv_a = pa[:, 256:512]
    kr_raw = pa[:, 512:640]
    kr_rot = pa[:, 640:768]
    dt_ref[...] = pa[:, 768:896]
    cos = cos_ref[...]
    sin = sin_ref[...]
    c = _rms_norm(kv_a, kvg_ref[...])
    c_ref[...] = c
    cbf_ref[...] = c.astype(BF16)
    kr = kr_raw * cos + kr_rot * sin
    kr_ref[...] = kr[:, 0:ROPE_DIM]
    krbf_ref[...] = kr[:, 0:ROPE_DIM].astype(BF16)
    qn = _rms_norm(q_a, qg_ref[...]).astype(BF16)
    qq = _dot(qn, wq_ref[...])
    cos4 = jnp.concatenate([cos] * 4, axis=1)
    sin4 = jnp.concatenate([sin] * 4, axis=1)
    qr = (qq[:, 1024:1536] * cos4 + qq[:, 1536:2048] * sin4) * ATTN_SCALE
    for h in range(MLA_HEADS):
        qrope_ref[h] = qr[:, h * ROPE_DIM:(h + 1) * ROPE_DIM].astype(BF16)
        nope = qq[:, h * QK_NOPE:(h + 1) * QK_NOPE].astype(BF16)
        qlat_ref[h] = (_dot(nope, wuk_ref[h]) * ATTN_SCALE).astype(BF16)


def _mla_in(x, wa, qg, kvg, wq, wuk, cos_t, sin_t, tm):
    m = x.shape[0]
    tab_blocks = cos_t.shape[0] // tm
    row = lambda w: pl.BlockSpec((tm, w), lambda i: (i, 0))
    tab = pl.BlockSpec((tm, LANES), lambda i: (i % tab_blocks, 0))
    full2 = lambda a: pl.BlockSpec(a.shape, lambda i: (0, 0))
    full3 = lambda a: pl.BlockSpec(a.shape, lambda i: (0, 0, 0))
    return pl.pallas_call(
        _mla_in_kernel,
        out_shape=(jax.ShapeDtypeStruct((MLA_HEADS, m, KV_LORA), BF16),
                   jax.ShapeDtypeStruct((MLA_HEADS, m, ROPE_DIM), BF16),
                   jax.ShapeDtypeStruct((m, KV_LORA), F32),
                   jax.ShapeDtypeStruct((m, KV_LORA), BF16),
                   jax.ShapeDtypeStruct((m, ROPE_DIM), F32),
                   jax.ShapeDtypeStruct((m, ROPE_DIM), BF16),
                   jax.ShapeDtypeStruct((m, LANES), F32)),
        grid=(m // tm,),
        in_specs=[row(D_MODEL), full2(wa), full2(qg), full2(kvg), full2(wq), full3(wuk), tab, tab],
        out_specs=(pl.BlockSpec((MLA_HEADS, tm, KV_LORA), lambda i: (0, i, 0)),
                   pl.BlockSpec((MLA_HEADS, tm, ROPE_DIM), lambda i: (0, i, 0)),
                   row(KV_LORA), row(KV_LORA), row(ROPE_DIM), row(ROPE_DIM), row(LANES)),
        compiler_params=_cparams(1),
        name="mla_in",
    )(x, wa, qg, kvg, wq, wuk, cos_t, sin_t)


def _expand_heads(v, e_ref, terms):
    out = None
    rem = v
    for _ in range(terms):
        part = rem.astype(BF16)
        rem = rem - part.astype(F32)
        d = _dot(part, e_ref[...])
        out = d if out is None else out + d
    return out


def _conv_silu(ext, r0, rows, cw_ref, cb_ref, out, out_r0):
    for j in range(CONV_DIM // LANES):
        sl = slice(j * LANES, (j + 1) * LANES)
        acc = cb_ref[:, sl]
        for k in range(D_CONV):
            acc = acc + ext[pl.ds(r0 + SUBLANES - (D_CONV - 1) + k, rows), sl] * cw_ref[k:k + 1, sl]
        out[out_r0:out_r0 + rows, sl] = acc * _sigmoid(acc)


def _ssd_kernel(*refs, t_in, L, has_h0):
    n_pairs = SSD_HEADS // 2
    refs = list(refs)
    hT = refs[-n_pairs:]
    it = iter(refs[:-n_pairs])
    xbc_ref, z_ref, dt_ref, conv0_ref = next(it), next(it), next(it), next(it)
    h0_ref = next(it) if has_h0 else None
    (cw_ref, cb_ref, dtb_ref, a_ref, dx_ref, ng_ref, e_ref, y_ref, hfin_ref, ctail_ref,
     ext, xcv, ybuf, acs_s, acsT_s, dtT_s, eax_s, wtx_s) = it
    c = pl.program_id(1)
    nc = pl.num_programs(1)

    @pl.when(c == 0)
    def _():
        ext[0:SUBLANES, :] = conv0_ref[0]
        if t_in < L:
            ext[SUBLANES:SUBLANES + L, :] = jnp.zeros((L, CONV_DIM), F32)
        for j in range(n_pairs):
            if has_h0:
                hT[j][...] = h0_ref[0, j * LANES:(j + 1) * LANES, :].T
            else:
                hT[j][...] = jnp.zeros((D_STATE, LANES), F32)

    ext[SUBLANES:SUBLANES + t_in, :] = xbc_ref[...]

    @pl.when(c == nc - 1)
    def _():
        ctail_ref[0] = ext[t_in:t_in + SUBLANES, :]

    _conv_silu(ext, 0, L, cw_ref, cb_ref, xcv, 0)
    ext[0:SUBLANES, :] = ext[L:L + SUBLANES, :]

    row = lax.broadcasted_iota(jnp.int32, (L, L), 0)
    col = lax.broadcasted_iota(jnp.int32, (L, L), 1)
    tri = row >= col
    tri_b = jnp.where(tri, 1.0, 0.0).astype(BF16)
    if t_in < L:
        dt_raw = jnp.concatenate([dt_ref[...], jnp.zeros((L - t_in, LANES), F32)], axis=0)
    else:
        dt_raw = dt_ref[...]
    xs = dt_raw + dtb_ref[...]
    dtv = jnp.maximum(xs, 0.0) + jnp.log1p(jnp.exp(-jnp.abs(xs)))
    if t_in < L:
        dtv = jnp.where(lax.broadcasted_iota(jnp.int32, (L, LANES), 0) < t_in, dtv, 0.0)
    da = dtv * a_ref[...]
    hi, mid, lo = _split3(da)
    acs = _dot(tri_b, hi) + _dot(tri_b, mid) + _dot(tri_b, lo)
    acs_s[...] = acs
    acsT_s[...] = acs.T
    dtT_s[...] = dtv.T
    eax_s[...] = _expand_heads(jnp.exp(acs), e_ref, 3)
    wtx_s[...] = _expand_heads(jnp.exp(acs[L - 1:L, :] - acs) * dtv, e_ref, 2)

    low_half = lax.broadcasted_iota(jnp.int32, (L, LANES), 1) < SSD_HEAD_DIM
    pairs_per_group = HEADS_PER_GROUP // 2
    for g in range(SSD_GROUPS):
        bo = D_INNER + g * D_STATE
        co = D_INNER + SSD_GROUPS * D_STATE + g * D_STATE
        bm = xcv[:, bo:bo + D_STATE]
        cmb = xcv[:, co:co + D_STATE].astype(BF16)
        cb = _dot_nt(cmb, bm.astype(BF16))
        bmT = bm.T.astype(BF16)
        for k in range(pairs_per_group):
            kk = g * pairs_per_group + k
            tl = slice(kk * LANES, (kk + 1) * LANES)
            x2 = xcv[:, tl]
            ws = []
            for idx in (2 * kk, 2 * kk + 1):
                seg = acs_s[:, idx:idx + 1] - acsT_s[idx:idx + 1, :]
                w = cb * jnp.exp(jnp.where(tri, seg, NEG)) * dtT_s[idx:idx + 1, :]
                ws.append(w.astype(BF16))
            w2 = jnp.concatenate(ws, axis=1)
            xbd = jnp.concatenate([jnp.where(low_half, x2, 0.0).astype(BF16),
                                   jnp.where(low_half, 0.0, x2).astype(BF16)], axis=0)
            yd = _dot(w2, xbd)
            hp = hT[kk][...]
            yo = _dot(cmb, hp.astype(BF16))
            ybuf[:, tl] = yd + yo * eax_s[:, tl] + dx_ref[:, tl] * x2
            s = _dot(bmT, (x2 * wtx_s[:, tl]).astype(BF16))
            hT[kk][...] = eax_s[L - 1:L, tl] * hp + s

    gw = D_INNER // SSD_GROUPS
    for g in range(SSD_GROUPS):
        sl = slice(g * gw, (g + 1) * gw)
        zg = z_ref[:, sl]
        v = ybuf[0:t_in, sl] * (zg * _sigmoid(zg))
        ms = jnp.mean(v * v, axis=-1, keepdims=True)
        y_ref[:, sl] = (v * lax.rsqrt(ms + RMS_EPS) * ng_ref[:, sl]).astype(y_ref.dtype)

    @pl.when(c == nc - 1)
    def _():
        for j in range(n_pairs):
            hfin_ref[0, j * LANES:(j + 1) * LANES, :] = hT[j][...].T


def _ssd(xbc, z, dt, conv0, h0, cw, cb, dtb, a_neg, d_x, ng, e_mat, *, nb, nc, t_in, chunk, y_dtype):
    L = chunk
    has_h0 = h0 is not None
    rowspec = lambda w: pl.BlockSpec((t_in, w), lambda b, c: (b * nc + c, 0))
    par = lambda a: pl.BlockSpec(a.shape, lambda b, c: (0, 0))
    per_seq = lambda r, w: pl.BlockSpec((1, r, w), lambda b, c: (b, 0, 0))
    in_specs = [rowspec(CONV_DIM), rowspec(D_INNER), rowspec(LANES), per_seq(SUBLANES, CONV_DIM)]
    args = [xbc, z, dt, conv0]
    if has_h0:
        in_specs.append(per_seq(D_INNER, D_STATE))
        args.append(h0)
    params = [cw, cb, dtb, a_neg, d_x, ng, e_mat]
    in_specs += [par(p) for p in params]
    args += params
    return pl.pallas_call(
        functools.partial(_ssd_kernel, t_in=t_in, L=L, has_h0=has_h0),
        out_shape=(jax.ShapeDtypeStruct((nb * nc * t_in, D_INNER), y_dtype),
                   jax.ShapeDtypeStruct((nb, D_INNER, D_STATE), F32),
                   jax.ShapeDtypeStruct((nb, SUBLANES, CONV_DIM), F32)),
        grid=(nb, nc),
        in_specs=in_specs,
        out_specs=(rowspec(D_INNER), per_seq(D_INNER, D_STATE), per_seq(SUBLANES, CONV_DIM)),
        scratch_shapes=[pltpu.VMEM((L + 2 * SUBLANES, CONV_DIM), F32),
                        pltpu.VMEM((L, CONV_DIM), F32),
                        pltpu.VMEM((L, D_INNER), F32),
                        pltpu.VMEM((L, LANES), F32), pltpu.VMEM((LANES, L), F32), pltpu.VMEM((LANES, L), F32),
                        pltpu.VMEM((L, D_INNER), F32), pltpu.VMEM((L, D_INNER), F32)]
                       + [pltpu.VMEM((D_STATE, LANES), F32)] * (SSD_HEADS // 2),
        compiler_params=_cparams(2),
        name="ssd",
    )(*args)


def _softmax_step(s, kc, m_scr, l_scr, acc_scr):
    reps = s.shape[1] // LANES
    m_prev = m_scr[...]
    m_new = jnp.maximum(m_prev, jnp.max(s, axis=1, keepdims=True))
    alpha = jnp.exp(m_prev - m_new)
    p = jnp.exp(s - (jnp.concatenate([m_new] * reps, axis=1) if reps > 1 else m_new))
    l_scr[...] = alpha * l_scr[...] + jnp.sum(p, axis=1, keepdims=True)
    acc_scr[...] = acc_scr[...] * jnp.concatenate([alpha] * (KV_LORA // LANES), axis=1) + _dot(p.astype(BF16), kc)
    m_scr[...] = m_new


def _attn_prompt_kernel(qlat_ref, qrope_ref, c_ref, kr_ref, wuv_ref, o_ref, m_scr, l_scr, acc_scr, *, tq):
    qi = pl.program_id(1)
    R = MLA_HEADS * tq

    def scores(block, width, g=0, groups=1):
        start = pl.multiple_of(block * width, width)
        kc = c_ref[pl.ds(start, width), :]
        kr = kr_ref[pl.ds(start, width), :]
        hpg = MLA_HEADS // groups
        q = qlat_ref[g * hpg:(g + 1) * hpg].reshape(hpg * tq, KV_LORA)
        qr = qrope_ref[g * hpg:(g + 1) * hpg].reshape(hpg * tq, ROPE_DIM)
        return _dot_nt(q, kc) + _dot_nt(qr, kr), kc

    s, kc = scores(qi, tq)
    t = lax.broadcasted_iota(jnp.int32, (R, tq), 0) & (tq - 1)
    kcol = lax.broadcasted_iota(jnp.int32, (R, tq), 1)
    s = jnp.where(kcol <= t, s, NEG)
    m0 = jnp.max(s, axis=1, keepdims=True)
    p = jnp.exp(s - m0)
    m_scr[...] = jnp.broadcast_to(m0, m_scr.shape)
    l_scr[...] = jnp.broadcast_to(jnp.sum(p, axis=1, keepdims=True), l_scr.shape)
    acc_scr[...] = _dot(p.astype(BF16), kc)

    def body(j, carry):
        rg = R // ATTN_HEAD_GROUPS
        for g in range(ATTN_HEAD_GROUPS):
            rows = pl.ds(g * rg, rg)
            s2, kc2 = scores(j, 2 * tq, g, ATTN_HEAD_GROUPS)
            _softmax_step(s2, kc2, m_scr.at[rows], l_scr.at[rows], acc_scr.at[rows])
        return carry

    lax.fori_loop(0, qi // 2, body, 0)

    @pl.when(qi % 2 == 1)
    def _():
        s1, kc1 = scores(qi - 1, tq)
        _softmax_step(s1, kc1, m_scr, l_scr, acc_scr)

    inv = 1.0 / l_scr[...]
    o_lat = acc_scr[...] * jnp.concatenate([inv] * (KV_LORA // LANES), axis=1)
    for h in range(MLA_HEADS):
        oh = _dot(o_lat[h * tq:(h + 1) * tq, :].astype(BF16), wuv_ref[h])
        o_ref[:, h * V_HEAD:(h + 1) * V_HEAD] = oh.astype(o_ref.dtype)


def _attn_prompt(qlat, qrope, cbf, krbf, wuv, *, nb, t, tq):
    nq = t // tq
    R = MLA_HEADS * tq
    return pl.pallas_call(
        functools.partial(_attn_prompt_kernel, tq=tq),
        out_shape=jax.ShapeDtypeStruct((nb * t, MLA_HEADS * V_HEAD), BF16),
        grid=(nb, nq),
        in_specs=[pl.BlockSpec((MLA_HEADS, tq, KV_LORA), lambda b, i: (0, b * nq + i, 0)),
                  pl.BlockSpec((MLA_HEADS, tq, ROPE_DIM), lambda b, i: (0, b * nq + i, 0)),
                  pl.BlockSpec((t, KV_LORA), lambda b, i: (b, 0)),
                  pl.BlockSpec((t, ROPE_DIM), lambda b, i: (b, 0)),
                  pl.BlockSpec(wuv.shape, lambda b, i: (0, 0, 0))],
        out_specs=pl.BlockSpec((tq, MLA_HEADS * V_HEAD), lambda b, i: (b * nq + i, 0)),
        scratch_shapes=[pltpu.VMEM((R, LANES), F32), pltpu.VMEM((R, LANES), F32),
                        pltpu.VMEM((R, KV_LORA), F32)],
        compiler_params=_cparams(2),
        name="attn_prompt",
    )(qlat, qrope, cbf, krbf, wuv)


def _attn_sample_kernel(pt_ref, qlat_ref, qrope_ref, cnew_ref, krnew_ref, kv_hbm, kr_hbm, o_ref,
                        kvbuf, krbuf, kv_sem, kr_sem, m_scr, l_scr, acc_scr, *, n_groups, gp, t_new):
    b = pl.program_id(0)
    nb = pl.num_programs(0)
    R = qlat_ref.shape[1]
    page = kvbuf.shape[2]
    q = qlat_ref[0]
    qr = qrope_ref[0]

    def start_group(bb, g, slot):
        for p in range(gp):
            pid = pt_ref[bb, g * gp + p]
            pltpu.make_async_copy(kv_hbm.at[pid], kvbuf.at[slot, p], kv_sem.at[slot]).start()
            pltpu.make_async_copy(kr_hbm.at[pid], krbuf.at[slot, p], kr_sem.at[slot]).start()

    def wait_group(slot):
        pltpu.make_async_copy(kvbuf.at[slot], kvbuf.at[slot], kv_sem.at[slot]).wait()
        pltpu.make_async_copy(krbuf.at[slot], krbuf.at[slot], kr_sem.at[slot]).wait()

    @pl.when(b == 0)
    def _():
        start_group(0, 0, 0)

    kc = cnew_ref[0].astype(BF16)
    kr = krnew_ref[0].astype(BF16)
    s = _dot_nt(q, kc) + _dot(qr, kr)
    t = lax.broadcasted_iota(jnp.int32, (R, LANES), 0) & (t_new - 1)
    kcol = lax.broadcasted_iota(jnp.int32, (R, LANES), 1)
    s = jnp.where(kcol <= t, s, NEG)
    m0 = jnp.max(s, axis=1, keepdims=True)
    p0 = jnp.exp(s - m0)
    m_scr[...] = jnp.broadcast_to(m0, m_scr.shape)
    l_scr[...] = jnp.broadcast_to(jnp.sum(p0, axis=1, keepdims=True), l_scr.shape)
    acc_scr[...] = _dot(p0.astype(BF16), kc)

    for g in range(n_groups):
        slot = g % 2
        if g + 1 < n_groups:
            start_group(b, g + 1, 1 - slot)
        else:
            @pl.when(b + 1 < nb)
            def _():
                start_group(b + 1, 0, 1 - slot)
        wait_group(slot)
        kc = kvbuf[slot].reshape(gp * page, KV_LORA).astype(BF16)
        kr = jnp.concatenate([krbuf[slot, p].astype(BF16) for p in range(gp)], axis=1)
        s = _dot_nt(q, kc) + _dot(qr, kr)
        _softmax_step(s, kc, m_scr, l_scr, acc_scr)

    inv = 1.0 / l_scr[...]
    o_ref[0] = acc_scr[...] * jnp.concatenate([inv] * (KV_LORA // LANES), axis=1)


def _attn_sample(page_table, qlat, qrope, cnew, krnew, cache_kv, cache_kr, *, t_new, gp):
    nb, n_pages = page_table.shape
    n_groups = n_pages // gp
    R = qlat.shape[1]
    page = cache_kv.shape[1]
    per_b = lambda r, w: pl.BlockSpec((1, r, w), lambda b, pt: (b, 0, 0))
    grid_spec = pltpu.PrefetchScalarGridSpec(
        num_scalar_prefetch=1,
        grid=(nb,),
        in_specs=[per_b(R, KV_LORA), per_b(R, ROPE_DIM), per_b(LANES, KV_LORA), per_b(ROPE_DIM, LANES),
                  pl.BlockSpec(memory_space=pl.ANY), pl.BlockSpec(memory_space=pl.ANY)],
        out_specs=per_b(R, KV_LORA),
        scratch_shapes=[pltpu.VMEM((2, gp, page, KV_LORA), F32), pltpu.VMEM((2, gp, ROPE_DIM, page), F32),
                        pltpu.SemaphoreType.DMA((2,)), pltpu.SemaphoreType.DMA((2,)),
                        pltpu.VMEM((R, LANES), F32), pltpu.VMEM((R, LANES), F32),
                        pltpu.VMEM((R, KV_LORA), F32)],
    )
    return pl.pallas_call(
        functools.partial(_attn_sample_kernel, n_groups=n_groups, gp=gp, t_new=t_new),
        out_shape=jax.ShapeDtypeStruct((nb, R, KV_LORA), F32),
        grid_spec=grid_spec,
        compiler_params=_cparams(1),
        name="attn_sample",
    )(page_table, qlat, qrope, cnew, krnew, cache_kv, cache_kr)


def _head_proj_kernel(x_ref, w_ref, o_ref):
    o_ref[...] = _dot(x_ref[0].astype(BF16), w_ref[0]).astype(o_ref.dtype)


def _head_proj(x, w):
    h, m, k = x.shape
    n = w.shape[2]
    return pl.pallas_call(
        _head_proj_kernel,
        out_shape=jax.ShapeDtypeStruct((m, h * n), BF16),
        grid=(h,),
        in_specs=[pl.BlockSpec((1, m, k), lambda i: (i, 0, 0)),
                  pl.BlockSpec((1, k, n), lambda i: (i, 0, 0))],
        out_specs=pl.BlockSpec((m, n), lambda i: (0, i)),
        compiler_params=_cparams(1),
        name="head_proj",
    )(x, w)


def _merge_kernel(ya_p_ref, ya_s_ref, o_p_ref, o_s_ref, g_p_ref, g_s_ref, u_p_ref, u_s_ref, wa_ref, wb_ref, wo_ref,
                  lg_ref, lb_ref, rwh_ref, rwl_ref, rb_ref,
                  h_ref, h3_ref, ti_ref, tg_ref, rank_ref, cnt_ref, run_ref, *, n_prompt_tiles):
    i = pl.program_id(0)
    is_p = i < n_prompt_tiles
    yn = jnp.where(is_p, ya_p_ref[...], ya_s_ref[...])
    ob = jnp.where(is_p, o_p_ref[...], o_s_ref[...])
    g = jnp.where(is_p, g_p_ref[...], g_s_ref[...])
    u = jnp.where(is_p, u_p_ref[...], u_s_ref[...])
    ya = _dot(yn, wa_ref[...])
    yb = _dot(ob, wb_ref[...])
    merged = _sigmoid(g[:, 0:D_MODEL]) * ya + _sigmoid(g[:, D_MODEL:2 * D_MODEL]) * yb
    mix = _dot(merged.astype(BF16), wo_ref[...])
    h = _layer_norm(DN_ALPHA * u + mix, lg_ref[...], lb_ref[...])
    h_ref[...] = h
    tm = h.shape[0]
    for j in range(D_MODEL // LANES):
        h3_ref[pl.ds(j, tm, stride=SUBLANES), :] = h[:, j * LANES:(j + 1) * LANES]
    hh = h.astype(BF16)
    hl = (h - hh.astype(F32)).astype(BF16)
    logits = _dot(hh, rwh_ref[...]) + _dot(hl, rwh_ref[...]) + _dot(hh, rwl_ref[...]) + rb_ref[...]
    lane = lax.broadcasted_iota(jnp.int32, logits.shape, 1).astype(F32)
    vals = jnp.zeros_like(logits)
    idxs = jnp.zeros_like(logits)
    cur = logits
    picked = []
    for k in range(TOP_K):
        mx = jnp.max(cur, axis=1, keepdims=True)
        ix = jnp.min(jnp.where(cur == mx, lane, float(LANES)), axis=1, keepdims=True)
        vals = jnp.where(lane == k, mx, vals)
        idxs = jnp.where(lane == k, ix, idxs)
        picked.append(lane == ix)
        cur = jnp.where(picked[k], NEG * 2.0, cur)
    v0 = vals[:, 0:1]
    e = jnp.where(lane < TOP_K, jnp.exp(vals - v0), 0.0)
    tg_ref[...] = e / jnp.sum(e, axis=1, keepdims=True)
    ti_ref[...] = idxs.astype(jnp.int32)

    @pl.when(i == 0)
    def _():
        run_ref[...] = jnp.zeros_like(run_ref)

    cnt = jnp.zeros_like(logits)
    for k in range(TOP_K):
        cnt = cnt + jnp.where(picked[k], 1.0, 0.0)
    r_i = lax.broadcasted_iota(jnp.int32, (tm, tm), 0)
    c_i = lax.broadcasted_iota(jnp.int32, (tm, tm), 1)
    before = _dot(jnp.where(r_i > c_i, 1.0, 0.0).astype(BF16), cnt.astype(BF16)) + run_ref[0:1, :]
    ranks = jnp.zeros_like(logits)
    for k in range(TOP_K):
        rk = jnp.sum(jnp.where(picked[k], before, 0.0), axis=1, keepdims=True)
        ranks = jnp.where(lane == k, rk, ranks)
    rank_ref[...] = ranks.astype(jnp.int32)
    run_ref[0:1, :] = run_ref[0:1, :] + jnp.sum(cnt, axis=0, keepdims=True)
    cnt_ref[...] = run_ref[...].astype(jnp.int32)


def _merge(ya_p, ya_s, o_p, o_s, g_p, g_s, u_p, u_s, wa, wb, wo, lg, lb, rwh, rwl, rb, tm):
    m = u_p.shape[0] + u_s.shape[0]
    npt = ya_p.shape[0] // tm
    row = lambda w: pl.BlockSpec((tm, w), lambda i: (i, 0))
    prow = lambda w: pl.BlockSpec((tm, w), lambda i: (jnp.minimum(i, npt - 1), 0))
    srow = lambda w: pl.BlockSpec((tm, w), lambda i: (jnp.maximum(i - npt, 0), 0))
    par = lambda a: pl.BlockSpec(a.shape, lambda i: (0, 0))
    return pl.pallas_call(
        functools.partial(_merge_kernel, n_prompt_tiles=npt),
        out_shape=(jax.ShapeDtypeStruct((m, D_MODEL), F32),
                   jax.ShapeDtypeStruct((m * SUBLANES, LANES), F32),
                   jax.ShapeDtypeStruct((m, LANES), jnp.int32),
                   jax.ShapeDtypeStruct((m, LANES), F32),
                   jax.ShapeDtypeStruct((m, LANES), jnp.int32),
                   jax.ShapeDtypeStruct((SUBLANES, LANES), jnp.int32)),
        grid=(m // tm,),
        in_specs=[prow(D_INNER), srow(D_INNER), prow(D_MODEL), srow(D_MODEL), prow(2 * D_MODEL), srow(2 * D_MODEL),
                  prow(D_MODEL), srow(D_MODEL),
                  par(wa), par(wb), par(wo), par(lg), par(lb), par(rwh), par(rwl), par(rb)],
        out_specs=(row(D_MODEL), pl.BlockSpec((tm * SUBLANES, LANES), lambda i: (i, 0)), row(LANES), row(LANES),
                   row(LANES), pl.BlockSpec((SUBLANES, LANES), lambda i: (0, 0))),
        scratch_shapes=[pltpu.VMEM((SUBLANES, LANES), F32)],
        compiler_params=_cparams(1),
        name="merge",
    )(ya_p, ya_s, o_p, o_s, g_p, g_s, u_p, u_s, wa, wb, wo, lg, lb, rwh, rwl, rb)


DEINT_IN = 2 * LANES


def _deint_perm():
    perm = np.zeros((DEINT_IN, DEINT_IN), np.float32)
    perm[2 * np.arange(LANES), np.arange(LANES)] = 1.0
    perm[2 * np.arange(LANES) + 1, LANES + np.arange(LANES)] = 1.0
    return jnp.asarray(perm, BF16)


def _row_pipeline(i, n_steps, idx_hbm, src_hbm, idx_smem, idx_sem, buf, row_sem, rows, compute):
    def idx_copy(chunk, s):
        return pltpu.make_async_copy(idx_hbm.at[chunk], idx_smem.at[s], idx_sem.at[s])

    def issue_row(s, r):
        src0 = pl.multiple_of(idx_smem[s, r] * SUBLANES, SUBLANES)
        dst0 = pl.multiple_of(r * SUBLANES, SUBLANES)
        priority = r % 2 if isinstance(r, int) else 0
        pltpu.make_async_copy(src_hbm.at[pl.ds(src0, SUBLANES), :],
                              buf.at[s, pl.ds(dst0, SUBLANES), :], row_sem.at[s]).start(priority=priority)

    @pl.when(i == 0)
    def _():
        idx_copy(0, 0).start()
        idx_copy(0, 0).wait()

        def body(r, carry):
            issue_row(0, r)
            return carry
        lax.fori_loop(0, rows, body, 0, unroll=8)
        if n_steps > 1:
            idx_copy(1, 1).start()

    def step(slot):
        has_next = i + 1 < n_steps

        @pl.when(has_next)
        def _():
            idx_copy(i + 1, 1 - slot).wait()

        @pl.when(i + 2 < n_steps)
        def _():
            idx_copy(i + 2, slot).start()

        pltpu.make_async_copy(src_hbm.at[pl.ds(0, rows * SUBLANES), :], buf.at[slot], row_sem.at[slot]).wait()

        @pl.when(has_next)
        def _():
            for r in range(rows):
                issue_row(1 - slot, r)
            compute(slot)

        @pl.when(i + 1 == n_steps)
        def _():
            compute(slot)

    step(lax.rem(i, 2))


def _fetch_scratch(rows):
    return [pltpu.SMEM((2, rows), jnp.int32), pltpu.SemaphoreType.DMA((2,)),
            pltpu.VMEM((2, rows * SUBLANES, LANES), F32), pltpu.SemaphoreType.DMA((2,))]


def _dispatch_kernel(pad_ref, dest_hbm, h3_ref, xr_hbm, idx_smem, idx_sem, stage, row_sem, zbuf, z_sem,
                     *, tm, tb, min_used):
    i = pl.program_id(0)
    n = pl.num_programs(0)
    rows = tm * TOP_K

    def idx_copy(chunk, s):
        return pltpu.make_async_copy(dest_hbm.at[chunk], idx_smem.at[s], idx_sem.at[s])

    def rows_wait(s):
        span = xr_hbm.at[pl.ds(0, rows * SUBLANES), :]
        pltpu.make_async_copy(span, span, row_sem.at[s]).wait()

    @pl.when(i == 0)
    def _():
        zbuf[...] = jnp.zeros_like(zbuf)
        zero_to = lambda start: pltpu.make_async_copy(zbuf, xr_hbm.at[pl.ds(start, tb * SUBLANES), :], z_sem)
        for e in range(N_EXPERTS):
            zero_to(pl.multiple_of(pad_ref[e] * SUBLANES, SUBLANES)).start()
        for e in range(N_EXPERTS):
            zero_to(0).wait()
        n_used = pad_ref[N_EXPERTS]
        tail = range(min_used, xr_hbm.shape[0] // (tb * SUBLANES))
        for blk in tail:
            pl.when(blk >= n_used)(zero_to(blk * tb * SUBLANES).start)
        for blk in tail:
            pl.when(blk >= n_used)(zero_to(0).wait)
        idx_copy(0, 0).start()

    def step(slot):
        idx_copy(i, slot).wait()

        @pl.when(i + 1 < n)
        def _():
            idx_copy(i + 1, 1 - slot).start()

        stage[slot] = h3_ref[...]
        for r in range(rows):
            dst0 = pl.multiple_of(idx_smem[slot, r] * SUBLANES, SUBLANES)
            pltpu.make_async_copy(stage.at[slot, pl.ds((r // TOP_K) * SUBLANES, SUBLANES), :],
                                  xr_hbm.at[pl.ds(dst0, SUBLANES), :], row_sem.at[slot]).start(priority=r % 2)

        @pl.when(i >= 1)
        def _():
            rows_wait(1 - slot)

        @pl.when(i == n - 1)
        def _():
            rows_wait(slot)

    step(lax.rem(i, 2))


def _dispatch(pad_start, dest, h3, cap_rows, tm, tb):
    n = h3.shape[0] // SUBLANES
    grid_spec = pltpu.PrefetchScalarGridSpec(
        num_scalar_prefetch=1,
        grid=(n // tm,),
        in_specs=[pl.BlockSpec(memory_space=pl.ANY),
                  pl.BlockSpec((tm * SUBLANES, LANES), lambda i, pad: (i, 0))],
        out_specs=pl.BlockSpec(memory_space=pl.ANY),
        scratch_shapes=[pltpu.SMEM((2, tm * TOP_K), jnp.int32), pltpu.SemaphoreType.DMA((2,)),
                        pltpu.VMEM((2, tm * SUBLANES, LANES), F32), pltpu.SemaphoreType.DMA((2,)),
                        pltpu.VMEM((tb * SUBLANES, LANES), F32), pltpu.SemaphoreType.DMA(())],
    )
    return pl.pallas_call(
        functools.partial(_dispatch_kernel, tm=tm, tb=tb, min_used=-(-(n * TOP_K) // tb)),
        out_shape=jax.ShapeDtypeStruct((cap_rows * SUBLANES, LANES), F32),
        grid_spec=grid_spec,
        compiler_params=_cparams(1),
        name="dispatch",
    )(pad_start, dest.reshape(n // tm, tm * TOP_K), h3)


def _ffn_kernel(be_ref, nu_ref, x_ref, wgu_ref, bgu_ref, wdn_ref, bdn_ref, perm_ref, o_ref,
                wg_s, wu_s, wd_s, *, tb):
    i = pl.program_id(0)
    nu = nu_ref[0]
    new_expert = jnp.logical_or(i == 0, be_ref[i] != be_ref[jnp.maximum(i - 1, 0)])

    @pl.when(jnp.logical_and(new_expert, i < nu))
    def _():
        for t in range(2 * D_FF // DEINT_IN):
            r = _dot(wgu_ref[:, t * DEINT_IN:(t + 1) * DEINT_IN].astype(BF16), perm_ref[...])
            wg_s[:, t * LANES:(t + 1) * LANES] = r[:, 0:LANES].astype(BF16)
            wu_s[:, t * LANES:(t + 1) * LANES] = r[:, LANES:DEINT_IN].astype(BF16)
        wd_s[...] = wdn_ref[...].astype(BF16)

    @pl.when(i < nu)
    def _():
        x = jnp.concatenate([x_ref[pl.ds(j, tb, stride=SUBLANES), :].astype(BF16)
                             for j in range(D_MODEL // LANES)], axis=1)
        bgu = bgu_ref[...]
        g = jnp.minimum(_dot(x, wg_s[...]) + bgu[:, 0:D_FF], SWIGLU_LIMIT)
        u = jnp.clip(_dot(x, wu_s[...]) + bgu[:, D_FF:2 * D_FF], -SWIGLU_LIMIT, SWIGLU_LIMIT)
        hh = (u + 1.0) * (g * _sigmoid(SWIGLU_ALPHA * g))
        out = _dot(hh.astype(BF16), wd_s[...]) + bdn_ref[...]
        for j in range(D_MODEL // LANES):
            o_ref[pl.ds(j, tb, stride=SUBLANES), :] = out[:, j * LANES:(j + 1) * LANES]

    @pl.when(i >= nu)
    def _():
        o_ref[...] = jnp.zeros_like(o_ref)


def _ffn(blk_e, n_used, xr3, nblk, wgu, bgu, wdn, bdn, tb):
    perm = _deint_perm()
    wspec = lambda a: pl.BlockSpec((None,) + a.shape[1:], lambda i, be, nu: (be[i], 0, 0))
    grid_spec = pltpu.PrefetchScalarGridSpec(
        num_scalar_prefetch=2,
        grid=(nblk,),
        in_specs=[pl.BlockSpec((tb * SUBLANES, LANES), lambda i, be, nu: (jnp.minimum(i, nu[0] - 1), 0)),
                  wspec(wgu), wspec(bgu), wspec(wdn), wspec(bdn),
                  pl.BlockSpec(perm.shape, lambda i, be, nu: (0, 0))],
        out_specs=pl.BlockSpec((tb * SUBLANES, LANES), lambda i, be, nu: (i, 0)),
        scratch_shapes=[pltpu.VMEM((D_MODEL, D_FF), BF16), pltpu.VMEM((D_MODEL, D_FF), BF16),
                        pltpu.VMEM((D_FF, D_MODEL), BF16)],
    )
    return pl.pallas_call(
        functools.partial(_ffn_kernel, tb=tb),
        out_shape=jax.ShapeDtypeStruct((nblk * tb * SUBLANES, LANES), F32),
        grid_spec=grid_spec,
        compiler_params=_cparams(1),
        name="moe_ffn",
    )(blk_e, n_used, xr3, wgu, bgu, wdn, bdn, perm)


def _combine_kernel(dest_hbm, yr3_hbm, tg_ref, h_ref, lg_ref, lb_ref, op_ref, os_ref,
                    idx_smem, idx_sem, ybuf, row_sem, *, tm, n_prompt_tiles, n_steps):
    i = pl.program_id(0)
    rows = tm * TOP_K

    def compute(slot):
        tg = tg_ref[...]
        stride = TOP_K * SUBLANES
        pieces = []
        for j in range(D_MODEL // LANES):
            acc = tg[:, 0:1] * ybuf[slot, pl.ds(j, tm, stride=stride), :]
            for k in range(1, TOP_K):
                acc = acc + tg[:, k:k + 1] * ybuf[slot, pl.ds(k * SUBLANES + j, tm, stride=stride), :]
            pieces.append(acc)
        y = jnp.concatenate(pieces, axis=1)
        out = _layer_norm(DN_ALPHA * h_ref[...] + y, lg_ref[...], lb_ref[...])

        @pl.when(i < n_prompt_tiles)
        def _():
            op_ref[...] = out

        @pl.when(i >= n_prompt_tiles)
        def _():
            os_ref[...] = out

    _row_pipeline(i, n_steps, dest_hbm, yr3_hbm, idx_smem, idx_sem, ybuf, row_sem, rows, compute)


def _combine(dest, yr3, tg, h, lg, lb, n_prompt, tm):
    m = h.shape[0]
    npt = n_prompt // tm
    row = lambda w: pl.BlockSpec((tm, w), lambda i: (i, 0))
    par = lambda a: pl.BlockSpec(a.shape, lambda i: (0, 0))
    return pl.pallas_call(
        functools.partial(_combine_kernel, tm=tm, n_prompt_tiles=npt, n_steps=m // tm),
        out_shape=(jax.ShapeDtypeStruct((n_prompt, D_MODEL), F32),
                   jax.ShapeDtypeStruct((m - n_prompt, D_MODEL), F32)),
        grid=(m // tm,),
        in_specs=[pl.BlockSpec(memory_space=pl.ANY), pl.BlockSpec(memory_space=pl.ANY),
                  row(LANES), row(D_MODEL), par(lg), par(lb)],
        out_specs=(pl.BlockSpec((tm, D_MODEL), lambda i: (jnp.minimum(i, npt - 1), 0)),
                   pl.BlockSpec((tm, D_MODEL), lambda i: (jnp.maximum(i - npt, 0), 0))),
        scratch_shapes=_fetch_scratch(tm * TOP_K),
        compiler_params=_cparams(1),
        name="combine",
    )(dest.reshape(m // tm, tm * TOP_K), yr3, tg, h, lg, lb)


def _rope_tables(pos):
    half = ROPE_DIM // 2
    inv = ROPE_THETA ** (-jnp.arange(half, dtype=F32) / half)
    ang = pos.astype(F32)[:, None] * inv[None, :]
    cos, sin = jnp.cos(ang), jnp.sin(ang)
    return jnp.concatenate([cos] * 4, axis=1), jnp.concatenate([-sin, sin] * 2, axis=1)


def _pad_cols(w, width):
    return jnp.pad(w, ((0, 0), (0, width - w.shape[1])))


def _routing(top_i, rank, counts, n_blocks, tb):
    flat_e = top_i.reshape(-1)
    nblk_e = (counts + tb - 1) // tb
    pend = jnp.cumsum(nblk_e)
    pstart = (pend - nblk_e) * tb
    onehot = flat_e[:, None] == jnp.arange(N_EXPERTS, dtype=jnp.int32)[None, :]
    dest = (jnp.sum(jnp.where(onehot, pstart[None, :], 0), axis=1) + rank.reshape(-1)).astype(jnp.int32)
    n_used = pend[-1].astype(jnp.int32)
    pad_start = jnp.concatenate([(pstart + counts).astype(jnp.int32), n_used.reshape(1)])
    bidx = jnp.minimum(jnp.arange(n_blocks, dtype=jnp.int32), n_used - 1)
    blk_e = jnp.minimum(jnp.sum((pend[None, :] <= bidx[:, None]).astype(jnp.int32), axis=1), N_EXPERTS - 1)
    return dest, pad_start, blk_e, n_used.reshape(1)


def kernel(x_prompt, x_sample, cache_kv_latent, cache_k_rope, state_ssm, state_conv, page_table, w_in, ssd_conv_w, ssd_conv_b, ssd_dt_bias, ssd_a_log, ssd_d, ssd_norm_g, mla_q_norm_g, w_q_b, mla_kv_norm_g, w_kv_b, w_branch_a, w_branch_b, w_out, ln1_g, ln1_b, router_w, router_b, exp_w_gu, exp_b_gu, exp_w_down, exp_b_down, ln2_g, ln2_b):
    b_p, t_p, _ = x_prompt.shape
    b_s, t_s, _ = x_sample.shape
    n_pages = page_table.shape[1]
    page = cache_kv_latent.shape[2]
    past_len = n_pages * page
    n_p = b_p * t_p
    n_s = b_s * t_s
    n = n_p + n_s
    assert DEPTH == 1 and w_in.shape[0] == 1
    assert n_p % PROJ_TILE == 0 and t_p % ROW_TILE == 0 and n_s % ROW_TILE == 0
    assert t_p % (2 * ATTN_TQ) == 0 and t_p % SSD_CHUNK == 0
    assert t_s == SUBLANES and n_pages % (2 * PAGE_GROUP) == 0 and page == LANES

    wi = w_in[0]
    seg = lambda k: wi[:, IN_OFFS[k]:IN_OFFS[k + 1]]
    half = ROPE_DIM // 2
    w_kr = seg(2)
    w_kr_rot = jnp.concatenate([w_kr[:, half:], w_kr[:, :half]], axis=1)
    w_mla = jnp.concatenate([seg(0), seg(1), _pad_cols(w_kr, LANES), _pad_cols(w_kr_rot, LANES),
                             _pad_cols(seg(5), LANES)], axis=1).astype(BF16)
    w_z = seg(3).astype(BF16)
    w_xbc = seg(4).astype(BF16)
    w_gates = jnp.concatenate([seg(6), seg(7)], axis=1).astype(BF16)
    wq3 = w_q_b[0].reshape(Q_LORA, MLA_HEADS, QK_NOPE + ROPE_DIM)
    wq_nope = wq3[:, :, :QK_NOPE].reshape(Q_LORA, MLA_HEADS * QK_NOPE)
    wq_rope = wq3[:, :, QK_NOPE:]
    wq_rope_rot = jnp.concatenate([wq_rope[..., half:], wq_rope[..., :half]], axis=-1)
    wq = jnp.concatenate([wq_nope, wq_rope.reshape(Q_LORA, -1), wq_rope_rot.reshape(Q_LORA, -1)],
                         axis=1).astype(BF16)
    wkv3 = w_kv_b[0].reshape(KV_LORA, MLA_HEADS, QK_NOPE + V_HEAD)
    wuk = jnp.transpose(wkv3[:, :, :QK_NOPE], (1, 2, 0)).astype(BF16)
    wuv = jnp.transpose(wkv3[:, :, QK_NOPE:], (1, 0, 2)).astype(BF16)
    qg = mla_q_norm_g[0].reshape(1, Q_LORA)
    kvg = mla_kv_norm_g[0].reshape(1, KV_LORA)
    cw = ssd_conv_w[0]
    cb = ssd_conv_b[0].reshape(1, CONV_DIM)
    dtb = _pad_cols(ssd_dt_bias[0].reshape(1, SSD_HEADS), LANES)
    a_neg = _pad_cols(-jnp.exp(ssd_a_log[0].astype(F32)).reshape(1, SSD_HEADS), LANES)
    d_x = jnp.repeat(ssd_d[0].astype(F32), SSD_HEAD_DIM).reshape(1, D_INNER)
    e_mat = jnp.asarray(np.arange(LANES)[:, None] == np.arange(D_INNER)[None, :] // SSD_HEAD_DIM, BF16)
    ng = ssd_norm_g[0].reshape(1, D_INNER)
    wa = w_branch_a[0].astype(BF16)
    wb = w_branch_b[0].astype(BF16)
    wo = w_out[0].astype(BF16)
    rw = _pad_cols(router_w[0], LANES)
    rwh = rw.astype(BF16)
    rwl = (rw - rwh.astype(F32)).astype(BF16)
    rb = jnp.concatenate([router_b[0].reshape(1, N_EXPERTS),
                          jnp.full((1, LANES - N_EXPERTS), NEG, F32)], axis=1)
    bgu = jnp.concatenate([exp_b_gu[0][:, 0::2], exp_b_gu[0][:, 1::2]], axis=1).reshape(N_EXPERTS, 1, 2 * D_FF)
    bdn = exp_b_down[0].reshape(N_EXPERTS, 1, D_MODEL)
    row2 = lambda v: v[0].reshape(1, D_MODEL)

    cos_p, sin_p = _rope_tables(jnp.arange(t_p, dtype=jnp.int32))
    cos_s, sin_s = _rope_tables(jnp.tile(past_len + jnp.arange(t_s, dtype=jnp.int32), b_s))

    xp = x_prompt.reshape(n_p, D_MODEL)
    xs = x_sample.reshape(n_s, D_MODEL)
    qlat_p, qrope_p, c_p, cbf_p, kr_p, krbf_p, dt_p = _mla_in(xp, w_mla, qg, kvg, wq, wuk, cos_p, sin_p, ROW_TILE)
    qlat_s, qrope_s, c_s, _, kr_s, _, dt_s = _mla_in(xs, w_mla, qg, kvg, wq, wuk, cos_s, sin_s, ROW_TILE)
    z_p, xbc_p, gates_p = (_proj(xp, w, PROJ_TILE) for w in (w_z, w_xbc, w_gates))
    z_s, xbc_s, gates_s = (_proj(xs, w, ROW_TILE) for w in (w_z, w_xbc, w_gates))

    conv0_p = jnp.zeros((b_p, SUBLANES, CONV_DIM), F32)
    conv0_s = jnp.pad(state_conv[0], ((0, 0), (SUBLANES - (D_CONV - 1), 0), (0, 0)))
    ya_p, ssm_p, ctail_p = _ssd(xbc_p, z_p, dt_p, conv0_p, None, cw, cb, dtb, a_neg, d_x, ng, e_mat,
                                nb=b_p, nc=t_p // SSD_CHUNK, t_in=SSD_CHUNK, chunk=SSD_CHUNK, y_dtype=BF16)
    ya_s, ssm_s, ctail_s = _ssd(xbc_s, z_s, dt_s, conv0_s, state_ssm[0].reshape(b_s, D_INNER, D_STATE),
                                cw, cb, dtb, a_neg, d_x, ng, e_mat,
                                nb=b_s, nc=1, t_in=t_s, chunk=SAMPLE_CHUNK, y_dtype=F32)
    conv_p = ctail_p[:, SUBLANES - (D_CONV - 1):, :]
    conv_s = ctail_s[:, SUBLANES - (D_CONV - 1):, :]

    o_p = _attn_prompt(qlat_p, qrope_p, cbf_p, krbf_p, wuv, nb=b_p, t=t_p, tq=ATTN_TQ)
    ql_s = jnp.transpose(qlat_s.reshape(MLA_HEADS, b_s, t_s, KV_LORA), (1, 0, 2, 3))
    qr_s = jnp.transpose(qrope_s.reshape(MLA_HEADS, b_s, t_s, ROPE_DIM), (1, 0, 2, 3))
    cnew_s = jnp.pad(c_s.reshape(b_s, t_s, KV_LORA), ((0, 0), (0, LANES - t_s), (0, 0)))
    krnew_s = jnp.swapaxes(jnp.pad(kr_s.reshape(b_s, t_s, ROPE_DIM), ((0, 0), (0, LANES - t_s), (0, 0))), 1, 2)
    olat_s = _attn_sample(page_table, ql_s.reshape(b_s, MLA_HEADS * t_s, KV_LORA),
                          qr_s.reshape(b_s, MLA_HEADS * t_s, ROPE_DIM), cnew_s, krnew_s,
                          cache_kv_latent[0], jnp.swapaxes(cache_k_rope[0], 1, 2),
                          t_new=t_s, gp=PAGE_GROUP)
    olat_s = jnp.transpose(olat_s.reshape(b_s, MLA_HEADS, t_s, KV_LORA), (1, 0, 2, 3)).reshape(MLA_HEADS, n_s, KV_LORA)
    o_s = _head_proj(olat_s, wuv)

    h, h3, top_i, top_g, rank, counts = _merge(ya_p, ya_s.astype(BF16), o_p, o_s, gates_p, gates_s, xp, xs, wa, wb, wo,
                                 row2(ln1_g), row2(ln1_b), rwh, rwl, rb, ROW_TILE)

    tb = MOE_TB
    n_blocks = -(-(n * TOP_K + N_EXPERTS * (tb - 1)) // tb)
    dest, pad_start, blk_e, n_used = _routing(top_i[:, :TOP_K], rank[:, :TOP_K], counts[0, :N_EXPERTS], n_blocks, tb)
    xr3 = _dispatch(pad_start, dest, h3, (n_blocks + 1) * tb, COMBINE_TILE, tb)
    yr3 = _ffn(blk_e, n_used, xr3, n_blocks, exp_w_gu[0], bgu, exp_w_down[0], bdn, tb)
    y_p, y_s = _combine(dest, yr3, top_g, h, row2(ln2_g), row2(ln2_b), n_p, COMBINE_TILE)

    return (y_p.reshape(b_p, t_p, D_MODEL), y_s.reshape(b_s, t_s, D_MODEL),
            c_p.reshape(1, b_p, t_p, KV_LORA), kr_p.reshape(1, b_p, t_p, ROPE_DIM),
            ssm_p.reshape(1, b_p, SSD_HEADS, SSD_HEAD_DIM, D_STATE), conv_p[None],
            c_s.reshape(1, b_s, t_s, KV_LORA), kr_s.reshape(1, b_s, t_s, ROPE_DIM),
            ssm_s.reshape(1, b_s, SSD_HEADS, SSD_HEAD_DIM, D_STATE), conv_s[None])
```
